```python
import math
import jax, jax.numpy as jnp
from jax import lax
import numpy as np

D_MODEL = 1024
BATCH = 8
SEQ = 2048
DEPTH = 2
DEC_BATCH = 128
DEC_SEQ = 8
PAST_LEN = 16384
PAGE_SIZE = 128

N_MIXERS = 2
N_MLSTM = (DEPTH + 1) // 2
N_S5 = DEPTH // 2
M_HEADS = 8
QK_DIM = D_MODEL // 2
V_DIM = D_MODEL
DK = QK_DIM // M_HEADS
DV = V_DIM // M_HEADS
MLSTM_CHUNK = 64
IN_COLS = 2 * QK_DIM + 2 * V_DIM + 2 * M_HEADS
S5_GROUP = 16
S5_GROUPS = D_MODEL // S5_GROUP
S5_STATE = 64
DT_MIN = 0.001
DT_MAX = 0.1
D_FF = 4 * D_MODEL
EPS = 1e-6

kernel_name = "mlstm_s5_hybrid_decode_step"


def rms_norm(x, g):
    xf = x.astype(jnp.float32)
    y = xf * lax.rsqrt(jnp.mean(xf * xf, axis=-1, keepdims=True) + EPS)
    return (y * g.astype(jnp.float32)).astype(x.dtype)


def sq_relu_mlp(x, w_up, w_down):
    h = jnp.square(jax.nn.relu(jnp.einsum('bld,df->blf', x, w_up)))
    return jnp.einsum('blf,fd->bld', h, w_down)


def mlstm_chunkwise(q, k, v, logi, logf, C0, n0, m0):
    B, L = q.shape[0], q.shape[1]
    cs = math.gcd(L, MLSTM_CHUNK)
    nc = L // cs

    def to_chunks(a):
        a = a.reshape((B, nc, cs) + a.shape[2:])
        return jnp.swapaxes(jnp.moveaxis(a, 1, 0), 2, 3)

    mask = jnp.tril(jnp.ones((cs, cs), dtype=bool))

    def step(carry, xs):
        C, n, m = carry
        qc, kc, vc, li, lf = xs
        b = jnp.cumsum(lf, axis=-1)
        logD = b[..., :, None] - b[..., None, :] + li[..., None, :]
        logD = jnp.where(mask, logD, -jnp.inf)
        inter = b + m[..., None]
        m_t = jnp.maximum(inter, jnp.max(logD, axis=-1))
        w_intra = jnp.exp(logD - m_t[..., None])
        w_inter = jnp.exp(inter - m_t)
        s = jnp.einsum('bhtd,bhsd->bhts', qc, kc) * w_intra
        num = (w_inter[..., None] * jnp.einsum('bhtd,bhde->bhte', qc, C)
               + jnp.einsum('bhts,bhse->bhte', s, vc))
        den = w_inter * jnp.einsum('bhtd,bhd->bht', qc, n) + jnp.sum(s, axis=-1)
        den = jnp.maximum(jnp.abs(den), jnp.exp(-m_t))
        h = num / den[..., None]
        m_new = m_t[..., -1]
        decay = jnp.exp(b[..., -1] + m - m_new)
        ws = jnp.exp(b[..., -1:] - b + li - m_new[..., None])
        C_new = decay[..., None, None] * C + jnp.einsum('bhs,bhsd,bhse->bhde', ws, kc, vc)
        n_new = decay[..., None] * n + jnp.einsum('bhs,bhsd->bhd', ws, kc)
        return (C_new, n_new, m_new), h

    xs = (to_chunks(q), to_chunks(k), to_chunks(v), to_chunks(logi), to_chunks(logf))
    (C, n, m), h = lax.scan(step, (C0, n0, m0), xs)
    h = jnp.moveaxis(jnp.swapaxes(h, 2, 3), 0, 1).reshape(B, L, M_HEADS, DV)
    return h, C, n, m


def mlstm_mixer(x, w_in, b_i, b_f, head_g, w_out, C0, n0, m0):
    B, L, _ = x.shape
    f32 = jnp.float32
    proj = jnp.einsum('bld,de->ble', x, w_in).astype(f32)
    q, k, v, o, gi, gf = jnp.split(
        proj, [QK_DIM, 2 * QK_DIM, 2 * QK_DIM + V_DIM, 2 * QK_DIM + 2 * V_DIM,
               2 * QK_DIM + 2 * V_DIM + M_HEADS], axis=-1)
    q = q.reshape(B, L, M_HEADS, DK)
    k = k.reshape(B, L, M_HEADS, DK) * (DK ** -0.5)
    v = v.reshape(B, L, M_HEADS, DV)
    logi = gi + b_i.astype(f32)
    logf = jax.nn.log_sigmoid(gf + b_f.astype(f32))
    h, C, n, m = mlstm_chunkwise(q, k, v, logi, logf,
                                 C0.astype(f32), n0.astype(f32), m0.astype(f32))
    h = h * lax.rsqrt(jnp.mean(h * h, axis=-1, keepdims=True) + EPS)
    h = h.reshape(B, L, V_DIM) * head_g.astype(f32) * jax.nn.sigmoid(o)
    out = jnp.einsum('ble,ed->bld', h.astype(x.dtype), w_out)
    return out, C, n, m


def s5_mixer(x, lam_re, lam_im, log_dt, b_re, b_im, c_re, c_im, d_skip, w_glu, h0_re, h0_im):
    B, L, _ = x.shape
    f32 = jnp.float32
    u = x.astype(f32).reshape(B, L, S5_GROUPS, S5_GROUP)
    dt = jnp.exp(log_dt.astype(f32))[:, None]
    lr, li = lam_re.astype(f32), lam_im.astype(f32)
    mag = jnp.exp(lr * dt)
    a_re, a_im = mag * jnp.cos(li * dt), mag * jnp.sin(li * dt)
    den = lr * lr + li * li
    z_re = a_re - 1.0
    coef_re = (z_re * lr + a_im * li) / den
    coef_im = (a_im * lr - z_re * li) / den
    br, bi = b_re.astype(f32), b_im.astype(f32)
    bb_re = coef_re[..., None] * br - coef_im[..., None] * bi
    bb_im = coef_re[..., None] * bi + coef_im[..., None] * br
    bu_re = jnp.einsum('blgc,gpc->blgp', u, bb_re)
    bu_im = jnp.einsum('blgc,gpc->blgp', u, bb_im)
    a_re_s = jnp.broadcast_to(a_re[None, None], (1, L, S5_GROUPS, S5_STATE))
    a_im_s = jnp.broadcast_to(a_im[None, None], (1, L, S5_GROUPS, S5_STATE))

    def combine(e1, e2):
        a1r, a1i, b1r, b1i = e1
        a2r, a2i, b2r, b2i = e2
        return (a1r * a2r - a1i * a2i, a1r * a2i + a1i * a2r,
                a2r * b1r - a2i * b1i + b2r, a2r * b1i + a2i * b1r + b2i)

    acr, aci, bcr, bci = lax.associative_scan(combine, (a_re_s, a_im_s, bu_re, bu_im), axis=1)
    h0r = h0_re.astype(f32)[:, None]
    h0i = h0_im.astype(f32)[:, None]
    h_re = acr * h0r - aci * h0i + bcr
    h_im = acr * h0i + aci * h0r + bci
    y = (jnp.einsum('gcp,blgp->blgc', c_re.astype(f32), h_re)
         - jnp.einsum('gcp,blgp->blgc', c_im.astype(f32), h_im)
         + d_skip.astype(f32).reshape(S5_GROUPS, S5_GROUP) * u)
    y = jax.nn.gelu(y.reshape(B, L, D_MODEL)).astype(x.dtype)
    ag = jnp.einsum('bld,de->ble', y, w_glu)
    a, g = jnp.split(ag, 2, axis=-1)
    out = a * jax.nn.sigmoid(g)
    return out, h_re[:, -1], h_im[:, -1]


def trunk(x, mC, mn, mm, sre, sim, p):
    new_C, new_n, new_m, new_re, new_im = [], [], [], [], []
    for layer in range(DEPTH):
        j = layer // N_MIXERS
        h = rms_norm(x, p['norm_mix_g'][layer])
        if layer % N_MIXERS == 0:
            out, C, n, m = mlstm_mixer(h, p['mlstm_w_in'][j], p['mlstm_b_i'][j], p['mlstm_b_f'][j],
                                       p['mlstm_head_norm_g'][j], p['mlstm_w_out'][j],
                                       mC[j], mn[j], mm[j])
            new_C.append(C)
            new_n.append(n)
            new_m.append(m)
        else:
            out, hr, hi = s5_mixer(h, p['s5_lambda_re'][j], p['s5_lambda_im'][j], p['s5_log_dt'][j],
                                   p['s5_b_re'][j], p['s5_b_im'][j], p['s5_c_re'][j], p['s5_c_im'][j],
                                   p['s5_d'][j], p['s5_w_glu'][j], sre[j], sim[j])
            new_re.append(hr)
            new_im.append(hi)
        x = x + out.astype(x.dtype)
        x = x + sq_relu_mlp(rms_norm(x, p['norm_ffn_g'][layer]),
                            p['ffn_w_up'][layer], p['ffn_w_down'][layer]).astype(x.dtype)
    y = rms_norm(x, p['norm_final_g'])
    return (y, jnp.stack(new_C), jnp.stack(new_n), jnp.stack(new_m),
            jnp.stack(new_re), jnp.stack(new_im))


def setup_inputs(seed: int = 0) -> dict:
    key = jax.random.key(seed)
    ks = jax.random.split(key, 32)
    f32 = jnp.float32
    nrm = lambda k, shape, s: jax.random.normal(k, shape, f32) * s
    lam_re = -0.5 + nrm(ks[0], (N_S5, S5_GROUPS, S5_STATE), 0.01)
    lam_im = (math.pi * jnp.arange(S5_STATE, dtype=f32))[None, None] + nrm(ks[1], (N_S5, S5_GROUPS, S5_STATE), 0.01)
    log_dt = jax.random.uniform(ks[2], (N_S5, S5_GROUPS), f32, math.log(DT_MIN), math.log(DT_MAX))
    b_f = jnp.linspace(3.0, 6.0, M_HEADS, dtype=f32)[None] + nrm(ks[3], (N_MLSTM, M_HEADS), 0.1)
    return {
        'x_prompt': nrm(ks[4], (BATCH, SEQ, D_MODEL), 1.0),
        'x_sample': nrm(ks[5], (DEC_BATCH, DEC_SEQ, D_MODEL), 1.0),
        'state_mlstm_C': nrm(ks[6], (N_MLSTM, DEC_BATCH, M_HEADS, DK, DV), 0.5),
        'state_mlstm_n': nrm(ks[7], (N_MLSTM, DEC_BATCH, M_HEADS, DK), 0.5),
        'state_mlstm_m': nrm(ks[8], (N_MLSTM, DEC_BATCH, M_HEADS), 0.5),
        'state_s5_re': nrm(ks[9], (N_S5, DEC_BATCH, S5_GROUPS, S5_STATE), 0.3),
        'state_s5_im': nrm(ks[10], (N_S5, DEC_BATCH, S5_GROUPS, S5_STATE), 0.3),
        'norm_mix_g': 1.0 + nrm(ks[11], (DEPTH, D_MODEL), 0.02),
        'norm_ffn_g': 1.0 + nrm(ks[12], (DEPTH, D_MODEL), 0.02),
        'norm_final_g': 1.0 + nrm(ks[13], (D_MODEL,), 0.02),
        'mlstm_w_in': nrm(ks[14], (N_MLSTM, D_MODEL, IN_COLS), D_MODEL ** -0.5),
        'mlstm_b_i': -1.0 + nrm(ks[15], (N_MLSTM, M_HEADS), 0.1),
        'mlstm_b_f': b_f,
        'mlstm_head_norm_g': 1.0 + nrm(ks[16], (N_MLSTM, V_DIM), 0.02),
        'mlstm_w_out': nrm(ks[17], (N_MLSTM, V_DIM, D_MODEL), V_DIM ** -0.5),
        's5_lambda_re': lam_re,
        's5_lambda_im': lam_im,
        's5_log_dt': log_dt,
        's5_b_re': nrm(ks[18], (N_S5, S5_GROUPS, S5_STATE, S5_GROUP), (2 * S5_GROUP) ** -0.5),
        's5_b_im': nrm(ks[19], (N_S5, S5_GROUPS, S5_STATE, S5_GROUP), (2 * S5_GROUP) ** -0.5),
        's5_c_re': nrm(ks[20], (N_S5, S5_GROUPS, S5_GROUP, S5_STATE), S5_STATE ** -0.5),
        's5_c_im': nrm(ks[21], (N_S5, S5_GROUPS, S5_GROUP, S5_STATE), S5_STATE ** -0.5),
        's5_d': nrm(ks[22], (N_S5, D_MODEL), 1.0),
        's5_w_glu': nrm(ks[23], (N_S5, D_MODEL, 2 * D_MODEL), D_MODEL ** -0.5),
        'ffn_w_up': nrm(ks[24], (DEPTH, D_MODEL, D_FF), D_MODEL ** -0.5),
        'ffn_w_down': nrm(ks[25], (DEPTH, D_FF, D_MODEL), D_FF ** -0.5),
    }


def reference(x_prompt, x_sample, state_mlstm_C, state_mlstm_n, state_mlstm_m, state_s5_re, state_s5_im,
              norm_mix_g, norm_ffn_g, norm_final_g, mlstm_w_in, mlstm_b_i, mlstm_b_f, mlstm_head_norm_g,
              mlstm_w_out, s5_lambda_re, s5_lambda_im, s5_log_dt, s5_b_re, s5_b_im, s5_c_re, s5_c_im,
              s5_d, s5_w_glu, ffn_w_up, ffn_w_down):
    params = dict(norm_mix_g=norm_mix_g, norm_ffn_g=norm_ffn_g, norm_final_g=norm_final_g,
                  mlstm_w_in=mlstm_w_in, mlstm_b_i=mlstm_b_i, mlstm_b_f=mlstm_b_f,
                  mlstm_head_norm_g=mlstm_head_norm_g, mlstm_w_out=mlstm_w_out,
                  s5_lambda_re=s5_lambda_re, s5_lambda_im=s5_lambda_im, s5_log_dt=s5_log_dt,
                  s5_b_re=s5_b_re, s5_b_im=s5_b_im, s5_c_re=s5_c_re, s5_c_im=s5_c_im,
                  s5_d=s5_d, s5_w_glu=s5_w_glu, ffn_w_up=ffn_w_up, ffn_w_down=ffn_w_down)
    f32 = jnp.float32
    bp = x_prompt.shape[0]
    z_C = jnp.zeros((N_MLSTM, bp, M_HEADS, DK, DV), f32)
    z_n = jnp.zeros((N_MLSTM, bp, M_HEADS, DK), f32)
    z_m = jnp.zeros((N_MLSTM, bp, M_HEADS), f32)
    z_s = jnp.zeros((N_S5, bp, S5_GROUPS, S5_STATE), f32)
    y_prompt, p_C, p_n, p_m, p_re, p_im = trunk(x_prompt, z_C, z_n, z_m, z_s, z_s, params)
    y_sample, s_C, s_n, s_m, s_re, s_im = trunk(x_sample, state_mlstm_C, state_mlstm_n, state_mlstm_m,
                                                state_s5_re, state_s5_im, params)
    return (y_prompt, y_sample, p_C, p_n, p_m, p_re, p_im, s_C, s_n, s_m, s_re, s_im)
```

```python
import functools
import math

import jax
import jax.numpy as jnp
from jax import lax
from jax.experimental import pallas as pl
from jax.experimental.pallas import tpu as pltpu

f32 = jnp.float32
bf16 = jnp.bfloat16

D_MODEL = 1024
N_HEADS = 8
DK = 64
DV = 128
QK_DIM = N_HEADS * DK
V_DIM = N_HEADS * DV
D_FF = 4 * D_MODEL
S5_GROUPS = 64
S5_GROUP = 16
S5_STATE = 64
S5_N = S5_GROUPS * S5_STATE
EPS = 1e-6

LANES = 128
SUBLANES = 8
ROW_TILE = 512
CORE_ROWS = 256
FF_CHUNK = 1024
S5_BLOCK_CH = LANES
S5_BLOCK_ST = S5_BLOCK_CH // S5_GROUP * S5_STATE
N_S5_BLOCKS = D_MODEL // S5_BLOCK_CH
VMEM_LIMIT_BYTES = 56 * 1024 * 1024
NEG_BIG = -1e30


def _params(n_axes):
    return pltpu.CompilerParams(dimension_semantics=("arbitrary",) * n_axes,
                                vmem_limit_bytes=VMEM_LIMIT_BYTES)


def _const_spec(shape):
    nd = len(shape)
    return pl.BlockSpec(shape, lambda *_: (0,) * nd, pipeline_mode=pl.Buffered(1))


def _dot(a, b):
    return jnp.dot(a, b, preferred_element_type=f32)


def _rms(x, g):
    return x * lax.rsqrt(jnp.mean(x * x, axis=-1, keepdims=True) + EPS) * g


def _ffn(x1, g_ref, wup_ref, wdn_ref):
    xn = _rms(x1, g_ref[...]).astype(bf16)
    acc = None
    for c in range(D_FF // FF_CHUNK):
        cols = slice(c * FF_CHUNK, (c + 1) * FF_CHUNK)
        hid = _dot(xn, wup_ref[:, cols])
        hid = jnp.square(jnp.maximum(hid, 0.0)).astype(bf16)
        part = _dot(hid, wdn_ref[cols, :])
        acc = part if acc is None else acc + part
    return x1 + acc


def _inproj_body(x_ref, g_ref, w_ref, wg_ref, bias_ref, q_ref, k_ref, v_ref, o_ref, gate_ref):
    xn = _rms(x_ref[...], g_ref[...]).astype(bf16)
    q_ref[...] = _dot(xn, w_ref[:, 0:QK_DIM]).astype(bf16)
    k_ref[...] = _dot(xn, w_ref[:, QK_DIM:2 * QK_DIM]) * (DK ** -0.5)
    v_ref[...] = _dot(xn, w_ref[:, 2 * QK_DIM:2 * QK_DIM + V_DIM]).astype(bf16)
    o_ref[...] = _dot(xn, w_ref[:, 2 * QK_DIM + V_DIM:2 * QK_DIM + 2 * V_DIM])
    gz = _dot(xn, wg_ref[...]) + bias_ref[...]
    lane = lax.broadcasted_iota(jnp.int32, gz.shape, 1)
    gate_ref[...] = jnp.where(lane < N_HEADS, gz, jax.nn.log_sigmoid(gz))


def _inproj(x2d, g, w_qkvo, w_gate, gate_bias):
    t = x2d.shape[0]
    row = lambda n: pl.BlockSpec((ROW_TILE, n), lambda i: (i, 0))
    return pl.pallas_call(
        _inproj_body,
        grid=(t // ROW_TILE,),
        in_specs=[row(D_MODEL), _const_spec((1, D_MODEL)), _const_spec(w_qkvo.shape),
                  _const_spec(w_gate.shape), _const_spec((1, LANES))],
        out_specs=[row(QK_DIM), row(QK_DIM), row(V_DIM), row(V_DIM), row(LANES)],
        out_shape=[jax.ShapeDtypeStruct((t, QK_DIM), bf16), jax.ShapeDtypeStruct((t, QK_DIM), f32),
                   jax.ShapeDtypeStruct((t, V_DIM), bf16), jax.ShapeDtypeStruct((t, V_DIM), f32),
                   jax.ShapeDtypeStruct((t, LANES), f32)],
        compiler_params=_params(1),
        name="l0_inproj",
    )(x2d, g, w_qkvo, w_gate, gate_bias)


def _seg_prefix(x, seg, pos, is_max):
    sh = 1
    while sh < seg:
        prev = pltpu.roll(x, sh, 1)
        if is_max:
            x = jnp.maximum(x, jnp.where(pos >= sh, prev, NEG_BIG))
        else:
            x = x + jnp.where(pos >= sh, prev, 0.0)
        sh *= 2
    return x


def _seg_allreduce(x, seg, pos, is_max):
    n = x.shape[1]
    sh = 1
    while sh < seg:
        partner = jnp.where((pos & sh) == 0, pltpu.roll(x, n - sh, 1), pltpu.roll(x, sh, 1))
        x = jnp.maximum(x, partner) if is_max else x + partner
        sh *= 2
    return x


def _mlstm_body(*refs, nseq, cs, has_state):
    if has_state:
        (q_ref, k_ref, v_ref, gate_ref, c0_ref, n0_ref, mrow_ref,
         h_ref, cout_ref, nout_ref, mout_ref) = refs
        m_rows = mrow_ref[...]
    else:
        (q_ref, k_ref, v_ref, gate_ref,
         h_ref, cout_ref, nout_ref, mout_ref, c_scr, n_scr, m_scr) = refs
        t_idx = pl.program_id(1)

        @pl.when(t_idx == 0)
        def _():
            c_scr[...] = jnp.zeros_like(c_scr)
            n_scr[...] = jnp.zeros_like(n_scr)
            m_scr[...] = jnp.zeros_like(m_scr)

        m_rows = m_scr[...]

    rows = nseq * cs
    log_cs = cs.bit_length() - 1

    gate_t = gate_ref[...].T
    logi = gate_t[0:N_HEADS]
    logf = gate_t[N_HEADS:2 * N_HEADS]
    pos = lax.broadcasted_iota(jnp.int32, (N_HEADS, rows), 1) & (cs - 1)
    b = _seg_prefix(logf, cs, pos, False)
    a = logi - b
    g = jnp.maximum(m_rows, _seg_prefix(a, cs, pos, True))
    g_last = _seg_allreduce(g, cs, pos, True)
    b_last = _seg_allreduce(logf, cs, pos, False)
    m_new = b_last + g_last
    stack = jnp.concatenate(
        [m_new, g, jnp.exp(m_rows - g), jnp.exp(-(b + g)), jnp.exp(a - g_last), jnp.exp(m_rows - g_last),
         jnp.zeros((LANES - 6 * N_HEADS, rows), f32)], axis=0)
    colz = stack.T
    col = lambda kind, h: colz[:, kind * N_HEADS + h:kind * N_HEADS + h + 1]
    M_NEW, G, W_INTER, CLAMP, W_STATE, DECAY = range(6)

    ri = lax.broadcasted_iota(jnp.int32, (rows, rows), 0)
    ci = lax.broadcasted_iota(jnp.int32, (rows, rows), 1)
    causal = ci <= ri
    if nseq > 1:
        log_dk = DK.bit_length() - 1
        causal = jnp.logical_and(causal, (ri >> log_cs) == (ci >> log_cs))
        bd_q = ((lax.broadcasted_iota(jnp.int32, (rows, nseq * DK), 0) >> log_cs)
                == (lax.broadcasted_iota(jnp.int32, (rows, nseq * DK), 1) >> log_dk))
        bd_k = ((lax.broadcasted_iota(jnp.int32, (nseq * DK, rows), 0) >> log_dk)
                == (lax.broadcasted_iota(jnp.int32, (nseq * DK, rows), 1) >> log_cs))

    for h in range(N_HEADS):
        qh = q_ref[:, h * DK:(h + 1) * DK]
        kh = k_ref[:, h * DK:(h + 1) * DK]
        vh = v_ref[:, h * DV:(h + 1) * DV]
        qf = qh.astype(f32)

        logw = a[h:h + 1, :] - col(G, h)
        w = jnp.exp(jnp.where(causal, logw, NEG_BIG))
        qk = lax.dot_general(qh, kh.astype(bf16), (((1,), (1,)), ((), ())), preferred_element_type=f32)
        s = qk * w
        intra = _dot(s.astype(bf16), vh)
        s_sum = jnp.sum(s, axis=1, keepdims=True)

        if has_state:
            c_prev = c0_ref[:, h].reshape(nseq * DK, DV)
            q_bd = jnp.where(bd_q, jnp.concatenate([qf] * nseq, axis=1), 0.0).astype(bf16)
            inter = _dot(q_bd, c_prev.astype(bf16))
            n_prev = n0_ref[:, h * DK:(h + 1) * DK]
            n_rows = jnp.broadcast_to(n_prev[:, None, :], (nseq, cs, DK)).reshape(rows, DK)
        else:
            c_prev = c_scr[h]
            inter = _dot(qh, c_prev.astype(bf16))
            n_prev = n_scr[h]
            n_rows = n_prev
        qn = jnp.sum(qf * n_rows, axis=1, keepdims=True)

        w_inter = col(W_INTER, h)
        num = w_inter * inter + intra
        den = jnp.maximum(jnp.abs(w_inter * qn + s_sum), col(CLAMP, h))
        h_ref[:, h * DV:(h + 1) * DV] = num / den

        wk = col(W_STATE, h) * kh
        if has_state:
            wk_bd = jnp.where(bd_k, jnp.broadcast_to(wk.T[None], (nseq, DK, rows)).reshape(nseq * DK, rows), 0.0)
            dc = _dot(wk_bd.astype(bf16), vh)
            dec = col(DECAY, h).reshape(nseq, 1, cs, 1)
            dec_rows = jnp.broadcast_to(dec, (nseq, DK // cs, cs, 1)).reshape(nseq * DK, 1)
            cout_ref[:, h] = (dec_rows * c_prev + dc).reshape(nseq, DK, DV)
            dec_seq = jnp.max(col(DECAY, h).reshape(nseq, cs, 1), axis=1)
            nout_ref[:, h * DK:(h + 1) * DK] = dec_seq * n_prev + jnp.sum(wk.reshape(nseq, cs, DK), axis=1)
        else:
            dc = lax.dot_general(wk.astype(bf16), vh, (((0,), (0,)), ((), ())), preferred_element_type=f32)
            dec = colz[rows - 1:rows, DECAY * N_HEADS + h:DECAY * N_HEADS + h + 1]
            c_scr[h] = dec * c_prev + dc
            n_scr[h] = dec * n_prev + jnp.sum(wk, axis=0, keepdims=True)

    if has_state:
        mout_ref[...] = jnp.max(colz.reshape(nseq, cs, LANES), axis=1)
    else:
        m_scr[...] = m_new

        @pl.when(t_idx == pl.num_programs(1) - 1)
        def _():
            cout_ref[0] = c_scr[...]
            for h in range(N_HEADS):
                nout_ref[0, :, h * DK:(h + 1) * DK] = n_scr[h]
            mout_ref[0] = colz[rows - 1:rows, :]


def _mlstm_core(q, k, v, gates, state, n_batch, seq_len):
    t = q.shape[0]
    if state is None:
        nt = seq_len // CORE_ROWS
        grid = (n_batch, nt)
        rmap = lambda b, i: (b * nt + i, 0)
        row = lambda n: pl.BlockSpec((CORE_ROWS, n), rmap)
        in_specs = [row(QK_DIM), row(QK_DIM), row(V_DIM), row(LANES)]
        out_specs = [row(V_DIM),
                     pl.BlockSpec((1, N_HEADS, DK, DV), lambda b, i: (b, 0, 0, 0)),
                     pl.BlockSpec((1, 1, QK_DIM), lambda b, i: (b, 0, 0)),
                     pl.BlockSpec((1, 1, LANES), lambda b, i: (b, 0, 0))]
        out_shape = [jax.ShapeDtypeStruct((t, V_DIM), f32),
                     jax.ShapeDtypeStruct((n_batch, N_HEADS, DK, DV), f32),
                     jax.ShapeDtypeStruct((n_batch, 1, QK_DIM), f32),
                     jax.ShapeDtypeStruct((n_batch, 1, LANES), f32)]
        scratch = [pltpu.VMEM((N_HEADS, DK, DV), f32), pltpu.VMEM((N_HEADS, 1, DK), f32),
                   pltpu.VMEM((N_HEADS, CORE_ROWS), f32)]
        body = functools.partial(_mlstm_body, nseq=1, cs=CORE_ROWS, has_state=False)
        args = (q, k, v, gates)
    else:
        c0, n0, m_rows = state
        nseq = CORE_ROWS // seq_len
        grid = (t // CORE_ROWS, 1)
        row = lambda n: pl.BlockSpec((CORE_ROWS, n), lambda i, _: (i, 0))
        cspec = pl.BlockSpec((nseq, N_HEADS, DK, DV), lambda i, _: (i, 0, 0, 0))
        nspec = pl.BlockSpec((nseq, QK_DIM), lambda i, _: (i, 0))
        in_specs = [row(QK_DIM), row(QK_DIM), row(V_DIM), row(LANES), cspec, nspec,
                    pl.BlockSpec((N_HEADS, CORE_ROWS), lambda i, _: (0, i))]
        out_specs = [row(V_DIM), cspec, nspec, pl.BlockSpec((nseq, LANES), lambda i, _: (i, 0))]
        out_shape = [jax.ShapeDtypeStruct((t, V_DIM), f32),
                     jax.ShapeDtypeStruct((n_batch, N_HEADS, DK, DV), f32),
                     jax.ShapeDtypeStruct((n_batch, QK_DIM), f32),
                     jax.ShapeDtypeStruct((n_batch, LANES), f32)]
        scratch = []
        body = functools.partial(_mlstm_body, nseq=nseq, cs=seq_len, has_state=True)
        args = (q, k, v, gates, c0, n0, m_rows)
    return pl.pallas_call(
        body, grid=grid, in_specs=in_specs, out_specs=out_specs, out_shape=out_shape,
        scratch_shapes=scratch, compiler_params=_params(2), name="l0_mlstm_core",
    )(*args)


def _outproj_ffn_body(h_ref, o_ref, x_ref, hg_ref, wout_ref, gffn_ref, wup_ref, wdn_ref, y_ref):
    parts = []
    for h in range(N_HEADS):
        hh = h_ref[:, h * DV:(h + 1) * DV]
        parts.append(hh * lax.rsqrt(jnp.mean(hh * hh, axis=-1, keepdims=True) + EPS))
    hn = jnp.concatenate(parts, axis=1) * hg_ref[...] * jax.nn.sigmoid(o_ref[...])
    x1 = x_ref[...] + _dot(hn.astype(bf16), wout_ref[...])
    y_ref[...] = _ffn(x1, gffn_ref, wup_ref, wdn_ref)


def _outproj_ffn(hmix, o, x2d, head_g, w_out, g_ffn, w_up, w_down):
    t = x2d.shape[0]
    row = pl.BlockSpec((ROW_TILE, D_MODEL), lambda i: (i, 0))
    return pl.pallas_call(
        _outproj_ffn_body,
        grid=(t // ROW_TILE,),
        in_specs=[row, row, row, _const_spec((1, V_DIM)), _const_spec(w_out.shape),
                  _const_spec((1, D_MODEL)), _const_spec(w_up.shape), _const_spec(w_down.shape)],
        out_specs=row,
        out_shape=jax.ShapeDtypeStruct((t, D_MODEL), f32),
        compiler_params=_params(1),
        name="l0_outproj_ffn",
    )(hmix, o, x2d, head_g, w_out, g_ffn, w_up, w_down)


def _s5_prep_body(lre_ref, lim_ref, ldt_ref, bre_ref, bim_ref, are_ref, aim_ref, bbre_ref, bbim_ref):
    lr = lre_ref[...]
    li = lim_ref[...]
    dt = jnp.exp(ldt_ref[...])
    mag = jnp.exp(lr * dt)
    a_re = mag * jnp.cos(li * dt)
    a_im = mag * jnp.sin(li * dt)
    den = lr * lr + li * li
    z_re = a_re - 1.0
    coef_re = ((z_re * lr + a_im * li) / den)[:, None, :]
    coef_im = ((a_im * lr - z_re * li) / den)[:, None, :]
    br = bre_ref[...]
    bi = bim_ref[...]
    are_ref[...] = a_re
    aim_ref[...] = a_im
    bbre_ref[...] = coef_re * br - coef_im * bi
    bbim_ref[...] = coef_re * bi + coef_im * br


def _s5_prep(lam_re, lam_im, log_dt, b_re_t, b_im_t):
    gp = jax.ShapeDtypeStruct((S5_GROUPS, S5_STATE), f32)
    gcp = jax.ShapeDtypeStruct((S5_GROUPS, S5_GROUP, S5_STATE), f32)
    return pl.pallas_call(_s5_prep_body, out_shape=[gp, gp, gcp, gcp], name="l1_s5_prep")(
        lam_re, lam_im, log_dt, b_re_t, b_im_t)


def _s5_body(x_ref, h0r_ref, h0i_ref, g_ref, are_ref, aim_ref, wbr_ref, wbi_ref, vre_ref, vim_ref,
             d_ref, wglu_ref, gffn_ref, wup_ref, wdn_ref, gfin_ref,
             y_ref, sre_ref, sim_ref, sr_scr, si_scr, hr_scr, hi_scr, act_scr, *, bt, lt):
    rows = bt * lt

    @pl.when(pl.program_id(1) == 0)
    def _():
        sr_scr[...] = h0r_ref[...]
        si_scr[...] = h0i_ref[...]

    xt = pltpu.einshape("btd->tbd", x_ref[...]).reshape(rows, D_MODEL)
    u = _rms(xt, g_ref[...])
    ub = u.astype(bf16)

    for j in range(N_S5_BLOCKS):
        ch = slice(j * S5_BLOCK_CH, (j + 1) * S5_BLOCK_CH)
        st = slice(j * S5_BLOCK_ST, (j + 1) * S5_BLOCK_ST)
        hr_scr[...] = _dot(ub[:, ch], wbr_ref[j])
        hi_scr[...] = _dot(ub[:, ch], wbi_ref[j])
        a_re = jnp.broadcast_to(are_ref[:, st], (bt, S5_BLOCK_ST))
        a_im = jnp.broadcast_to(aim_ref[:, st], (bt, S5_BLOCK_ST))

        def step(t, carry):
            s_re, s_im = carry
            r = pl.ds(pl.multiple_of(t * bt, SUBLANES), bt)
            n_re = a_re * s_re - a_im * s_im + hr_scr[r, :]
            n_im = a_re * s_im + a_im * s_re + hi_scr[r, :]
            hr_scr[r, :] = n_re
            hi_scr[r, :] = n_im
            return n_re, n_im

        s_re, s_im = lax.fori_loop(0, lt, step, (sr_scr[:, st], si_scr[:, st]), unroll=min(lt, 8))
        sr_scr[:, st] = s_re
        si_scr[:, st] = s_im
        yj = (_dot(hr_scr[...].astype(bf16), vre_ref[j]) - _dot(hi_scr[...].astype(bf16), vim_ref[j])
              + d_ref[:, ch] * u[:, ch])
        act_scr[:, ch] = jax.nn.gelu(yj).astype(bf16)

    sre_ref[...] = sr_scr[...]
    sim_ref[...] = si_scr[...]
    ag = _dot(act_scr[...], wglu_ref[...])
    x3 = xt + ag[:, :D_MODEL] * jax.nn.sigmoid(ag[:, D_MODEL:])
    x4 = _ffn(x3, gffn_ref, wup_ref, wdn_ref)
    y = _rms(x4, gfin_ref[...])
    y_ref[...] = pltpu.einshape("tbd->btd", y.reshape(lt, bt, D_MODEL))


def _s5_layer(x3d, h0_re, h0_im, bt, lt, g_mix, a_re, a_im, wb_re, wb_im, v_re, v_im, d_skip,
              w_glu, g_ffn, w_up, w_down, g_final):
    n_batch, seq_len, _ = x3d.shape
    grid = (n_batch // bt, seq_len // lt)
    sspec = pl.BlockSpec((bt, S5_N), lambda i, t: (i, 0))
    xspec = pl.BlockSpec((bt, lt, D_MODEL), lambda i, t: (i, t, 0))
    consts = (g_mix, a_re, a_im, wb_re, wb_im, v_re, v_im, d_skip, w_glu, g_ffn, w_up, w_down, g_final)
    rows = bt * lt
    return pl.pallas_call(
        functools.partial(_s5_body, bt=bt, lt=lt),
        grid=grid,
        in_specs=[xspec, sspec, sspec] + [_const_spec(c.shape) for c in consts],
        out_specs=[xspec, sspec, sspec],
        out_shape=[jax.ShapeDtypeStruct(x3d.shape, f32), jax.ShapeDtypeStruct((n_batch, S5_N), f32),
                   jax.ShapeDtypeStruct((n_batch, S5_N), f32)],
        scratch_shapes=[pltpu.VMEM((bt, S5_N), f32), pltpu.VMEM((bt, S5_N), f32),
                        pltpu.VMEM((rows, S5_BLOCK_ST), f32), pltpu.VMEM((rows, S5_BLOCK_ST), f32),
                        pltpu.VMEM((rows, D_MODEL), bf16)],
        compiler_params=_params(2),
        name="l1_s5_ffn",
    )(x3d, h0_re, h0_im, *consts)


def _block_diag(w):
    g, a, b = w.shape
    nb = g // 8
    eye = jnp.eye(8, dtype=w.dtype)
    out = w.reshape(nb, 8, a, 1, b) * eye[None, :, None, :, None]
    return out.reshape(nb, 8 * a, 8 * b)


def _trunk(x, state, p):
    n_batch, seq_len, _ = x.shape
    x2d = x.reshape(n_batch * seq_len, D_MODEL)
    q, k, v, o, gates = _inproj(x2d, p["g_mix0"], p["w_qkvo"], p["w_gate"], p["gate_bias"])
    if state is None:
        hmix, c_new, n_new, m_new = _mlstm_core(q, k, v, gates, None, n_batch, seq_len)
        h0_re = jnp.zeros((n_batch, S5_N), f32)
        h0_im = h0_re
        bt, lt = n_batch, 512 // n_batch
    else:
        c0, n0, m0, h0_re, h0_im = state
        m_rows = jnp.repeat(m0.T, seq_len, axis=1)
        hmix, c_new, n_new, m_new = _mlstm_core(q, k, v, gates, (c0, n0.reshape(n_batch, QK_DIM), m_rows),
                                                n_batch, seq_len)
        h0_re = h0_re.reshape(n_batch, S5_N)
        h0_im = h0_im.reshape(n_batch, S5_N)
        bt, lt = 512 // seq_len, seq_len
    x2 = _outproj_ffn(hmix, o, x2d, p["head_g"], p["w_out"], p["g_ffn0"], p["w_up0"], p["w_down0"])
    y, s_re, s_im = _s5_layer(x2.reshape(n_batch, seq_len, D_MODEL), h0_re, h0_im, bt, lt,
                              p["g_mix1"], p["a_re"], p["a_im"], p["wb_re"], p["wb_im"], p["v_re"], p["v_im"],
                              p["d_skip"], p["w_glu"], p["g_ffn1"], p["w_up1"], p["w_down1"], p["g_final"])
    return (y,
            c_new.reshape(1, n_batch, N_HEADS, DK, DV),
            n_new.reshape(1, n_batch, N_HEADS, DK),
            m_new.reshape(n_batch, LANES)[:, :N_HEADS].reshape(1, n_batch, N_HEADS),
            s_re.reshape(1, n_batch, S5_GROUPS, S5_STATE),
            s_im.reshape(1, n_batch, S5_GROUPS, S5_STATE))


def kernel(x_prompt, x_sample, state_mlstm_C, state_mlstm_n, state_mlstm_m, state_s5_re, state_s5_im,
           norm_mix_g, norm_ffn_g, norm_final_g, mlstm_w_in, mlstm_b_i, mlstm_b_f, mlstm_head_norm_g,
           mlstm_w_out, s5_lambda_re, s5_lambda_im, s5_log_dt, s5_b_re, s5_b_im, s5_c_re, s5_c_im,
           s5_d, s5_w_glu, ffn_w_up, ffn_w_down):
    w_in = mlstm_w_in[0]
    n_qkvo = 2 * QK_DIM + 2 * V_DIM
    a_re, a_im, bb_re, bb_im = _s5_prep(
        s5_lambda_re[0], s5_lambda_im[0], s5_log_dt[0].reshape(S5_GROUPS, 1),
        jnp.swapaxes(s5_b_re[0], 1, 2), jnp.swapaxes(s5_b_im[0], 1, 2))
    p = dict(
        g_mix0=norm_mix_g[0].reshape(1, D_MODEL), g_mix1=norm_mix_g[1].reshape(1, D_MODEL),
        g_ffn0=norm_ffn_g[0].reshape(1, D_MODEL), g_ffn1=norm_ffn_g[1].reshape(1, D_MODEL),
        g_final=norm_final_g.reshape(1, D_MODEL),
        w_qkvo=w_in[:, :n_qkvo].astype(bf16),
        w_gate=jnp.pad(w_in[:, n_qkvo:], ((0, 0), (0, LANES - 2 * N_HEADS))).astype(bf16),
        gate_bias=jnp.pad(jnp.concatenate([mlstm_b_i[0], mlstm_b_f[0]]), (0, LANES - 2 * N_HEADS)).reshape(1, LANES),
        head_g=mlstm_head_norm_g[0].reshape(1, V_DIM),
        w_out=mlstm_w_out[0].astype(bf16),
        w_up0=ffn_w_up[0].astype(bf16), w_down0=ffn_w_down[0].astype(bf16),
        w_up1=ffn_w_up[1].astype(bf16), w_down1=ffn_w_down[1].astype(bf16),
        a_re=a_re.reshape(1, S5_N), a_im=a_im.reshape(1, S5_N),
        wb_re=_block_diag(bb_re).astype(bf16), wb_im=_block_diag(bb_im).astype(bf16),
        v_re=_block_diag(jnp.swapaxes(s5_c_re[0], 1, 2)).astype(bf16),
        v_im=_block_diag(jnp.swapaxes(s5_c_im[0], 1, 2)).astype(bf16),
        d_skip=s5_d[0].reshape(1, D_MODEL),
        w_glu=s5_w_glu[0].astype(bf16),
    )
    prompt = _trunk(x_prompt, None, p)
    sample = _trunk(x_sample, (state_mlstm_C[0], state_mlstm_n[0], state_mlstm_m[0],
                               state_s5_re[0], state_s5_im[0]), p)
    return (prompt[0], sample[0]) + prompt[1:] + sample[1:]
```

```python
import functools

import jax
import jax.numpy as jnp
import numpy as np
from jax import lax
from jax.experimental import pallas as pl
from jax.experimental.pallas import tpu as pltpu

f32 = jnp.float32
bf16 = jnp.bfloat16

D_MODEL = 1024
N_HEADS = 8
DK = 64
DV = 128
QK_DIM = N_HEADS * DK
V_DIM = N_HEADS * DV
D_FF = 4 * D_MODEL
S5_GROUPS = 64
S5_GROUP = 16
S5_STATE = 64
S5_N = S5_GROUPS * S5_STATE
EPS = 1e-6

LANES = 128
SUBLANES = 8
ROW_TILE = 512
CORE_ROWS = 256
FF_CHUNK = 1024
S5_BLOCK_CH = LANES
S5_BLOCK_ST = S5_BLOCK_CH // S5_GROUP * S5_STATE
N_S5_BLOCKS = D_MODEL // S5_BLOCK_CH
VMEM_LIMIT_BYTES = 56 * 1024 * 1024
NEG_BIG = -1e30


def _params(n_axes):
    return pltpu.CompilerParams(dimension_semantics=("arbitrary",) * n_axes,
                                vmem_limit_bytes=VMEM_LIMIT_BYTES)


def _const_spec(shape):
    nd = len(shape)
    return pl.BlockSpec(shape, lambda *_: (0,) * nd, pipeline_mode=pl.Buffered(1))


def _dot(a, b):
    return jnp.dot(a, b, preferred_element_type=f32)


def _rms(x, g):
    return x * lax.rsqrt(jnp.mean(x * x, axis=-1, keepdims=True) + EPS) * g


def _ffn(x1, g_ref, wup_ref, wdn_ref):
    xn = _rms(x1, g_ref[...]).astype(bf16)
    acc = None
    for c in range(D_FF // FF_CHUNK):
        cols = slice(c * FF_CHUNK, (c + 1) * FF_CHUNK)
        hid = _dot(xn, wup_ref[:, cols])
        hid = jnp.square(jnp.maximum(hid, 0.0)).astype(bf16)
        part = _dot(hid, wdn_ref[cols, :])
        acc = part if acc is None else acc + part
    return x1 + acc


def _inproj_body(x_ref, g_ref, w_ref, wg_ref, bias_ref, q_ref, k_ref, v_ref, o_ref, gate_ref):
    xn = _rms(x_ref[...], g_ref[...]).astype(bf16)
    q_ref[...] = _dot(xn, w_ref[:, 0:QK_DIM]).astype(bf16)
    k_ref[...] = _dot(xn, w_ref[:, QK_DIM:2 * QK_DIM]) * (DK ** -0.5)
    v_ref[...] = _dot(xn, w_ref[:, 2 * QK_DIM:2 * QK_DIM + V_DIM]).astype(bf16)
    o_ref[...] = _dot(xn, w_ref[:, 2 * QK_DIM + V_DIM:2 * QK_DIM + 2 * V_DIM])
    gz = _dot(xn, wg_ref[...]) + bias_ref[...]
    gate_ref[:, :LANES] = gz[:, :LANES]
    gate_ref[:, LANES:] = jax.nn.log_sigmoid(gz[:, LANES:])


def _inproj(x2d, g, w_qkvo, w_gate, gate_bias):
    t = x2d.shape[0]
    row = lambda n: pl.BlockSpec((ROW_TILE, n), lambda i: (i, 0))
    return pl.pallas_call(
        _inproj_body,
        grid=(t // ROW_TILE,),
        in_specs=[row(D_MODEL), _const_spec((1, D_MODEL)), _const_spec(w_qkvo.shape),
                  _const_spec(w_gate.shape), _const_spec((1, 2 * LANES))],
        out_specs=[row(QK_DIM), row(QK_DIM), row(V_DIM), row(V_DIM), row(2 * LANES)],
        out_shape=[jax.ShapeDtypeStruct((t, QK_DIM), bf16), jax.ShapeDtypeStruct((t, QK_DIM), f32),
                   jax.ShapeDtypeStruct((t, V_DIM), bf16), jax.ShapeDtypeStruct((t, V_DIM), f32),
                   jax.ShapeDtypeStruct((t, 2 * LANES), f32)],
        compiler_params=_params(1),
        name="l0_inproj",
    )(x2d, g, w_qkvo, w_gate, gate_bias)


def _row_prefix(x, cs, rpos, is_max):
    sh = 1
    while sh < cs:
        prev = pltpu.roll(x, sh, 0)
        if is_max:
            x = jnp.maximum(x, jnp.where(rpos >= sh, prev, NEG_BIG))
        else:
            x = x + jnp.where(rpos >= sh, prev, 0.0)
        sh *= 2
    return x


def _seg_last(x, nseq, cs):
    if nseq == 1:
        return x[cs - 1:cs, :]
    last = x.reshape(nseq, cs, x.shape[1])[:, cs - 1:cs, :]
    return jnp.broadcast_to(last, (nseq, cs, x.shape[1])).reshape(nseq * cs, x.shape[1])


def _mlstm_body(*refs, nseq, cs, has_state):
    rows = nseq * cs
    log_cs = cs.bit_length() - 1
    if has_state:
        (q_ref, k_ref, v_ref, gate_ref, sel_ref, c0_ref, n0_ref, m0_ref,
         h_ref, cout_ref, nout_ref, mout_ref) = refs
        m_in = jnp.broadcast_to(m0_ref[...][:, None, :], (nseq, cs, LANES)).reshape(rows, LANES)
    else:
        (q_ref, k_ref, v_ref, gate_ref, sel_ref,
         h_ref, cout_ref, nout_ref, mout_ref, c_scr, n_scr, m_scr) = refs
        t_idx = pl.program_id(1)

        @pl.when(t_idx == 0)
        def _():
            c_scr[...] = jnp.zeros_like(c_scr)
            n_scr[...] = jnp.zeros_like(n_scr)
            m_scr[...] = jnp.zeros_like(m_scr)

        m_in = m_scr[...]

    logi = gate_ref[:, :LANES]
    logf = gate_ref[:, LANES:]
    rpos = lax.broadcasted_iota(jnp.int32, (rows, LANES), 0) & (cs - 1)
    b = _row_prefix(logf, cs, rpos, False)
    a = logi - b
    g = jnp.maximum(m_in, _row_prefix(a, cs, rpos, True))
    g_last = _seg_last(g, nseq, cs)
    m_new = _seg_last(b, nseq, cs) + g_last
    decay_all = jnp.exp(m_in - g_last)
    a_t = a.T
    ws_all = jnp.exp(a - g_last)
    ws_t = ws_all.T

    group = lax.broadcasted_iota(jnp.int32, (rows, LANES), 1) >> 3
    pieces = jnp.zeros((rows, LANES), f32)
    for i, val in enumerate((g, jnp.exp(m_in - g), jnp.exp(-(b + g)))):
        hi = val.astype(bf16).astype(f32)
        mid = (val - hi).astype(bf16).astype(f32)
        lo = (val - hi - mid).astype(bf16).astype(f32)
        for j, piece in enumerate((hi, mid, lo)):
            pieces = jnp.where(group == 3 * i + j, piece, pieces)
    bcast = _dot(pieces.astype(bf16), sel_ref[...])
    g_b = lambda h: bcast[:, h * DV:(h + 1) * DV]
    w_inter_b = lambda h: bcast[:, (N_HEADS + h) * DV:(N_HEADS + h + 1) * DV]
    clamp_b = lambda h: bcast[:, (2 * N_HEADS + h) * DV:(2 * N_HEADS + h + 1) * DV]

    ri = lax.broadcasted_iota(jnp.int32, (rows, rows), 0)
    ci = lax.broadcasted_iota(jnp.int32, (rows, rows), 1)
    causal = ci <= ri
    if has_state:
        log_dk = DK.bit_length() - 1
        causal = jnp.logical_and(causal, (ri >> log_cs) == (ci >> log_cs))
        bd_q = ((lax.broadcasted_iota(jnp.int32, (rows, nseq * DK), 0) >> log_cs)
                == (lax.broadcasted_iota(jnp.int32, (rows, nseq * DK), 1) >> log_dk))
        bd_k = ((lax.broadcasted_iota(jnp.int32, (nseq * DK, rows), 0) >> log_dk)
                == (lax.broadcasted_iota(jnp.int32, (nseq * DK, rows), 1) >> log_cs))
    k_t = k_ref[...].T

    heads = range(N_HEADS)
    q_h = [q_ref[:, h * DK:(h + 1) * DK] for h in heads]
    k_h = [k_ref[:, h * DK:(h + 1) * DK] for h in heads]
    v_h = [v_ref[:, h * DV:(h + 1) * DV] for h in heads]

    s_bf, s_sum = [], []
    for h in heads:
        g_rows = jnp.concatenate([g_b(h)] * (rows // DV), axis=1)
        w = jnp.exp(jnp.where(causal, a_t[h:h + 1, :] - g_rows, NEG_BIG))
        qk = lax.dot_general(q_h[h], k_h[h].astype(bf16), (((1,), (1,)), ((), ())), preferred_element_type=f32)
        s = qk * w
        s_sum.append(jnp.sum(s, axis=1, keepdims=True))
        s_bf.append(s.astype(bf16))

    intra, inter, qn, c_prev, n_prev = [], [], [], [], []
    for h in heads:
        intra.append(_dot(s_bf[h], v_h[h]))
        if has_state:
            c_prev.append(c0_ref[:, h].reshape(nseq * DK, DV))
            q32 = q_h[h].astype(f32)
            q_bd = jnp.where(bd_q, jnp.concatenate([q32] * nseq, axis=1), 0.0).astype(bf16)
            inter.append(_dot(q_bd, c_prev[h].astype(bf16)))
            n_prev.append(n0_ref[:, h * DK:(h + 1) * DK])
            n_rows = jnp.broadcast_to(n_prev[h][:, None, :], (nseq, cs, DK)).reshape(rows, DK)
            qn.append(jnp.sum(q32 * n_rows, axis=1, keepdims=True))
        else:
            c_prev.append(c_scr[h])
            n_prev.append(n_scr[h])
            rhs = jnp.concatenate([c_prev[h].astype(bf16),
                                   jnp.broadcast_to(n_prev[h], (DK, DV)).astype(bf16)], axis=1)
            inter2 = _dot(q_h[h], rhs)
            inter.append(inter2[:, :DV])
            qn.append(inter2[:, DV:])

    for h in heads:
        num = w_inter_b(h) * inter[h] + intra[h]
        den = jnp.maximum(jnp.abs(w_inter_b(h) * qn[h] + s_sum[h]), clamp_b(h))
        h_ref[:, h * DV:(h + 1) * DV] = num * lax.rsqrt(
            jnp.mean(num * num, axis=1, keepdims=True) + EPS * (den * den))

    for h in heads:
        wk_t = k_t[h * DK:(h + 1) * DK, :] * ws_t[h:h + 1, :]
        if has_state:
            wk_bd = jnp.where(bd_k, jnp.broadcast_to(wk_t[None], (nseq, DK, rows)).reshape(nseq * DK, rows), 0.0)
            dc = _dot(wk_bd.astype(bf16), v_h[h])
            dec_col = decay_all[:, h:h + 1]
            dec_rows = jnp.broadcast_to(dec_col.reshape(nseq, 1, cs, 1), (nseq, DK // cs, cs, 1)).reshape(nseq * DK, 1)
            cout_ref[:, h] = (dec_rows * c_prev[h] + dc).reshape(nseq, DK, DV)
            dec_seq = jnp.max(dec_col.reshape(nseq, cs, 1), axis=1)
            nout_ref[:, h * DK:(h + 1) * DK] = dec_seq * n_prev[h] + jnp.sum(
                (ws_all[:, h:h + 1] * k_h[h]).reshape(nseq, cs, DK), axis=1)
        else:
            dec = decay_all[:, h:h + 1]
            c_scr[h] = dec * c_prev[h] + _dot(wk_t.astype(bf16), v_h[h])
            n_scr[h] = dec * n_prev[h] + jnp.sum(wk_t, axis=1, keepdims=True)

    if has_state:
        mout_ref[...] = jnp.max(m_new.reshape(nseq, cs, LANES), axis=1)
    else:
        m_scr[...] = m_new

        @pl.when(t_idx == pl.num_programs(1) - 1)
        def _():
            cout_ref[0] = c_scr[...]
            lane = lax.broadcasted_iota(jnp.int32, (DK, LANES), 1)
            n_mat = jnp.zeros((DK, LANES), f32)
            for h in range(N_HEADS):
                n_mat = jnp.where(lane == h, jnp.broadcast_to(n_scr[h], (DK, LANES)), n_mat)
            nout_ref[0] = n_mat
            mout_ref[0] = m_new


def _bcast_selector():
    src = np.arange(LANES)
    dst = np.arange(3 * N_HEADS * DV) // DV
    hit = ((src[:, None] >> 3) // 3 == dst[None, :] // N_HEADS) & ((src[:, None] & 7) == dst[None, :] % N_HEADS)
    return jnp.asarray(hit, dtype=bf16)


def _mlstm_core(q, k, v, gates, state, n_batch, seq_len):
    t = q.shape[0]
    sel = _bcast_selector()
    if state is None:
        nt = seq_len // CORE_ROWS
        grid = (n_batch, nt)
        row = lambda n: pl.BlockSpec((CORE_ROWS, n), lambda b, i: (b * nt + i, 0))
        in_specs = [row(QK_DIM), row(QK_DIM), row(V_DIM), row(2 * LANES), _const_spec(sel.shape)]
        out_specs = [row(V_DIM),
                     pl.BlockSpec((1, N_HEADS, DK, DV), lambda b, i: (b, 0, 0, 0)),
                     pl.BlockSpec((1, DK, LANES), lambda b, i: (b, 0, 0)),
                     pl.BlockSpec((1, 1, LANES), lambda b, i: (b, 0, 0))]
        out_shape = [jax.ShapeDtypeStruct((t, V_DIM), f32),
                     jax.ShapeDtypeStruct((n_batch, N_HEADS, DK, DV), f32),
                     jax.ShapeDtypeStruct((n_batch, DK, LANES), f32),
                     jax.ShapeDtypeStruct((n_batch, 1, LANES), f32)]
        scratch = [pltpu.VMEM((N_HEADS, DK, DV), f32), pltpu.VMEM((N_HEADS, DK, 1), f32),
                   pltpu.VMEM((1, LANES), f32)]
        body = functools.partial(_mlstm_body, nseq=1, cs=CORE_ROWS, has_state=False)
        args = (q, k, v, gates, sel)
    else:
        c0, n0, m0 = state
        nseq = CORE_ROWS // seq_len
        grid = (t // CORE_ROWS, 1)
        row = lambda n: pl.BlockSpec((CORE_ROWS, n), lambda i, _: (i, 0))
        cspec = pl.BlockSpec((nseq, N_HEADS, DK, DV), lambda i, _: (i, 0, 0, 0))
        nspec = pl.BlockSpec((nseq, QK_DIM), lambda i, _: (i, 0))
        mspec = pl.BlockSpec((nseq, LANES), lambda i, _: (i, 0))
        in_specs = [row(QK_DIM), row(QK_DIM), row(V_DIM), row(2 * LANES), _const_spec(sel.shape),
                    cspec, nspec, mspec]
        out_specs = [row(V_DIM), cspec, nspec, mspec]
        out_shape = [jax.ShapeDtypeStruct((t, V_DIM), f32),
                     jax.ShapeDtypeStruct((n_batch, N_HEADS, DK, DV), f32),
                     jax.ShapeDtypeStruct((n_batch, QK_DIM), f32),
                     jax.ShapeDtypeStruct((n_batch, LANES), f32)]
        scratch = []
        body = functools.partial(_mlstm_body, nseq=nseq, cs=seq_len, has_state=True)
        args = (q, k, v, gates, sel, c0, n0, m0)
    return pl.pallas_call(
        body, grid=grid, in_specs=in_specs, out_specs=out_specs, out_shape=out_shape,
        scratch_shapes=scratch, compiler_params=_params(2), name="l0_mlstm_core",
    )(*args)


def _outproj_ffn_body(h_ref, o_ref, x_ref, hg_ref, wout_ref, gffn_ref, wup_ref, wdn_ref, y_ref):
    hn = h_ref[...] * hg_ref[...] * jax.nn.sigmoid(o_ref[...])
    x1 = x_ref[...] + _dot(hn.astype(bf16), wout_ref[...])
    y_ref[...] = _ffn(x1, gffn_ref, wup_ref, wdn_ref)


def _outproj_ffn(hmix, o, x2d, head_g, w_out, g_ffn, w_up, w_down):
    t = x2d.shape[0]
    row = pl.BlockSpec((ROW_TILE, D_MODEL), lambda i: (i, 0))
    return pl.pallas_call(
        _outproj_ffn_body,
        grid=(t // ROW_TILE,),
        in_specs=[row, row, row, _const_spec((1, V_DIM)), _const_spec(w_out.shape),
                  _const_spec((1, D_MODEL)), _const_spec(w_up.shape), _const_spec(w_down.shape)],
        out_specs=row,
        out_shape=jax.ShapeDtypeStruct((t, D_MODEL), f32),
        compiler_params=_params(1),
        name="l0_outproj_ffn",
    )(hmix, o, x2d, head_g, w_out, g_ffn, w_up, w_down)


def _s5_prep_body(lre_ref, lim_ref, ldt_ref, bre_ref, bim_ref, are_ref, aim_ref, bbre_ref, bbim_ref):
    lr = lre_ref[...]
    li = lim_ref[...]
    dt = jnp.exp(ldt_ref[...])
    mag = jnp.exp(lr * dt)
    a_re = mag * jnp.cos(li * dt)
    a_im = mag * jnp.sin(li * dt)
    den = lr * lr + li * li
    z_re = a_re - 1.0
    coef_re = ((z_re * lr + a_im * li) / den)[:, None, :]
    coef_im = ((a_im * lr - z_re * li) / den)[:, None, :]
    br = bre_ref[...]
    bi = bim_ref[...]
    are_ref[...] = a_re
    aim_ref[...] = a_im
    bbre_ref[...] = coef_re * br - coef_im * bi
    bbim_ref[...] = coef_re * bi + coef_im * br


def _s5_prep(lam_re, lam_im, log_dt, b_re_t, b_im_t):
    gp = jax.ShapeDtypeStruct((S5_GROUPS, S5_STATE), f32)
    gcp = jax.ShapeDtypeStruct((S5_GROUPS, S5_GROUP, S5_STATE), f32)
    return pl.pallas_call(_s5_prep_body, out_shape=[gp, gp, gcp, gcp], name="l1_s5_prep")(
        lam_re, lam_im, log_dt, b_re_t, b_im_t)


def _s5_body(x_ref, h0r_ref, h0i_ref, g_ref, are_ref, aim_ref, wbr_ref, wbi_ref, vre_ref, vim_ref,
             d_ref, wglu_ref, gffn_ref, wup_ref, wdn_ref, gfin_ref,
             y_ref, sre_ref, sim_ref, sr_scr, si_scr, hr_scr, hi_scr, act_scr, *, bt, lt):
    rows = bt * lt

    @pl.when(pl.program_id(1) == 0)
    def _():
        sr_scr[...] = h0r_ref[...]
        si_scr[...] = h0i_ref[...]

    xt = jnp.concatenate([x_ref[:, t, :] for t in range(lt)], axis=0)
    u = _rms(xt, g_ref[...])
    ub = u.astype(bf16)

    def project_in(j):
        ch = slice(j * S5_BLOCK_CH, (j + 1) * S5_BLOCK_CH)
        hr_scr[j % 2] = _dot(ub[:, ch], wbr_ref[j])
        hi_scr[j % 2] = _dot(ub[:, ch], wbi_ref[j])

    project_in(0)
    for j in range(N_S5_BLOCKS):
        ch = slice(j * S5_BLOCK_CH, (j + 1) * S5_BLOCK_CH)
        st = slice(j * S5_BLOCK_ST, (j + 1) * S5_BLOCK_ST)
        if j + 1 < N_S5_BLOCKS:
            project_in(j + 1)
        a_re = jnp.broadcast_to(are_ref[:, st], (bt, S5_BLOCK_ST))
        a_im = jnp.broadcast_to(aim_ref[:, st], (bt, S5_BLOCK_ST))
        s_re = sr_scr[:, st]
        s_im = si_scr[:, st]
        for t in range(lt):
            r = slice(t * bt, (t + 1) * bt)
            n_re = a_re * s_re - a_im * s_im + hr_scr[j % 2, r, :]
            n_im = a_re * s_im + a_im * s_re + hi_scr[j % 2, r, :]
            hr_scr[j % 2, r, :] = n_re
            hi_scr[j % 2, r, :] = n_im
            s_re, s_im = n_re, n_im
        sr_scr[:, st] = s_re
        si_scr[:, st] = s_im
        yj = (_dot(hr_scr[j % 2].astype(bf16), vre_ref[j]) - _dot(hi_scr[j % 2].astype(bf16), vim_ref[j])
              + d_ref[:, ch] * u[:, ch])
        act_scr[:, ch] = jax.nn.gelu(yj).astype(bf16)

    sre_ref[...] = sr_scr[...]
    sim_ref[...] = si_scr[...]
    ag = _dot(act_scr[...], wglu_ref[...])
    x3 = xt + ag[:, :D_MODEL] * jax.nn.sigmoid(ag[:, D_MODEL:])
    x4 = _ffn(x3, gffn_ref, wup_ref, wdn_ref)
    y = _rms(x4, gfin_ref[...])
    for t in range(lt):
        y_ref[:, t, :] = y[t * bt:(t + 1) * bt, :]


def _s5_layer(x3d, h0_re, h0_im, bt, lt, g_mix, a_re, a_im, wb_re, wb_im, v_re, v_im, d_skip,
              w_glu, g_ffn, w_up, w_down, g_final):
    n_batch, seq_len, _ = x3d.shape
    grid = (n_batch // bt, seq_len // lt)
    sspec = pl.BlockSpec((bt, S5_N), lambda i, t: (i, 0))
    xspec = pl.BlockSpec((bt, lt, D_MODEL), lambda i, t: (i, t, 0))
    consts = (g_mix, a_re, a_im, wb_re, wb_im, v_re, v_im, d_skip, w_glu, g_ffn, w_up, w_down, g_final)
    rows = bt * lt
    return pl.pallas_call(
        functools.partial(_s5_body, bt=bt, lt=lt),
        grid=grid,
        in_specs=[xspec, sspec, sspec] + [_const_spec(c.shape) for c in consts],
        out_specs=[xspec, sspec, sspec],
        out_shape=[jax.ShapeDtypeStruct(x3d.shape, f32), jax.ShapeDtypeStruct((n_batch, S5_N), f32),
                   jax.ShapeDtypeStruct((n_batch, S5_N), f32)],
        scratch_shapes=[pltpu.VMEM((bt, S5_N), f32), pltpu.VMEM((bt, S5_N), f32),
                        pltpu.VMEM((2, rows, S5_BLOCK_ST), f32), pltpu.VMEM((2, rows, S5_BLOCK_ST), f32),
                        pltpu.VMEM((rows, D_MODEL), bf16)],
        compiler_params=_params(2),
        name="l1_s5_ffn",
    )(x3d, h0_re, h0_im, *consts)


def _block_diag(w):
    g, a, b = w.shape
    nb = g // 8
    eye = jnp.eye(8, dtype=w.dtype)
    out = w.reshape(nb, 8, a, 1, b) * eye[None, :, None, :, None]
    return out.reshape(nb, 8 * a, 8 * b)


def _rep_lanes(x):
    return jnp.tile(x, (1,) * (x.ndim - 1) + (LANES // x.shape[-1],))


def _trunk(x, state, p):
    n_batch, seq_len, _ = x.shape
    x2d = x.reshape(n_batch * seq_len, D_MODEL)
    q, k, v, o, gates = _inproj(x2d, p["g_mix0"], p["w_qkvo"], p["w_gate"], p["gate_bias"])
    if state is None:
        hmix, c_new, n_new, m_new = _mlstm_core(q, k, v, gates, None, n_batch, seq_len)
        n_new = jnp.swapaxes(n_new[:, :, :N_HEADS], 1, 2)
        h0_re = jnp.zeros((n_batch, S5_N), f32)
        h0_im = h0_re
        bt, lt = n_batch, 512 // n_batch
    else:
        c0, n0, m0, h0_re, h0_im = state
        hmix, c_new, n_new, m_new = _mlstm_core(
            q, k, v, gates, (c0, n0.reshape(n_batch, QK_DIM), _rep_lanes(m0)), n_batch, seq_len)
        h0_re = h0_re.reshape(n_batch, S5_N)
        h0_im = h0_im.reshape(n_batch, S5_N)
        bt, lt = 512 // seq_len, seq_len
    x2 = _outproj_ffn(hmix, o, x2d, p["head_g"], p["w_out"], p["g_ffn0"], p["w_up0"], p["w_down0"])
    y, s_re, s_im = _s5_layer(x2.reshape(n_batch, seq_len, D_MODEL), h0_re, h0_im, bt, lt,
                              p["g_mix1"], p["a_re"], p["a_im"], p["wb_re"], p["wb_im"], p["v_re"], p["v_im"],
                              p["d_skip"], p["w_glu"], p["g_ffn1"], p["w_up1"], p["w_down1"], p["g_final"])
    return (y,
            c_new.reshape(1, n_batch, N_HEADS, DK, DV),
            n_new.reshape(1, n_batch, N_HEADS, DK),
            m_new.reshape(n_batch, LANES)[:, :N_HEADS].reshape(1, n_batch, N_HEADS),
            s_re.reshape(1, n_batch, S5_GROUPS, S5_STATE),
            s_im.reshape(1, n_batch, S5_GROUPS, S5_STATE))


def kernel(x_prompt, x_sample, state_mlstm_C, state_mlstm_n, state_mlstm_m, state_s5_re, state_s5_im,
           norm_mix_g, norm_ffn_g, norm_final_g, mlstm_w_in, mlstm_b_i, mlstm_b_f, mlstm_head_norm_g,
           mlstm_w_out, s5_lambda_re, s5_lambda_im, s5_log_dt, s5_b_re, s5_b_im, s5_c_re, s5_c_im,
           s5_d, s5_w_glu, ffn_w_up, ffn_w_down):
    w_in = mlstm_w_in[0]
    n_qkvo = 2 * QK_DIM + 2 * V_DIM
    a_re, a_im, bb_re, bb_im = _s5_prep(
        s5_lambda_re[0], s5_lambda_im[0], s5_log_dt[0].reshape(S5_GROUPS, 1),
        jnp.swapaxes(s5_b_re[0], 1, 2), jnp.swapaxes(s5_b_im[0], 1, 2))
    p = dict(
        g_mix0=norm_mix_g[0].reshape(1, D_MODEL), g_mix1=norm_mix_g[1].reshape(1, D_MODEL),
        g_ffn0=norm_ffn_g[0].reshape(1, D_MODEL), g_ffn1=norm_ffn_g[1].reshape(1, D_MODEL),
        g_final=norm_final_g.reshape(1, D_MODEL),
        w_qkvo=w_in[:, :n_qkvo].astype(bf16),
        w_gate=jnp.concatenate([_rep_lanes(w_in[:, n_qkvo:n_qkvo + N_HEADS]),
                                _rep_lanes(w_in[:, n_qkvo + N_HEADS:])], axis=1).astype(bf16),
        gate_bias=jnp.concatenate([_rep_lanes(mlstm_b_i[0][None]), _rep_lanes(mlstm_b_f[0][None])], axis=1),
        head_g=mlstm_head_norm_g[0].reshape(1, V_DIM),
        w_out=mlstm_w_out[0].astype(bf16),
        w_up0=ffn_w_up[0].astype(bf16), w_down0=ffn_w_down[0].astype(bf16),
        w_up1=ffn_w_up[1].astype(bf16), w_down1=ffn_w_down[1].astype(bf16),
        a_re=a_re.reshape(1, S5_N), a_im=a_im.reshape(1, S5_N),
        wb_re=_block_diag(bb_re).astype(bf16), wb_im=_block_diag(bb_im).astype(bf16),
        v_re=_block_diag(jnp.swapaxes(s5_c_re[0], 1, 2)).astype(bf16),
        v_im=_block_diag(jnp.swapaxes(s5_c_im[0], 1, 2)).astype(bf16),
        d_skip=s5_d[0].reshape(1, D_MODEL),
        w_glu=s5_w_glu[0].astype(bf16),
    )
    prompt = _trunk(x_prompt, None, p)
    sample = _trunk(x_sample, (state_mlstm_C[0], state_mlstm_n[0], state_mlstm_m[0],
                               state_s5_re[0], state_s5_im[0]), p)
    return (prompt[0], sample[0]) + prompt[1:] + sample[1:]
```

```python
import functools
import types

import jax
import jax.numpy as jnp
import numpy as np
from jax import lax
from jax.experimental import pallas as pl
from jax.experimental.pallas import tpu as pltpu

f32 = jnp.float32
bf16 = jnp.bfloat16

D_MODEL = 1024
N_HEADS = 8
DK = 64
DV = 128
QK_DIM = N_HEADS * DK
V_DIM = N_HEADS * DV
D_FF = 4 * D_MODEL
S5_GROUPS = 64
S5_GROUP = 16
S5_STATE = 64
S5_N = S5_GROUPS * S5_STATE
EPS = 1e-6

LANES = 128
SUBLANES = 8
ROW_TILE = 512
CHUNK_ROWS = 256
PROMPT_CHUNKS = 2
FF_CHUNK = 1024
S5_BLOCK_CH = LANES
S5_BLOCK_ST = S5_BLOCK_CH // S5_GROUP * S5_STATE
N_S5_BLOCKS = D_MODEL // S5_BLOCK_CH
VMEM_LIMIT_BYTES = 56 * 1024 * 1024
NEG_BIG = -1e30
LOG_HEADS = N_HEADS.bit_length() - 1
N_BCAST = 3


def _params(n_axes):
    return pltpu.CompilerParams(dimension_semantics=("arbitrary",) * n_axes,
                                vmem_limit_bytes=VMEM_LIMIT_BYTES)


def _const_spec(shape):
    nd = len(shape)
    return pl.BlockSpec(shape, lambda *_: (0,) * nd, pipeline_mode=pl.Buffered(1))


def _layer_spec(shape, layer):
    nd = len(shape)
    return pl.BlockSpec((None,) + tuple(shape[1:]), lambda *_: (layer,) + (0,) * (nd - 1),
                        pipeline_mode=pl.Buffered(1))


def _dot(a, b):
    return jnp.dot(a, b, preferred_element_type=f32)


def _dot_nt(a, b):
    return lax.dot_general(a, b, (((1,), (1,)), ((), ())), preferred_element_type=f32)


def _rms(x, g):
    return x * lax.rsqrt(jnp.mean(x * x, axis=-1, keepdims=True) + EPS) * g


def _ffn(x1, g_ref, wup_ref, wdn_ref):
    xn = _rms(x1, g_ref[...]).astype(bf16)
    acc = None
    for c in range(D_FF // FF_CHUNK):
        cols = slice(c * FF_CHUNK, (c + 1) * FF_CHUNK)
        hid = _dot(xn, wup_ref[:, cols])
        hid = jnp.square(jnp.maximum(hid, 0.0)).astype(bf16)
        part = _dot(hid, wdn_ref[cols, :])
        acc = part if acc is None else acc + part
    return x1 + acc


def _inproj_body(x_ref, g_ref, w_ref, wg_ref, bias_ref, q_ref, k_ref, v_ref, o_ref, gate_ref):
    xn = _rms(x_ref[...], g_ref[...]).astype(bf16)
    q_ref[...] = _dot(xn, w_ref[:, 0:QK_DIM]).astype(bf16)
    k_ref[...] = _dot(xn, w_ref[:, QK_DIM:2 * QK_DIM]) * (DK ** -0.5)
    v_ref[...] = _dot(xn, w_ref[:, 2 * QK_DIM:2 * QK_DIM + V_DIM]).astype(bf16)
    o_ref[...] = _dot(xn, w_ref[:, 2 * QK_DIM + V_DIM:2 * QK_DIM + 2 * V_DIM])
    gz = _dot(xn, wg_ref[...]) + bias_ref[...]
    gate_ref[:, :LANES] = gz[:, :LANES]
    gate_ref[:, LANES:] = jax.nn.log_sigmoid(gz[:, LANES:])


def _inproj(x2d, g, w_qkvo, w_gate, gate_bias):
    t = x2d.shape[0]
    row = lambda n: pl.BlockSpec((ROW_TILE, n), lambda i: (i, 0))
    return pl.pallas_call(
        _inproj_body,
        grid=(t // ROW_TILE,),
        in_specs=[row(D_MODEL), _const_spec((1, D_MODEL)), _const_spec(w_qkvo.shape),
                  _const_spec(w_gate.shape), _const_spec((1, 2 * LANES))],
        out_specs=[row(QK_DIM), row(QK_DIM), row(V_DIM), row(V_DIM), row(2 * LANES)],
        out_shape=[jax.ShapeDtypeStruct((t, QK_DIM), bf16), jax.ShapeDtypeStruct((t, QK_DIM), f32),
                   jax.ShapeDtypeStruct((t, V_DIM), bf16), jax.ShapeDtypeStruct((t, V_DIM), f32),
                   jax.ShapeDtypeStruct((t, 2 * LANES), f32)],
        compiler_params=_params(1),
        name="l0_inproj",
    )(x2d, g, w_qkvo, w_gate, gate_bias)


def _row_prefix(x, cs, rpos, is_max):
    sh = 1
    while sh < cs:
        prev = pltpu.roll(x, sh, 0)
        if is_max:
            x = jnp.maximum(x, jnp.where(rpos >= sh, prev, NEG_BIG))
        else:
            x = x + jnp.where(rpos >= sh, prev, 0.0)
        sh *= 2
    return x


def _seg_last(x, nseq, cs):
    if nseq == 1:
        return x[cs - 1:cs, :]
    last = x.reshape(nseq, cs, x.shape[1])[:, cs - 1:cs, :]
    return jnp.broadcast_to(last, (nseq, cs, x.shape[1])).reshape(nseq * cs, x.shape[1])


def _gate_algebra(logi, logf, m_in, nseq, cs, sel_ref):
    rows = nseq * cs
    rpos = lax.broadcasted_iota(jnp.int32, (rows, LANES), 0) & (cs - 1)
    b = _row_prefix(logf, cs, rpos, False)
    a = logi - b
    g = jnp.maximum(m_in, _row_prefix(a, cs, rpos, True))
    g_last = _seg_last(g, nseq, cs)
    w_state = jnp.exp(a - g_last)

    group = lax.broadcasted_iota(jnp.int32, (rows, LANES), 1) >> LOG_HEADS
    pieces = jnp.zeros((rows, LANES), f32)
    for i, val in enumerate((g, jnp.exp(m_in - g), jnp.exp(-(b + g)))):
        hi = val.astype(bf16).astype(f32)
        mid = (val - hi).astype(bf16).astype(f32)
        lo = (val - hi - mid).astype(bf16).astype(f32)
        for j, piece in enumerate((hi, mid, lo)):
            pieces = jnp.where(group == 3 * i + j, piece, pieces)
    bcast = _dot(pieces.astype(bf16), sel_ref[...])
    tile = lambda i, h: bcast[:, (i * N_HEADS + h) * DV:(i * N_HEADS + h + 1) * DV]
    return types.SimpleNamespace(
        a_t=a.T,
        w_state=w_state, w_state_t=w_state.T,
        decay=jnp.exp(m_in - g_last),
        m_new=_seg_last(b, nseq, cs) + g_last,
        g=lambda h: tile(0, h), w_inter=lambda h: tile(1, h), clamp=lambda h: tile(2, h))


def _mlstm_body(*refs, nseq, cs, n_chunks, has_state):
    rows = nseq * cs
    log_cs = cs.bit_length() - 1
    heads = range(N_HEADS)
    chunks = range(n_chunks)
    if has_state:
        (q_ref, k_ref, v_ref, gate_ref, sel_ref, c0_ref, n0_ref, m0_ref,
         h_ref, cout_ref, nout_ref, mout_ref) = refs
    else:
        (q_ref, k_ref, v_ref, gate_ref, sel_ref,
         h_ref, cout_ref, nout_ref, mout_ref, c_scr, n_scr, m_scr) = refs
        t_idx = pl.program_id(1)

        @pl.when(t_idx == 0)
        def _():
            c_scr[...] = jnp.zeros_like(c_scr)
            n_scr[...] = jnp.zeros_like(n_scr)
            m_scr[...] = jnp.zeros_like(m_scr)

    rsl = lambda c: slice(c * rows, (c + 1) * rows)
    q_h = [[q_ref[rsl(c), h * DK:(h + 1) * DK] for h in heads] for c in chunks]
    k_h = [[k_ref[rsl(c), h * DK:(h + 1) * DK] for h in heads] for c in chunks]
    v_h = [[v_ref[rsl(c), h * DV:(h + 1) * DV] for h in heads] for c in chunks]

    qk = [[_dot_nt(q_h[c][h], k_h[c][h].astype(bf16)) for h in heads] for c in chunks]

    gates = []
    for c in chunks:
        if has_state:
            m_in = jnp.broadcast_to(m0_ref[...][:, None, :], (nseq, cs, LANES)).reshape(rows, LANES)
        else:
            m_in = m_scr[...] if c == 0 else gates[c - 1].m_new
        gates.append(_gate_algebra(gate_ref[rsl(c), :LANES], gate_ref[rsl(c), LANES:], m_in, nseq, cs, sel_ref))

    inter = [[None] * N_HEADS for _ in chunks]
    qn = [[None] * N_HEADS for _ in chunks]
    if has_state:
        log_dk = DK.bit_length() - 1
        bd_q = ((lax.broadcasted_iota(jnp.int32, (rows, nseq * DK), 0) >> log_cs)
                == (lax.broadcasted_iota(jnp.int32, (rows, nseq * DK), 1) >> log_dk))
        bd_k = ((lax.broadcasted_iota(jnp.int32, (nseq * DK, rows), 0) >> log_dk)
                == (lax.broadcasted_iota(jnp.int32, (nseq * DK, rows), 1) >> log_cs))
        ga = gates[0]
        k_t = k_ref[...].T
        for h in heads:
            c_prev = c0_ref[:, h].reshape(nseq * DK, DV)
            q32 = q_h[0][h].astype(f32)
            q_bd = jnp.where(bd_q, jnp.concatenate([q32] * nseq, axis=1), 0.0).astype(bf16)
            inter[0][h] = _dot(q_bd, c_prev.astype(bf16))
            n_prev = n0_ref[:, h * DK:(h + 1) * DK]
            n_rows = jnp.broadcast_to(n_prev[:, None, :], (nseq, cs, DK)).reshape(rows, DK)
            qn[0][h] = jnp.sum(q32 * n_rows, axis=1, keepdims=True)

            wk_t = k_t[h * DK:(h + 1) * DK, :] * ga.w_state_t[h:h + 1, :]
            wk_bd = jnp.where(bd_k, jnp.broadcast_to(wk_t[None], (nseq, DK, rows)).reshape(nseq * DK, rows), 0.0)
            dc = _dot(wk_bd.astype(bf16), v_h[0][h])
            dec_col = ga.decay[:, h:h + 1]
            dec_rows = jnp.broadcast_to(dec_col.reshape(nseq, 1, cs, 1), (nseq, DK // cs, cs, 1)).reshape(nseq * DK, 1)
            cout_ref[:, h] = (dec_rows * c_prev + dc).reshape(nseq, DK, DV)
            dec_seq = jnp.max(dec_col.reshape(nseq, cs, 1), axis=1)
            nout_ref[:, h * DK:(h + 1) * DK] = dec_seq * n_prev + jnp.sum(
                (ga.w_state[:, h:h + 1] * k_h[0][h]).reshape(nseq, cs, DK), axis=1)
        mout_ref[...] = jnp.max(ga.m_new.reshape(nseq, cs, LANES), axis=1)
    else:
        c_state = [c_scr[h] for h in heads]
        n_state = [n_scr[h] for h in heads]
        for c in chunks:
            ga = gates[c]
            k_t = k_ref[rsl(c), :].T
            for h in heads:
                rhs = jnp.concatenate([c_state[h].astype(bf16),
                                       jnp.broadcast_to(n_state[h], (DK, DV)).astype(bf16)], axis=1)
                inter2 = _dot(q_h[c][h], rhs)
                inter[c][h] = inter2[:, :DV]
                qn[c][h] = inter2[:, DV:]
                wk_t = k_t[h * DK:(h + 1) * DK, :] * ga.w_state_t[h:h + 1, :]
                dec = ga.decay[:, h:h + 1]
                c_state[h] = dec * c_state[h] + _dot(wk_t.astype(bf16), v_h[c][h])
                n_state[h] = dec * n_state[h] + jnp.sum(wk_t, axis=1, keepdims=True)
        for h in heads:
            c_scr[h] = c_state[h]
            n_scr[h] = n_state[h]
        m_scr[...] = gates[-1].m_new

    ri = lax.broadcasted_iota(jnp.int32, (rows, rows), 0)
    ci = lax.broadcasted_iota(jnp.int32, (rows, rows), 1)
    causal = ci <= ri
    if nseq > 1:
        causal = jnp.logical_and(causal, (ri >> log_cs) == (ci >> log_cs))
    intra = [[None] * N_HEADS for _ in chunks]
    s_sum = [[None] * N_HEADS for _ in chunks]
    for c in chunks:
        for h in heads:
            g_rows = jnp.concatenate([gates[c].g(h)] * (rows // DV), axis=1)
            w = jnp.exp(jnp.where(causal, gates[c].a_t[h:h + 1, :] - g_rows, NEG_BIG))
            s = qk[c][h] * w
            s_sum[c][h] = jnp.sum(s, axis=1, keepdims=True)
            intra[c][h] = _dot(s.astype(bf16), v_h[c][h])

    for c in chunks:
        for h in heads:
            w_inter = gates[c].w_inter(h)
            num = w_inter * inter[c][h] + intra[c][h]
            den = jnp.maximum(jnp.abs(w_inter * qn[c][h] + s_sum[c][h]), gates[c].clamp(h))
            h_ref[rsl(c), h * DV:(h + 1) * DV] = num * lax.rsqrt(
                jnp.mean(num * num, axis=1, keepdims=True) + EPS * (den * den))

    if not has_state:
        @pl.when(t_idx == pl.num_programs(1) - 1)
        def _():
            cout_ref[0] = c_scr[...]
            lane = lax.broadcasted_iota(jnp.int32, (DK, LANES), 1)
            n_mat = jnp.zeros((DK, LANES), f32)
            for h in heads:
                n_mat = jnp.where(lane == h, jnp.broadcast_to(n_scr[h], (DK, LANES)), n_mat)
            nout_ref[0] = n_mat
            mout_ref[0] = m_scr[...]


def _bcast_selector():
    src = np.arange(LANES)
    dst = np.arange(N_BCAST * N_HEADS * DV) // DV
    hit = (((src[:, None] >> LOG_HEADS) // 3 == dst[None, :] // N_HEADS)
           & ((src[:, None] & (N_HEADS - 1)) == dst[None, :] % N_HEADS))
    return jnp.asarray(hit, dtype=bf16)


def _mlstm_core(q, k, v, gates, state, n_batch, seq_len):
    t = q.shape[0]
    sel = _bcast_selector()
    if state is None:
        step_rows = PROMPT_CHUNKS * CHUNK_ROWS
        nt = seq_len // step_rows
        grid = (n_batch, nt)
        row = lambda n: pl.BlockSpec((step_rows, n), lambda b, i: (b * nt + i, 0))
        in_specs = [row(QK_DIM), row(QK_DIM), row(V_DIM), row(2 * LANES), _const_spec(sel.shape)]
        out_specs = [row(V_DIM),
                     pl.BlockSpec((1, N_HEADS, DK, DV), lambda b, i: (b, 0, 0, 0)),
                     pl.BlockSpec((1, DK, LANES), lambda b, i: (b, 0, 0)),
                     pl.BlockSpec((1, 1, LANES), lambda b, i: (b, 0, 0))]
        out_shape = [jax.ShapeDtypeStruct((t, V_DIM), f32),
                     jax.ShapeDtypeStruct((n_batch, N_HEADS, DK, DV), f32),
                     jax.ShapeDtypeStruct((n_batch, DK, LANES), f32),
                     jax.ShapeDtypeStruct((n_batch, 1, LANES), f32)]
        scratch = [pltpu.VMEM((N_HEADS, DK, DV), f32), pltpu.VMEM((N_HEADS, DK, 1), f32),
                   pltpu.VMEM((1, LANES), f32)]
        body = functools.partial(_mlstm_body, nseq=1, cs=CHUNK_ROWS, n_chunks=PROMPT_CHUNKS, has_state=False)
        args = (q, k, v, gates, sel)
    else:
        c0, n0, m0 = state
        nseq = CHUNK_ROWS // seq_len
        grid = (t // CHUNK_ROWS, 1)
        row = lambda n: pl.BlockSpec((CHUNK_ROWS, n), lambda i, _: (i, 0))
        cspec = pl.BlockSpec((nseq, N_HEADS, DK, DV), lambda i, _: (i, 0, 0, 0))
        nspec = pl.BlockSpec((nseq, QK_DIM), lambda i, _: (i, 0))
        mspec = pl.BlockSpec((nseq, LANES), lambda i, _: (i, 0))
        in_specs = [row(QK_DIM), row(QK_DIM), row(V_DIM), row(2 * LANES), _const_spec(sel.shape),
                    cspec, nspec, mspec]
        out_specs = [row(V_DIM), cspec, nspec, mspec]
        out_shape = [jax.ShapeDtypeStruct((t, V_DIM), f32),
                     jax.ShapeDtypeStruct((n_batch, N_HEADS, DK, DV), f32),
                     jax.ShapeDtypeStruct((n_batch, QK_DIM), f32),
                     jax.ShapeDtypeStruct((n_batch, LANES), f32)]
        scratch = []
        body = functools.partial(_mlstm_body, nseq=nseq, cs=seq_len, n_chunks=1, has_state=True)
        args = (q, k, v, gates, sel, c0, n0, m0)
    return pl.pallas_call(
        body, grid=grid, in_specs=in_specs, out_specs=out_specs, out_shape=out_shape,
        scratch_shapes=scratch, compiler_params=_params(2), name="l0_mlstm_core",
    )(*args)


def _outproj_ffn_body(h_ref, o_ref, x_ref, hg_ref, wout_ref, gffn_ref, wup_ref, wdn_ref, y_ref):
    hn = h_ref[...] * hg_ref[...] * jax.nn.sigmoid(o_ref[...])
    x1 = x_ref[...] + _dot(hn.astype(bf16), wout_ref[...])
    y_ref[...] = _ffn(x1, gffn_ref, wup_ref, wdn_ref)


def _outproj_ffn(hmix, o, x2d, head_g, w_out, g_ffn, w_up, w_down):
    t = x2d.shape[0]
    row = pl.BlockSpec((ROW_TILE, D_MODEL), lambda i: (i, 0))
    return pl.pallas_call(
        _outproj_ffn_body,
        grid=(t // ROW_TILE,),
        in_specs=[row, row, row, _const_spec((1, V_DIM)), _const_spec(w_out.shape),
                  _const_spec((1, D_MODEL)), _layer_spec(w_up.shape, 0), _layer_spec(w_down.shape, 0)],
        out_specs=row,
        out_shape=jax.ShapeDtypeStruct((t, D_MODEL), f32),
        compiler_params=_params(1),
        name="l0_outproj_ffn",
    )(hmix, o, x2d, head_g, w_out, g_ffn, w_up, w_down)


def _s5_prep_body(lre_ref, lim_ref, ldt_ref, bre_ref, bim_ref, are_ref, aim_ref, bbre_ref, bbim_ref):
    lr = lre_ref[...]
    li = lim_ref[...]
    dt = jnp.exp(ldt_ref[...])
    mag = jnp.exp(lr * dt)
    a_re = mag * jnp.cos(li * dt)
    a_im = mag * jnp.sin(li * dt)
    den = lr * lr + li * li
    z_re = a_re - 1.0
    coef_re = ((z_re * lr + a_im * li) / den)[:, None, :]
    coef_im = ((a_im * lr - z_re * li) / den)[:, None, :]
    br = bre_ref[...]
    bi = bim_ref[...]
    are_ref[...] = a_re
    aim_ref[...] = a_im
    bbre_ref[...] = coef_re * br - coef_im * bi
    bbim_ref[...] = coef_re * bi + coef_im * br


def _s5_prep(lam_re, lam_im, log_dt, b_re_t, b_im_t):
    gp = jax.ShapeDtypeStruct((S5_GROUPS, S5_STATE), f32)
    gcp = jax.ShapeDtypeStruct((S5_GROUPS, S5_GROUP, S5_STATE), f32)
    return pl.pallas_call(_s5_prep_body, out_shape=[gp, gp, gcp, gcp], name="l1_s5_prep")(
        lam_re, lam_im, log_dt, b_re_t, b_im_t)


def _s5_body(x_ref, h0r_ref, h0i_ref, g_ref, are_ref, aim_ref, wbr_ref, wbi_ref, vre_ref, vim_ref,
             d_ref, wglu_ref, gffn_ref, wup_ref, wdn_ref, gfin_ref,
             y_ref, sre_ref, sim_ref, sr_scr, si_scr, hr_scr, hi_scr, act_scr, *, bt, lt):
    @pl.when(pl.program_id(1) == 0)
    def _():
        sr_scr[...] = h0r_ref[...]
        si_scr[...] = h0i_ref[...]

    xt = jnp.concatenate([x_ref[:, t, :] for t in range(lt)], axis=0)
    u = _rms(xt, g_ref[...])
    ub = u.astype(bf16)

    def project_in(j):
        ch = slice(j * S5_BLOCK_CH, (j + 1) * S5_BLOCK_CH)
        hr_scr[j % 2] = _dot(ub[:, ch], wbr_ref[j])
        hi_scr[j % 2] = _dot(ub[:, ch], wbi_ref[j])

    project_in(0)
    for j in range(N_S5_BLOCKS):
        ch = slice(j * S5_BLOCK_CH, (j + 1) * S5_BLOCK_CH)
        st = slice(j * S5_BLOCK_ST, (j + 1) * S5_BLOCK_ST)
        if j + 1 < N_S5_BLOCKS:
            project_in(j + 1)
        a_re = jnp.broadcast_to(are_ref[:, st], (bt, S5_BLOCK_ST))
        a_im = jnp.broadcast_to(aim_ref[:, st], (bt, S5_BLOCK_ST))
        s_re = sr_scr[:, st]
        s_im = si_scr[:, st]
        for t in range(lt):
            r = slice(t * bt, (t + 1) * bt)
            n_re = a_re * s_re - a_im * s_im + hr_scr[j % 2, r, :]
            n_im = a_re * s_im + a_im * s_re + hi_scr[j % 2, r, :]
            hr_scr[j % 2, r, :] = n_re
            hi_scr[j % 2, r, :] = n_im
            s_re, s_im = n_re, n_im
        sr_scr[:, st] = s_re
        si_scr[:, st] = s_im
        yj = (_dot(hr_scr[j % 2].astype(bf16), vre_ref[j]) - _dot(hi_scr[j % 2].astype(bf16), vim_ref[j])
              + d_ref[:, ch] * u[:, ch])
        act_scr[:, ch] = jax.nn.gelu(yj).astype(bf16)

    sre_ref[...] = sr_scr[...]
    sim_ref[...] = si_scr[...]
    ag = _dot(act_scr[...], wglu_ref[...])
    x3 = xt + ag[:, :D_MODEL] * jax.nn.sigmoid(ag[:, D_MODEL:])
    x4 = _ffn(x3, gffn_ref, wup_ref, wdn_ref)
    y = _rms(x4, gfin_ref[...])
    for t in range(lt):
        y_ref[:, t, :] = y[t * bt:(t + 1) * bt, :]


def _s5_layer(x3d, h0_re, h0_im, bt, lt, g_mix, a_re, a_im, wb_re, wb_im, v_re, v_im, d_skip,
              w_glu, g_ffn, w_up, w_down, g_final):
    n_batch, seq_len, _ = x3d.shape
    grid = (n_batch // bt, seq_len // lt)
    sspec = pl.BlockSpec((bt, S5_N), lambda i, t: (i, 0))
    xspec = pl.BlockSpec((bt, lt, D_MODEL), lambda i, t: (i, t, 0))
    consts = (g_mix, a_re, a_im, wb_re, wb_im, v_re, v_im, d_skip, w_glu, g_ffn, w_up, w_down, g_final)
    const_specs = [_layer_spec(c.shape, 1) if c is w_up or c is w_down else _const_spec(c.shape) for c in consts]
    rows = bt * lt
    return pl.pallas_call(
        functools.partial(_s5_body, bt=bt, lt=lt),
        grid=grid,
        in_specs=[xspec, sspec, sspec] + const_specs,
        out_specs=[xspec, sspec, sspec],
        out_shape=[jax.ShapeDtypeStruct(x3d.shape, f32), jax.ShapeDtypeStruct((n_batch, S5_N), f32),
                   jax.ShapeDtypeStruct((n_batch, S5_N), f32)],
        scratch_shapes=[pltpu.VMEM((bt, S5_N), f32), pltpu.VMEM((bt, S5_N), f32),
                        pltpu.VMEM((2, rows, S5_BLOCK_ST), f32), pltpu.VMEM((2, rows, S5_BLOCK_ST), f32),
                        pltpu.VMEM((rows, D_MODEL), bf16)],
        compiler_params=_params(2),
        name="l1_s5_ffn",
    )(x3d, h0_re, h0_im, *consts)


def _block_diag(w):
    g, a, b = w.shape
    nb = g // 8
    eye = jnp.eye(8, dtype=w.dtype)
    out = w.reshape(nb, 8, a, 1, b) * eye[None, :, None, :, None]
    return out.reshape(nb, 8 * a, 8 * b)


def _rep_lanes(x):
    return jnp.tile(x, (1,) * (x.ndim - 1) + (LANES // x.shape[-1],))


def _trunk(x, state, p):
    n_batch, seq_len, _ = x.shape
    x2d = x.reshape(n_batch * seq_len, D_MODEL)
    q, k, v, o, gates = _inproj(x2d, p["g_mix0"], p["w_qkvo"], p["w_gate"], p["gate_bias"])
    if state is None:
        hmix, c_new, n_new, m_new = _mlstm_core(q, k, v, gates, None, n_batch, seq_len)
        n_new = jnp.swapaxes(n_new[:, :, :N_HEADS], 1, 2)
        h0_re = jnp.zeros((n_batch, S5_N), f32)
        h0_im = h0_re
        bt, lt = n_batch, 512 // n_batch
    else:
        c0, n0, m0, h0_re, h0_im = state
        hmix, c_new, n_new, m_new = _mlstm_core(
            q, k, v, gates, (c0, n0.reshape(n_batch, QK_DIM), _rep_lanes(m0)), n_batch, seq_len)
        h0_re = h0_re.reshape(n_batch, S5_N)
        h0_im = h0_im.reshape(n_batch, S5_N)
        bt, lt = 512 // seq_len, seq_len
    x2 = _outproj_ffn(hmix, o, x2d, p["head_g"], p["w_out"], p["g_ffn0"], p["w_up"], p["w_down"])
    y, s_re, s_im = _s5_layer(x2.reshape(n_batch, seq_len, D_MODEL), h0_re, h0_im, bt, lt,
                              p["g_mix1"], p["a_re"], p["a_im"], p["wb_re"], p["wb_im"], p["v_re"], p["v_im"],
                              p["d_skip"], p["w_glu"], p["g_ffn1"], p["w_up"], p["w_down"], p["g_final"])
    return (y,
            c_new.reshape(1, n_batch, N_HEADS, DK, DV),
            n_new.reshape(1, n_batch, N_HEADS, DK),
            m_new.reshape(n_batch, LANES)[:, :N_HEADS].reshape(1, n_batch, N_HEADS),
            s_re.reshape(1, n_batch, S5_GROUPS, S5_STATE),
            s_im.reshape(1, n_batch, S5_GROUPS, S5_STATE))


def kernel(x_prompt, x_sample, state_mlstm_C, state_mlstm_n, state_mlstm_m, state_s5_re, state_s5_im,
           norm_mix_g, norm_ffn_g, norm_final_g, mlstm_w_in, mlstm_b_i, mlstm_b_f, mlstm_head_norm_g,
           mlstm_w_out, s5_lambda_re, s5_lambda_im, s5_log_dt, s5_b_re, s5_b_im, s5_c_re, s5_c_im,
           s5_d, s5_w_glu, ffn_w_up, ffn_w_down):
    w_in = mlstm_w_in[0]
    n_qkvo = 2 * QK_DIM + 2 * V_DIM
    a_re, a_im, bb_re, bb_im = _s5_prep(
        s5_lambda_re[0], s5_lambda_im[0], s5_log_dt[0].reshape(S5_GROUPS, 1),
        jnp.swapaxes(s5_b_re[0], 1, 2), jnp.swapaxes(s5_b_im[0], 1, 2))
    p = dict(
        g_mix0=norm_mix_g[0].reshape(1, D_MODEL), g_mix1=norm_mix_g[1].reshape(1, D_MODEL),
        g_ffn0=norm_ffn_g[0].reshape(1, D_MODEL), g_ffn1=norm_ffn_g[1].reshape(1, D_MODEL),
        g_final=norm_final_g.reshape(1, D_MODEL),
        w_qkvo=w_in[:, :n_qkvo].astype(bf16),
        w_gate=jnp.concatenate([_rep_lanes(w_in[:, n_qkvo:n_qkvo + N_HEADS]),
                                _rep_lanes(w_in[:, n_qkvo + N_HEADS:])], axis=1).astype(bf16),
        gate_bias=jnp.concatenate([_rep_lanes(mlstm_b_i[0][None]), _rep_lanes(mlstm_b_f[0][None])], axis=1),
        head_g=mlstm_head_norm_g[0].reshape(1, V_DIM),
        w_out=mlstm_w_out[0].astype(bf16),
        w_up=ffn_w_up.astype(bf16), w_down=ffn_w_down.astype(bf16),
        a_re=a_re.reshape(1, S5_N), a_im=a_im.reshape(1, S5_N),
        wb_re=_block_diag(bb_re).astype(bf16), wb_im=_block_diag(bb_im).astype(bf16),
        v_re=_block_diag(jnp.swapaxes(s5_c_re[0], 1, 2)).astype(bf16),
        v_im=_block_diag(jnp.swapaxes(s5_c_im[0], 1, 2)).astype(bf16),
        d_skip=s5_d[0].reshape(1, D_MODEL),
        w_glu=s5_w_glu[0].astype(bf16),
    )
    prompt = _trunk(x_prompt, None, p)
    sample = _trunk(x_sample, (state_mlstm_C[0], state_mlstm_n[0], state_mlstm_m[0],
                               state_s5_re[0], state_s5_im[0]), p)
    return (prompt[0], sample[0]) + prompt[1:] + sample[1:]
```

```python
import functools
import types

import jax
import jax.numpy as jnp
import numpy as np
from jax import lax
from jax.experimental import pallas as pl
from jax.experimental.pallas import tpu as pltpu

f32 = jnp.float32
bf16 = jnp.bfloat16

D_MODEL = 1024
N_HEADS = 8
DK = 64
DV = 128
QK_DIM = N_HEADS * DK
V_DIM = N_HEADS * DV
W_QKVO_COLS = 2 * QK_DIM + 2 * V_DIM
D_FF = 4 * D_MODEL
S5_GROUPS = 64
S5_GROUP = 16
S5_STATE = 64
S5_N = S5_GROUPS * S5_STATE
EPS = 1e-6

LANES = 128
SUBLANES = 8
ROW_TILE = 512
CHUNK_ROWS = 256
PROMPT_CHUNKS = 2
FF_CHUNK = 1024
S5_BLOCK_CH = LANES
S5_BLOCK_ST = S5_BLOCK_CH // S5_GROUP * S5_STATE
N_S5_BLOCKS = D_MODEL // S5_BLOCK_CH
VMEM_LIMIT_BYTES = 56 * 1024 * 1024
NEG_BIG = -1e30
LOG_HEADS = N_HEADS.bit_length() - 1
N_BCAST = 3


def _params(n_axes):
    return pltpu.CompilerParams(dimension_semantics=("arbitrary",) * n_axes,
                                vmem_limit_bytes=VMEM_LIMIT_BYTES)


def _const_spec(shape):
    nd = len(shape)
    return pl.BlockSpec(shape, lambda *_: (0,) * nd, pipeline_mode=pl.Buffered(1))


def _layer_spec(shape, layer):
    nd = len(shape)
    return pl.BlockSpec((None,) + tuple(shape[1:]), lambda *_: (layer,) + (0,) * (nd - 1),
                        pipeline_mode=pl.Buffered(1))


def _dot(a, b):
    return jnp.dot(a, b, preferred_element_type=f32)


def _dot_nt(a, b):
    return lax.dot_general(a, b, (((1,), (1,)), ((), ())), preferred_element_type=f32)


def _rms(x, g):
    return x * lax.rsqrt(jnp.mean(x * x, axis=-1, keepdims=True) + EPS) * g


def _ffn(x1, g_ref, wup_ref, wdn_ref):
    xn = _rms(x1, g_ref[...]).astype(bf16)
    acc = None
    for c in range(D_FF // FF_CHUNK):
        cols = slice(c * FF_CHUNK, (c + 1) * FF_CHUNK)
        hid = _dot(xn, wup_ref[:, cols])
        hid = jnp.square(jnp.maximum(hid, 0.0)).astype(bf16)
        part = _dot(hid, wdn_ref[cols, :])
        acc = part if acc is None else acc + part
    return x1 + acc


def _inproj_body(x_ref, g_ref, w32_ref, wg_ref, bias_ref, q_ref, k_ref, v_ref, o_ref, gate_ref, w_ref):
    @pl.when(pl.program_id(0) == 0)
    def _():
        for c in range(0, W_QKVO_COLS, FF_CHUNK):
            w_ref[:, c:c + FF_CHUNK] = w32_ref[:, c:c + FF_CHUNK].astype(bf16)

    xn = _rms(x_ref[...], g_ref[...]).astype(bf16)
    q_ref[...] = _dot(xn, w_ref[:, 0:QK_DIM]).astype(bf16)
    k_ref[...] = _dot(xn, w_ref[:, QK_DIM:2 * QK_DIM]) * (DK ** -0.5)
    v_ref[...] = _dot(xn, w_ref[:, 2 * QK_DIM:2 * QK_DIM + V_DIM]).astype(bf16)
    o_ref[...] = _dot(xn, w_ref[:, 2 * QK_DIM + V_DIM:2 * QK_DIM + 2 * V_DIM])
    gz = _dot(xn, wg_ref[...]) + bias_ref[...]
    gate_ref[:, :LANES] = gz[:, :LANES]
    gate_ref[:, LANES:] = jax.nn.log_sigmoid(gz[:, LANES:])


def _inproj(x2d, g, w_in, w_gate, gate_bias):
    t = x2d.shape[0]
    row = lambda n: pl.BlockSpec((ROW_TILE, n), lambda i: (i, 0))
    return pl.pallas_call(
        _inproj_body,
        grid=(t // ROW_TILE,),
        in_specs=[row(D_MODEL), _const_spec((1, D_MODEL)), _const_spec(w_in.shape),
                  _const_spec(w_gate.shape), _const_spec((1, 2 * LANES))],
        out_specs=[row(QK_DIM), row(QK_DIM), row(V_DIM), row(V_DIM), row(2 * LANES)],
        out_shape=[jax.ShapeDtypeStruct((t, QK_DIM), bf16), jax.ShapeDtypeStruct((t, QK_DIM), f32),
                   jax.ShapeDtypeStruct((t, V_DIM), bf16), jax.ShapeDtypeStruct((t, V_DIM), f32),
                   jax.ShapeDtypeStruct((t, 2 * LANES), f32)],
        scratch_shapes=[pltpu.VMEM((D_MODEL, W_QKVO_COLS), bf16)],
        compiler_params=_params(1),
        name="l0_inproj",
    )(x2d, g, w_in, w_gate, gate_bias)


def _row_prefix(x, cs, rpos, is_max):
    sh = 1
    while sh < cs:
        prev = pltpu.roll(x, sh, 0)
        if is_max:
            x = jnp.maximum(x, jnp.where(rpos >= sh, prev, NEG_BIG))
        else:
            x = x + jnp.where(rpos >= sh, prev, 0.0)
        sh *= 2
    return x


def _seg_last(x, nseq, cs):
    if nseq == 1:
        return x[cs - 1:cs, :]
    last = x.reshape(nseq, cs, x.shape[1])[:, cs - 1:cs, :]
    return jnp.broadcast_to(last, (nseq, cs, x.shape[1])).reshape(nseq * cs, x.shape[1])


def _gate_algebra(logi, logf, m_in, nseq, cs, sel_ref):
    rows = nseq * cs
    rpos = lax.broadcasted_iota(jnp.int32, (rows, LANES), 0) & (cs - 1)
    b = _row_prefix(logf, cs, rpos, False)
    a = logi - b
    g = jnp.maximum(m_in, _row_prefix(a, cs, rpos, True))
    g_last = _seg_last(g, nseq, cs)
    w_state = jnp.exp(a - g_last)

    group = lax.broadcasted_iota(jnp.int32, (rows, LANES), 1) >> LOG_HEADS
    pieces = jnp.zeros((rows, LANES), f32)
    for i, val in enumerate((g, jnp.exp(m_in - g), jnp.exp(-(b + g)))):
        hi = val.astype(bf16).astype(f32)
        mid = (val - hi).astype(bf16).astype(f32)
        lo = (val - hi - mid).astype(bf16).astype(f32)
        for j, piece in enumerate((hi, mid, lo)):
            pieces = jnp.where(group == 3 * i + j, piece, pieces)
    bcast = _dot(pieces.astype(bf16), sel_ref[...])
    tile = lambda i, h: bcast[:, (i * N_HEADS + h) * DV:(i * N_HEADS + h + 1) * DV]
    return types.SimpleNamespace(
        a_t=a.T,
        w_state=w_state, w_state_t=w_state.T,
        decay=jnp.exp(m_in - g_last),
        m_new=_seg_last(b, nseq, cs) + g_last,
        g=lambda h: tile(0, h), w_inter=lambda h: tile(1, h), clamp=lambda h: tile(2, h))


def _mlstm_body(*refs, nseq, cs, n_chunks, has_state):
    rows = nseq * cs
    log_cs = cs.bit_length() - 1
    heads = range(N_HEADS)
    chunks = range(n_chunks)
    if has_state:
        (q_ref, k_ref, v_ref, gate_ref, sel_ref, c0_ref, n0_ref, m0_ref,
         h_ref, cout_ref, nout_ref, mout_ref) = refs
    else:
        (q_ref, k_ref, v_ref, gate_ref, sel_ref,
         h_ref, cout_ref, nout_ref, mout_ref, c_scr, n_scr, m_scr) = refs
        t_idx = pl.program_id(1)

        @pl.when(t_idx == 0)
        def _():
            c_scr[...] = jnp.zeros_like(c_scr)
            n_scr[...] = jnp.zeros_like(n_scr)
            m_scr[...] = jnp.zeros_like(m_scr)

    rsl = lambda c: slice(c * rows, (c + 1) * rows)
    q_h = [[q_ref[rsl(c), h * DK:(h + 1) * DK] for h in heads] for c in chunks]
    k_h = [[k_ref[rsl(c), h * DK:(h + 1) * DK] for h in heads] for c in chunks]
    v_h = [[v_ref[rsl(c), h * DV:(h + 1) * DV] for h in heads] for c in chunks]

    qk = [[_dot_nt(q_h[c][h], k_h[c][h].astype(bf16)) for h in heads] for c in chunks]

    gates = []
    for c in chunks:
        if has_state:
            m_in = jnp.broadcast_to(m0_ref[...][:, None, :], (nseq, cs, LANES)).reshape(rows, LANES)
        else:
            m_in = m_scr[...] if c == 0 else gates[c - 1].m_new
        gates.append(_gate_algebra(gate_ref[rsl(c), :LANES], gate_ref[rsl(c), LANES:], m_in, nseq, cs, sel_ref))

    inter = [[None] * N_HEADS for _ in chunks]
    qn = [[None] * N_HEADS for _ in chunks]
    if has_state:
        log_dk = DK.bit_length() - 1
        bd_q = ((lax.broadcasted_iota(jnp.int32, (rows, nseq * DK), 0) >> log_cs)
                == (lax.broadcasted_iota(jnp.int32, (rows, nseq * DK), 1) >> log_dk))
        bd_k = ((lax.broadcasted_iota(jnp.int32, (nseq * DK, rows), 0) >> log_dk)
                == (lax.broadcasted_iota(jnp.int32, (nseq * DK, rows), 1) >> log_cs))
        ga = gates[0]
        k_t = k_ref[...].T
        for h in heads:
            c_prev = c0_ref[:, h].reshape(nseq * DK, DV)
            q32 = q_h[0][h].astype(f32)
            q_bd = jnp.where(bd_q, jnp.concatenate([q32] * nseq, axis=1), 0.0).astype(bf16)
            inter[0][h] = _dot(q_bd, c_prev.astype(bf16))
            n_prev = n0_ref[:, h * DK:(h + 1) * DK]
            n_rows = jnp.broadcast_to(n_prev[:, None, :], (nseq, cs, DK)).reshape(rows, DK)
            qn[0][h] = jnp.sum(q32 * n_rows, axis=1, keepdims=True)

            wk_t = k_t[h * DK:(h + 1) * DK, :] * ga.w_state_t[h:h + 1, :]
            wk_bd = jnp.where(bd_k, jnp.broadcast_to(wk_t[None], (nseq, DK, rows)).reshape(nseq * DK, rows), 0.0)
            dc = _dot(wk_bd.astype(bf16), v_h[0][h])
            dec_col = ga.decay[:, h:h + 1]
            dec_rows = jnp.broadcast_to(dec_col.reshape(nseq, 1, cs, 1), (nseq, DK // cs, cs, 1)).reshape(nseq * DK, 1)
            cout_ref[:, h] = (dec_rows * c_prev + dc).reshape(nseq, DK, DV)
            dec_seq = jnp.max(dec_col.reshape(nseq, cs, 1), axis=1)
            nout_ref[:, h * DK:(h + 1) * DK] = dec_seq * n_prev + jnp.sum(
                (ga.w_state[:, h:h + 1] * k_h[0][h]).reshape(nseq, cs, DK), axis=1)
        mout_ref[...] = jnp.max(ga.m_new.reshape(nseq, cs, LANES), axis=1)
    else:
        c_state = [c_scr[h] for h in heads]
        n_state = [n_scr[h] for h in heads]
        for c in chunks:
            ga = gates[c]
            k_t = k_ref[rsl(c), :].T
            for h in heads:
                rhs = jnp.concatenate([c_state[h].astype(bf16),
                                       jnp.broadcast_to(n_state[h], (DK, DV)).astype(bf16)], axis=1)
                inter2 = _dot(q_h[c][h], rhs)
                inter[c][h] = inter2[:, :DV]
                qn[c][h] = inter2[:, DV:]
                wk_t = k_t[h * DK:(h + 1) * DK, :] * ga.w_state_t[h:h + 1, :]
                dec = ga.decay[:, h:h + 1]
                c_state[h] = dec * c_state[h] + _dot(wk_t.astype(bf16), v_h[c][h])
                n_state[h] = dec * n_state[h] + jnp.sum(wk_t, axis=1, keepdims=True)
        for h in heads:
            c_scr[h] = c_state[h]
            n_scr[h] = n_state[h]
        m_scr[...] = gates[-1].m_new

    ri = lax.broadcasted_iota(jnp.int32, (rows, rows), 0)
    ci = lax.broadcasted_iota(jnp.int32, (rows, rows), 1)
    causal = ci <= ri
    if nseq > 1:
        causal = jnp.logical_and(causal, (ri >> log_cs) == (ci >> log_cs))
    intra = [[None] * N_HEADS for _ in chunks]
    s_sum = [[None] * N_HEADS for _ in chunks]
    for c in chunks:
        for h in heads:
            g_rows = jnp.concatenate([gates[c].g(h)] * (rows // DV), axis=1)
            w = jnp.exp(jnp.where(causal, gates[c].a_t[h:h + 1, :] - g_rows, NEG_BIG))
            s = qk[c][h] * w
            s_sum[c][h] = jnp.sum(s, axis=1, keepdims=True)
            intra[c][h] = _dot(s.astype(bf16), v_h[c][h])

    for c in chunks:
        for h in heads:
            w_inter = gates[c].w_inter(h)
            num = w_inter * inter[c][h] + intra[c][h]
            den = jnp.maximum(jnp.abs(w_inter * qn[c][h] + s_sum[c][h]), gates[c].clamp(h))
            h_ref[rsl(c), h * DV:(h + 1) * DV] = num * lax.rsqrt(
                jnp.mean(num * num, axis=1, keepdims=True) + EPS * (den * den))

    if not has_state:
        @pl.when(t_idx == pl.num_programs(1) - 1)
        def _():
            cout_ref[0] = c_scr[...]
            lane = lax.broadcasted_iota(jnp.int32, (DK, LANES), 1)
            n_mat = jnp.zeros((DK, LANES), f32)
            for h in heads:
                n_mat = jnp.where(lane == h, jnp.broadcast_to(n_scr[h], (DK, LANES)), n_mat)
            nout_ref[0] = n_mat
            mout_ref[0] = m_scr[...]


def _bcast_selector():
    src = np.arange(LANES)
    dst = np.arange(N_BCAST * N_HEADS * DV) // DV
    hit = (((src[:, None] >> LOG_HEADS) // 3 == dst[None, :] // N_HEADS)
           & ((src[:, None] & (N_HEADS - 1)) == dst[None, :] % N_HEADS))
    return jnp.asarray(hit, dtype=bf16)


def _mlstm_core(q, k, v, gates, state, n_batch, seq_len):
    t = q.shape[0]
    sel = _bcast_selector()
    if state is None:
        step_rows = PROMPT_CHUNKS * CHUNK_ROWS
        nt = seq_len // step_rows
        grid = (n_batch, nt)
        row = lambda n: pl.BlockSpec((step_rows, n), lambda b, i: (b * nt + i, 0))
        in_specs = [row(QK_DIM), row(QK_DIM), row(V_DIM), row(2 * LANES), _const_spec(sel.shape)]
        out_specs = [row(V_DIM),
                     pl.BlockSpec((1, N_HEADS, DK, DV), lambda b, i: (b, 0, 0, 0)),
                     pl.BlockSpec((1, DK, LANES), lambda b, i: (b, 0, 0)),
                     pl.BlockSpec((1, 1, LANES), lambda b, i: (b, 0, 0))]
        out_shape = [jax.ShapeDtypeStruct((t, V_DIM), f32),
                     jax.ShapeDtypeStruct((n_batch, N_HEADS, DK, DV), f32),
                     jax.ShapeDtypeStruct((n_batch, DK, LANES), f32),
                     jax.ShapeDtypeStruct((n_batch, 1, LANES), f32)]
        scratch = [pltpu.VMEM((N_HEADS, DK, DV), f32), pltpu.VMEM((N_HEADS, DK, 1), f32),
                   pltpu.VMEM((1, LANES), f32)]
        body = functools.partial(_mlstm_body, nseq=1, cs=CHUNK_ROWS, n_chunks=PROMPT_CHUNKS, has_state=False)
        args = (q, k, v, gates, sel)
    else:
        c0, n0, m0 = state
        nseq = CHUNK_ROWS // seq_len
        grid = (t // CHUNK_ROWS, 1)
        row = lambda n: pl.BlockSpec((CHUNK_ROWS, n), lambda i, _: (i, 0))
        cspec = pl.BlockSpec((nseq, N_HEADS, DK, DV), lambda i, _: (i, 0, 0, 0))
        nspec = pl.BlockSpec((nseq, QK_DIM), lambda i, _: (i, 0))
        mspec = pl.BlockSpec((nseq, LANES), lambda i, _: (i, 0))
        in_specs = [row(QK_DIM), row(QK_DIM), row(V_DIM), row(2 * LANES), _const_spec(sel.shape),
                    cspec, nspec, mspec]
        out_specs = [row(V_DIM), cspec, nspec, mspec]
        out_shape = [jax.ShapeDtypeStruct((t, V_DIM), f32),
                     jax.ShapeDtypeStruct((n_batch, N_HEADS, DK, DV), f32),
                     jax.ShapeDtypeStruct((n_batch, QK_DIM), f32),
                     jax.ShapeDtypeStruct((n_batch, LANES), f32)]
        scratch = []
        body = functools.partial(_mlstm_body, nseq=nseq, cs=seq_len, n_chunks=1, has_state=True)
        args = (q, k, v, gates, sel, c0, n0, m0)
    return pl.pallas_call(
        body, grid=grid, in_specs=in_specs, out_specs=out_specs, out_shape=out_shape,
        scratch_shapes=scratch, compiler_params=_params(2), name="l0_mlstm_core",
    )(*args)


def _outproj_ffn_body(h_ref, o_ref, x_ref, hg_ref, wout_ref, gffn_ref, wup_ref, wdn_ref, y_ref):
    hn = h_ref[...] * hg_ref[...] * jax.nn.sigmoid(o_ref[...])
    x1 = x_ref[...] + _dot(hn.astype(bf16), wout_ref[...])
    y_ref[...] = _ffn(x1, gffn_ref, wup_ref, wdn_ref)


def _outproj_ffn(hmix, o, x2d, head_g, w_out, g_ffn, w_up, w_down):
    t = x2d.shape[0]
    row = pl.BlockSpec((ROW_TILE, D_MODEL), lambda i: (i, 0))
    return pl.pallas_call(
        _outproj_ffn_body,
        grid=(t // ROW_TILE,),
        in_specs=[row, row, row, _const_spec((1, V_DIM)), _const_spec(w_out.shape),
                  _const_spec((1, D_MODEL)), _layer_spec(w_up.shape, 0), _layer_spec(w_down.shape, 0)],
        out_specs=row,
        out_shape=jax.ShapeDtypeStruct((t, D_MODEL), f32),
        compiler_params=_params(1),
        name="l0_outproj_ffn",
    )(hmix, o, x2d, head_g, w_out, g_ffn, w_up, w_down)


def _s5_prep_body(lre_ref, lim_ref, ldt_ref, bre_ref, bim_ref, cre_ref, cim_ref,
                  are_ref, aim_ref, wbr_ref, wbi_ref, vre_ref, vim_ref):
    lr = lre_ref[...]
    li = lim_ref[...]
    dt = jnp.exp(ldt_ref[...])
    mag = jnp.exp(lr * dt)
    a_re = mag * jnp.cos(li * dt)
    a_im = mag * jnp.sin(li * dt)
    den = lr * lr + li * li
    z_re = a_re - 1.0
    coef_re = ((z_re * lr + a_im * li) / den)[:, None, :]
    coef_im = ((a_im * lr - z_re * li) / den)[:, None, :]
    br = bre_ref[...]
    bi = bim_ref[...]
    are_ref[...] = a_re
    aim_ref[...] = a_im
    bb = (coef_re * br - coef_im * bi, coef_re * bi + coef_im * br)
    c_t = (cre_ref[...].reshape(S5_GROUPS * S5_GROUP, S5_STATE).T,
           cim_ref[...].reshape(S5_GROUPS * S5_GROUP, S5_STATE).T)

    groups = S5_BLOCK_CH // S5_GROUP
    log_ch, log_st = S5_GROUP.bit_length() - 1, S5_STATE.bit_length() - 1
    shape_b = (S5_BLOCK_CH, S5_BLOCK_ST)
    diag_b = ((lax.broadcasted_iota(jnp.int32, shape_b, 0) >> log_ch)
              == (lax.broadcasted_iota(jnp.int32, shape_b, 1) >> log_st))
    shape_c = (S5_BLOCK_ST, S5_BLOCK_CH)
    diag_c = ((lax.broadcasted_iota(jnp.int32, shape_c, 0) >> log_st)
              == (lax.broadcasted_iota(jnp.int32, shape_c, 1) >> log_ch))
    for j in range(N_S5_BLOCKS):
        for src, dst in zip(bb, (wbr_ref, wbi_ref)):
            blk = src[j * groups:(j + 1) * groups].reshape(S5_BLOCK_CH, S5_STATE)
            dst[j] = jnp.where(diag_b, jnp.concatenate([blk] * groups, axis=1), 0.0).astype(bf16)
        for src, dst in zip(c_t, (vre_ref, vim_ref)):
            blk = src[:, j * S5_BLOCK_CH:(j + 1) * S5_BLOCK_CH]
            rep = jnp.broadcast_to(blk[None], (groups, S5_STATE, S5_BLOCK_CH)).reshape(S5_BLOCK_ST, S5_BLOCK_CH)
            dst[j] = jnp.where(diag_c, rep, 0.0).astype(bf16)


def _s5_prep(lam_re, lam_im, log_dt, b_re_t, b_im_t, c_re, c_im):
    gp = jax.ShapeDtypeStruct((S5_GROUPS, S5_STATE), f32)
    wb = jax.ShapeDtypeStruct((N_S5_BLOCKS, S5_BLOCK_CH, S5_BLOCK_ST), bf16)
    vc = jax.ShapeDtypeStruct((N_S5_BLOCKS, S5_BLOCK_ST, S5_BLOCK_CH), bf16)
    return pl.pallas_call(_s5_prep_body, out_shape=[gp, gp, wb, wb, vc, vc], name="l1_s5_prep")(
        lam_re, lam_im, log_dt, b_re_t, b_im_t, c_re, c_im)


def _s5_body(x_ref, h0r_ref, h0i_ref, g_ref, are_ref, aim_ref, wbr_ref, wbi_ref, vre_ref, vim_ref,
             d_ref, wglu_ref, gffn_ref, wup_ref, wdn_ref, gfin_ref,
             y_ref, sre_ref, sim_ref, sr_scr, si_scr, hr_scr, hi_scr, act_scr, *io_scr, bt, lt, dma_io):
    rows = bt * lt
    step = pl.program_id(1)
    n_steps = pl.num_programs(1)

    @pl.when(step == 0)
    def _():
        sr_scr[...] = h0r_ref[...]
        si_scr[...] = h0i_ref[...]

    if dma_io:
        xbuf, ybuf, in_sem, out_sem = io_scr
        slot = lax.rem(step, 2)

        def in_copy(s, sl, b):
            return pltpu.make_async_copy(x_ref.at[b, pl.ds(s * lt, lt), :], xbuf.at[sl, :, b, :], in_sem.at[sl, b])

        def out_copy(s, sl, b):
            return pltpu.make_async_copy(ybuf.at[sl, :, b, :], y_ref.at[b, pl.ds(s * lt, lt), :], out_sem.at[sl, b])

        @pl.when(step == 0)
        def _():
            for b in range(bt):
                in_copy(0, 0, b).start()

        @pl.when(step + 1 < n_steps)
        def _():
            for b in range(bt):
                in_copy(step + 1, 1 - slot, b).start()

        for b in range(bt):
            in_copy(step, slot, b).wait()
        xt = xbuf[slot].reshape(rows, D_MODEL)
    else:
        xt = jnp.concatenate([x_ref[:, t, :] for t in range(lt)], axis=0)
    u = _rms(xt, g_ref[...])
    ub = u.astype(bf16)

    def project_in(j):
        ch = slice(j * S5_BLOCK_CH, (j + 1) * S5_BLOCK_CH)
        hr_scr[j % 2] = _dot(ub[:, ch], wbr_ref[j])
        hi_scr[j % 2] = _dot(ub[:, ch], wbi_ref[j])

    project_in(0)
    for j in range(N_S5_BLOCKS):
        ch = slice(j * S5_BLOCK_CH, (j + 1) * S5_BLOCK_CH)
        st = slice(j * S5_BLOCK_ST, (j + 1) * S5_BLOCK_ST)
        if j + 1 < N_S5_BLOCKS:
            project_in(j + 1)
        a_re = jnp.broadcast_to(are_ref[:, st], (bt, S5_BLOCK_ST))
        a_im = jnp.broadcast_to(aim_ref[:, st], (bt, S5_BLOCK_ST))
        s_re = sr_scr[:, st]
        s_im = si_scr[:, st]
        for t in range(lt):
            r = slice(t * bt, (t + 1) * bt)
            n_re = a_re * s_re - a_im * s_im + hr_scr[j % 2, r, :]
            n_im = a_re * s_im + a_im * s_re + hi_scr[j % 2, r, :]
            hr_scr[j % 2, r, :] = n_re
            hi_scr[j % 2, r, :] = n_im
            s_re, s_im = n_re, n_im
        sr_scr[:, st] = s_re
        si_scr[:, st] = s_im
        yj = (_dot(hr_scr[j % 2].astype(bf16), vre_ref[j]) - _dot(hi_scr[j % 2].astype(bf16), vim_ref[j])
              + d_ref[:, ch] * u[:, ch])
        act_scr[:, ch] = jax.nn.gelu(yj).astype(bf16)

    sre_ref[...] = sr_scr[...]
    sim_ref[...] = si_scr[...]
    ag = _dot(act_scr[...], wglu_ref[...])
    x3 = xt + ag[:, :D_MODEL] * jax.nn.sigmoid(ag[:, D_MODEL:])
    x4 = _ffn(x3, gffn_ref, wup_ref, wdn_ref)
    y = _rms(x4, gfin_ref[...])
    if dma_io:
        @pl.when(step >= 2)
        def _():
            for b in range(bt):
                out_copy(step - 2, slot, b).wait()

        ybuf[slot] = y.reshape(lt, bt, D_MODEL)
        for b in range(bt):
            out_copy(step, slot, b).start()

        @pl.when(step == n_steps - 1)
        def _():
            @pl.when(step >= 1)
            def _():
                for b in range(bt):
                    out_copy(step - 1, 1 - slot, b).wait()

            for b in range(bt):
                out_copy(step, slot, b).wait()
    else:
        for t in range(lt):
            y_ref[:, t, :] = y[t * bt:(t + 1) * bt, :]


def _s5_layer(x3d, h0_re, h0_im, bt, lt, g_mix, a_re, a_im, wb_re, wb_im, v_re, v_im, d_skip,
              w_glu, g_ffn, w_up, w_down, g_final):
    n_batch, seq_len, _ = x3d.shape
    grid = (n_batch // bt, seq_len // lt)
    sspec = pl.BlockSpec((bt, S5_N), lambda i, t: (i, 0))
    dma_io = n_batch == bt
    if dma_io:
        xspec = pl.BlockSpec(memory_space=pl.ANY)
        io_scratch = [pltpu.VMEM((2, lt, bt, D_MODEL), f32), pltpu.VMEM((2, lt, bt, D_MODEL), f32),
                      pltpu.SemaphoreType.DMA((2, bt)), pltpu.SemaphoreType.DMA((2, bt))]
    else:
        xspec = pl.BlockSpec((bt, lt, D_MODEL), lambda i, t: (i, t, 0))
        io_scratch = []
    consts = (g_mix, a_re, a_im, wb_re, wb_im, v_re, v_im, d_skip, w_glu, g_ffn, w_up, w_down, g_final)
    const_specs = [_layer_spec(c.shape, 1) if c is w_up or c is w_down else _const_spec(c.shape) for c in consts]
    rows = bt * lt
    return pl.pallas_call(
        functools.partial(_s5_body, bt=bt, lt=lt, dma_io=dma_io),
        grid=grid,
        in_specs=[xspec, sspec, sspec] + const_specs,
        out_specs=[xspec, sspec, sspec],
        out_shape=[jax.ShapeDtypeStruct(x3d.shape, f32), jax.ShapeDtypeStruct((n_batch, S5_N), f32),
                   jax.ShapeDtypeStruct((n_batch, S5_N), f32)],
        scratch_shapes=[pltpu.VMEM((bt, S5_N), f32), pltpu.VMEM((bt, S5_N), f32),
                        pltpu.VMEM((2, rows, S5_BLOCK_ST), f32), pltpu.VMEM((2, rows, S5_BLOCK_ST), f32),
                        pltpu.VMEM((rows, D_MODEL), bf16)] + io_scratch,
        compiler_params=_params(2),
        name="l1_s5_ffn",
    )(x3d, h0_re, h0_im, *consts)


def _rep_lanes(x):
    return jnp.tile(x, (1,) * (x.ndim - 1) + (LANES // x.shape[-1],))


def _trunk(x, state, p):
    n_batch, seq_len, _ = x.shape
    x2d = x.reshape(n_batch * seq_len, D_MODEL)
    q, k, v, o, gates = _inproj(x2d, p["g_mix0"], p["w_in"], p["w_gate"], p["gate_bias"])
    if state is None:
        hmix, c_new, n_new, m_new = _mlstm_core(q, k, v, gates, None, n_batch, seq_len)
        n_new = jnp.swapaxes(n_new[:, :, :N_HEADS], 1, 2)
        h0_re = jnp.zeros((n_batch, S5_N), f32)
        h0_im = h0_re
        bt, lt = n_batch, 512 // n_batch
    else:
        c0, n0, m0, h0_re, h0_im = state
        hmix, c_new, n_new, m_new = _mlstm_core(
            q, k, v, gates, (c0, n0.reshape(n_batch, QK_DIM), _rep_lanes(m0)), n_batch, seq_len)
        h0_re = h0_re.reshape(n_batch, S5_N)
        h0_im = h0_im.reshape(n_batch, S5_N)
        bt, lt = 512 // seq_len, seq_len
    x2 = _outproj_ffn(hmix, o, x2d, p["head_g"], p["w_out"], p["g_ffn0"], p["w_up"], p["w_down"])
    y, s_re, s_im = _s5_layer(x2.reshape(n_batch, seq_len, D_MODEL), h0_re, h0_im, bt, lt,
                              p["g_mix1"], p["a_re"], p["a_im"], p["wb_re"], p["wb_im"], p["v_re"], p["v_im"],
                              p["d_skip"], p["w_glu"], p["g_ffn1"], p["w_up"], p["w_down"], p["g_final"])
    return (y,
            c_new.reshape(1, n_batch, N_HEADS, DK, DV),
            n_new.reshape(1, n_batch, N_HEADS, DK),
            m_new.reshape(n_batch, LANES)[:, :N_HEADS].reshape(1, n_batch, N_HEADS),
            s_re.reshape(1, n_batch, S5_GROUPS, S5_STATE),
            s_im.reshape(1, n_batch, S5_GROUPS, S5_STATE))


def kernel(x_prompt, x_sample, state_mlstm_C, state_mlstm_n, state_mlstm_m, state_s5_re, state_s5_im,
           norm_mix_g, norm_ffn_g, norm_final_g, mlstm_w_in, mlstm_b_i, mlstm_b_f, mlstm_head_norm_g,
           mlstm_w_out, s5_lambda_re, s5_lambda_im, s5_log_dt, s5_b_re, s5_b_im, s5_c_re, s5_c_im,
           s5_d, s5_w_glu, ffn_w_up, ffn_w_down):
    w_in = mlstm_w_in[0]
    n_qkvo = W_QKVO_COLS
    a_re, a_im, wb_re, wb_im, v_re, v_im = _s5_prep(
        s5_lambda_re[0], s5_lambda_im[0], s5_log_dt[0].reshape(S5_GROUPS, 1),
        jnp.swapaxes(s5_b_re[0], 1, 2), jnp.swapaxes(s5_b_im[0], 1, 2), s5_c_re[0], s5_c_im[0])
    p = dict(
        g_mix0=norm_mix_g[0].reshape(1, D_MODEL), g_mix1=norm_mix_g[1].reshape(1, D_MODEL),
        g_ffn0=norm_ffn_g[0].reshape(1, D_MODEL), g_ffn1=norm_ffn_g[1].reshape(1, D_MODEL),
        g_final=norm_final_g.reshape(1, D_MODEL),
        w_in=w_in,
        w_gate=jnp.concatenate([_rep_lanes(w_in[:, n_qkvo:n_qkvo + N_HEADS]),
                                _rep_lanes(w_in[:, n_qkvo + N_HEADS:])], axis=1).astype(bf16),
        gate_bias=jnp.concatenate([_rep_lanes(mlstm_b_i[0][None]), _rep_lanes(mlstm_b_f[0][None])], axis=1),
        head_g=mlstm_head_norm_g[0].reshape(1, V_DIM),
        w_out=mlstm_w_out[0].astype(bf16),
        w_up=ffn_w_up.astype(bf16), w_down=ffn_w_down.astype(bf16),
        a_re=a_re.reshape(1, S5_N), a_im=a_im.reshape(1, S5_N),
        wb_re=wb_re, wb_im=wb_im, v_re=v_re, v_im=v_im,
        d_skip=s5_d[0].reshape(1, D_MODEL),
        w_glu=s5_w_glu[0].astype(bf16),
    )
    prompt = _trunk(x_prompt, None, p)
    sample = _trunk(x_sample, (state_mlstm_C[0], state_mlstm_n[0], state_mlstm_m[0],
                               state_s5_re[0], state_s5_im[0]), p)
    return (prompt[0], sample[0]) + prompt[1:] + sample[1:]
```

```python
import functools
import types

import jax
import jax.numpy as jnp
import numpy as np
from jax import lax
from jax.experimental import pallas as pl
from jax.experimental.pallas import tpu as pltpu

f32 = jnp.float32
bf16 = jnp.bfloat16

D_MODEL = 1024
N_HEADS = 8
DK = 64
DV = 128
QK_DIM = N_HEADS * DK
V_DIM = N_HEADS * DV
W_QKVO_COLS = 2 * QK_DIM + 2 * V_DIM
D_FF = 4 * D_MODEL
S5_GROUPS = 64
S5_GROUP = 16
S5_STATE = 64
S5_N = S5_GROUPS * S5_STATE
EPS = 1e-6

LANES = 128
SUBLANES = 8
ROW_TILE = 512
CHUNK_ROWS = 256
PROMPT_CHUNKS = 2
FF_CHUNK = 1024
S5_BLOCK_CH = LANES
S5_BLOCK_ST = S5_BLOCK_CH // S5_GROUP * S5_STATE
N_S5_BLOCKS = D_MODEL // S5_BLOCK_CH
VMEM_LIMIT_BYTES = 56 * 1024 * 1024
NEG_BIG = -1e30
LOG_HEADS = N_HEADS.bit_length() - 1
N_BCAST = 3


def _params(n_axes):
    return pltpu.CompilerParams(dimension_semantics=("arbitrary",) * n_axes,
                                vmem_limit_bytes=VMEM_LIMIT_BYTES)


def _const_spec(shape):
    nd = len(shape)
    return pl.BlockSpec(shape, lambda *_: (0,) * nd, pipeline_mode=pl.Buffered(1))


def _layer_spec(shape, layer):
    nd = len(shape)
    return pl.BlockSpec((None,) + tuple(shape[1:]), lambda *_: (layer,) + (0,) * (nd - 1),
                        pipeline_mode=pl.Buffered(1))


def _dot(a, b):
    return jnp.dot(a, b, preferred_element_type=f32)


def _dot_nt(a, b):
    return lax.dot_general(a, b, (((1,), (1,)), ((), ())), preferred_element_type=f32)


def _rms(x, g):
    return x * lax.rsqrt(jnp.mean(x * x, axis=-1, keepdims=True) + EPS) * g


def _ffn(x1, g_ref, wup_ref, wdn_ref):
    xn = _rms(x1, g_ref[...]).astype(bf16)
    acc = None
    for c in range(D_FF // FF_CHUNK):
        cols = slice(c * FF_CHUNK, (c + 1) * FF_CHUNK)
        hid = _dot(xn, wup_ref[:, cols])
        hid = jnp.square(jnp.maximum(hid, 0.0)).astype(bf16)
        part = _dot(hid, wdn_ref[cols, :])
        acc = part if acc is None else acc + part
    return x1 + acc


def _inproj_body(x_ref, g_ref, w32_ref, wg_ref, bias_ref, q_ref, k_ref, v_ref, o_ref, gate_ref, w_ref):
    @pl.when(pl.program_id(0) == 0)
    def _():
        for c in range(0, W_QKVO_COLS, FF_CHUNK):
            w_ref[:, c:c + FF_CHUNK] = w32_ref[:, c:c + FF_CHUNK].astype(bf16)

    xn = _rms(x_ref[...], g_ref[...]).astype(bf16)
    q, k, v, o, logi, logf = _inproj_values(xn, w_ref, wg_ref, bias_ref)
    q_ref[...] = q
    k_ref[...] = k
    v_ref[...] = v
    o_ref[...] = o
    gate_ref[:, :LANES] = logi
    gate_ref[:, LANES:] = logf


def _inproj(x2d, g, w_in, w_gate, gate_bias):
    t = x2d.shape[0]
    row = lambda n: pl.BlockSpec((ROW_TILE, n), lambda i: (i, 0))
    return pl.pallas_call(
        _inproj_body,
        grid=(t // ROW_TILE,),
        in_specs=[row(D_MODEL), _const_spec((1, D_MODEL)), _const_spec(w_in.shape),
                  _const_spec(w_gate.shape), _const_spec((1, 2 * LANES))],
        out_specs=[row(QK_DIM), row(QK_DIM), row(V_DIM), row(V_DIM), row(2 * LANES)],
        out_shape=[jax.ShapeDtypeStruct((t, QK_DIM), bf16), jax.ShapeDtypeStruct((t, QK_DIM), f32),
                   jax.ShapeDtypeStruct((t, V_DIM), bf16), jax.ShapeDtypeStruct((t, V_DIM), f32),
                   jax.ShapeDtypeStruct((t, 2 * LANES), f32)],
        scratch_shapes=[pltpu.VMEM((D_MODEL, W_QKVO_COLS), bf16)],
        compiler_params=_params(1),
        name="l0_inproj",
    )(x2d, g, w_in, w_gate, gate_bias)


def _row_prefix(x, cs, rpos, is_max):
    sh = 1
    while sh < cs:
        prev = pltpu.roll(x, sh, 0)
        if is_max:
            x = jnp.maximum(x, jnp.where(rpos >= sh, prev, NEG_BIG))
        else:
            x = x + jnp.where(rpos >= sh, prev, 0.0)
        sh *= 2
    return x


def _seg_last(x, nseq, cs):
    if nseq == 1:
        return x[cs - 1:cs, :]
    last = x.reshape(nseq, cs, x.shape[1])[:, cs - 1:cs, :]
    return jnp.broadcast_to(last, (nseq, cs, x.shape[1])).reshape(nseq * cs, x.shape[1])


def _gate_algebra(logi, logf, m_in, nseq, cs, sel_ref):
    rows = nseq * cs
    rpos = lax.broadcasted_iota(jnp.int32, (rows, LANES), 0) & (cs - 1)
    b = _row_prefix(logf, cs, rpos, False)
    a = logi - b
    g = jnp.maximum(m_in, _row_prefix(a, cs, rpos, True))
    g_last = _seg_last(g, nseq, cs)
    w_state = jnp.exp(a - g_last)

    group = lax.broadcasted_iota(jnp.int32, (rows, LANES), 1) >> LOG_HEADS
    pieces = jnp.zeros((rows, LANES), f32)
    for i, val in enumerate((g, jnp.exp(m_in - g), jnp.exp(-(b + g)))):
        hi = val.astype(bf16).astype(f32)
        mid = (val - hi).astype(bf16).astype(f32)
        lo = (val - hi - mid).astype(bf16).astype(f32)
        for j, piece in enumerate((hi, mid, lo)):
            pieces = jnp.where(group == 3 * i + j, piece, pieces)
    bcast = _dot(pieces.astype(bf16), sel_ref[...])
    tile = lambda i, h: bcast[:, (i * N_HEADS + h) * DV:(i * N_HEADS + h + 1) * DV]
    return types.SimpleNamespace(
        a_t=a.T,
        w_state=w_state, w_state_t=w_state.T,
        decay=jnp.exp(m_in - g_last),
        m_new=_seg_last(b, nseq, cs) + g_last,
        g=lambda h: tile(0, h), w_inter=lambda h: tile(1, h), clamp=lambda h: tile(2, h))


def _head_outputs(qk, gates, inter, qn, v_h, h_ref, nseq, cs):
    rows = nseq * cs
    log_cs = cs.bit_length() - 1
    heads = range(N_HEADS)
    chunks = range(len(qk))
    ri = lax.broadcasted_iota(jnp.int32, (rows, rows), 0)
    ci = lax.broadcasted_iota(jnp.int32, (rows, rows), 1)
    causal = ci <= ri
    if nseq > 1:
        causal = jnp.logical_and(causal, (ri >> log_cs) == (ci >> log_cs))
    for c in chunks:
        for h in heads:
            g_rows = jnp.concatenate([gates[c].g(h)] * (rows // DV), axis=1)
            w = jnp.exp(jnp.where(causal, gates[c].a_t[h:h + 1, :] - g_rows, NEG_BIG))
            s = qk[c][h]() * w
            s_sum = jnp.sum(s, axis=1, keepdims=True)
            intra = _dot(s.astype(bf16), v_h[c][h])
            w_inter = gates[c].w_inter(h)
            num = w_inter * inter[c][h] + intra
            den = jnp.maximum(jnp.abs(w_inter * qn[c][h] + s_sum), gates[c].clamp(h))
            h_ref[c * rows:(c + 1) * rows, h * DV:(h + 1) * DV] = num * lax.rsqrt(
                jnp.mean(num * num, axis=1, keepdims=True) + EPS * (den * den))


def _sequence_chunks(q_ref, k_ref, v_ref, gate_ref, sel_ref, h_ref, c_state, n_state, m_in, n_chunks,
                     after_gates=lambda: None):
    rows = CHUNK_ROWS
    heads = range(N_HEADS)
    chunks = range(n_chunks)
    c_state, n_state = list(c_state), list(n_state)
    rsl = lambda c: slice(c * rows, (c + 1) * rows)
    q_h = [[q_ref[rsl(c), h * DK:(h + 1) * DK] for h in heads] for c in chunks]
    k_h = [[k_ref[rsl(c), h * DK:(h + 1) * DK] for h in heads] for c in chunks]
    v_h = [[v_ref[rsl(c), h * DV:(h + 1) * DV] for h in heads] for c in chunks]

    qk = [[functools.partial(_dot_nt, q_h[c][h], k_h[c][h].astype(bf16)) for h in heads] for c in chunks]

    gates = []
    for c in chunks:
        gates.append(_gate_algebra(gate_ref[rsl(c), :LANES], gate_ref[rsl(c), LANES:],
                                   m_in if c == 0 else gates[c - 1].m_new, 1, rows, sel_ref))
    after_gates()

    inter = [[None] * N_HEADS for _ in chunks]
    qn = [[None] * N_HEADS for _ in chunks]
    for c in chunks:
        ga = gates[c]
        k_t = k_ref[rsl(c), :].T
        for h in heads:
            rhs = jnp.concatenate([c_state[h].astype(bf16),
                                   jnp.broadcast_to(n_state[h], (DK, DV)).astype(bf16)], axis=1)
            inter2 = _dot(q_h[c][h], rhs)
            inter[c][h] = inter2[:, :DV]
            qn[c][h] = inter2[:, DV:]
            wk_t = k_t[h * DK:(h + 1) * DK, :] * ga.w_state_t[h:h + 1, :]
            dec = ga.decay[:, h:h + 1]
            c_state[h] = dec * c_state[h] + _dot(wk_t.astype(bf16), v_h[c][h])
            n_state[h] = dec * n_state[h] + jnp.sum(wk_t, axis=1, keepdims=True)

    _head_outputs(qk, gates, inter, qn, v_h, h_ref, 1, rows)
    return c_state, n_state, gates[-1].m_new


def _mlstm_sample_body(q_ref, k_ref, v_ref, gate_ref, sel_ref, c0_ref, n0_ref, m0_ref,
                       h_ref, cout_ref, nout_ref, mout_ref, *, nseq, cs):
    rows = nseq * cs
    log_cs = cs.bit_length() - 1
    log_dk = DK.bit_length() - 1
    heads = range(N_HEADS)
    q_h = [q_ref[:, h * DK:(h + 1) * DK] for h in heads]
    k_h = [k_ref[:, h * DK:(h + 1) * DK] for h in heads]
    v_h = [v_ref[:, h * DV:(h + 1) * DV] for h in heads]
    qk = [functools.partial(_dot_nt, q_h[h], k_h[h].astype(bf16)) for h in heads]
    m_in = jnp.broadcast_to(m0_ref[...][:, None, :], (nseq, cs, LANES)).reshape(rows, LANES)
    ga = _gate_algebra(gate_ref[:, :LANES], gate_ref[:, LANES:], m_in, nseq, cs, sel_ref)

    bd_q = ((lax.broadcasted_iota(jnp.int32, (rows, nseq * DK), 0) >> log_cs)
            == (lax.broadcasted_iota(jnp.int32, (rows, nseq * DK), 1) >> log_dk))
    bd_k = ((lax.broadcasted_iota(jnp.int32, (nseq * DK, rows), 0) >> log_dk)
            == (lax.broadcasted_iota(jnp.int32, (nseq * DK, rows), 1) >> log_cs))
    k_t = k_ref[...].T
    inter, qn = [], []
    for h in heads:
        c_prev = c0_ref[:, h].reshape(nseq * DK, DV)
        q32 = q_h[h].astype(f32)
        q_bd = jnp.where(bd_q, jnp.concatenate([q32] * nseq, axis=1), 0.0).astype(bf16)
        inter.append(_dot(q_bd, c_prev.astype(bf16)))
        n_prev = n0_ref[:, h * DK:(h + 1) * DK]
        n_rows = jnp.broadcast_to(n_prev[:, None, :], (nseq, cs, DK)).reshape(rows, DK)
        qn.append(jnp.sum(q32 * n_rows, axis=1, keepdims=True))

        wk_t = k_t[h * DK:(h + 1) * DK, :] * ga.w_state_t[h:h + 1, :]
        wk_bd = jnp.where(bd_k, jnp.broadcast_to(wk_t[None], (nseq, DK, rows)).reshape(nseq * DK, rows), 0.0)
        dc = _dot(wk_bd.astype(bf16), v_h[h])
        dec_col = ga.decay[:, h:h + 1]
        dec_rows = jnp.broadcast_to(dec_col.reshape(nseq, 1, cs, 1), (nseq, DK // cs, cs, 1)).reshape(nseq * DK, 1)
        cout_ref[:, h] = (dec_rows * c_prev + dc).reshape(nseq, DK, DV)
        dec_seq = jnp.max(dec_col.reshape(nseq, cs, 1), axis=1)
        nout_ref[:, h * DK:(h + 1) * DK] = dec_seq * n_prev + jnp.sum(
            (ga.w_state[:, h:h + 1] * k_h[h]).reshape(nseq, cs, DK), axis=1)
    mout_ref[...] = jnp.max(ga.m_new.reshape(nseq, cs, LANES), axis=1)
    _head_outputs([qk], [ga], [inter], [qn], [v_h], h_ref, nseq, cs)


def _inproj_values(xn, w_ref, wg_ref, bias_ref):
    q = _dot(xn, w_ref[:, 0:QK_DIM]).astype(bf16)
    k = _dot(xn, w_ref[:, QK_DIM:2 * QK_DIM]) * (DK ** -0.5)
    v = _dot(xn, w_ref[:, 2 * QK_DIM:2 * QK_DIM + V_DIM]).astype(bf16)
    o = _dot(xn, w_ref[:, 2 * QK_DIM + V_DIM:W_QKVO_COLS])
    gz = _dot(xn, wg_ref[...]) + bias_ref[...]
    return q, k, v, o, gz[:, :LANES], jax.nn.log_sigmoid(gz[:, LANES:])


def _l0_prompt_body(x_ref, g_ref, w32_hbm, wg_ref, bias_ref, sel_ref,
                    o_ref, h_ref, cout_ref, nout_ref, mout_ref,
                    w_scr, stage, dma_sem, qa, ka, va, ga, qb, kb, vb, gb, c_scr, n_scr, m_scr, *, nt, n_chunks):
    step = pl.program_id(0)
    heads = range(N_HEADS)

    @pl.when(step == 0)
    def _():
        for c in range(0, W_QKVO_COLS, FF_CHUNK):
            cp = pltpu.make_async_copy(w32_hbm.at[:, pl.ds(c, FF_CHUNK)], stage, dma_sem.at[0])
            cp.start()
            cp.wait()
            w_scr[:, c:c + FF_CHUNK] = stage[...].astype(bf16)
        for ref in (qb, kb, vb, gb, c_scr, n_scr, m_scr):
            ref[...] = jnp.zeros_like(ref)

    def half_step(wr, rd):
        def project():
            q_w, k_w, v_w, g_w = wr
            xn = _rms(x_ref[...], g_ref[...]).astype(bf16)
            q, k, v, o, logi, logf = _inproj_values(xn, w_scr, wg_ref, bias_ref)
            q_w[...] = q
            k_w[...] = k
            v_w[...] = v
            o_ref[...] = o
            g_w[:, :LANES] = logi
            g_w[:, LANES:] = logf

        first = lax.rem(step + nt - 1, nt) == 0
        c_state = [jnp.where(first, 0.0, c_scr[h]) for h in heads]
        n_state = [jnp.where(first, 0.0, n_scr[h]) for h in heads]
        m_in = jnp.where(first, 0.0, m_scr[...])
        c_state, n_state, m_new = _sequence_chunks(*rd, sel_ref, h_ref, c_state, n_state, m_in, n_chunks,
                                                   after_gates=project)
        for h in heads:
            c_scr[h] = c_state[h]
            n_scr[h] = n_state[h]
        m_scr[...] = m_new

    @pl.when(lax.rem(step, 2) == 0)
    def _():
        half_step((qa, ka, va, ga), (qb, kb, vb, gb))

    @pl.when(lax.rem(step, 2) == 1)
    def _():
        half_step((qb, kb, vb, gb), (qa, ka, va, ga))

    @pl.when(jnp.logical_and(step >= 1, lax.rem(step, nt) == 0))
    def _():
        cout_ref[0] = c_scr[...]
        lane = lax.broadcasted_iota(jnp.int32, (DK, LANES), 1)
        n_mat = jnp.zeros((DK, LANES), f32)
        for h in heads:
            n_mat = jnp.where(lane == h, jnp.broadcast_to(n_scr[h], (DK, LANES)), n_mat)
        nout_ref[0] = n_mat
        mout_ref[0] = m_scr[...]


def _bcast_selector():
    src = np.arange(LANES)
    dst = np.arange(N_BCAST * N_HEADS * DV) // DV
    hit = (((src[:, None] >> LOG_HEADS) // 3 == dst[None, :] // N_HEADS)
           & ((src[:, None] & (N_HEADS - 1)) == dst[None, :] % N_HEADS))
    return jnp.asarray(hit, dtype=bf16)


def _l0_prompt(x2d, g, w_in, w_gate, gate_bias, n_batch, seq_len):
    t = x2d.shape[0]
    sel = _bcast_selector()
    rows = PROMPT_CHUNKS * CHUNK_ROWS
    nt = seq_len // rows
    n_blocks = t // rows
    cur = lambda s: (jnp.minimum(s, n_blocks - 1), 0)
    prev = lambda s: (jnp.maximum(s - 1, 0), 0)
    seq = lambda s: jnp.maximum(s - 1, 0) // nt
    bufs = [pltpu.VMEM((rows, QK_DIM), bf16), pltpu.VMEM((rows, QK_DIM), f32),
            pltpu.VMEM((rows, V_DIM), bf16), pltpu.VMEM((rows, 2 * LANES), f32)]
    return pl.pallas_call(
        functools.partial(_l0_prompt_body, nt=nt, n_chunks=PROMPT_CHUNKS),
        grid=(n_blocks + 1,),
        in_specs=[pl.BlockSpec((rows, D_MODEL), cur), _const_spec((1, D_MODEL)), pl.BlockSpec(memory_space=pl.ANY),
                  _const_spec(w_gate.shape), _const_spec((1, 2 * LANES)), _const_spec(sel.shape)],
        out_specs=[pl.BlockSpec((rows, V_DIM), cur), pl.BlockSpec((rows, V_DIM), prev),
                   pl.BlockSpec((1, N_HEADS, DK, DV), lambda s: (seq(s), 0, 0, 0)),
                   pl.BlockSpec((1, DK, LANES), lambda s: (seq(s), 0, 0)),
                   pl.BlockSpec((1, 1, LANES), lambda s: (seq(s), 0, 0))],
        out_shape=[jax.ShapeDtypeStruct((t, V_DIM), f32), jax.ShapeDtypeStruct((t, V_DIM), f32),
                   jax.ShapeDtypeStruct((n_batch, N_HEADS, DK, DV), f32),
                   jax.ShapeDtypeStruct((n_batch, DK, LANES), f32),
                   jax.ShapeDtypeStruct((n_batch, 1, LANES), f32)],
        scratch_shapes=[pltpu.VMEM((D_MODEL, W_QKVO_COLS), bf16), pltpu.VMEM((D_MODEL, FF_CHUNK), f32),
                        pltpu.SemaphoreType.DMA((1,))] + bufs + bufs
                       + [pltpu.VMEM((N_HEADS, DK, DV), f32), pltpu.VMEM((N_HEADS, DK, 1), f32),
                          pltpu.VMEM((1, LANES), f32)],
        compiler_params=_params(1),
        name="l0_inproj_mlstm",
    )(x2d, g, w_in, w_gate, gate_bias, sel)


def _mlstm_core_sample(q, k, v, gates, c0, n0, m0, n_batch, seq_len):
    t = q.shape[0]
    sel = _bcast_selector()
    nseq = CHUNK_ROWS // seq_len
    row = lambda n: pl.BlockSpec((CHUNK_ROWS, n), lambda i: (i, 0))
    cspec = pl.BlockSpec((nseq, N_HEADS, DK, DV), lambda i: (i, 0, 0, 0))
    nspec = pl.BlockSpec((nseq, QK_DIM), lambda i: (i, 0))
    mspec = pl.BlockSpec((nseq, LANES), lambda i: (i, 0))
    return pl.pallas_call(
        functools.partial(_mlstm_sample_body, nseq=nseq, cs=seq_len),
        grid=(t // CHUNK_ROWS,),
        in_specs=[row(QK_DIM), row(QK_DIM), row(V_DIM), row(2 * LANES), _const_spec(sel.shape), cspec, nspec, mspec],
        out_specs=[row(V_DIM), cspec, nspec, mspec],
        out_shape=[jax.ShapeDtypeStruct((t, V_DIM), f32),
                   jax.ShapeDtypeStruct((n_batch, N_HEADS, DK, DV), f32),
                   jax.ShapeDtypeStruct((n_batch, QK_DIM), f32),
                   jax.ShapeDtypeStruct((n_batch, LANES), f32)],
        compiler_params=_params(1),
        name="l0_mlstm_core",
    )(q, k, v, gates, sel, c0, n0, m0)


def _outproj_ffn_body(h_ref, o_ref, x_ref, hg_ref, wout_ref, gffn_ref, wup_ref, wdn_ref, y_ref):
    hn = h_ref[...] * hg_ref[...] * jax.nn.sigmoid(o_ref[...])
    x1 = x_ref[...] + _dot(hn.astype(bf16), wout_ref[...])
    y_ref[...] = _ffn(x1, gffn_ref, wup_ref, wdn_ref)


def _outproj_ffn(hmix, o, x2d, head_g, w_out, g_ffn, w_up, w_down):
    t = x2d.shape[0]
    row = pl.BlockSpec((ROW_TILE, D_MODEL), lambda i: (i, 0))
    return pl.pallas_call(
        _outproj_ffn_body,
        grid=(t // ROW_TILE,),
        in_specs=[row, row, row, _const_spec((1, V_DIM)), _const_spec(w_out.shape),
                  _const_spec((1, D_MODEL)), _layer_spec(w_up.shape, 0), _layer_spec(w_down.shape, 0)],
        out_specs=row,
        out_shape=jax.ShapeDtypeStruct((t, D_MODEL), f32),
        compiler_params=_params(1),
        name="l0_outproj_ffn",
    )(hmix, o, x2d, head_g, w_out, g_ffn, w_up, w_down)


def _s5_prep_body(lre_ref, lim_ref, ldt_ref, bre_ref, bim_ref, cre_ref, cim_ref,
                  are_ref, aim_ref, wbr_ref, wbi_ref, vre_ref, vim_ref):
    lr = lre_ref[...]
    li = lim_ref[...]
    dt = jnp.exp(ldt_ref[...])
    mag = jnp.exp(lr * dt)
    a_re = mag * jnp.cos(li * dt)
    a_im = mag * jnp.sin(li * dt)
    den = lr * lr + li * li
    z_re = a_re - 1.0
    coef_re = ((z_re * lr + a_im * li) / den)[:, None, :]
    coef_im = ((a_im * lr - z_re * li) / den)[:, None, :]
    br = bre_ref[...]
    bi = bim_ref[...]
    are_ref[...] = a_re
    aim_ref[...] = a_im
    bb = (coef_re * br - coef_im * bi, coef_re * bi + coef_im * br)
    c_t = (cre_ref[...].reshape(S5_GROUPS * S5_GROUP, S5_STATE).T,
           cim_ref[...].reshape(S5_GROUPS * S5_GROUP, S5_STATE).T)

    groups = S5_BLOCK_CH // S5_GROUP
    log_ch, log_st = S5_GROUP.bit_length() - 1, S5_STATE.bit_length() - 1
    shape_b = (S5_BLOCK_CH, S5_BLOCK_ST)
    diag_b = ((lax.broadcasted_iota(jnp.int32, shape_b, 0) >> log_ch)
              == (lax.broadcasted_iota(jnp.int32, shape_b, 1) >> log_st))
    shape_c = (S5_BLOCK_ST, S5_BLOCK_CH)
    diag_c = ((lax.broadcasted_iota(jnp.int32, shape_c, 0) >> log_st)
              == (lax.broadcasted_iota(jnp.int32, shape_c, 1) >> log_ch))
    for j in range(N_S5_BLOCKS):
        for src, dst in zip(bb, (wbr_ref, wbi_ref)):
            blk = src[j * groups:(j + 1) * groups].reshape(S5_BLOCK_CH, S5_STATE)
            dst[j] = jnp.where(diag_b, jnp.concatenate([blk] * groups, axis=1), 0.0).astype(bf16)
        for src, dst in zip(c_t, (vre_ref, vim_ref)):
            blk = src[:, j * S5_BLOCK_CH:(j + 1) * S5_BLOCK_CH]
            rep = jnp.broadcast_to(blk[None], (groups, S5_STATE, S5_BLOCK_CH)).reshape(S5_BLOCK_ST, S5_BLOCK_CH)
            dst[j] = jnp.where(diag_c, rep, 0.0).astype(bf16)


def _s5_prep(lam_re, lam_im, log_dt, b_re_t, b_im_t, c_re, c_im):
    gp = jax.ShapeDtypeStruct((S5_GROUPS, S5_STATE), f32)
    wb = jax.ShapeDtypeStruct((N_S5_BLOCKS, S5_BLOCK_CH, S5_BLOCK_ST), bf16)
    vc = jax.ShapeDtypeStruct((N_S5_BLOCKS, S5_BLOCK_ST, S5_BLOCK_CH), bf16)
    return pl.pallas_call(_s5_prep_body, out_shape=[gp, gp, wb, wb, vc, vc], name="l1_s5_prep")(
        lam_re, lam_im, log_dt, b_re_t, b_im_t, c_re, c_im)


def _s5_body(x_ref, h0r_ref, h0i_ref, g_ref, are_ref, aim_ref, wbr_ref, wbi_ref, vre_ref, vim_ref,
             d_ref, wglu_ref, gffn_ref, wup_ref, wdn_ref, gfin_ref,
             y_ref, sre_ref, sim_ref, sr_scr, si_scr, hr_scr, hi_scr, act_scr, *io_scr, bt, lt, dma_io):
    rows = bt * lt
    step = pl.program_id(1)
    n_steps = pl.num_programs(1)

    @pl.when(step == 0)
    def _():
        sr_scr[...] = h0r_ref[...]
        si_scr[...] = h0i_ref[...]

    if dma_io:
        xbuf, ybuf, in_sem, out_sem = io_scr
        slot = lax.rem(step, 2)

        def in_copy(s, sl, b):
            return pltpu.make_async_copy(x_ref.at[b, pl.ds(s * lt, lt), :], xbuf.at[sl, :, b, :], in_sem.at[sl, b])

        def out_copy(s, sl, b):
            return pltpu.make_async_copy(ybuf.at[sl, :, b, :], y_ref.at[b, pl.ds(s * lt, lt), :], out_sem.at[sl, b])

        @pl.when(step == 0)
        def _():
            for b in range(bt):
                in_copy(0, 0, b).start()

        @pl.when(step + 1 < n_steps)
        def _():
            for b in range(bt):
                in_copy(step + 1, 1 - slot, b).start()

        for b in range(bt):
            in_copy(step, slot, b).wait()
        xt = xbuf[slot].reshape(rows, D_MODEL)
    else:
        xt = jnp.concatenate([x_ref[:, t, :] for t in range(lt)], axis=0)
    u = _rms(xt, g_ref[...])
    ub = u.astype(bf16)

    def project_in(j):
        ch = slice(j * S5_BLOCK_CH, (j + 1) * S5_BLOCK_CH)
        hr_scr[j % 2] = _dot(ub[:, ch], wbr_ref[j])
        hi_scr[j % 2] = _dot(ub[:, ch], wbi_ref[j])

    project_in(0)
    for j in range(N_S5_BLOCKS):
        ch = slice(j * S5_BLOCK_CH, (j + 1) * S5_BLOCK_CH)
        st = slice(j * S5_BLOCK_ST, (j + 1) * S5_BLOCK_ST)
        if j + 1 < N_S5_BLOCKS:
            project_in(j + 1)
        a_re = jnp.broadcast_to(are_ref[:, st], (bt, S5_BLOCK_ST))
        a_im = jnp.broadcast_to(aim_ref[:, st], (bt, S5_BLOCK_ST))
        s_re = sr_scr[:, st]
        s_im = si_scr[:, st]
        for t in range(lt):
            r = slice(t * bt, (t + 1) * bt)
            n_re = a_re * s_re - a_im * s_im + hr_scr[j % 2, r, :]
            n_im = a_re * s_im + a_im * s_re + hi_scr[j % 2, r, :]
            hr_scr[j % 2, r, :] = n_re
            hi_scr[j % 2, r, :] = n_im
            s_re, s_im = n_re, n_im
        sr_scr[:, st] = s_re
        si_scr[:, st] = s_im
        yj = (_dot(hr_scr[j % 2].astype(bf16), vre_ref[j]) - _dot(hi_scr[j % 2].astype(bf16), vim_ref[j])
              + d_ref[:, ch] * u[:, ch])
        act_scr[:, ch] = jax.nn.gelu(yj).astype(bf16)

    sre_ref[...] = sr_scr[...]
    sim_ref[...] = si_scr[...]
    ag = _dot(act_scr[...], wglu_ref[...])
    x3 = xt + ag[:, :D_MODEL] * jax.nn.sigmoid(ag[:, D_MODEL:])
    x4 = _ffn(x3, gffn_ref, wup_ref, wdn_ref)
    y = _rms(x4, gfin_ref[...])
    if dma_io:
        @pl.when(step >= 2)
        def _():
            for b in range(bt):
                out_copy(step - 2, slot, b).wait()

        ybuf[slot] = y.reshape(lt, bt, D_MODEL)
        for b in range(bt):
            out_copy(step, slot, b).start()

        @pl.when(step == n_steps - 1)
        def _():
            @pl.when(step >= 1)
            def _():
                for b in range(bt):
                    out_copy(step - 1, 1 - slot, b).wait()

            for b in range(bt):
                out_copy(step, slot, b).wait()
    else:
        for t in range(lt):
            y_ref[:, t, :] = y[t * bt:(t + 1) * bt, :]


def _s5_layer(x3d, h0_re, h0_im, bt, lt, g_mix, a_re, a_im, wb_re, wb_im, v_re, v_im, d_skip,
              w_glu, g_ffn, w_up, w_down, g_final):
    n_batch, seq_len, _ = x3d.shape
    grid = (n_batch // bt, seq_len // lt)
    sspec = pl.BlockSpec((bt, S5_N), lambda i, t: (i, 0))
    dma_io = n_batch == bt
    if dma_io:
        xspec = pl.BlockSpec(memory_space=pl.ANY)
        io_scratch = [pltpu.VMEM((2, lt, bt, D_MODEL), f32), pltpu.VMEM((2, lt, bt, D_MODEL), f32),
                      pltpu.SemaphoreType.DMA((2, bt)), pltpu.SemaphoreType.DMA((2, bt))]
    else:
        xspec = pl.BlockSpec((bt, lt, D_MODEL), lambda i, t: (i, t, 0))
        io_scratch = []
    consts = (g_mix, a_re, a_im, wb_re, wb_im, v_re, v_im, d_skip, w_glu, g_ffn, w_up, w_down, g_final)
    const_specs = [_layer_spec(c.shape, 1) if c is w_up or c is w_down else _const_spec(c.shape) for c in consts]
    rows = bt * lt
    return pl.pallas_call(
        functools.partial(_s5_body, bt=bt, lt=lt, dma_io=dma_io),
        grid=grid,
        in_specs=[xspec, sspec, sspec] + const_specs,
        out_specs=[xspec, sspec, sspec],
        out_shape=[jax.ShapeDtypeStruct(x3d.shape, f32), jax.ShapeDtypeStruct((n_batch, S5_N), f32),
                   jax.ShapeDtypeStruct((n_batch, S5_N), f32)],
        scratch_shapes=[pltpu.VMEM((bt, S5_N), f32), pltpu.VMEM((bt, S5_N), f32),
                        pltpu.VMEM((2, rows, S5_BLOCK_ST), f32), pltpu.VMEM((2, rows, S5_BLOCK_ST), f32),
                        pltpu.VMEM((rows, D_MODEL), bf16)] + io_scratch,
        compiler_params=_params(2),
        name="l1_s5_ffn",
    )(x3d, h0_re, h0_im, *consts)


def _rep_lanes(x):
    return jnp.tile(x, (1,) * (x.ndim - 1) + (LANES // x.shape[-1],))


def _trunk(x, state, p):
    n_batch, seq_len, _ = x.shape
    x2d = x.reshape(n_batch * seq_len, D_MODEL)
    if state is None:
        o, hmix, c_new, n_new, m_new = _l0_prompt(x2d, p["g_mix0"], p["w_in"], p["w_gate"], p["gate_bias"],
                                                  n_batch, seq_len)
        n_new = jnp.swapaxes(n_new[:, :, :N_HEADS], 1, 2)
        h0_re = jnp.zeros((n_batch, S5_N), f32)
        h0_im = h0_re
        bt, lt = n_batch, 512 // n_batch
    else:
        c0, n0, m0, h0_re, h0_im = state
        q, k, v, o, gates = _inproj(x2d, p["g_mix0"], p["w_in"], p["w_gate"], p["gate_bias"])
        hmix, c_new, n_new, m_new = _mlstm_core_sample(
            q, k, v, gates, c0, n0.reshape(n_batch, QK_DIM), _rep_lanes(m0), n_batch, seq_len)
        h0_re = h0_re.reshape(n_batch, S5_N)
        h0_im = h0_im.reshape(n_batch, S5_N)
        bt, lt = 512 // seq_len, seq_len
    x2 = _outproj_ffn(hmix, o, x2d, p["head_g"], p["w_out"], p["g_ffn0"], p["w_up"], p["w_down"])
    y, s_re, s_im = _s5_layer(x2.reshape(n_batch, seq_len, D_MODEL), h0_re, h0_im, bt, lt,
                              p["g_mix1"], p["a_re"], p["a_im"], p["wb_re"], p["wb_im"], p["v_re"], p["v_im"],
                              p["d_skip"], p["w_glu"], p["g_ffn1"], p["w_up"], p["w_down"], p["g_final"])
    return (y,
            c_new.reshape(1, n_batch, N_HEADS, DK, DV),
            n_new.reshape(1, n_batch, N_HEADS, DK),
            m_new.reshape(n_batch, LANES)[:, :N_HEADS].reshape(1, n_batch, N_HEADS),
            s_re.reshape(1, n_batch, S5_GROUPS, S5_STATE),
            s_im.reshape(1, n_batch, S5_GROUPS, S5_STATE))


def kernel(x_prompt, x_sample, state_mlstm_C, state_mlstm_n, state_mlstm_m, state_s5_re, state_s5_im,
           norm_mix_g, norm_ffn_g, norm_final_g, mlstm_w_in, mlstm_b_i, mlstm_b_f, mlstm_head_norm_g,
           mlstm_w_out, s5_lambda_re, s5_lambda_im, s5_log_dt, s5_b_re, s5_b_im, s5_c_re, s5_c_im,
           s5_d, s5_w_glu, ffn_w_up, ffn_w_down):
    w_in = mlstm_w_in[0]
    n_qkvo = W_QKVO_COLS
    a_re, a_im, wb_re, wb_im, v_re, v_im = _s5_prep(
        s5_lambda_re[0], s5_lambda_im[0], s5_log_dt[0].reshape(S5_GROUPS, 1),
        jnp.swapaxes(s5_b_re[0], 1, 2), jnp.swapaxes(s5_b_im[0], 1, 2), s5_c_re[0], s5_c_im[0])
    p = dict(
        g_mix0=norm_mix_g[0].reshape(1, D_MODEL), g_mix1=norm_mix_g[1].reshape(1, D_MODEL),
        g_ffn0=norm_ffn_g[0].reshape(1, D_MODEL), g_ffn1=norm_ffn_g[1].reshape(1, D_MODEL),
        g_final=norm_final_g.reshape(1, D_MODEL),
        w_in=w_in,
        w_gate=jnp.concatenate([_rep_lanes(w_in[:, n_qkvo:n_qkvo + N_HEADS]),
                                _rep_lanes(w_in[:, n_qkvo + N_HEADS:])], axis=1).astype(bf16),
        gate_bias=jnp.concatenate([_rep_lanes(mlstm_b_i[0][None]), _rep_lanes(mlstm_b_f[0][None])], axis=1),
        head_g=mlstm_head_norm_g[0].reshape(1, V_DIM),
        w_out=mlstm_w_out[0].astype(bf16),
        w_up=ffn_w_up.astype(bf16), w_down=ffn_w_down.astype(bf16),
        a_re=a_re.reshape(1, S5_N), a_im=a_im.reshape(1, S5_N),
        wb_re=wb_re, wb_im=wb_im, v_re=v_re, v_im=v_im,
        d_skip=s5_d[0].reshape(1, D_MODEL),
        w_glu=s5_w_glu[0].astype(bf16),
    )
    prompt = _trunk(x_prompt, None, p)
    sample = _trunk(x_sample, (state_mlstm_C[0], state_mlstm_n[0], state_mlstm_m[0],
                               state_s5_re[0], state_s5_im[0]), p)
    return (prompt[0], sample[0]) + prompt[1:] + sample[1:]
```

```python
import functools
import types

import jax
import jax.numpy as jnp
import numpy as np
from jax import lax
from jax.experimental import pallas as pl
from jax.experimental.pallas import tpu as pltpu

f32 = jnp.float32
bf16 = jnp.bfloat16

D_MODEL = 1024
N_HEADS = 8
DK = 64
DV = 128
QK_DIM = N_HEADS * DK
V_DIM = N_HEADS * DV
W_QKVO_COLS = 2 * QK_DIM + 2 * V_DIM
D_FF = 4 * D_MODEL
S5_GROUPS = 64
S5_GROUP = 16
S5_STATE = 64
S5_N = S5_GROUPS * S5_STATE
EPS = 1e-6

LANES = 128
SUBLANES = 8
ROW_TILE = 512
CHUNK_ROWS = 256
PROMPT_CHUNKS = 2
FF_CHUNK = 1024
ROW_SPLITS = 2
CAST_TILE = (512, 1024)
S5_BLOCK_CH = LANES
S5_BLOCK_ST = S5_BLOCK_CH // S5_GROUP * S5_STATE
N_S5_BLOCKS = D_MODEL // S5_BLOCK_CH
VMEM_LIMIT_BYTES = 56 * 1024 * 1024
NEG_BIG = -1e30
LOG_HEADS = N_HEADS.bit_length() - 1
N_BCAST = 3


def _params(n_axes):
    return pltpu.CompilerParams(dimension_semantics=("arbitrary",) * n_axes,
                                vmem_limit_bytes=VMEM_LIMIT_BYTES)


def _const_spec(shape):
    nd = len(shape)
    return pl.BlockSpec(shape, lambda *_: (0,) * nd, pipeline_mode=pl.Buffered(1))


def _dot(a, b):
    return jnp.dot(a, b, preferred_element_type=f32)


def _dot_nt(a, b):
    return lax.dot_general(a, b, (((1,), (1,)), ((), ())), preferred_element_type=f32)


def _rms(x, g):
    return x * lax.rsqrt(jnp.mean(x * x, axis=-1, keepdims=True) + EPS) * g


def _row_splits(rows):
    size = rows // ROW_SPLITS
    return [slice(i * size, (i + 1) * size) for i in range(ROW_SPLITS)]


def _ffn(x1s, g_ref, wup_ref, wdn_ref):
    xn = [_rms(x1, g_ref[...]).astype(bf16) for x1 in x1s]
    acc = [None] * len(x1s)
    for c in range(D_FF // FF_CHUNK):
        cols = slice(c * FF_CHUNK, (c + 1) * FF_CHUNK)
        for i in range(len(x1s)):
            hid = _dot(xn[i], wup_ref[:, cols])
            hid = jnp.square(jnp.maximum(hid, 0.0)).astype(bf16)
            part = _dot(hid, wdn_ref[cols, :])
            acc[i] = part if acc[i] is None else acc[i] + part
    return [x1 + a for x1, a in zip(x1s, acc)]


def _inproj_body(x_ref, g_ref, w32_ref, wg_ref, bias_ref, q_ref, k_ref, v_ref, o_ref, gate_ref, w_ref):
    @pl.when(pl.program_id(0) == 0)
    def _():
        for c in range(0, W_QKVO_COLS, FF_CHUNK):
            w_ref[:, c:c + FF_CHUNK] = w32_ref[:, c:c + FF_CHUNK].astype(bf16)

    xn = _rms(x_ref[...], g_ref[...]).astype(bf16)
    q, k, v, o, logi, logf = _inproj_values(xn, w_ref, wg_ref, bias_ref)
    q_ref[...] = q
    k_ref[...] = k
    v_ref[...] = v
    o_ref[...] = o
    gate_ref[:, :LANES] = logi
    gate_ref[:, LANES:] = logf


def _inproj(x2d, g, w_in, w_gate, gate_bias):
    t = x2d.shape[0]
    row = lambda n: pl.BlockSpec((ROW_TILE, n), lambda i: (i, 0))
    return pl.pallas_call(
        _inproj_body,
        grid=(t // ROW_TILE,),
        in_specs=[row(D_MODEL), _const_spec((1, D_MODEL)), _const_spec(w_in.shape),
                  _const_spec(w_gate.shape), _const_spec((1, 2 * LANES))],
        out_specs=[row(QK_DIM), row(QK_DIM), row(V_DIM), row(V_DIM), row(2 * LANES)],
        out_shape=[jax.ShapeDtypeStruct((t, QK_DIM), bf16), jax.ShapeDtypeStruct((t, QK_DIM), f32),
                   jax.ShapeDtypeStruct((t, V_DIM), bf16), jax.ShapeDtypeStruct((t, V_DIM), f32),
                   jax.ShapeDtypeStruct((t, 2 * LANES), f32)],
        scratch_shapes=[pltpu.VMEM((D_MODEL, W_QKVO_COLS), bf16)],
        compiler_params=_params(1),
        name="l0_inproj",
    )(x2d, g, w_in, w_gate, gate_bias)


def _row_prefix(x, cs, rpos, is_max):
    sh = 1
    while sh < cs:
        prev = pltpu.roll(x, sh, 0)
        if is_max:
            x = jnp.maximum(x, jnp.where(rpos >= sh, prev, NEG_BIG))
        else:
            x = x + jnp.where(rpos >= sh, prev, 0.0)
        sh *= 2
    return x


def _seg_last(x, nseq, cs):
    if nseq == 1:
        return x[cs - 1:cs, :]
    last = x.reshape(nseq, cs, x.shape[1])[:, cs - 1:cs, :]
    return jnp.broadcast_to(last, (nseq, cs, x.shape[1])).reshape(nseq * cs, x.shape[1])


def _gate_algebra(logi, logf, m_in, nseq, cs, sel_ref):
    rows = nseq * cs
    rpos = lax.broadcasted_iota(jnp.int32, (rows, LANES), 0) & (cs - 1)
    b = _row_prefix(logf, cs, rpos, False)
    a = logi - b
    g = jnp.maximum(m_in, _row_prefix(a, cs, rpos, True))
    g_last = _seg_last(g, nseq, cs)
    w_state = jnp.exp(a - g_last)

    group = lax.broadcasted_iota(jnp.int32, (rows, LANES), 1) >> LOG_HEADS
    pieces = jnp.zeros((rows, LANES), f32)
    for i, val in enumerate((g, jnp.exp(m_in - g), jnp.exp(-(b + g)))):
        hi = val.astype(bf16).astype(f32)
        mid = (val - hi).astype(bf16).astype(f32)
        lo = (val - hi - mid).astype(bf16).astype(f32)
        for j, piece in enumerate((hi, mid, lo)):
            pieces = jnp.where(group == 3 * i + j, piece, pieces)
    bcast = _dot(pieces.astype(bf16), sel_ref[...])
    tile = lambda i, h: bcast[:, (i * N_HEADS + h) * DV:(i * N_HEADS + h + 1) * DV]
    return types.SimpleNamespace(
        a_t=a.T,
        w_state=w_state, w_state_t=w_state.T,
        decay=jnp.exp(m_in - g_last),
        m_new=_seg_last(b, nseq, cs) + g_last,
        g=lambda h: tile(0, h), w_inter=lambda h: tile(1, h), clamp=lambda h: tile(2, h))


def _head_outputs(qk, gates, inter, qn, v_h, h_ref, nseq, cs):
    rows = nseq * cs
    log_cs = cs.bit_length() - 1
    heads = range(N_HEADS)
    chunks = range(len(qk))
    ri = lax.broadcasted_iota(jnp.int32, (rows, rows), 0)
    ci = lax.broadcasted_iota(jnp.int32, (rows, rows), 1)
    causal = ci <= ri
    if nseq > 1:
        causal = jnp.logical_and(causal, (ri >> log_cs) == (ci >> log_cs))
    for c in chunks:
        for h in heads:
            g_rows = jnp.concatenate([gates[c].g(h)] * (rows // DV), axis=1)
            w = jnp.exp(jnp.where(causal, gates[c].a_t[h:h + 1, :] - g_rows, NEG_BIG))
            s = qk[c][h]() * w
            s_sum = jnp.sum(s, axis=1, keepdims=True)
            intra = _dot(s.astype(bf16), v_h[c][h])
            w_inter = gates[c].w_inter(h)
            num = w_inter * inter[c][h] + intra
            den = jnp.maximum(jnp.abs(w_inter * qn[c][h] + s_sum), gates[c].clamp(h))
            h_ref[c * rows:(c + 1) * rows, h * DV:(h + 1) * DV] = num * lax.rsqrt(
                jnp.mean(num * num, axis=1, keepdims=True) + EPS * (den * den))


def _sequence_chunks(q_ref, k_ref, v_ref, gate_ref, sel_ref, h_ref, c_state, n_state, m_in, n_chunks,
                     after_gates=lambda: None):
    rows = CHUNK_ROWS
    heads = range(N_HEADS)
    chunks = range(n_chunks)
    c_state, n_state = list(c_state), list(n_state)
    rsl = lambda c: slice(c * rows, (c + 1) * rows)
    q_h = [[q_ref[rsl(c), h * DK:(h + 1) * DK] for h in heads] for c in chunks]
    k_h = [[k_ref[rsl(c), h * DK:(h + 1) * DK] for h in heads] for c in chunks]
    v_h = [[v_ref[rsl(c), h * DV:(h + 1) * DV] for h in heads] for c in chunks]

    qk = [[functools.partial(_dot_nt, q_h[c][h], k_h[c][h].astype(bf16)) for h in heads] for c in chunks]

    gates = []
    for c in chunks:
        gates.append(_gate_algebra(gate_ref[rsl(c), :LANES], gate_ref[rsl(c), LANES:],
                                   m_in if c == 0 else gates[c - 1].m_new, 1, rows, sel_ref))
    after_gates()

    inter = [[None] * N_HEADS for _ in chunks]
    qn = [[None] * N_HEADS for _ in chunks]
    for c in chunks:
        ga = gates[c]
        k_t = k_ref[rsl(c), :].T
        for h in heads:
            rhs = jnp.concatenate([c_state[h].astype(bf16),
                                   jnp.broadcast_to(n_state[h], (DK, DV)).astype(bf16)], axis=1)
            inter2 = _dot(q_h[c][h], rhs)
            inter[c][h] = inter2[:, :DV]
            qn[c][h] = inter2[:, DV:]
            wk_t = k_t[h * DK:(h + 1) * DK, :] * ga.w_state_t[h:h + 1, :]
            dec = ga.decay[:, h:h + 1]
            c_state[h] = dec * c_state[h] + _dot(wk_t.astype(bf16), v_h[c][h])
            n_state[h] = dec * n_state[h] + jnp.sum(wk_t, axis=1, keepdims=True)

    _head_outputs(qk, gates, inter, qn, v_h, h_ref, 1, rows)
    return c_state, n_state, gates[-1].m_new


def _mlstm_sample_body(q_ref, k_ref, v_ref, gate_ref, sel_ref, c0_ref, n0_ref, m0_ref,
                       h_ref, cout_ref, nout_ref, mout_ref, *, nseq, cs):
    rows = nseq * cs
    log_cs = cs.bit_length() - 1
    log_dk = DK.bit_length() - 1
    heads = range(N_HEADS)
    q_h = [q_ref[:, h * DK:(h + 1) * DK] for h in heads]
    k_h = [k_ref[:, h * DK:(h + 1) * DK] for h in heads]
    v_h = [v_ref[:, h * DV:(h + 1) * DV] for h in heads]
    qk = [functools.partial(_dot_nt, q_h[h], k_h[h].astype(bf16)) for h in heads]
    m_in = jnp.broadcast_to(m0_ref[...][:, None, :], (nseq, cs, LANES)).reshape(rows, LANES)
    ga = _gate_algebra(gate_ref[:, :LANES], gate_ref[:, LANES:], m_in, nseq, cs, sel_ref)

    bd_q = ((lax.broadcasted_iota(jnp.int32, (rows, nseq * DK), 0) >> log_cs)
            == (lax.broadcasted_iota(jnp.int32, (rows, nseq * DK), 1) >> log_dk))
    bd_k = ((lax.broadcasted_iota(jnp.int32, (nseq * DK, rows), 0) >> log_dk)
            == (lax.broadcasted_iota(jnp.int32, (nseq * DK, rows), 1) >> log_cs))
    k_t = k_ref[...].T
    inter, qn = [], []
    for h in heads:
        c_prev = c0_ref[:, h].reshape(nseq * DK, DV)
        q32 = q_h[h].astype(f32)
        q_bd = jnp.where(bd_q, jnp.concatenate([q32] * nseq, axis=1), 0.0).astype(bf16)
        inter.append(_dot(q_bd, c_prev.astype(bf16)))
        n_prev = n0_ref[:, h * DK:(h + 1) * DK]
        n_rows = jnp.broadcast_to(n_prev[:, None, :], (nseq, cs, DK)).reshape(rows, DK)
        qn.append(jnp.sum(q32 * n_rows, axis=1, keepdims=True))

        wk_t = k_t[h * DK:(h + 1) * DK, :] * ga.w_state_t[h:h + 1, :]
        wk_bd = jnp.where(bd_k, jnp.broadcast_to(wk_t[None], (nseq, DK, rows)).reshape(nseq * DK, rows), 0.0)
        dc = _dot(wk_bd.astype(bf16), v_h[h])
        dec_col = ga.decay[:, h:h + 1]
        dec_rows = jnp.broadcast_to(dec_col.reshape(nseq, 1, cs, 1), (nseq, DK // cs, cs, 1)).reshape(nseq * DK, 1)
        cout_ref[:, h] = (dec_rows * c_prev + dc).reshape(nseq, DK, DV)
        dec_seq = jnp.max(dec_col.reshape(nseq, cs, 1), axis=1)
        nout_ref[:, h * DK:(h + 1) * DK] = dec_seq * n_prev + jnp.sum(
            (ga.w_state[:, h:h + 1] * k_h[h]).reshape(nseq, cs, DK), axis=1)
    mout_ref[...] = jnp.max(ga.m_new.reshape(nseq, cs, LANES), axis=1)
    _head_outputs([qk], [ga], [inter], [qn], [v_h], h_ref, nseq, cs)


def _inproj_values(xn, w_ref, wg_ref, bias_ref):
    q = _dot(xn, w_ref[:, 0:QK_DIM]).astype(bf16)
    k = _dot(xn, w_ref[:, QK_DIM:2 * QK_DIM]) * (DK ** -0.5)
    v = _dot(xn, w_ref[:, 2 * QK_DIM:2 * QK_DIM + V_DIM]).astype(bf16)
    o = _dot(xn, w_ref[:, 2 * QK_DIM + V_DIM:W_QKVO_COLS])
    gz = _dot(xn, wg_ref[...]) + bias_ref[...]
    return q, k, v, o, gz[:, :LANES], jax.nn.log_sigmoid(gz[:, LANES:])


def _l0_prompt_body(x_ref, g_ref, w32_hbm, wg_ref, bias_ref, sel_ref,
                    o_ref, h_ref, cout_ref, nout_ref, mout_ref,
                    w_scr, stage, dma_sem, qa, ka, va, ga, qb, kb, vb, gb, c_scr, n_scr, m_scr, *, nt, n_chunks):
    step = pl.program_id(0)
    heads = range(N_HEADS)

    @pl.when(step == 0)
    def _():
        for c in range(0, W_QKVO_COLS, FF_CHUNK):
            cp = pltpu.make_async_copy(w32_hbm.at[:, pl.ds(c, FF_CHUNK)], stage, dma_sem.at[0])
            cp.start()
            cp.wait()
            w_scr[:, c:c + FF_CHUNK] = stage[...].astype(bf16)
        for ref in (qb, kb, vb, gb, c_scr, n_scr, m_scr):
            ref[...] = jnp.zeros_like(ref)

    def half_step(wr, rd):
        def project():
            q_w, k_w, v_w, g_w = wr
            xn = _rms(x_ref[...], g_ref[...]).astype(bf16)
            q, k, v, o, logi, logf = _inproj_values(xn, w_scr, wg_ref, bias_ref)
            q_w[...] = q
            k_w[...] = k
            v_w[...] = v
            o_ref[...] = o
            g_w[:, :LANES] = logi
            g_w[:, LANES:] = logf

        first = lax.rem(step + nt - 1, nt) == 0
        c_state = [jnp.where(first, 0.0, c_scr[h]) for h in heads]
        n_state = [jnp.where(first, 0.0, n_scr[h]) for h in heads]
        m_in = jnp.where(first, 0.0, m_scr[...])
        c_state, n_state, m_new = _sequence_chunks(*rd, sel_ref, h_ref, c_state, n_state, m_in, n_chunks,
                                                   after_gates=project)
        for h in heads:
            c_scr[h] = c_state[h]
            n_scr[h] = n_state[h]
        m_scr[...] = m_new

    @pl.when(lax.rem(step, 2) == 0)
    def _():
        half_step((qa, ka, va, ga), (qb, kb, vb, gb))

    @pl.when(lax.rem(step, 2) == 1)
    def _():
        half_step((qb, kb, vb, gb), (qa, ka, va, ga))

    @pl.when(jnp.logical_and(step >= 1, lax.rem(step, nt) == 0))
    def _():
        cout_ref[0] = c_scr[...]
        lane = lax.broadcasted_iota(jnp.int32, (DK, LANES), 1)
        n_mat = jnp.zeros((DK, LANES), f32)
        for h in heads:
            n_mat = jnp.where(lane == h, jnp.broadcast_to(n_scr[h], (DK, LANES)), n_mat)
        nout_ref[0] = n_mat
        mout_ref[0] = m_scr[...]


def _bcast_selector():
    src = np.arange(LANES)
    dst = np.arange(N_BCAST * N_HEADS * DV) // DV
    hit = (((src[:, None] >> LOG_HEADS) // 3 == dst[None, :] // N_HEADS)
           & ((src[:, None] & (N_HEADS - 1)) == dst[None, :] % N_HEADS))
    return jnp.asarray(hit, dtype=bf16)


def _l0_prompt(x2d, g, w_in, w_gate, gate_bias, n_batch, seq_len):
    t = x2d.shape[0]
    sel = _bcast_selector()
    rows = PROMPT_CHUNKS * CHUNK_ROWS
    nt = seq_len // rows
    n_blocks = t // rows
    cur = lambda s: (jnp.minimum(s, n_blocks - 1), 0)
    prev = lambda s: (jnp.maximum(s - 1, 0), 0)
    seq = lambda s: jnp.maximum(s - 1, 0) // nt
    bufs = [pltpu.VMEM((rows, QK_DIM), bf16), pltpu.VMEM((rows, QK_DIM), f32),
            pltpu.VMEM((rows, V_DIM), bf16), pltpu.VMEM((rows, 2 * LANES), f32)]
    return pl.pallas_call(
        functools.partial(_l0_prompt_body, nt=nt, n_chunks=PROMPT_CHUNKS),
        grid=(n_blocks + 1,),
        in_specs=[pl.BlockSpec((rows, D_MODEL), cur), _const_spec((1, D_MODEL)), pl.BlockSpec(memory_space=pl.ANY),
                  _const_spec(w_gate.shape), _const_spec((1, 2 * LANES)), _const_spec(sel.shape)],
        out_specs=[pl.BlockSpec((rows, V_DIM), cur), pl.BlockSpec((rows, V_DIM), prev),
                   pl.BlockSpec((1, N_HEADS, DK, DV), lambda s: (seq(s), 0, 0, 0)),
                   pl.BlockSpec((1, DK, LANES), lambda s: (seq(s), 0, 0)),
                   pl.BlockSpec((1, 1, LANES), lambda s: (seq(s), 0, 0))],
        out_shape=[jax.ShapeDtypeStruct((t, V_DIM), f32), jax.ShapeDtypeStruct((t, V_DIM), f32),
                   jax.ShapeDtypeStruct((n_batch, N_HEADS, DK, DV), f32),
                   jax.ShapeDtypeStruct((n_batch, DK, LANES), f32),
                   jax.ShapeDtypeStruct((n_batch, 1, LANES), f32)],
        scratch_shapes=[pltpu.VMEM((D_MODEL, W_QKVO_COLS), bf16), pltpu.VMEM((D_MODEL, FF_CHUNK), f32),
                        pltpu.SemaphoreType.DMA((1,))] + bufs + bufs
                       + [pltpu.VMEM((N_HEADS, DK, DV), f32), pltpu.VMEM((N_HEADS, DK, 1), f32),
                          pltpu.VMEM((1, LANES), f32)],
        compiler_params=_params(1),
        name="l0_inproj_mlstm",
    )(x2d, g, w_in, w_gate, gate_bias, sel)


def _mlstm_core_sample(q, k, v, gates, c0, n0, m0, n_batch, seq_len):
    t = q.shape[0]
    sel = _bcast_selector()
    nseq = CHUNK_ROWS // seq_len
    row = lambda n: pl.BlockSpec((CHUNK_ROWS, n), lambda i: (i, 0))
    cspec = pl.BlockSpec((nseq, N_HEADS, DK, DV), lambda i: (i, 0, 0, 0))
    nspec = pl.BlockSpec((nseq, QK_DIM), lambda i: (i, 0))
    mspec = pl.BlockSpec((nseq, LANES), lambda i: (i, 0))
    return pl.pallas_call(
        functools.partial(_mlstm_sample_body, nseq=nseq, cs=seq_len),
        grid=(t // CHUNK_ROWS,),
        in_specs=[row(QK_DIM), row(QK_DIM), row(V_DIM), row(2 * LANES), _const_spec(sel.shape), cspec, nspec, mspec],
        out_specs=[row(V_DIM), cspec, nspec, mspec],
        out_shape=[jax.ShapeDtypeStruct((t, V_DIM), f32),
                   jax.ShapeDtypeStruct((n_batch, N_HEADS, DK, DV), f32),
                   jax.ShapeDtypeStruct((n_batch, QK_DIM), f32),
                   jax.ShapeDtypeStruct((n_batch, LANES), f32)],
        compiler_params=_params(1),
        name="l0_mlstm_core",
    )(q, k, v, gates, sel, c0, n0, m0)


def _cast_weights(pairs, stage, sem):
    tr, tc = CAST_TILE
    jobs = []
    for src, dst in pairs:
        rows, cols = dst.shape
        for r0 in range(0, rows, tr):
            for c0 in range(0, cols, tc):
                jobs.append((src.at[pl.ds(r0, tr), pl.ds(c0, tc)], dst, r0, c0))
    copies = [pltpu.make_async_copy(job[0], stage.at[i % 2], sem.at[i % 2]) for i, job in enumerate(jobs)]
    copies[0].start()
    for i, (_, dst, r0, c0) in enumerate(jobs):
        if i + 1 < len(jobs):
            copies[i + 1].start()
        copies[i].wait()
        dst[r0:r0 + tr, c0:c0 + tc] = stage[i % 2].astype(bf16)


def _cast_scratch():
    return [pltpu.VMEM((2,) + CAST_TILE, f32), pltpu.SemaphoreType.DMA((2,))]


def _outproj_ffn_body(h_ref, o_ref, x_ref, hg_ref, wout_hbm, gffn_ref, wup_hbm, wdn_hbm, y_ref,
                      wout_ref, wup_ref, wdn_ref, stage, sem):
    @pl.when(pl.program_id(0) == 0)
    def _():
        _cast_weights([(wout_hbm.at[0], wout_ref), (wup_hbm.at[0], wup_ref), (wdn_hbm.at[0], wdn_ref)], stage, sem)

    x1s = []
    for r in _row_splits(h_ref.shape[0]):
        hn = h_ref[r, :] * hg_ref[...] * jax.nn.sigmoid(o_ref[r, :])
        x1s.append(x_ref[r, :] + _dot(hn.astype(bf16), wout_ref[...]))
    for r, y in zip(_row_splits(h_ref.shape[0]), _ffn(x1s, gffn_ref, wup_ref, wdn_ref)):
        y_ref[r, :] = y


def _outproj_ffn(hmix, o, x2d, head_g, w_out, g_ffn, w_up, w_down):
    t = x2d.shape[0]
    row = pl.BlockSpec((ROW_TILE, D_MODEL), lambda i: (i, 0))
    hbm = pl.BlockSpec(memory_space=pl.ANY)
    return pl.pallas_call(
        _outproj_ffn_body,
        grid=(t // ROW_TILE,),
        in_specs=[row, row, row, _const_spec((1, V_DIM)), hbm, _const_spec((1, D_MODEL)), hbm, hbm],
        out_specs=row,
        out_shape=jax.ShapeDtypeStruct((t, D_MODEL), f32),
        scratch_shapes=[pltpu.VMEM((V_DIM, D_MODEL), bf16), pltpu.VMEM((D_MODEL, D_FF), bf16),
                        pltpu.VMEM((D_FF, D_MODEL), bf16)] + _cast_scratch(),
        compiler_params=_params(1),
        name="l0_outproj_ffn",
    )(hmix, o, x2d, head_g, w_out, g_ffn, w_up, w_down)


def _s5_prep_body(lre_ref, lim_ref, ldt_ref, bre_ref, bim_ref, cre_ref, cim_ref,
                  are_ref, aim_ref, wbr_ref, wbi_ref, vre_ref, vim_ref):
    lr = lre_ref[...]
    li = lim_ref[...]
    dt = jnp.exp(ldt_ref[...])
    mag = jnp.exp(lr * dt)
    a_re = mag * jnp.cos(li * dt)
    a_im = mag * jnp.sin(li * dt)
    den = lr * lr + li * li
    z_re = a_re - 1.0
    coef_re = ((z_re * lr + a_im * li) / den)[:, None, :]
    coef_im = ((a_im * lr - z_re * li) / den)[:, None, :]
    br = bre_ref[...]
    bi = bim_ref[...]
    are_ref[...] = a_re
    aim_ref[...] = a_im
    bb = (coef_re * br - coef_im * bi, coef_re * bi + coef_im * br)
    c_t = (cre_ref[...].reshape(S5_GROUPS * S5_GROUP, S5_STATE).T,
           cim_ref[...].reshape(S5_GROUPS * S5_GROUP, S5_STATE).T)

    groups = S5_BLOCK_CH // S5_GROUP
    log_ch, log_st = S5_GROUP.bit_length() - 1, S5_STATE.bit_length() - 1
    shape_b = (S5_BLOCK_CH, S5_BLOCK_ST)
    diag_b = ((lax.broadcasted_iota(jnp.int32, shape_b, 0) >> log_ch)
              == (lax.broadcasted_iota(jnp.int32, shape_b, 1) >> log_st))
    shape_c = (S5_BLOCK_ST, S5_BLOCK_CH)
    diag_c = ((lax.broadcasted_iota(jnp.int32, shape_c, 0) >> log_st)
              == (lax.broadcasted_iota(jnp.int32, shape_c, 1) >> log_ch))
    for j in range(N_S5_BLOCKS):
        for src, dst in zip(bb, (wbr_ref, wbi_ref)):
            blk = src[j * groups:(j + 1) * groups].reshape(S5_BLOCK_CH, S5_STATE)
            dst[j] = jnp.where(diag_b, jnp.concatenate([blk] * groups, axis=1), 0.0).astype(bf16)
        for src, dst in zip(c_t, (vre_ref, vim_ref)):
            blk = src[:, j * S5_BLOCK_CH:(j + 1) * S5_BLOCK_CH]
            rep = jnp.broadcast_to(blk[None], (groups, S5_STATE, S5_BLOCK_CH)).reshape(S5_BLOCK_ST, S5_BLOCK_CH)
            dst[j] = jnp.where(diag_c, rep, 0.0).astype(bf16)


def _s5_prep(lam_re, lam_im, log_dt, b_re_t, b_im_t, c_re, c_im):
    gp = jax.ShapeDtypeStruct((S5_GROUPS, S5_STATE), f32)
    wb = jax.ShapeDtypeStruct((N_S5_BLOCKS, S5_BLOCK_CH, S5_BLOCK_ST), bf16)
    vc = jax.ShapeDtypeStruct((N_S5_BLOCKS, S5_BLOCK_ST, S5_BLOCK_CH), bf16)
    return pl.pallas_call(_s5_prep_body, out_shape=[gp, gp, wb, wb, vc, vc], name="l1_s5_prep")(
        lam_re, lam_im, log_dt, b_re_t, b_im_t, c_re, c_im)


def _s5_body(x_ref, h0r_ref, h0i_ref, g_ref, are_ref, aim_ref, wbr_ref, wbi_ref, vre_ref, vim_ref,
             d_ref, wglu_hbm, gffn_ref, wup_hbm, wdn_hbm, gfin_ref,
             y_ref, sre_ref, sim_ref, sr_scr, si_scr, hr_scr, hi_scr, act_scr,
             wglu_ref, wup_ref, wdn_ref, stage, cast_sem, *io_scr, bt, lt, dma_io):
    rows = bt * lt
    step = pl.program_id(1)
    n_steps = pl.num_programs(1)

    @pl.when(jnp.logical_and(pl.program_id(0) == 0, step == 0))
    def _():
        _cast_weights([(wglu_hbm.at[0], wglu_ref), (wup_hbm.at[1], wup_ref), (wdn_hbm.at[1], wdn_ref)],
                      stage, cast_sem)

    @pl.when(step == 0)
    def _():
        sr_scr[...] = h0r_ref[...]
        si_scr[...] = h0i_ref[...]

    if dma_io:
        xbuf, ybuf, in_sem, out_sem = io_scr
        slot = lax.rem(step, 2)

        def in_copy(s, sl, b):
            return pltpu.make_async_copy(x_ref.at[b, pl.ds(s * lt, lt), :], xbuf.at[sl, :, b, :], in_sem.at[sl, b])

        def out_copy(s, sl, b):
            return pltpu.make_async_copy(ybuf.at[sl, :, b, :], y_ref.at[b, pl.ds(s * lt, lt), :], out_sem.at[sl, b])

        @pl.when(step == 0)
        def _():
            for b in range(bt):
                in_copy(0, 0, b).start()

        @pl.when(step + 1 < n_steps)
        def _():
            for b in range(bt):
                in_copy(step + 1, 1 - slot, b).start()

        for b in range(bt):
            in_copy(step, slot, b).wait()
        xt = xbuf[slot].reshape(rows, D_MODEL)
    else:
        xt = jnp.concatenate([x_ref[:, t, :] for t in range(lt)], axis=0)
    u = _rms(xt, g_ref[...])
    ub = u.astype(bf16)

    def project_in(j):
        ch = slice(j * S5_BLOCK_CH, (j + 1) * S5_BLOCK_CH)
        hr_scr[j % 2] = _dot(ub[:, ch], wbr_ref[j])
        hi_scr[j % 2] = _dot(ub[:, ch], wbi_ref[j])

    project_in(0)
    for j in range(N_S5_BLOCKS):
        ch = slice(j * S5_BLOCK_CH, (j + 1) * S5_BLOCK_CH)
        st = slice(j * S5_BLOCK_ST, (j + 1) * S5_BLOCK_ST)
        if j + 1 < N_S5_BLOCKS:
            project_in(j + 1)
        a_re = jnp.broadcast_to(are_ref[:, st], (bt, S5_BLOCK_ST))
        a_im = jnp.broadcast_to(aim_ref[:, st], (bt, S5_BLOCK_ST))
        s_re = sr_scr[:, st]
        s_im = si_scr[:, st]
        for t in range(lt):
            r = slice(t * bt, (t + 1) * bt)
            n_re = a_re * s_re - a_im * s_im + hr_scr[j % 2, r, :]
            n_im = a_re * s_im + a_im * s_re + hi_scr[j % 2, r, :]
            hr_scr[j % 2, r, :] = n_re
            hi_scr[j % 2, r, :] = n_im
            s_re, s_im = n_re, n_im
        sr_scr[:, st] = s_re
        si_scr[:, st] = s_im
        yj = (_dot(hr_scr[j % 2].astype(bf16), vre_ref[j]) - _dot(hi_scr[j % 2].astype(bf16), vim_ref[j])
              + d_ref[:, ch] * u[:, ch])
        act_scr[:, ch] = jax.nn.gelu(yj).astype(bf16)

    sre_ref[...] = sr_scr[...]
    sim_ref[...] = si_scr[...]
    x3s = []
    for r in _row_splits(rows):
        ag = _dot(act_scr[r, :], wglu_ref[...])
        x3s.append(xt[r, :] + ag[:, :D_MODEL] * jax.nn.sigmoid(ag[:, D_MODEL:]))
    y = jnp.concatenate([_rms(x4, gfin_ref[...]) for x4 in _ffn(x3s, gffn_ref, wup_ref, wdn_ref)], axis=0)
    if dma_io:
        @pl.when(step >= 2)
        def _():
            for b in range(bt):
                out_copy(step - 2, slot, b).wait()

        ybuf[slot] = y.reshape(lt, bt, D_MODEL)
        for b in range(bt):
            out_copy(step, slot, b).start()

        @pl.when(step == n_steps - 1)
        def _():
            @pl.when(step >= 1)
            def _():
                for b in range(bt):
                    out_copy(step - 1, 1 - slot, b).wait()

            for b in range(bt):
                out_copy(step, slot, b).wait()
    else:
        for t in range(lt):
            y_ref[:, t, :] = y[t * bt:(t + 1) * bt, :]


def _s5_layer(x3d, h0_re, h0_im, bt, lt, g_mix, a_re, a_im, wb_re, wb_im, v_re, v_im, d_skip,
              w_glu, g_ffn, w_up, w_down, g_final):
    n_batch, seq_len, _ = x3d.shape
    grid = (n_batch // bt, seq_len // lt)
    sspec = pl.BlockSpec((bt, S5_N), lambda i, t: (i, 0))
    dma_io = n_batch == bt
    if dma_io:
        xspec = pl.BlockSpec(memory_space=pl.ANY)
        io_scratch = [pltpu.VMEM((2, lt, bt, D_MODEL), f32), pltpu.VMEM((2, lt, bt, D_MODEL), f32),
                      pltpu.SemaphoreType.DMA((2, bt)), pltpu.SemaphoreType.DMA((2, bt))]
    else:
        xspec = pl.BlockSpec((bt, lt, D_MODEL), lambda i, t: (i, t, 0))
        io_scratch = []
    consts = (g_mix, a_re, a_im, wb_re, wb_im, v_re, v_im, d_skip, w_glu, g_ffn, w_up, w_down, g_final)
    in_hbm = (w_glu, w_up, w_down)
    const_specs = [pl.BlockSpec(memory_space=pl.ANY) if any(c is w for w in in_hbm) else _const_spec(c.shape)
                   for c in consts]
    rows = bt * lt
    return pl.pallas_call(
        functools.partial(_s5_body, bt=bt, lt=lt, dma_io=dma_io),
        grid=grid,
        in_specs=[xspec, sspec, sspec] + const_specs,
        out_specs=[xspec, sspec, sspec],
        out_shape=[jax.ShapeDtypeStruct(x3d.shape, f32), jax.ShapeDtypeStruct((n_batch, S5_N), f32),
                   jax.ShapeDtypeStruct((n_batch, S5_N), f32)],
        scratch_shapes=[pltpu.VMEM((bt, S5_N), f32), pltpu.VMEM((bt, S5_N), f32),
                        pltpu.VMEM((2, rows, S5_BLOCK_ST), f32), pltpu.VMEM((2, rows, S5_BLOCK_ST), f32),
                        pltpu.VMEM((rows, D_MODEL), bf16),
                        pltpu.VMEM((D_MODEL, 2 * D_MODEL), bf16), pltpu.VMEM((D_MODEL, D_FF), bf16),
                        pltpu.VMEM((D_FF, D_MODEL), bf16)] + _cast_scratch() + io_scratch,
        compiler_params=_params(2),
        name="l1_s5_ffn",
    )(x3d, h0_re, h0_im, *consts)


def _rep_lanes(x):
    return jnp.tile(x, (1,) * (x.ndim - 1) + (LANES // x.shape[-1],))


def _trunk(x, state, p):
    n_batch, seq_len, _ = x.shape
    x2d = x.reshape(n_batch * seq_len, D_MODEL)
    if state is None:
        o, hmix, c_new, n_new, m_new = _l0_prompt(x2d, p["g_mix0"], p["w_in"], p["w_gate"], p["gate_bias"],
                                                  n_batch, seq_len)
        n_new = jnp.swapaxes(n_new[:, :, :N_HEADS], 1, 2)
        h0_re = jnp.zeros((n_batch, S5_N), f32)
        h0_im = h0_re
        bt, lt = n_batch, 512 // n_batch
    else:
        c0, n0, m0, h0_re, h0_im = state
        q, k, v, o, gates = _inproj(x2d, p["g_mix0"], p["w_in"], p["w_gate"], p["gate_bias"])
        hmix, c_new, n_new, m_new = _mlstm_core_sample(
            q, k, v, gates, c0, n0.reshape(n_batch, QK_DIM), _rep_lanes(m0), n_batch, seq_len)
        h0_re = h0_re.reshape(n_batch, S5_N)
        h0_im = h0_im.reshape(n_batch, S5_N)
        bt, lt = 512 // seq_len, seq_len
    x2 = _outproj_ffn(hmix, o, x2d, p["head_g"], p["w_out"], p["g_ffn0"], p["w_up"], p["w_down"])
    y, s_re, s_im = _s5_layer(x2.reshape(n_batch, seq_len, D_MODEL), h0_re, h0_im, bt, lt,
                              p["g_mix1"], p["a_re"], p["a_im"], p["wb_re"], p["wb_im"], p["v_re"], p["v_im"],
                              p["d_skip"], p["w_glu"], p["g_ffn1"], p["w_up"], p["w_down"], p["g_final"])
    return (y,
            c_new.reshape(1, n_batch, N_HEADS, DK, DV),
            n_new.reshape(1, n_batch, N_HEADS, DK),
            m_new.reshape(n_batch, LANES)[:, :N_HEADS].reshape(1, n_batch, N_HEADS),
            s_re.reshape(1, n_batch, S5_GROUPS, S5_STATE),
            s_im.reshape(1, n_batch, S5_GROUPS, S5_STATE))


def kernel(x_prompt, x_sample, state_mlstm_C, state_mlstm_n, state_mlstm_m, state_s5_re, state_s5_im,
           norm_mix_g, norm_ffn_g, norm_final_g, mlstm_w_in, mlstm_b_i, mlstm_b_f, mlstm_head_norm_g,
           mlstm_w_out, s5_lambda_re, s5_lambda_im, s5_log_dt, s5_b_re, s5_b_im, s5_c_re, s5_c_im,
           s5_d, s5_w_glu, ffn_w_up, ffn_w_down):
    w_in = mlstm_w_in[0]
    n_qkvo = W_QKVO_COLS
    a_re, a_im, wb_re, wb_im, v_re, v_im = _s5_prep(
        s5_lambda_re[0], s5_lambda_im[0], s5_log_dt[0].reshape(S5_GROUPS, 1),
        jnp.swapaxes(s5_b_re[0], 1, 2), jnp.swapaxes(s5_b_im[0], 1, 2), s5_c_re[0], s5_c_im[0])
    p = dict(
        g_mix0=norm_mix_g[0].reshape(1, D_MODEL), g_mix1=norm_mix_g[1].reshape(1, D_MODEL),
        g_ffn0=norm_ffn_g[0].reshape(1, D_MODEL), g_ffn1=norm_ffn_g[1].reshape(1, D_MODEL),
        g_final=norm_final_g.reshape(1, D_MODEL),
        w_in=w_in,
        w_gate=jnp.concatenate([_rep_lanes(w_in[:, n_qkvo:n_qkvo + N_HEADS]),
                                _rep_lanes(w_in[:, n_qkvo + N_HEADS:])], axis=1).astype(bf16),
        gate_bias=jnp.concatenate([_rep_lanes(mlstm_b_i[0][None]), _rep_lanes(mlstm_b_f[0][None])], axis=1),
        head_g=mlstm_head_norm_g[0].reshape(1, V_DIM),
        w_out=mlstm_w_out, w_up=ffn_w_up, w_down=ffn_w_down, w_glu=s5_w_glu,
        a_re=a_re.reshape(1, S5_N), a_im=a_im.reshape(1, S5_N),
        wb_re=wb_re, wb_im=wb_im, v_re=v_re, v_im=v_im,
        d_skip=s5_d[0].reshape(1, D_MODEL),
    )
    prompt = _trunk(x_prompt, None, p)
    sample = _trunk(x_sample, (state_mlstm_C[0], state_mlstm_n[0], state_mlstm_m[0],
                               state_s5_re[0], state_s5_im[0]), p)
    return (prompt[0], sample[0]) + prompt[1:] + sample[1:]
```

```python
import functools
import types

import jax
import jax.numpy as jnp
import numpy as np
from jax import lax
from jax.experimental import pallas as pl
from jax.experimental.pallas import tpu as pltpu

f32 = jnp.float32
bf16 = jnp.bfloat16

D_MODEL = 1024
N_HEADS = 8
DK = 64
DV = 128
QK_DIM = N_HEADS * DK
V_DIM = N_HEADS * DV
W_QKVO_COLS = 2 * QK_DIM + 2 * V_DIM
D_FF = 4 * D_MODEL
S5_GROUPS = 64
S5_GROUP = 16
S5_STATE = 64
S5_N = S5_GROUPS * S5_STATE
EPS = 1e-6

LANES = 128
SUBLANES = 8
ROW_TILE = 512
CHUNK_ROWS = 256
PROMPT_CHUNKS = 2
FF_CHUNK = 1024
ROW_SPLITS = 2
S5_BLOCK_CH = LANES
S5_BLOCK_ST = S5_BLOCK_CH // S5_GROUP * S5_STATE
N_S5_BLOCKS = D_MODEL // S5_BLOCK_CH
VMEM_LIMIT_BYTES = 56 * 1024 * 1024
NEG_BIG = -1e30
LOG_HEADS = N_HEADS.bit_length() - 1
N_BCAST = 3


def _params(n_axes):
    return pltpu.CompilerParams(dimension_semantics=("arbitrary",) * n_axes,
                                vmem_limit_bytes=VMEM_LIMIT_BYTES)


def _const_spec(shape):
    nd = len(shape)
    return pl.BlockSpec(shape, lambda *_: (0,) * nd, pipeline_mode=pl.Buffered(1))


def _layer_spec(shape, layer):
    nd = len(shape)
    return pl.BlockSpec((None,) + tuple(shape[1:]), lambda *_: (layer,) + (0,) * (nd - 1),
                        pipeline_mode=pl.Buffered(1))


def _dot(a, b):
    return jnp.dot(a, b, preferred_element_type=f32)


def _dot_nt(a, b):
    return lax.dot_general(a, b, (((1,), (1,)), ((), ())), preferred_element_type=f32)


def _rms(x, g):
    return x * lax.rsqrt(jnp.mean(x * x, axis=-1, keepdims=True) + EPS) * g


def _row_splits(rows):
    size = rows // ROW_SPLITS
    return [slice(i * size, (i + 1) * size) for i in range(ROW_SPLITS)]


def _ffn(x1s, g_ref, wup_ref, wdn_ref):
    xn = [_rms(x1, g_ref[...]).astype(bf16) for x1 in x1s]
    acc = [None] * len(x1s)
    for c in range(D_FF // FF_CHUNK):
        cols = slice(c * FF_CHUNK, (c + 1) * FF_CHUNK)
        for i in range(len(x1s)):
            hid = _dot(xn[i], wup_ref[:, cols])
            hid = jnp.square(jnp.maximum(hid, 0.0)).astype(bf16)
            part = _dot(hid, wdn_ref[cols, :])
            acc[i] = part if acc[i] is None else acc[i] + part
    return [x1 + a for x1, a in zip(x1s, acc)]


def _inproj_body(x_ref, g_ref, w32_ref, wg_ref, bias_ref, q_ref, k_ref, v_ref, o_ref, gate_ref, w_ref):
    @pl.when(pl.program_id(0) == 0)
    def _():
        for c in range(0, W_QKVO_COLS, FF_CHUNK):
            w_ref[:, c:c + FF_CHUNK] = w32_ref[:, c:c + FF_CHUNK].astype(bf16)

    xn = _rms(x_ref[...], g_ref[...]).astype(bf16)
    q, k, v, o, logi, logf = _inproj_values(xn, w_ref, wg_ref, bias_ref)
    q_ref[...] = q
    k_ref[...] = k
    v_ref[...] = v
    o_ref[...] = o
    gate_ref[:, :LANES] = logi
    gate_ref[:, LANES:] = logf


def _inproj(x2d, g, w_in, w_gate, gate_bias):
    t = x2d.shape[0]
    row = lambda n: pl.BlockSpec((ROW_TILE, n), lambda i: (i, 0))
    return pl.pallas_call(
        _inproj_body,
        grid=(t // ROW_TILE,),
        in_specs=[row(D_MODEL), _const_spec((1, D_MODEL)), _const_spec(w_in.shape),
                  _const_spec(w_gate.shape), _const_spec((1, 2 * LANES))],
        out_specs=[row(QK_DIM), row(QK_DIM), row(V_DIM), row(V_DIM), row(2 * LANES)],
        out_shape=[jax.ShapeDtypeStruct((t, QK_DIM), bf16), jax.ShapeDtypeStruct((t, QK_DIM), f32),
                   jax.ShapeDtypeStruct((t, V_DIM), bf16), jax.ShapeDtypeStruct((t, V_DIM), f32),
                   jax.ShapeDtypeStruct((t, 2 * LANES), f32)],
        scratch_shapes=[pltpu.VMEM((D_MODEL, W_QKVO_COLS), bf16)],
        compiler_params=_params(1),
        name="l0_inproj",
    )(x2d, g, w_in, w_gate, gate_bias)


def _row_prefix(x, cs, rpos, is_max):
    sh = 1
    while sh < cs:
        prev = pltpu.roll(x, sh, 0)
        if is_max:
            x = jnp.maximum(x, jnp.where(rpos >= sh, prev, NEG_BIG))
        else:
            x = x + jnp.where(rpos >= sh, prev, 0.0)
        sh *= 2
    return x


def _seg_last(x, nseq, cs):
    if nseq == 1:
        return x[cs - 1:cs, :]
    last = x.reshape(nseq, cs, x.shape[1])[:, cs - 1:cs, :]
    return jnp.broadcast_to(last, (nseq, cs, x.shape[1])).reshape(nseq * cs, x.shape[1])


def _gate_algebra(logi, logf, m_in, nseq, cs, sel_ref):
    rows = nseq * cs
    rpos = lax.broadcasted_iota(jnp.int32, (rows, LANES), 0) & (cs - 1)
    b = _row_prefix(logf, cs, rpos, False)
    a = logi - b
    g = jnp.maximum(m_in, _row_prefix(a, cs, rpos, True))
    g_last = _seg_last(g, nseq, cs)
    w_state = jnp.exp(a - g_last)

    group = lax.broadcasted_iota(jnp.int32, (rows, LANES), 1) >> LOG_HEADS
    pieces = jnp.zeros((rows, LANES), f32)
    for i, val in enumerate((g, jnp.exp(m_in - g), jnp.exp(-(b + g)))):
        hi = val.astype(bf16).astype(f32)
        mid = (val - hi).astype(bf16).astype(f32)
        lo = (val - hi - mid).astype(bf16).astype(f32)
        for j, piece in enumerate((hi, mid, lo)):
            pieces = jnp.where(group == 3 * i + j, piece, pieces)
    bcast = _dot(pieces.astype(bf16), sel_ref[...])
    tile = lambda i, h: bcast[:, (i * N_HEADS + h) * DV:(i * N_HEADS + h + 1) * DV]
    return types.SimpleNamespace(
        a_t=a.T,
        w_state=w_state, w_state_t=w_state.T,
        decay=jnp.exp(m_in - g_last),
        m_new=_seg_last(b, nseq, cs) + g_last,
        g=lambda h: tile(0, h), w_inter=lambda h: tile(1, h), clamp=lambda h: tile(2, h))


def _head_outputs(qk, gates, inter, qn, v_h, h_ref, nseq, cs):
    rows = nseq * cs
    log_cs = cs.bit_length() - 1
    heads = range(N_HEADS)
    chunks = range(len(qk))
    ri = lax.broadcasted_iota(jnp.int32, (rows, rows), 0)
    ci = lax.broadcasted_iota(jnp.int32, (rows, rows), 1)
    causal = ci <= ri
    if nseq > 1:
        causal = jnp.logical_and(causal, (ri >> log_cs) == (ci >> log_cs))
    for c in chunks:
        for h in heads:
            g_rows = jnp.concatenate([gates[c].g(h)] * (rows // DV), axis=1)
            w = jnp.exp(jnp.where(causal, gates[c].a_t[h:h + 1, :] - g_rows, NEG_BIG))
            s = qk[c][h]() * w
            s_sum = jnp.sum(s, axis=1, keepdims=True)
            intra = _dot(s.astype(bf16), v_h[c][h])
            w_inter = gates[c].w_inter(h)
            num = w_inter * inter[c][h] + intra
            den = jnp.maximum(jnp.abs(w_inter * qn[c][h] + s_sum), gates[c].clamp(h))
            h_ref[c * rows:(c + 1) * rows, h * DV:(h + 1) * DV] = num * lax.rsqrt(
                jnp.mean(num * num, axis=1, keepdims=True) + EPS * (den * den))


def _sequence_chunks(q_ref, k_ref, v_ref, gate_ref, sel_ref, h_ref, c_state, n_state, m_in, n_chunks,
                     after_gates=lambda: None):
    rows = CHUNK_ROWS
    heads = range(N_HEADS)
    chunks = range(n_chunks)
    c_state, n_state = list(c_state), list(n_state)
    rsl = lambda c: slice(c * rows, (c + 1) * rows)
    q_h = [[q_ref[rsl(c), h * DK:(h + 1) * DK] for h in heads] for c in chunks]
    k_h = [[k_ref[rsl(c), h * DK:(h + 1) * DK] for h in heads] for c in chunks]
    v_h = [[v_ref[rsl(c), h * DV:(h + 1) * DV] for h in heads] for c in chunks]

    qk = [[functools.partial(_dot_nt, q_h[c][h], k_h[c][h].astype(bf16)) for h in heads] for c in chunks]

    gates = []
    for c in chunks:
        gates.append(_gate_algebra(gate_ref[rsl(c), :LANES], gate_ref[rsl(c), LANES:],
                                   m_in if c == 0 else gates[c - 1].m_new, 1, rows, sel_ref))
    after_gates()

    inter = [[None] * N_HEADS for _ in chunks]
    qn = [[None] * N_HEADS for _ in chunks]
    for c in chunks:
        ga = gates[c]
        k_t = k_ref[rsl(c), :].T
        for h in heads:
            rhs = jnp.concatenate([c_state[h].astype(bf16),
                                   jnp.broadcast_to(n_state[h], (DK, DV)).astype(bf16)], axis=1)
            inter2 = _dot(q_h[c][h], rhs)
            inter[c][h] = inter2[:, :DV]
            qn[c][h] = inter2[:, DV:]
            wk_t = k_t[h * DK:(h + 1) * DK, :] * ga.w_state_t[h:h + 1, :]
            dec = ga.decay[:, h:h + 1]
            c_state[h] = dec * c_state[h] + _dot(wk_t.astype(bf16), v_h[c][h])
            n_state[h] = dec * n_state[h] + jnp.sum(wk_t, axis=1, keepdims=True)

    _head_outputs(qk, gates, inter, qn, v_h, h_ref, 1, rows)
    return c_state, n_state, gates[-1].m_new


def _mlstm_sample_body(q_ref, k_ref, v_ref, gate_ref, sel_ref, c0_ref, n0_ref, m0_ref,
                       h_ref, cout_ref, nout_ref, mout_ref, *, nseq, cs):
    rows = nseq * cs
    log_cs = cs.bit_length() - 1
    log_dk = DK.bit_length() - 1
    heads = range(N_HEADS)
    q_h = [q_ref[:, h * DK:(h + 1) * DK] for h in heads]
    k_h = [k_ref[:, h * DK:(h + 1) * DK] for h in heads]
    v_h = [v_ref[:, h * DV:(h + 1) * DV] for h in heads]
    qk = [functools.partial(_dot_nt, q_h[h], k_h[h].astype(bf16)) for h in heads]
    m_in = jnp.broadcast_to(m0_ref[...][:, None, :], (nseq, cs, LANES)).reshape(rows, LANES)
    ga = _gate_algebra(gate_ref[:, :LANES], gate_ref[:, LANES:], m_in, nseq, cs, sel_ref)

    bd_q = ((lax.broadcasted_iota(jnp.int32, (rows, nseq * DK), 0) >> log_cs)
            == (lax.broadcasted_iota(jnp.int32, (rows, nseq * DK), 1) >> log_dk))
    bd_k = ((lax.broadcasted_iota(jnp.int32, (nseq * DK, rows), 0) >> log_dk)
            == (lax.broadcasted_iota(jnp.int32, (nseq * DK, rows), 1) >> log_cs))
    k_t = k_ref[...].T
    inter, qn = [], []
    for h in heads:
        c_prev = c0_ref[:, h].reshape(nseq * DK, DV)
        q32 = q_h[h].astype(f32)
        q_bd = jnp.where(bd_q, jnp.concatenate([q32] * nseq, axis=1), 0.0).astype(bf16)
        inter.append(_dot(q_bd, c_prev.astype(bf16)))
        n_prev = n0_ref[:, h * DK:(h + 1) * DK]
        n_rows = jnp.broadcast_to(n_prev[:, None, :], (nseq, cs, DK)).reshape(rows, DK)
        qn.append(jnp.sum(q32 * n_rows, axis=1, keepdims=True))

        wk_t = k_t[h * DK:(h + 1) * DK, :] * ga.w_state_t[h:h + 1, :]
        wk_bd = jnp.where(bd_k, jnp.broadcast_to(wk_t[None], (nseq, DK, rows)).reshape(nseq * DK, rows), 0.0)
        dc = _dot(wk_bd.astype(bf16), v_h[h])
        dec_col = ga.decay[:, h:h + 1]
        dec_rows = jnp.broadcast_to(dec_col.reshape(nseq, 1, cs, 1), (nseq, DK // cs, cs, 1)).reshape(nseq * DK, 1)
        cout_ref[:, h] = (dec_rows * c_prev + dc).reshape(nseq, DK, DV)
        dec_seq = jnp.max(dec_col.reshape(nseq, cs, 1), axis=1)
        nout_ref[:, h * DK:(h + 1) * DK] = dec_seq * n_prev + jnp.sum(
            (ga.w_state[:, h:h + 1] * k_h[h]).reshape(nseq, cs, DK), axis=1)
    mout_ref[...] = jnp.max(ga.m_new.reshape(nseq, cs, LANES), axis=1)
    _head_outputs([qk], [ga], [inter], [qn], [v_h], h_ref, nseq, cs)


def _inproj_values(xn, w_ref, wg_ref, bias_ref):
    q = _dot(xn, w_ref[:, 0:QK_DIM]).astype(bf16)
    k = _dot(xn, w_ref[:, QK_DIM:2 * QK_DIM]) * (DK ** -0.5)
    v = _dot(xn, w_ref[:, 2 * QK_DIM:2 * QK_DIM + V_DIM]).astype(bf16)
    o = _dot(xn, w_ref[:, 2 * QK_DIM + V_DIM:W_QKVO_COLS])
    gz = _dot(xn, wg_ref[...]) + bias_ref[...]
    return q, k, v, o, gz[:, :LANES], jax.nn.log_sigmoid(gz[:, LANES:])


def _l0_prompt_body(x_ref, g_ref, w32_hbm, wg_ref, bias_ref, sel_ref,
                    o_ref, h_ref, cout_ref, nout_ref, mout_ref,
                    w_scr, stage, dma_sem, qa, ka, va, ga, qb, kb, vb, gb, c_scr, n_scr, m_scr, *, nt, n_chunks):
    step = pl.program_id(0)
    heads = range(N_HEADS)

    @pl.when(step == 0)
    def _():
        for c in range(0, W_QKVO_COLS, FF_CHUNK):
            cp = pltpu.make_async_copy(w32_hbm.at[:, pl.ds(c, FF_CHUNK)], stage, dma_sem.at[0])
            cp.start()
            cp.wait()
            w_scr[:, c:c + FF_CHUNK] = stage[...].astype(bf16)
        for ref in (qb, kb, vb, gb, c_scr, n_scr, m_scr):
            ref[...] = jnp.zeros_like(ref)

    def half_step(wr, rd):
        def project():
            q_w, k_w, v_w, g_w = wr
            xn = _rms(x_ref[...], g_ref[...]).astype(bf16)
            q, k, v, o, logi, logf = _inproj_values(xn, w_scr, wg_ref, bias_ref)
            q_w[...] = q
            k_w[...] = k
            v_w[...] = v
            o_ref[...] = o
            g_w[:, :LANES] = logi
            g_w[:, LANES:] = logf

        first = lax.rem(step + nt - 1, nt) == 0
        c_state = [jnp.where(first, 0.0, c_scr[h]) for h in heads]
        n_state = [jnp.where(first, 0.0, n_scr[h]) for h in heads]
        m_in = jnp.where(first, 0.0, m_scr[...])
        c_state, n_state, m_new = _sequence_chunks(*rd, sel_ref, h_ref, c_state, n_state, m_in, n_chunks,
                                                   after_gates=project)
        for h in heads:
            c_scr[h] = c_state[h]
            n_scr[h] = n_state[h]
        m_scr[...] = m_new

    @pl.when(lax.rem(step, 2) == 0)
    def _():
        half_step((qa, ka, va, ga), (qb, kb, vb, gb))

    @pl.when(lax.rem(step, 2) == 1)
    def _():
        half_step((qb, kb, vb, gb), (qa, ka, va, ga))

    @pl.when(jnp.logical_and(step >= 1, lax.rem(step, nt) == 0))
    def _():
        cout_ref[0] = c_scr[...]
        lane = lax.broadcasted_iota(jnp.int32, (DK, LANES), 1)
        n_mat = jnp.zeros((DK, LANES), f32)
        for h in heads:
            n_mat = jnp.where(lane == h, jnp.broadcast_to(n_scr[h], (DK, LANES)), n_mat)
        nout_ref[0] = n_mat
        mout_ref[0] = m_scr[...]


def _bcast_selector():
    src = np.arange(LANES)
    dst = np.arange(N_BCAST * N_HEADS * DV) // DV
    hit = (((src[:, None] >> LOG_HEADS) // 3 == dst[None, :] // N_HEADS)
           & ((src[:, None] & (N_HEADS - 1)) == dst[None, :] % N_HEADS))
    return jnp.asarray(hit, dtype=bf16)


def _l0_prompt(x2d, g, w_in, w_gate, gate_bias, n_batch, seq_len):
    t = x2d.shape[0]
    sel = _bcast_selector()
    rows = PROMPT_CHUNKS * CHUNK_ROWS
    nt = seq_len // rows
    n_blocks = t // rows
    cur = lambda s: (jnp.minimum(s, n_blocks - 1), 0)
    prev = lambda s: (jnp.maximum(s - 1, 0), 0)
    seq = lambda s: jnp.maximum(s - 1, 0) // nt
    bufs = [pltpu.VMEM((rows, QK_DIM), bf16), pltpu.VMEM((rows, QK_DIM), f32),
            pltpu.VMEM((rows, V_DIM), bf16), pltpu.VMEM((rows, 2 * LANES), f32)]
    return pl.pallas_call(
        functools.partial(_l0_prompt_body, nt=nt, n_chunks=PROMPT_CHUNKS),
        grid=(n_blocks + 1,),
        in_specs=[pl.BlockSpec((rows, D_MODEL), cur), _const_spec((1, D_MODEL)), pl.BlockSpec(memory_space=pl.ANY),
                  _const_spec(w_gate.shape), _const_spec((1, 2 * LANES)), _const_spec(sel.shape)],
        out_specs=[pl.BlockSpec((rows, V_DIM), cur), pl.BlockSpec((rows, V_DIM), prev),
                   pl.BlockSpec((1, N_HEADS, DK, DV), lambda s: (seq(s), 0, 0, 0)),
                   pl.BlockSpec((1, DK, LANES), lambda s: (seq(s), 0, 0)),
                   pl.BlockSpec((1, 1, LANES), lambda s: (seq(s), 0, 0))],
        out_shape=[jax.ShapeDtypeStruct((t, V_DIM), f32), jax.ShapeDtypeStruct((t, V_DIM), f32),
                   jax.ShapeDtypeStruct((n_batch, N_HEADS, DK, DV), f32),
                   jax.ShapeDtypeStruct((n_batch, DK, LANES), f32),
                   jax.ShapeDtypeStruct((n_batch, 1, LANES), f32)],
        scratch_shapes=[pltpu.VMEM((D_MODEL, W_QKVO_COLS), bf16), pltpu.VMEM((D_MODEL, FF_CHUNK), f32),
                        pltpu.SemaphoreType.DMA((1,))] + bufs + bufs
                       + [pltpu.VMEM((N_HEADS, DK, DV), f32), pltpu.VMEM((N_HEADS, DK, 1), f32),
                          pltpu.VMEM((1, LANES), f32)],
        compiler_params=_params(1),
        name="l0_inproj_mlstm",
    )(x2d, g, w_in, w_gate, gate_bias, sel)


def _mlstm_core_sample(q, k, v, gates, c0, n0, m0, n_batch, seq_len):
    t = q.shape[0]
    sel = _bcast_selector()
    nseq = CHUNK_ROWS // seq_len
    row = lambda n: pl.BlockSpec((CHUNK_ROWS, n), lambda i: (i, 0))
    cspec = pl.BlockSpec((nseq, N_HEADS, DK, DV), lambda i: (i, 0, 0, 0))
    nspec = pl.BlockSpec((nseq, QK_DIM), lambda i: (i, 0))
    mspec = pl.BlockSpec((nseq, LANES), lambda i: (i, 0))
    return pl.pallas_call(
        functools.partial(_mlstm_sample_body, nseq=nseq, cs=seq_len),
        grid=(t // CHUNK_ROWS,),
        in_specs=[row(QK_DIM), row(QK_DIM), row(V_DIM), row(2 * LANES), _const_spec(sel.shape), cspec, nspec, mspec],
        out_specs=[row(V_DIM), cspec, nspec, mspec],
        out_shape=[jax.ShapeDtypeStruct((t, V_DIM), f32),
                   jax.ShapeDtypeStruct((n_batch, N_HEADS, DK, DV), f32),
                   jax.ShapeDtypeStruct((n_batch, QK_DIM), f32),
                   jax.ShapeDtypeStruct((n_batch, LANES), f32)],
        compiler_params=_params(1),
        name="l0_mlstm_core",
    )(q, k, v, gates, sel, c0, n0, m0)


def _outproj_ffn_body(h_ref, o_ref, x_ref, hg_ref, wout_ref, gffn_ref, wup_ref, wdn_ref, y_ref):
    x1s = []
    for r in _row_splits(h_ref.shape[0]):
        hn = h_ref[r, :] * hg_ref[...] * jax.nn.sigmoid(o_ref[r, :])
        x1s.append(x_ref[r, :] + _dot(hn.astype(bf16), wout_ref[...]))
    for r, y in zip(_row_splits(h_ref.shape[0]), _ffn(x1s, gffn_ref, wup_ref, wdn_ref)):
        y_ref[r, :] = y


def _outproj_ffn(hmix, o, x2d, head_g, w_out, g_ffn, w_up, w_down):
    t = x2d.shape[0]
    row = pl.BlockSpec((ROW_TILE, D_MODEL), lambda i: (i, 0))
    return pl.pallas_call(
        _outproj_ffn_body,
        grid=(t // ROW_TILE,),
        in_specs=[row, row, row, _const_spec((1, V_DIM)), _const_spec(w_out.shape),
                  _const_spec((1, D_MODEL)), _layer_spec(w_up.shape, 0), _layer_spec(w_down.shape, 0)],
        out_specs=row,
        out_shape=jax.ShapeDtypeStruct((t, D_MODEL), f32),
        compiler_params=_params(1),
        name="l0_outproj_ffn",
    )(hmix, o, x2d, head_g, w_out, g_ffn, w_up, w_down)


def _s5_prep_body(lre_ref, lim_ref, ldt_ref, bre_ref, bim_ref, cre_ref, cim_ref,
                  are_ref, aim_ref, wbr_ref, wbi_ref, vre_ref, vim_ref):
    lr = lre_ref[...]
    li = lim_ref[...]
    dt = jnp.exp(ldt_ref[...])
    mag = jnp.exp(lr * dt)
    a_re = mag * jnp.cos(li * dt)
    a_im = mag * jnp.sin(li * dt)
    den = lr * lr + li * li
    z_re = a_re - 1.0
    coef_re = ((z_re * lr + a_im * li) / den)[:, None, :]
    coef_im = ((a_im * lr - z_re * li) / den)[:, None, :]
    br = bre_ref[...]
    bi = bim_ref[...]
    are_ref[...] = a_re
    aim_ref[...] = a_im
    bb = (coef_re * br - coef_im * bi, coef_re * bi + coef_im * br)
    c_t = (cre_ref[...].reshape(S5_GROUPS * S5_GROUP, S5_STATE).T,
           cim_ref[...].reshape(S5_GROUPS * S5_GROUP, S5_STATE).T)

    groups = S5_BLOCK_CH // S5_GROUP
    log_ch, log_st = S5_GROUP.bit_length() - 1, S5_STATE.bit_length() - 1
    shape_b = (S5_BLOCK_CH, S5_BLOCK_ST)
    diag_b = ((lax.broadcasted_iota(jnp.int32, shape_b, 0) >> log_ch)
              == (lax.broadcasted_iota(jnp.int32, shape_b, 1) >> log_st))
    shape_c = (S5_BLOCK_ST, S5_BLOCK_CH)
    diag_c = ((lax.broadcasted_iota(jnp.int32, shape_c, 0) >> log_st)
              == (lax.broadcasted_iota(jnp.int32, shape_c, 1) >> log_ch))
    for j in range(N_S5_BLOCKS):
        for src, dst in zip(bb, (wbr_ref, wbi_ref)):
            blk = src[j * groups:(j + 1) * groups].reshape(S5_BLOCK_CH, S5_STATE)
            dst[j] = jnp.where(diag_b, jnp.concatenate([blk] * groups, axis=1), 0.0).astype(bf16)
        for src, dst in zip(c_t, (vre_ref, vim_ref)):
            blk = src[:, j * S5_BLOCK_CH:(j + 1) * S5_BLOCK_CH]
            rep = jnp.broadcast_to(blk[None], (groups, S5_STATE, S5_BLOCK_CH)).reshape(S5_BLOCK_ST, S5_BLOCK_CH)
            dst[j] = jnp.where(diag_c, rep, 0.0).astype(bf16)


def _s5_prep(lam_re, lam_im, log_dt, b_re_t, b_im_t, c_re, c_im):
    gp = jax.ShapeDtypeStruct((S5_GROUPS, S5_STATE), f32)
    wb = jax.ShapeDtypeStruct((N_S5_BLOCKS, S5_BLOCK_CH, S5_BLOCK_ST), bf16)
    vc = jax.ShapeDtypeStruct((N_S5_BLOCKS, S5_BLOCK_ST, S5_BLOCK_CH), bf16)
    return pl.pallas_call(_s5_prep_body, out_shape=[gp, gp, wb, wb, vc, vc], name="l1_s5_prep")(
        lam_re, lam_im, log_dt, b_re_t, b_im_t, c_re, c_im)


def _s5_body(x_ref, h0r_ref, h0i_ref, g_ref, are_ref, aim_ref, wbr_ref, wbi_ref, vre_ref, vim_ref,
             d_ref, wglu_ref, gffn_ref, wup_ref, wdn_ref, gfin_ref,
             y_ref, sre_ref, sim_ref, sr_scr, si_scr, hr_scr, hi_scr, act_scr, *io_scr, bt, lt, dma_io):
    rows = bt * lt
    step = pl.program_id(1)
    n_steps = pl.num_programs(1)

    @pl.when(step == 0)
    def _():
        sr_scr[...] = h0r_ref[...]
        si_scr[...] = h0i_ref[...]

    if dma_io:
        xbuf, ybuf, in_sem, out_sem = io_scr
        slot = lax.rem(step, 2)

        def in_copy(s, sl, b):
            return pltpu.make_async_copy(x_ref.at[b, pl.ds(s * lt, lt), :], xbuf.at[sl, :, b, :], in_sem.at[sl, b])

        def out_copy(s, sl, b):
            return pltpu.make_async_copy(ybuf.at[sl, :, b, :], y_ref.at[b, pl.ds(s * lt, lt), :], out_sem.at[sl, b])

        @pl.when(step == 0)
        def _():
            for b in range(bt):
                in_copy(0, 0, b).start()

        @pl.when(step + 1 < n_steps)
        def _():
            for b in range(bt):
                in_copy(step + 1, 1 - slot, b).start()

        for b in range(bt):
            in_copy(step, slot, b).wait()
        xt = xbuf[slot].reshape(rows, D_MODEL)
    else:
        xt = jnp.concatenate([x_ref[:, t, :] for t in range(lt)], axis=0)
    u = _rms(xt, g_ref[...])
    ub = u.astype(bf16)

    def project_in(j):
        ch = slice(j * S5_BLOCK_CH, (j + 1) * S5_BLOCK_CH)
        hr_scr[j % 2] = _dot(ub[:, ch], wbr_ref[j])
        hi_scr[j % 2] = _dot(ub[:, ch], wbi_ref[j])

    project_in(0)
    for j in range(N_S5_BLOCKS):
        ch = slice(j * S5_BLOCK_CH, (j + 1) * S5_BLOCK_CH)
        st = slice(j * S5_BLOCK_ST, (j + 1) * S5_BLOCK_ST)
        if j + 1 < N_S5_BLOCKS:
            project_in(j + 1)
        a_re = jnp.broadcast_to(are_ref[:, st], (bt, S5_BLOCK_ST))
        a_im = jnp.broadcast_to(aim_ref[:, st], (bt, S5_BLOCK_ST))
        s_re = sr_scr[:, st]
        s_im = si_scr[:, st]
        for t in range(lt):
            r = slice(t * bt, (t + 1) * bt)
            n_re = a_re * s_re - a_im * s_im + hr_scr[j % 2, r, :]
            n_im = a_re * s_im + a_im * s_re + hi_scr[j % 2, r, :]
            hr_scr[j % 2, r, :] = n_re
            hi_scr[j % 2, r, :] = n_im
            s_re, s_im = n_re, n_im
        sr_scr[:, st] = s_re
        si_scr[:, st] = s_im
        yj = (_dot(hr_scr[j % 2].astype(bf16), vre_ref[j]) - _dot(hi_scr[j % 2].astype(bf16), vim_ref[j])
              + d_ref[:, ch] * u[:, ch])
        act_scr[:, ch] = jax.nn.gelu(yj).astype(bf16)

    sre_ref[...] = sr_scr[...]
    sim_ref[...] = si_scr[...]
    x3s = []
    for r in _row_splits(rows):
        ag = _dot(act_scr[r, :], wglu_ref[...])
        x3s.append(xt[r, :] + ag[:, :D_MODEL] * jax.nn.sigmoid(ag[:, D_MODEL:]))
    y = jnp.concatenate([_rms(x4, gfin_ref[...]) for x4 in _ffn(x3s, gffn_ref, wup_ref, wdn_ref)], axis=0)
    if dma_io:
        @pl.when(step >= 2)
        def _():
            for b in range(bt):
                out_copy(step - 2, slot, b).wait()

        ybuf[slot] = y.reshape(lt, bt, D_MODEL)
        for b in range(bt):
            out_copy(step, slot, b).start()

        @pl.when(step == n_steps - 1)
        def _():
            @pl.when(step >= 1)
            def _():
                for b in range(bt):
                    out_copy(step - 1, 1 - slot, b).wait()

            for b in range(bt):
                out_copy(step, slot, b).wait()
    else:
        for t in range(lt):
            y_ref[:, t, :] = y[t * bt:(t + 1) * bt, :]


def _s5_layer(x3d, h0_re, h0_im, bt, lt, g_mix, a_re, a_im, wb_re, wb_im, v_re, v_im, d_skip,
              w_glu, g_ffn, w_up, w_down, g_final):
    n_batch, seq_len, _ = x3d.shape
    grid = (n_batch // bt, seq_len // lt)
    sspec = pl.BlockSpec((bt, S5_N), lambda i, t: (i, 0))
    dma_io = n_batch == bt
    if dma_io:
        xspec = pl.BlockSpec(memory_space=pl.ANY)
        io_scratch = [pltpu.VMEM((2, lt, bt, D_MODEL), f32), pltpu.VMEM((2, lt, bt, D_MODEL), f32),
                      pltpu.SemaphoreType.DMA((2, bt)), pltpu.SemaphoreType.DMA((2, bt))]
    else:
        xspec = pl.BlockSpec((bt, lt, D_MODEL), lambda i, t: (i, t, 0))
        io_scratch = []
    consts = (g_mix, a_re, a_im, wb_re, wb_im, v_re, v_im, d_skip, w_glu, g_ffn, w_up, w_down, g_final)
    const_specs = [_layer_spec(c.shape, 1) if c is w_up or c is w_down else _const_spec(c.shape) for c in consts]
    rows = bt * lt
    return pl.pallas_call(
        functools.partial(_s5_body, bt=bt, lt=lt, dma_io=dma_io),
        grid=grid,
        in_specs=[xspec, sspec, sspec] + const_specs,
        out_specs=[xspec, sspec, sspec],
        out_shape=[jax.ShapeDtypeStruct(x3d.shape, f32), jax.ShapeDtypeStruct((n_batch, S5_N), f32),
                   jax.ShapeDtypeStruct((n_batch, S5_N), f32)],
        scratch_shapes=[pltpu.VMEM((bt, S5_N), f32), pltpu.VMEM((bt, S5_N), f32),
                        pltpu.VMEM((2, rows, S5_BLOCK_ST), f32), pltpu.VMEM((2, rows, S5_BLOCK_ST), f32),
                        pltpu.VMEM((rows, D_MODEL), bf16)] + io_scratch,
        compiler_params=_params(2),
        name="l1_s5_ffn",
    )(x3d, h0_re, h0_im, *consts)


def _rep_lanes(x):
    return jnp.tile(x, (1,) * (x.ndim - 1) + (LANES // x.shape[-1],))


def _trunk(x, state, p):
    n_batch, seq_len, _ = x.shape
    x2d = x.reshape(n_batch * seq_len, D_MODEL)
    if state is None:
        o, hmix, c_new, n_new, m_new = _l0_prompt(x2d, p["g_mix0"], p["w_in"], p["w_gate"], p["gate_bias"],
                                                  n_batch, seq_len)
        n_new = jnp.swapaxes(n_new[:, :, :N_HEADS], 1, 2)
        h0_re = jnp.zeros((n_batch, S5_N), f32)
        h0_im = h0_re
        bt, lt = n_batch, 512 // n_batch
    else:
        c0, n0, m0, h0_re, h0_im = state
        q, k, v, o, gates = _inproj(x2d, p["g_mix0"], p["w_in"], p["w_gate"], p["gate_bias"])
        hmix, c_new, n_new, m_new = _mlstm_core_sample(
            q, k, v, gates, c0, n0.reshape(n_batch, QK_DIM), _rep_lanes(m0), n_batch, seq_len)
        h0_re = h0_re.reshape(n_batch, S5_N)
        h0_im = h0_im.reshape(n_batch, S5_N)
        bt, lt = 512 // seq_len, seq_len
    x2 = _outproj_ffn(hmix, o, x2d, p["head_g"], p["w_out"], p["g_ffn0"], p["w_up"], p["w_down"])
    y, s_re, s_im = _s5_layer(x2.reshape(n_batch, seq_len, D_MODEL), h0_re, h0_im, bt, lt,
                              p["g_mix1"], p["a_re"], p["a_im"], p["wb_re"], p["wb_im"], p["v_re"], p["v_im"],
                              p["d_skip"], p["w_glu"], p["g_ffn1"], p["w_up"], p["w_down"], p["g_final"])
    return (y,
            c_new.reshape(1, n_batch, N_HEADS, DK, DV),
            n_new.reshape(1, n_batch, N_HEADS, DK),
            m_new.reshape(n_batch, LANES)[:, :N_HEADS].reshape(1, n_batch, N_HEADS),
            s_re.reshape(1, n_batch, S5_GROUPS, S5_STATE),
            s_im.reshape(1, n_batch, S5_GROUPS, S5_STATE))


def kernel(x_prompt, x_sample, state_mlstm_C, state_mlstm_n, state_mlstm_m, state_s5_re, state_s5_im,
           norm_mix_g, norm_ffn_g, norm_final_g, mlstm_w_in, mlstm_b_i, mlstm_b_f, mlstm_head_norm_g,
           mlstm_w_out, s5_lambda_re, s5_lambda_im, s5_log_dt, s5_b_re, s5_b_im, s5_c_re, s5_c_im,
           s5_d, s5_w_glu, ffn_w_up, ffn_w_down):
    w_in = mlstm_w_in[0]
    n_qkvo = W_QKVO_COLS
    a_re, a_im, wb_re, wb_im, v_re, v_im = _s5_prep(
        s5_lambda_re[0], s5_lambda_im[0], s5_log_dt[0].reshape(S5_GROUPS, 1),
        jnp.swapaxes(s5_b_re[0], 1, 2), jnp.swapaxes(s5_b_im[0], 1, 2), s5_c_re[0], s5_c_im[0])
    p = dict(
        g_mix0=norm_mix_g[0].reshape(1, D_MODEL), g_mix1=norm_mix_g[1].reshape(1, D_MODEL),
        g_ffn0=norm_ffn_g[0].reshape(1, D_MODEL), g_ffn1=norm_ffn_g[1].reshape(1, D_MODEL),
        g_final=norm_final_g.reshape(1, D_MODEL),
        w_in=w_in,
        w_gate=jnp.concatenate([_rep_lanes(w_in[:, n_qkvo:n_qkvo + N_HEADS]),
                                _rep_lanes(w_in[:, n_qkvo + N_HEADS:])], axis=1).astype(bf16),
        gate_bias=jnp.concatenate([_rep_lanes(mlstm_b_i[0][None]), _rep_lanes(mlstm_b_f[0][None])], axis=1),
        head_g=mlstm_head_norm_g[0].reshape(1, V_DIM),
        w_out=mlstm_w_out[0].astype(bf16),
        w_up=ffn_w_up.astype(bf16), w_down=ffn_w_down.astype(bf16),
        a_re=a_re.reshape(1, S5_N), a_im=a_im.reshape(1, S5_N),
        wb_re=wb_re, wb_im=wb_im, v_re=v_re, v_im=v_im,
        d_skip=s5_d[0].reshape(1, D_MODEL),
        w_glu=s5_w_glu[0].astype(bf16),
    )
    prompt = _trunk(x_prompt, None, p)
    sample = _trunk(x_sample, (state_mlstm_C[0], state_mlstm_n[0], state_mlstm_m[0],
                               state_s5_re[0], state_s5_im[0]), p)
    return (prompt[0], sample[0]) + prompt[1:] + sample[1:]
```

```python
import functools
import types

import jax
import jax.numpy as jnp
import numpy as np
from jax import lax
from jax.experimental import pallas as pl
from jax.experimental.pallas import tpu as pltpu

f32 = jnp.float32
bf16 = jnp.bfloat16

D_MODEL = 1024
N_HEADS = 8
DK = 64
DV = 128
QK_DIM = N_HEADS * DK
V_DIM = N_HEADS * DV
W_QKVO_COLS = 2 * QK_DIM + 2 * V_DIM
D_FF = 4 * D_MODEL
S5_GROUPS = 64
S5_GROUP = 16
S5_STATE = 64
S5_N = S5_GROUPS * S5_STATE
EPS = 1e-6

LANES = 128
SUBLANES = 8
ROW_TILE = 512
CHUNK_ROWS = 256
PROMPT_CHUNKS = 2
FF_CHUNK = 1024
ROW_SPLITS = 2
S5_BLOCK_CH = LANES
S5_BLOCK_ST = S5_BLOCK_CH // S5_GROUP * S5_STATE
N_S5_BLOCKS = D_MODEL // S5_BLOCK_CH
VMEM_LIMIT_BYTES = 56 * 1024 * 1024
NEG_BIG = -1e30
LOG_HEADS = N_HEADS.bit_length() - 1
N_BCAST = 3


def _params(n_axes):
    return pltpu.CompilerParams(dimension_semantics=("arbitrary",) * n_axes,
                                vmem_limit_bytes=VMEM_LIMIT_BYTES)


def _const_spec(shape):
    nd = len(shape)
    return pl.BlockSpec(shape, lambda *_: (0,) * nd, pipeline_mode=pl.Buffered(1))


def _layer_spec(shape, layer):
    nd = len(shape)
    return pl.BlockSpec((None,) + tuple(shape[1:]), lambda *_: (layer,) + (0,) * (nd - 1),
                        pipeline_mode=pl.Buffered(1))


def _dot(a, b):
    return jnp.dot(a, b, preferred_element_type=f32)


def _dot_nt(a, b):
    return lax.dot_general(a, b, (((1,), (1,)), ((), ())), preferred_element_type=f32)


def _rms(x, g):
    return x * lax.rsqrt(jnp.mean(x * x, axis=-1, keepdims=True) + EPS) * g


def _row_splits(rows):
    size = rows // ROW_SPLITS
    return [slice(i * size, (i + 1) * size) for i in range(ROW_SPLITS)]


def _ffn(x1s, g_ref, wup_ref, wdn_ref):
    xn = [_rms(x1, g_ref[...]).astype(bf16) for x1 in x1s]
    acc = [None] * len(x1s)
    for c in range(D_FF // FF_CHUNK):
        cols = slice(c * FF_CHUNK, (c + 1) * FF_CHUNK)
        for i in range(len(x1s)):
            hid = _dot(xn[i], wup_ref[:, cols])
            hid = jnp.square(jnp.maximum(hid, 0.0)).astype(bf16)
            part = _dot(hid, wdn_ref[cols, :])
            acc[i] = part if acc[i] is None else acc[i] + part
    return [x1 + a for x1, a in zip(x1s, acc)]


def _inproj_body(x_ref, g_ref, w_ref, wg_ref, bias_ref, q_ref, k_ref, v_ref, o_ref, gate_ref):
    xn = _rms(x_ref[...], g_ref[...]).astype(bf16)
    q, k, v, o, logi, logf = _inproj_values(xn, w_ref, wg_ref, bias_ref)
    q_ref[...] = q
    k_ref[...] = k
    v_ref[...] = v
    o_ref[...] = o
    gate_ref[:, :LANES] = logi
    gate_ref[:, LANES:] = logf


def _inproj(x2d, g, w_in, w_gate, gate_bias):
    t = x2d.shape[0]
    row = lambda n: pl.BlockSpec((ROW_TILE, n), lambda i: (i, 0))
    return pl.pallas_call(
        _inproj_body,
        grid=(t // ROW_TILE,),
        in_specs=[row(D_MODEL), _const_spec((1, D_MODEL)), _const_spec(w_in.shape),
                  _const_spec(w_gate.shape), _const_spec((1, 2 * LANES))],
        out_specs=[row(QK_DIM), row(QK_DIM), row(V_DIM), row(V_DIM), row(2 * LANES)],
        out_shape=[jax.ShapeDtypeStruct((t, QK_DIM), bf16), jax.ShapeDtypeStruct((t, QK_DIM), f32),
                   jax.ShapeDtypeStruct((t, V_DIM), bf16), jax.ShapeDtypeStruct((t, V_DIM), f32),
                   jax.ShapeDtypeStruct((t, 2 * LANES), f32)],
        compiler_params=_params(1),
        name="l0_inproj",
    )(x2d, g, w_in, w_gate, gate_bias)


def _row_prefix(x, cs, rpos, is_max):
    sh = 1
    while sh < cs:
        prev = pltpu.roll(x, sh, 0)
        if is_max:
            x = jnp.maximum(x, jnp.where(rpos >= sh, prev, NEG_BIG))
        else:
            x = x + jnp.where(rpos >= sh, prev, 0.0)
        sh *= 2
    return x


def _seg_last(x, nseq, cs):
    if nseq == 1:
        return x[cs - 1:cs, :]
    last = x.reshape(nseq, cs, x.shape[1])[:, cs - 1:cs, :]
    return jnp.broadcast_to(last, (nseq, cs, x.shape[1])).reshape(nseq * cs, x.shape[1])


def _gate_algebra(logi, logf, m_in, nseq, cs, sel_ref):
    rows = nseq * cs
    rpos = lax.broadcasted_iota(jnp.int32, (rows, LANES), 0) & (cs - 1)
    b = _row_prefix(logf, cs, rpos, False)
    a = logi - b
    g = jnp.maximum(m_in, _row_prefix(a, cs, rpos, True))
    g_last = _seg_last(g, nseq, cs)
    w_state = jnp.exp(a - g_last)

    group = lax.broadcasted_iota(jnp.int32, (rows, LANES), 1) >> LOG_HEADS
    pieces = jnp.zeros((rows, LANES), f32)
    for i, val in enumerate((g, jnp.exp(m_in - g), jnp.exp(-(b + g)))):
        hi = val.astype(bf16).astype(f32)
        mid = (val - hi).astype(bf16).astype(f32)
        lo = (val - hi - mid).astype(bf16).astype(f32)
        for j, piece in enumerate((hi, mid, lo)):
            pieces = jnp.where(group == 3 * i + j, piece, pieces)
    bcast = _dot(pieces.astype(bf16), sel_ref[...])
    tile = lambda i, h: bcast[:, (i * N_HEADS + h) * DV:(i * N_HEADS + h + 1) * DV]
    return types.SimpleNamespace(
        a_t=a.T,
        w_state=w_state, w_state_t=w_state.T,
        decay=jnp.exp(m_in - g_last),
        m_new=_seg_last(b, nseq, cs) + g_last,
        g=lambda h: tile(0, h), w_inter=lambda h: tile(1, h), clamp=lambda h: tile(2, h))


def _head_outputs(qk, gates, inter, qn, v_h, h_ref, nseq, cs):
    rows = nseq * cs
    log_cs = cs.bit_length() - 1
    heads = range(N_HEADS)
    chunks = range(len(qk))
    ri = lax.broadcasted_iota(jnp.int32, (rows, rows), 0)
    ci = lax.broadcasted_iota(jnp.int32, (rows, rows), 1)
    causal = ci <= ri
    if nseq > 1:
        causal = jnp.logical_and(causal, (ri >> log_cs) == (ci >> log_cs))
    for c in chunks:
        for h in heads:
            g_rows = jnp.concatenate([gates[c].g(h)] * (rows // DV), axis=1)
            w = jnp.exp(jnp.where(causal, gates[c].a_t[h:h + 1, :] - g_rows, NEG_BIG))
            s = qk[c][h]() * w
            s_sum = jnp.sum(s, axis=1, keepdims=True)
            intra = _dot(s.astype(bf16), v_h[c][h])
            w_inter = gates[c].w_inter(h)
            num = w_inter * inter[c][h] + intra
            den = jnp.maximum(jnp.abs(w_inter * qn[c][h] + s_sum), gates[c].clamp(h))
            h_ref[c * rows:(c + 1) * rows, h * DV:(h + 1) * DV] = num * lax.rsqrt(
                jnp.mean(num * num, axis=1, keepdims=True) + EPS * (den * den))


def _sequence_chunks(q_ref, k_ref, v_ref, gate_ref, sel_ref, h_ref, c_state, n_state, m_in, n_chunks,
                     after_gates=lambda: None):
    rows = CHUNK_ROWS
    heads = range(N_HEADS)
    chunks = range(n_chunks)
    c_state, n_state = list(c_state), list(n_state)
    rsl = lambda c: slice(c * rows, (c + 1) * rows)
    q_h = [[q_ref[rsl(c), h * DK:(h + 1) * DK] for h in heads] for c in chunks]
    k_h = [[k_ref[rsl(c), h * DK:(h + 1) * DK] for h in heads] for c in chunks]
    v_h = [[v_ref[rsl(c), h * DV:(h + 1) * DV] for h in heads] for c in chunks]

    qk = [[functools.partial(_dot_nt, q_h[c][h], k_h[c][h].astype(bf16)) for h in heads] for c in chunks]

    gates = []
    for c in chunks:
        gates.append(_gate_algebra(gate_ref[rsl(c), :LANES], gate_ref[rsl(c), LANES:],
                                   m_in if c == 0 else gates[c - 1].m_new, 1, rows, sel_ref))
    after_gates()

    inter = [[None] * N_HEADS for _ in chunks]
    qn = [[None] * N_HEADS for _ in chunks]
    for c in chunks:
        ga = gates[c]
        k_t = k_ref[rsl(c), :].T
        for h in heads:
            rhs = jnp.concatenate([c_state[h].astype(bf16),
                                   jnp.broadcast_to(n_state[h], (DK, DV)).astype(bf16)], axis=1)
            inter2 = _dot(q_h[c][h], rhs)
            inter[c][h] = inter2[:, :DV]
            qn[c][h] = inter2[:, DV:]
            wk_t = k_t[h * DK:(h + 1) * DK, :] * ga.w_state_t[h:h + 1, :]
            dec = ga.decay[:, h:h + 1]
            c_state[h] = dec * c_state[h] + _dot(wk_t.astype(bf16), v_h[c][h])
            n_state[h] = dec * n_state[h] + jnp.sum(wk_t, axis=1, keepdims=True)

    _head_outputs(qk, gates, inter, qn, v_h, h_ref, 1, rows)
    return c_state, n_state, gates[-1].m_new


def _mlstm_sample_body(q_ref, k_ref, v_ref, gate_ref, sel_ref, c0_ref, n0_ref, m0_ref,
                       h_ref, cout_ref, nout_ref, mout_ref, *, nseq, cs):
    rows = nseq * cs
    log_cs = cs.bit_length() - 1
    log_dk = DK.bit_length() - 1
    heads = range(N_HEADS)
    q_h = [q_ref[:, h * DK:(h + 1) * DK] for h in heads]
    k_h = [k_ref[:, h * DK:(h + 1) * DK] for h in heads]
    v_h = [v_ref[:, h * DV:(h + 1) * DV] for h in heads]
    qk = [functools.partial(_dot_nt, q_h[h], k_h[h].astype(bf16)) for h in heads]
    m_in = jnp.broadcast_to(m0_ref[...][:, None, :], (nseq, cs, LANES)).reshape(rows, LANES)
    ga = _gate_algebra(gate_ref[:, :LANES], gate_ref[:, LANES:], m_in, nseq, cs, sel_ref)

    bd_q = ((lax.broadcasted_iota(jnp.int32, (rows, nseq * DK), 0) >> log_cs)
            == (lax.broadcasted_iota(jnp.int32, (rows, nseq * DK), 1) >> log_dk))
    bd_k = ((lax.broadcasted_iota(jnp.int32, (nseq * DK, rows), 0) >> log_dk)
            == (lax.broadcasted_iota(jnp.int32, (nseq * DK, rows), 1) >> log_cs))
    k_t = k_ref[...].T
    inter, qn = [], []
    for h in heads:
        c_prev = c0_ref[:, h].reshape(nseq * DK, DV)
        q32 = q_h[h].astype(f32)
        q_bd = jnp.where(bd_q, jnp.concatenate([q32] * nseq, axis=1), 0.0).astype(bf16)
        inter.append(_dot(q_bd, c_prev.astype(bf16)))
        n_prev = n0_ref[:, h * DK:(h + 1) * DK]
        n_rows = jnp.broadcast_to(n_prev[:, None, :], (nseq, cs, DK)).reshape(rows, DK)
        qn.append(jnp.sum(q32 * n_rows, axis=1, keepdims=True))

        wk_t = k_t[h * DK:(h + 1) * DK, :] * ga.w_state_t[h:h + 1, :]
        wk_bd = jnp.where(bd_k, jnp.broadcast_to(wk_t[None], (nseq, DK, rows)).reshape(nseq * DK, rows), 0.0)
        dc = _dot(wk_bd.astype(bf16), v_h[h])
        dec_col = ga.decay[:, h:h + 1]
        dec_rows = jnp.broadcast_to(dec_col.reshape(nseq, 1, cs, 1), (nseq, DK // cs, cs, 1)).reshape(nseq * DK, 1)
        cout_ref[:, h] = (dec_rows * c_prev + dc).reshape(nseq, DK, DV)
        dec_seq = jnp.max(dec_col.reshape(nseq, cs, 1), axis=1)
        nout_ref[:, h * DK:(h + 1) * DK] = dec_seq * n_prev + jnp.sum(
            (ga.w_state[:, h:h + 1] * k_h[h]).reshape(nseq, cs, DK), axis=1)
    mout_ref[...] = jnp.max(ga.m_new.reshape(nseq, cs, LANES), axis=1)
    _head_outputs([qk], [ga], [inter], [qn], [v_h], h_ref, nseq, cs)


def _inproj_values(xn, w_ref, wg_ref, bias_ref):
    q = _dot(xn, w_ref[:, 0:QK_DIM]).astype(bf16)
    k = _dot(xn, w_ref[:, QK_DIM:2 * QK_DIM]) * (DK ** -0.5)
    v = _dot(xn, w_ref[:, 2 * QK_DIM:2 * QK_DIM + V_DIM]).astype(bf16)
    o = _dot(xn, w_ref[:, 2 * QK_DIM + V_DIM:W_QKVO_COLS])
    gz = _dot(xn, wg_ref[...]) + bias_ref[...]
    return q, k, v, o, gz[:, :LANES], jax.nn.log_sigmoid(gz[:, LANES:])


def _l0_prompt_body(x_ref, g_ref, w_ref, wg_ref, bias_ref, sel_ref,
                    o_ref, h_ref, cout_ref, nout_ref, mout_ref,
                    qa, ka, va, ga, qb, kb, vb, gb, c_scr, n_scr, m_scr, *, nt, n_chunks):
    step = pl.program_id(0)
    heads = range(N_HEADS)

    @pl.when(step == 0)
    def _():
        for ref in (qb, kb, vb, gb, c_scr, n_scr, m_scr):
            ref[...] = jnp.zeros_like(ref)

    def half_step(wr, rd):
        def project():
            q_w, k_w, v_w, g_w = wr
            xn = _rms(x_ref[...], g_ref[...]).astype(bf16)
            q, k, v, o, logi, logf = _inproj_values(xn, w_ref, wg_ref, bias_ref)
            q_w[...] = q
            k_w[...] = k
            v_w[...] = v
            o_ref[...] = o
            g_w[:, :LANES] = logi
            g_w[:, LANES:] = logf

        first = lax.rem(step + nt - 1, nt) == 0
        c_state = [jnp.where(first, 0.0, c_scr[h]) for h in heads]
        n_state = [jnp.where(first, 0.0, n_scr[h]) for h in heads]
        m_in = jnp.where(first, 0.0, m_scr[...])
        c_state, n_state, m_new = _sequence_chunks(*rd, sel_ref, h_ref, c_state, n_state, m_in, n_chunks,
                                                   after_gates=project)
        for h in heads:
            c_scr[h] = c_state[h]
            n_scr[h] = n_state[h]
        m_scr[...] = m_new

    @pl.when(lax.rem(step, 2) == 0)
    def _():
        half_step((qa, ka, va, ga), (qb, kb, vb, gb))

    @pl.when(lax.rem(step, 2) == 1)
    def _():
        half_step((qb, kb, vb, gb), (qa, ka, va, ga))

    @pl.when(jnp.logical_and(step >= 1, lax.rem(step, nt) == 0))
    def _():
        cout_ref[0] = c_scr[...]
        lane = lax.broadcasted_iota(jnp.int32, (DK, LANES), 1)
        n_mat = jnp.zeros((DK, LANES), f32)
        for h in heads:
            n_mat = jnp.where(lane == h, jnp.broadcast_to(n_scr[h], (DK, LANES)), n_mat)
        nout_ref[0] = n_mat
        mout_ref[0] = m_scr[...]


def _bcast_selector():
    src = np.arange(LANES)
    dst = np.arange(N_BCAST * N_HEADS * DV) // DV
    hit = (((src[:, None] >> LOG_HEADS) // 3 == dst[None, :] // N_HEADS)
           & ((src[:, None] & (N_HEADS - 1)) == dst[None, :] % N_HEADS))
    return jnp.asarray(hit, dtype=bf16)


def _l0_prompt(x2d, g, w_in, w_gate, gate_bias, n_batch, seq_len):
    t = x2d.shape[0]
    sel = _bcast_selector()
    rows = PROMPT_CHUNKS * CHUNK_ROWS
    nt = seq_len // rows
    n_blocks = t // rows
    cur = lambda s: (jnp.minimum(s, n_blocks - 1), 0)
    prev = lambda s: (jnp.maximum(s - 1, 0), 0)
    seq = lambda s: jnp.maximum(s - 1, 0) // nt
    bufs = [pltpu.VMEM((rows, QK_DIM), bf16), pltpu.VMEM((rows, QK_DIM), f32),
            pltpu.VMEM((rows, V_DIM), bf16), pltpu.VMEM((rows, 2 * LANES), f32)]
    return pl.pallas_call(
        functools.partial(_l0_prompt_body, nt=nt, n_chunks=PROMPT_CHUNKS),
        grid=(n_blocks + 1,),
        in_specs=[pl.BlockSpec((rows, D_MODEL), cur), _const_spec((1, D_MODEL)), _const_spec(w_in.shape),
                  _const_spec(w_gate.shape), _const_spec((1, 2 * LANES)), _const_spec(sel.shape)],
        out_specs=[pl.BlockSpec((rows, V_DIM), cur), pl.BlockSpec((rows, V_DIM), prev),
                   pl.BlockSpec((1, N_HEADS, DK, DV), lambda s: (seq(s), 0, 0, 0)),
                   pl.BlockSpec((1, DK, LANES), lambda s: (seq(s), 0, 0)),
                   pl.BlockSpec((1, 1, LANES), lambda s: (seq(s), 0, 0))],
        out_shape=[jax.ShapeDtypeStruct((t, V_DIM), f32), jax.ShapeDtypeStruct((t, V_DIM), f32),
                   jax.ShapeDtypeStruct((n_batch, N_HEADS, DK, DV), f32),
                   jax.ShapeDtypeStruct((n_batch, DK, LANES), f32),
                   jax.ShapeDtypeStruct((n_batch, 1, LANES), f32)],
        scratch_shapes=bufs + bufs + [pltpu.VMEM((N_HEADS, DK, DV), f32), pltpu.VMEM((N_HEADS, DK, 1), f32),
                                      pltpu.VMEM((1, LANES), f32)],
        compiler_params=_params(1),
        name="l0_inproj_mlstm",
    )(x2d, g, w_in, w_gate, gate_bias, sel)


def _mlstm_core_sample(q, k, v, gates, c0, n0, m0, n_batch, seq_len):
    t = q.shape[0]
    sel = _bcast_selector()
    nseq = CHUNK_ROWS // seq_len
    row = lambda n: pl.BlockSpec((CHUNK_ROWS, n), lambda i: (i, 0))
    cspec = pl.BlockSpec((nseq, N_HEADS, DK, DV), lambda i: (i, 0, 0, 0))
    nspec = pl.BlockSpec((nseq, QK_DIM), lambda i: (i, 0))
    mspec = pl.BlockSpec((nseq, LANES), lambda i: (i, 0))
    return pl.pallas_call(
        functools.partial(_mlstm_sample_body, nseq=nseq, cs=seq_len),
        grid=(t // CHUNK_ROWS,),
        in_specs=[row(QK_DIM), row(QK_DIM), row(V_DIM), row(2 * LANES), _const_spec(sel.shape), cspec, nspec, mspec],
        out_specs=[row(V_DIM), cspec, nspec, mspec],
        out_shape=[jax.ShapeDtypeStruct((t, V_DIM), f32),
                   jax.ShapeDtypeStruct((n_batch, N_HEADS, DK, DV), f32),
                   jax.ShapeDtypeStruct((n_batch, QK_DIM), f32),
                   jax.ShapeDtypeStruct((n_batch, LANES), f32)],
        compiler_params=_params(1),
        name="l0_mlstm_core",
    )(q, k, v, gates, sel, c0, n0, m0)


def _outproj_ffn_body(h_ref, o_ref, x_ref, hg_ref, wout_ref, gffn_ref, wup_ref, wdn_ref, y_ref):
    x1s = []
    for r in _row_splits(h_ref.shape[0]):
        hn = h_ref[r, :] * hg_ref[...] * jax.nn.sigmoid(o_ref[r, :])
        x1s.append(x_ref[r, :] + _dot(hn.astype(bf16), wout_ref[...]))
    for r, y in zip(_row_splits(h_ref.shape[0]), _ffn(x1s, gffn_ref, wup_ref, wdn_ref)):
        y_ref[r, :] = y


def _outproj_ffn(hmix, o, x2d, head_g, w_out, g_ffn, w_up, w_down):
    t = x2d.shape[0]
    row = pl.BlockSpec((ROW_TILE, D_MODEL), lambda i: (i, 0))
    return pl.pallas_call(
        _outproj_ffn_body,
        grid=(t // ROW_TILE,),
        in_specs=[row, row, row, _const_spec((1, V_DIM)), _const_spec(w_out.shape),
                  _const_spec((1, D_MODEL)), _layer_spec(w_up.shape, 0), _layer_spec(w_down.shape, 0)],
        out_specs=row,
        out_shape=jax.ShapeDtypeStruct((t, D_MODEL), f32),
        compiler_params=_params(1),
        name="l0_outproj_ffn",
    )(hmix, o, x2d, head_g, w_out, g_ffn, w_up, w_down)


def _s5_prep_body(lre_ref, lim_ref, ldt_ref, bre_ref, bim_ref, cre_ref, cim_ref,
                  are_ref, aim_ref, wbr_ref, wbi_ref, vre_ref, vim_ref):
    lr = lre_ref[...]
    li = lim_ref[...]
    dt = jnp.exp(ldt_ref[...])
    mag = jnp.exp(lr * dt)
    a_re = mag * jnp.cos(li * dt)
    a_im = mag * jnp.sin(li * dt)
    den = lr * lr + li * li
    z_re = a_re - 1.0
    coef_re = ((z_re * lr + a_im * li) / den)[:, None, :]
    coef_im = ((a_im * lr - z_re * li) / den)[:, None, :]
    br = bre_ref[...]
    bi = bim_ref[...]
    are_ref[...] = a_re
    aim_ref[...] = a_im
    bb = (coef_re * br - coef_im * bi, coef_re * bi + coef_im * br)
    c_t = (cre_ref[...].reshape(S5_GROUPS * S5_GROUP, S5_STATE).T,
           cim_ref[...].reshape(S5_GROUPS * S5_GROUP, S5_STATE).T)

    groups = S5_BLOCK_CH // S5_GROUP
    log_ch, log_st = S5_GROUP.bit_length() - 1, S5_STATE.bit_length() - 1
    shape_b = (S5_BLOCK_CH, S5_BLOCK_ST)
    diag_b = ((lax.broadcasted_iota(jnp.int32, shape_b, 0) >> log_ch)
              == (lax.broadcasted_iota(jnp.int32, shape_b, 1) >> log_st))
    shape_c = (S5_BLOCK_ST, S5_BLOCK_CH)
    diag_c = ((lax.broadcasted_iota(jnp.int32, shape_c, 0) >> log_st)
              == (lax.broadcasted_iota(jnp.int32, shape_c, 1) >> log_ch))
    for j in range(N_S5_BLOCKS):
        for src, dst in zip(bb, (wbr_ref, wbi_ref)):
            blk = src[j * groups:(j + 1) * groups].reshape(S5_BLOCK_CH, S5_STATE)
            dst[j] = jnp.where(diag_b, jnp.concatenate([blk] * groups, axis=1), 0.0).astype(bf16)
        for src, dst in zip(c_t, (vre_ref, vim_ref)):
            blk = src[:, j * S5_BLOCK_CH:(j + 1) * S5_BLOCK_CH]
            rep = jnp.broadcast_to(blk[None], (groups, S5_STATE, S5_BLOCK_CH)).reshape(S5_BLOCK_ST, S5_BLOCK_CH)
            dst[j] = jnp.where(diag_c, rep, 0.0).astype(bf16)


def _s5_prep(lam_re, lam_im, log_dt, b_re_t, b_im_t, c_re, c_im):
    gp = jax.ShapeDtypeStruct((S5_GROUPS, S5_STATE), f32)
    wb = jax.ShapeDtypeStruct((N_S5_BLOCKS, S5_BLOCK_CH, S5_BLOCK_ST), bf16)
    vc = jax.ShapeDtypeStruct((N_S5_BLOCKS, S5_BLOCK_ST, S5_BLOCK_CH), bf16)
    return pl.pallas_call(_s5_prep_body, out_shape=[gp, gp, wb, wb, vc, vc], name="l1_s5_prep")(
        lam_re, lam_im, log_dt, b_re_t, b_im_t, c_re, c_im)


def _s5_body(x_ref, h0r_ref, h0i_ref, g_ref, are_ref, aim_ref, wbr_ref, wbi_ref, vre_ref, vim_ref,
             d_ref, wglu_ref, gffn_ref, wup_ref, wdn_ref, gfin_ref,
             y_ref, sre_ref, sim_ref, sr_scr, si_scr, hr_scr, hi_scr, act_scr, *io_scr, bt, lt, dma_io):
    rows = bt * lt
    step = pl.program_id(1)
    n_steps = pl.num_programs(1)

    @pl.when(step == 0)
    def _():
        sr_scr[...] = h0r_ref[...]
        si_scr[...] = h0i_ref[...]

    if dma_io:
        xbuf, ybuf, in_sem, out_sem = io_scr
        slot = lax.rem(step, 2)

        def in_copy(s, sl, b):
            return pltpu.make_async_copy(x_ref.at[b, pl.ds(s * lt, lt), :], xbuf.at[sl, :, b, :], in_sem.at[sl, b])

        def out_copy(s, sl, b):
            return pltpu.make_async_copy(ybuf.at[sl, :, b, :], y_ref.at[b, pl.ds(s * lt, lt), :], out_sem.at[sl, b])

        @pl.when(step == 0)
        def _():
            for b in range(bt):
                in_copy(0, 0, b).start()

        @pl.when(step + 1 < n_steps)
        def _():
            for b in range(bt):
                in_copy(step + 1, 1 - slot, b).start()

        for b in range(bt):
            in_copy(step, slot, b).wait()
        xt = xbuf[slot].reshape(rows, D_MODEL)
    else:
        xt = jnp.concatenate([x_ref[:, t, :] for t in range(lt)], axis=0)
    u = _rms(xt, g_ref[...])
    ub = u.astype(bf16)

    def project_in(j):
        ch = slice(j * S5_BLOCK_CH, (j + 1) * S5_BLOCK_CH)
        hr_scr[j % 2] = _dot(ub[:, ch], wbr_ref[j])
        hi_scr[j % 2] = _dot(ub[:, ch], wbi_ref[j])

    project_in(0)
    for j in range(N_S5_BLOCKS):
        ch = slice(j * S5_BLOCK_CH, (j + 1) * S5_BLOCK_CH)
        st = slice(j * S5_BLOCK_ST, (j + 1) * S5_BLOCK_ST)
        if j + 1 < N_S5_BLOCKS:
            project_in(j + 1)
        a_re = jnp.broadcast_to(are_ref[:, st], (bt, S5_BLOCK_ST))
        a_im = jnp.broadcast_to(aim_ref[:, st], (bt, S5_BLOCK_ST))
        s_re = sr_scr[:, st]
        s_im = si_scr[:, st]
        for t in range(lt):
            r = slice(t * bt, (t + 1) * bt)
            n_re = a_re * s_re - a_im * s_im + hr_scr[j % 2, r, :]
            n_im = a_re * s_im + a_im * s_re + hi_scr[j % 2, r, :]
            hr_scr[j % 2, r, :] = n_re
            hi_scr[j % 2, r, :] = n_im
            s_re, s_im = n_re, n_im
        sr_scr[:, st] = s_re
        si_scr[:, st] = s_im
        yj = (_dot(hr_scr[j % 2].astype(bf16), vre_ref[j]) - _dot(hi_scr[j % 2].astype(bf16), vim_ref[j])
              + d_ref[:, ch] * u[:, ch])
        act_scr[:, ch] = jax.nn.gelu(yj).astype(bf16)

    sre_ref[...] = sr_scr[...]
    sim_ref[...] = si_scr[...]
    x3s = []
    for r in _row_splits(rows):
        ag = _dot(act_scr[r, :], wglu_ref[...])
        x3s.append(xt[r, :] + ag[:, :D_MODEL] * jax.nn.sigmoid(ag[:, D_MODEL:]))
    y = jnp.concatenate([_rms(x4, gfin_ref[...]) for x4 in _ffn(x3s, gffn_ref, wup_ref, wdn_ref)], axis=0)
    if dma_io:
        @pl.when(step >= 2)
        def _():
            for b in range(bt):
                out_copy(step - 2, slot, b).wait()

        ybuf[slot] = y.reshape(lt, bt, D_MODEL)
        for b in range(bt):
            out_copy(step, slot, b).start()

        @pl.when(step == n_steps - 1)
        def _():
            @pl.when(step >= 1)
            def _():
                for b in range(bt):
                    out_copy(step - 1, 1 - slot, b).wait()

            for b in range(bt):
                out_copy(step, slot, b).wait()
    else:
        for t in range(lt):
            y_ref[:, t, :] = y[t * bt:(t + 1) * bt, :]


def _s5_layer(x3d, h0_re, h0_im, bt, lt, g_mix, a_re, a_im, wb_re, wb_im, v_re, v_im, d_skip,
              w_glu, g_ffn, w_up, w_down, g_final):
    n_batch, seq_len, _ = x3d.shape
    grid = (n_batch // bt, seq_len // lt)
    sspec = pl.BlockSpec((bt, S5_N), lambda i, t: (i, 0))
    dma_io = n_batch == bt
    if dma_io:
        xspec = pl.BlockSpec(memory_space=pl.ANY)
        io_scratch = [pltpu.VMEM((2, lt, bt, D_MODEL), f32), pltpu.VMEM((2, lt, bt, D_MODEL), f32),
                      pltpu.SemaphoreType.DMA((2, bt)), pltpu.SemaphoreType.DMA((2, bt))]
    else:
        xspec = pl.BlockSpec((bt, lt, D_MODEL), lambda i, t: (i, t, 0))
        io_scratch = []
    consts = (g_mix, a_re, a_im, wb_re, wb_im, v_re, v_im, d_skip, w_glu, g_ffn, w_up, w_down, g_final)
    const_specs = [_layer_spec(c.shape, 1) if c is w_up or c is w_down else _const_spec(c.shape) for c in consts]
    rows = bt * lt
    return pl.pallas_call(
        functools.partial(_s5_body, bt=bt, lt=lt, dma_io=dma_io),
        grid=grid,
        in_specs=[xspec, sspec, sspec] + const_specs,
        out_specs=[xspec, sspec, sspec],
        out_shape=[jax.ShapeDtypeStruct(x3d.shape, f32), jax.ShapeDtypeStruct((n_batch, S5_N), f32),
                   jax.ShapeDtypeStruct((n_batch, S5_N), f32)],
        scratch_shapes=[pltpu.VMEM((bt, S5_N), f32), pltpu.VMEM((bt, S5_N), f32),
                        pltpu.VMEM((2, rows, S5_BLOCK_ST), f32), pltpu.VMEM((2, rows, S5_BLOCK_ST), f32),
                        pltpu.VMEM((rows, D_MODEL), bf16)] + io_scratch,
        compiler_params=_params(2),
        name="l1_s5_ffn",
    )(x3d, h0_re, h0_im, *consts)


def _rep_lanes(x):
    return jnp.tile(x, (1,) * (x.ndim - 1) + (LANES // x.shape[-1],))


def _trunk(x, state, p):
    n_batch, seq_len, _ = x.shape
    x2d = x.reshape(n_batch * seq_len, D_MODEL)
    if state is None:
        o, hmix, c_new, n_new, m_new = _l0_prompt(x2d, p["g_mix0"], p["w_in"], p["w_gate"], p["gate_bias"],
                                                  n_batch, seq_len)
        n_new = jnp.swapaxes(n_new[:, :, :N_HEADS], 1, 2)
        h0_re = jnp.zeros((n_batch, S5_N), f32)
        h0_im = h0_re
        bt, lt = n_batch, 512 // n_batch
    else:
        c0, n0, m0, h0_re, h0_im = state
        q, k, v, o, gates = _inproj(x2d, p["g_mix0"], p["w_in"], p["w_gate"], p["gate_bias"])
        hmix, c_new, n_new, m_new = _mlstm_core_sample(
            q, k, v, gates, c0, n0.reshape(n_batch, QK_DIM), _rep_lanes(m0), n_batch, seq_len)
        h0_re = h0_re.reshape(n_batch, S5_N)
        h0_im = h0_im.reshape(n_batch, S5_N)
        bt, lt = 512 // seq_len, seq_len
    x2 = _outproj_ffn(hmix, o, x2d, p["head_g"], p["w_out"], p["g_ffn0"], p["w_up"], p["w_down"])
    y, s_re, s_im = _s5_layer(x2.reshape(n_batch, seq_len, D_MODEL), h0_re, h0_im, bt, lt,
                              p["g_mix1"], p["a_re"], p["a_im"], p["wb_re"], p["wb_im"], p["v_re"], p["v_im"],
                              p["d_skip"], p["w_glu"], p["g_ffn1"], p["w_up"], p["w_down"], p["g_final"])
    return (y,
            c_new.reshape(1, n_batch, N_HEADS, DK, DV),
            n_new.reshape(1, n_batch, N_HEADS, DK),
            m_new.reshape(n_batch, LANES)[:, :N_HEADS].reshape(1, n_batch, N_HEADS),
            s_re.reshape(1, n_batch, S5_GROUPS, S5_STATE),
            s_im.reshape(1, n_batch, S5_GROUPS, S5_STATE))


def kernel(x_prompt, x_sample, state_mlstm_C, state_mlstm_n, state_mlstm_m, state_s5_re, state_s5_im,
           norm_mix_g, norm_ffn_g, norm_final_g, mlstm_w_in, mlstm_b_i, mlstm_b_f, mlstm_head_norm_g,
           mlstm_w_out, s5_lambda_re, s5_lambda_im, s5_log_dt, s5_b_re, s5_b_im, s5_c_re, s5_c_im,
           s5_d, s5_w_glu, ffn_w_up, ffn_w_down):
    w_in = mlstm_w_in[0]
    n_qkvo = W_QKVO_COLS
    a_re, a_im, wb_re, wb_im, v_re, v_im = _s5_prep(
        s5_lambda_re[0], s5_lambda_im[0], s5_log_dt[0].reshape(S5_GROUPS, 1),
        jnp.swapaxes(s5_b_re[0], 1, 2), jnp.swapaxes(s5_b_im[0], 1, 2), s5_c_re[0], s5_c_im[0])
    p = dict(
        g_mix0=norm_mix_g[0].reshape(1, D_MODEL), g_mix1=norm_mix_g[1].reshape(1, D_MODEL),
        g_ffn0=norm_ffn_g[0].reshape(1, D_MODEL), g_ffn1=norm_ffn_g[1].reshape(1, D_MODEL),
        g_final=norm_final_g.reshape(1, D_MODEL),
        w_in=w_in.astype(bf16),
        w_gate=jnp.concatenate([_rep_lanes(w_in[:, n_qkvo:n_qkvo + N_HEADS]),
                                _rep_lanes(w_in[:, n_qkvo + N_HEADS:])], axis=1).astype(bf16),
        gate_bias=jnp.concatenate([_rep_lanes(mlstm_b_i[0][None]), _rep_lanes(mlstm_b_f[0][None])], axis=1),
        head_g=mlstm_head_norm_g[0].reshape(1, V_DIM),
        w_out=mlstm_w_out[0].astype(bf16),
        w_up=ffn_w_up.astype(bf16), w_down=ffn_w_down.astype(bf16),
        a_re=a_re.reshape(1, S5_N), a_im=a_im.reshape(1, S5_N),
        wb_re=wb_re, wb_im=wb_im, v_re=v_re, v_im=v_im,
        d_skip=s5_d[0].reshape(1, D_MODEL),
        w_glu=s5_w_glu[0].astype(bf16),
    )
    prompt = _trunk(x_prompt, None, p)
    sample = _trunk(x_sample, (state_mlstm_C[0], state_mlstm_n[0], state_mlstm_m[0],
                               state_s5_re[0], state_s5_im[0]), p)
    return (prompt[0], sample[0]) + prompt[1:] + sample[1:]
```

```python
import functools
import types

import jax
import jax.numpy as jnp
import numpy as np
from jax import lax
from jax.experimental import pallas as pl
from jax.experimental.pallas import tpu as pltpu

f32 = jnp.float32
bf16 = jnp.bfloat16

D_MODEL = 1024
N_HEADS = 8
DK = 64
DV = 128
QK_DIM = N_HEADS * DK
V_DIM = N_HEADS * DV
W_QKVO_COLS = 2 * QK_DIM + 2 * V_DIM
D_FF = 4 * D_MODEL
S5_GROUPS = 64
S5_GROUP = 16
S5_STATE = 64
S5_N = S5_GROUPS * S5_STATE
EPS = 1e-6

LANES = 128
SUBLANES = 8
ROW_TILE = 512
CHUNK_ROWS = 256
PROMPT_CHUNKS = 2
FF_CHUNK = 1024
ROW_SPLITS = 2
S5_BLOCK_CH = LANES
S5_BLOCK_ST = S5_BLOCK_CH // S5_GROUP * S5_STATE
N_S5_BLOCKS = D_MODEL // S5_BLOCK_CH
VMEM_LIMIT_BYTES = 56 * 1024 * 1024
NEG_BIG = -1e30
LOG_HEADS = N_HEADS.bit_length() - 1
N_BCAST = 3


def _params(n_axes):
    return pltpu.CompilerParams(dimension_semantics=("arbitrary",) * n_axes,
                                vmem_limit_bytes=VMEM_LIMIT_BYTES)


def _const_spec(shape):
    nd = len(shape)
    return pl.BlockSpec(shape, lambda *_: (0,) * nd, pipeline_mode=pl.Buffered(1))


def _layer_spec(shape, layer):
    nd = len(shape)
    return pl.BlockSpec((None,) + tuple(shape[1:]), lambda *_: (layer,) + (0,) * (nd - 1),
                        pipeline_mode=pl.Buffered(1))


def _dot(a, b):
    return jnp.dot(a, b, preferred_element_type=f32)


def _dot_nt(a, b):
    return lax.dot_general(a, b, (((1,), (1,)), ((), ())), preferred_element_type=f32)


def _rms(x, g):
    return x * lax.rsqrt(jnp.mean(x * x, axis=-1, keepdims=True) + EPS) * g


def _row_splits(rows):
    size = rows // ROW_SPLITS
    return [slice(i * size, (i + 1) * size) for i in range(ROW_SPLITS)]


def _ffn(x1s, g_ref, wup_ref, wdn_ref):
    xn = [_rms(x1, g_ref[...]).astype(bf16) for x1 in x1s]
    acc = [None] * len(x1s)
    for c in range(D_FF // FF_CHUNK):
        cols = slice(c * FF_CHUNK, (c + 1) * FF_CHUNK)
        for i in range(len(x1s)):
            hid = _dot(xn[i], wup_ref[:, cols])
            hid = jnp.square(jnp.maximum(hid, 0.0)).astype(bf16)
            part = _dot(hid, wdn_ref[cols, :])
            acc[i] = part if acc[i] is None else acc[i] + part
    return [x1 + a for x1, a in zip(x1s, acc)]


def _inproj_body(x_ref, g_ref, w_ref, wg_ref, bias_ref, q_ref, k_ref, v_ref, o_ref, gate_ref):
    xn = _rms(x_ref[...], g_ref[...]).astype(bf16)
    q, k, v, o, logi, logf = _inproj_values(xn, w_ref, wg_ref, bias_ref)
    q_ref[...] = q
    k_ref[...] = k
    v_ref[...] = v
    o_ref[...] = o
    gate_ref[:, :LANES] = logi
    gate_ref[:, LANES:] = logf


def _inproj(x2d, g, w_in, w_gate, gate_bias):
    t = x2d.shape[0]
    row = lambda n: pl.BlockSpec((ROW_TILE, n), lambda i: (i, 0))
    return pl.pallas_call(
        _inproj_body,
        grid=(t // ROW_TILE,),
        in_specs=[row(D_MODEL), _const_spec((1, D_MODEL)), _const_spec(w_in.shape),
                  _const_spec(w_gate.shape), _const_spec((1, 2 * LANES))],
        out_specs=[row(QK_DIM), row(QK_DIM), row(V_DIM), row(V_DIM), row(2 * LANES)],
        out_shape=[jax.ShapeDtypeStruct((t, QK_DIM), bf16), jax.ShapeDtypeStruct((t, QK_DIM), f32),
                   jax.ShapeDtypeStruct((t, V_DIM), bf16), jax.ShapeDtypeStruct((t, V_DIM), f32),
                   jax.ShapeDtypeStruct((t, 2 * LANES), f32)],
        compiler_params=_params(1),
        name="l0_inproj",
    )(x2d, g, w_in, w_gate, gate_bias)


def _row_prefix(x, cs, rpos, is_max):
    sh = 1
    while sh < cs:
        prev = pltpu.roll(x, sh, 0)
        if is_max:
            x = jnp.maximum(x, jnp.where(rpos >= sh, prev, NEG_BIG))
        else:
            x = x + jnp.where(rpos >= sh, prev, 0.0)
        sh *= 2
    return x


def _seg_last(x, nseq, cs):
    if nseq == 1:
        return x[cs - 1:cs, :]
    last = x.reshape(nseq, cs, x.shape[1])[:, cs - 1:cs, :]
    return jnp.broadcast_to(last, (nseq, cs, x.shape[1])).reshape(nseq * cs, x.shape[1])


def _gate_algebra(logi, logf, m_in, nseq, cs, sel_ref):
    rows = nseq * cs
    rpos = lax.broadcasted_iota(jnp.int32, (rows, LANES), 0) & (cs - 1)
    b = _row_prefix(logf, cs, rpos, False)
    a = logi - b
    g = jnp.maximum(m_in, _row_prefix(a, cs, rpos, True))
    g_last = _seg_last(g, nseq, cs)
    w_state = jnp.exp(a - g_last)

    group = lax.broadcasted_iota(jnp.int32, (rows, LANES), 1) >> LOG_HEADS
    pieces = jnp.zeros((rows, LANES), f32)
    for i, val in enumerate((g, jnp.exp(m_in - g), jnp.exp(-(b + g)))):
        hi = val.astype(bf16).astype(f32)
        mid = (val - hi).astype(bf16).astype(f32)
        lo = (val - hi - mid).astype(bf16).astype(f32)
        for j, piece in enumerate((hi, mid, lo)):
            pieces = jnp.where(group == 3 * i + j, piece, pieces)
    bcast = _dot(pieces.astype(bf16), sel_ref[...])
    tile = lambda i, h: bcast[:, (i * N_HEADS + h) * DV:(i * N_HEADS + h + 1) * DV]
    return types.SimpleNamespace(
        a_t=a.T,
        w_state=w_state, w_state_t=w_state.T,
        decay=jnp.exp(m_in - g_last),
        m_new=_seg_last(b, nseq, cs) + g_last,
        g=lambda h: tile(0, h), w_inter=lambda h: tile(1, h), clamp=lambda h: tile(2, h))


def _head_outputs(qk, gates, inter, qn, v_h, h_ref, nseq, cs):
    rows = nseq * cs
    log_cs = cs.bit_length() - 1
    heads = range(N_HEADS)
    chunks = range(len(qk))
    ri = lax.broadcasted_iota(jnp.int32, (rows, rows), 0)
    ci = lax.broadcasted_iota(jnp.int32, (rows, rows), 1)
    causal = ci <= ri
    if nseq > 1:
        causal = jnp.logical_and(causal, (ri >> log_cs) == (ci >> log_cs))
    for c in chunks:
        for h in heads:
            g_rows = jnp.concatenate([gates[c].g(h)] * (rows // DV), axis=1)
            w = jnp.exp(jnp.where(causal, gates[c].a_t[h:h + 1, :] - g_rows, NEG_BIG))
            s = qk[c][h]() * w
            s_sum = jnp.sum(s, axis=1, keepdims=True)
            intra = _dot(s.astype(bf16), v_h[c][h])
            w_inter = gates[c].w_inter(h)
            num = w_inter * inter[c][h] + intra
            den = jnp.maximum(jnp.abs(w_inter * qn[c][h] + s_sum), gates[c].clamp(h))
            h_ref[c * rows:(c + 1) * rows, h * DV:(h + 1) * DV] = num * lax.rsqrt(
                jnp.mean(num * num, axis=1, keepdims=True) + EPS * (den * den))


def _sequence_chunks(q_ref, k_ref, v_ref, gate_ref, sel_ref, h_ref, c_state, n_state, m_in, n_chunks,
                     after_gates=lambda: None):
    rows = CHUNK_ROWS
    heads = range(N_HEADS)
    chunks = range(n_chunks)
    c_state, n_state = list(c_state), list(n_state)
    rsl = lambda c: slice(c * rows, (c + 1) * rows)
    q_h = [[q_ref[rsl(c), h * DK:(h + 1) * DK] for h in heads] for c in chunks]
    k_h = [[k_ref[rsl(c), h * DK:(h + 1) * DK] for h in heads] for c in chunks]
    v_h = [[v_ref[rsl(c), h * DV:(h + 1) * DV] for h in heads] for c in chunks]

    qk = [[functools.partial(_dot_nt, q_h[c][h], k_h[c][h].astype(bf16)) for h in heads] for c in chunks]

    gates = []
    for c in chunks:
        gates.append(_gate_algebra(gate_ref[rsl(c), :LANES], gate_ref[rsl(c), LANES:],
                                   m_in if c == 0 else gates[c - 1].m_new, 1, rows, sel_ref))
    after_gates()

    inter = [[None] * N_HEADS for _ in chunks]
    qn = [[None] * N_HEADS for _ in chunks]
    for c in chunks:
        ga = gates[c]
        k_t = k_ref[rsl(c), :].T
        for h in heads:
            rhs = jnp.concatenate([c_state[h].astype(bf16),
                                   jnp.broadcast_to(n_state[h], (DK, DV)).astype(bf16)], axis=1)
            inter2 = _dot(q_h[c][h], rhs)
            inter[c][h] = inter2[:, :DV]
            qn[c][h] = inter2[:, DV:]
            wk_t = k_t[h * DK:(h + 1) * DK, :] * ga.w_state_t[h:h + 1, :]
            dec = ga.decay[:, h:h + 1]
            c_state[h] = dec * c_state[h] + _dot(wk_t.astype(bf16), v_h[c][h])
            n_state[h] = dec * n_state[h] + jnp.sum(wk_t, axis=1, keepdims=True)

    _head_outputs(qk, gates, inter, qn, v_h, h_ref, 1, rows)
    return c_state, n_state, gates[-1].m_new


def _mlstm_sample_body(q_ref, k_ref, v_ref, gate_ref, sel_ref, c0_ref, n0_ref, m0_ref,
                       h_ref, cout_ref, nout_ref, mout_ref, *, nseq, cs):
    rows = nseq * cs
    log_cs = cs.bit_length() - 1
    log_dk = DK.bit_length() - 1
    heads = range(N_HEADS)
    q_h = [q_ref[:, h * DK:(h + 1) * DK] for h in heads]
    k_h = [k_ref[:, h * DK:(h + 1) * DK] for h in heads]
    v_h = [v_ref[:, h * DV:(h + 1) * DV] for h in heads]
    qk = [functools.partial(_dot_nt, q_h[h], k_h[h].astype(bf16)) for h in heads]
    m_in = jnp.broadcast_to(m0_ref[...][:, None, :], (nseq, cs, LANES)).reshape(rows, LANES)
    ga = _gate_algebra(gate_ref[:, :LANES], gate_ref[:, LANES:], m_in, nseq, cs, sel_ref)

    bd_q = ((lax.broadcasted_iota(jnp.int32, (rows, nseq * DK), 0) >> log_cs)
            == (lax.broadcasted_iota(jnp.int32, (rows, nseq * DK), 1) >> log_dk))
    bd_k = ((lax.broadcasted_iota(jnp.int32, (nseq * DK, rows), 0) >> log_dk)
            == (lax.broadcasted_iota(jnp.int32, (nseq * DK, rows), 1) >> log_cs))
    k_t = k_ref[...].T
    inter, qn = [], []
    for h in heads:
        c_prev = c0_ref[:, h].reshape(nseq * DK, DV)
        q32 = q_h[h].astype(f32)
        q_bd = jnp.where(bd_q, jnp.concatenate([q32] * nseq, axis=1), 0.0).astype(bf16)
        inter.append(_dot(q_bd, c_prev.astype(bf16)))
        n_prev = n0_ref[:, h * DK:(h + 1) * DK]
        n_rows = jnp.broadcast_to(n_prev[:, None, :], (nseq, cs, DK)).reshape(rows, DK)
        qn.append(jnp.sum(q32 * n_rows, axis=1, keepdims=True))

        wk_t = k_t[h * DK:(h + 1) * DK, :] * ga.w_state_t[h:h + 1, :]
        wk_bd = jnp.where(bd_k, jnp.broadcast_to(wk_t[None], (nseq, DK, rows)).reshape(nseq * DK, rows), 0.0)
        dc = _dot(wk_bd.astype(bf16), v_h[h])
        dec_col = ga.decay[:, h:h + 1]
        dec_rows = jnp.broadcast_to(dec_col.reshape(nseq, 1, cs, 1), (nseq, DK // cs, cs, 1)).reshape(nseq * DK, 1)
        cout_ref[:, h] = (dec_rows * c_prev + dc).reshape(nseq, DK, DV)
        dec_seq = jnp.max(dec_col.reshape(nseq, cs, 1), axis=1)
        nout_ref[:, h * DK:(h + 1) * DK] = dec_seq * n_prev + jnp.sum(
            (ga.w_state[:, h:h + 1] * k_h[h]).reshape(nseq, cs, DK), axis=1)
    mout_ref[...] = jnp.max(ga.m_new.reshape(nseq, cs, LANES), axis=1)
    _head_outputs([qk], [ga], [inter], [qn], [v_h], h_ref, nseq, cs)


def _inproj_values(xn, w_ref, wg_ref, bias_ref):
    q = _dot(xn, w_ref[:, 0:QK_DIM]).astype(bf16)
    k = _dot(xn, w_ref[:, QK_DIM:2 * QK_DIM]) * (DK ** -0.5)
    v = _dot(xn, w_ref[:, 2 * QK_DIM:2 * QK_DIM + V_DIM]).astype(bf16)
    o = _dot(xn, w_ref[:, 2 * QK_DIM + V_DIM:W_QKVO_COLS])
    gz = _dot(xn, wg_ref[...]) + bias_ref[...]
    return q, k, v, o, gz[:, :LANES], jax.nn.log_sigmoid(gz[:, LANES:])


def _l0_prompt_body(x_ref, g_ref, w_ref, wg_ref, bias_ref, sel_ref,
                    o_ref, h_ref, cout_ref, nout_ref, mout_ref,
                    qa, ka, va, ga, qb, kb, vb, gb, c_scr, n_scr, m_scr, *, nt, n_chunks):
    step = pl.program_id(0)
    heads = range(N_HEADS)

    @pl.when(step == 0)
    def _():
        for ref in (qb, kb, vb, gb, c_scr, n_scr, m_scr):
            ref[...] = jnp.zeros_like(ref)

    def half_step(wr, rd):
        def project():
            q_w, k_w, v_w, g_w = wr
            xn = _rms(x_ref[...], g_ref[...]).astype(bf16)
            q, k, v, o, logi, logf = _inproj_values(xn, w_ref, wg_ref, bias_ref)
            q_w[...] = q
            k_w[...] = k
            v_w[...] = v
            o_ref[...] = o
            g_w[:, :LANES] = logi
            g_w[:, LANES:] = logf

        first = lax.rem(step + nt - 1, nt) == 0
        c_state = [jnp.where(first, 0.0, c_scr[h]) for h in heads]
        n_state = [jnp.where(first, 0.0, n_scr[h]) for h in heads]
        m_in = jnp.where(first, 0.0, m_scr[...])
        c_state, n_state, m_new = _sequence_chunks(*rd, sel_ref, h_ref, c_state, n_state, m_in, n_chunks,
                                                   after_gates=project)
        for h in heads:
            c_scr[h] = c_state[h]
            n_scr[h] = n_state[h]
        m_scr[...] = m_new

    @pl.when(lax.rem(step, 2) == 0)
    def _():
        half_step((qa, ka, va, ga), (qb, kb, vb, gb))

    @pl.when(lax.rem(step, 2) == 1)
    def _():
        half_step((qb, kb, vb, gb), (qa, ka, va, ga))

    @pl.when(jnp.logical_and(step >= 1, lax.rem(step, nt) == 0))
    def _():
        cout_ref[0] = c_scr[...]
        lane = lax.broadcasted_iota(jnp.int32, (DK, LANES), 1)
        n_mat = jnp.zeros((DK, LANES), f32)
        for h in heads:
            n_mat = jnp.where(lane == h, jnp.broadcast_to(n_scr[h], (DK, LANES)), n_mat)
        nout_ref[0] = n_mat
        mout_ref[0] = m_scr[...]


def _bcast_selector():
    src = np.arange(LANES)
    dst = np.arange(N_BCAST * N_HEADS * DV) // DV
    hit = (((src[:, None] >> LOG_HEADS) // 3 == dst[None, :] // N_HEADS)
           & ((src[:, None] & (N_HEADS - 1)) == dst[None, :] % N_HEADS))
    return jnp.asarray(hit, dtype=bf16)


def _l0_prompt(x2d, g, w_in, w_gate, gate_bias, n_batch, seq_len):
    t = x2d.shape[0]
    sel = _bcast_selector()
    rows = PROMPT_CHUNKS * CHUNK_ROWS
    nt = seq_len // rows
    n_blocks = t // rows
    cur = lambda s: (jnp.minimum(s, n_blocks - 1), 0)
    prev = lambda s: (jnp.maximum(s - 1, 0), 0)
    seq = lambda s: jnp.maximum(s - 1, 0) // nt
    bufs = [pltpu.VMEM((rows, QK_DIM), bf16), pltpu.VMEM((rows, QK_DIM), f32),
            pltpu.VMEM((rows, V_DIM), bf16), pltpu.VMEM((rows, 2 * LANES), f32)]
    return pl.pallas_call(
        functools.partial(_l0_prompt_body, nt=nt, n_chunks=PROMPT_CHUNKS),
        grid=(n_blocks + 1,),
        in_specs=[pl.BlockSpec((rows, D_MODEL), cur), _const_spec((1, D_MODEL)), _const_spec(w_in.shape),
                  _const_spec(w_gate.shape), _const_spec((1, 2 * LANES)), _const_spec(sel.shape)],
        out_specs=[pl.BlockSpec((rows, V_DIM), cur), pl.BlockSpec((rows, V_DIM), prev),
                   pl.BlockSpec((1, N_HEADS, DK, DV), lambda s: (seq(s), 0, 0, 0)),
                   pl.BlockSpec((1, DK, LANES), lambda s: (seq(s), 0, 0)),
                   pl.BlockSpec((1, 1, LANES), lambda s: (seq(s), 0, 0))],
        out_shape=[jax.ShapeDtypeStruct((t, V_DIM), f32), jax.ShapeDtypeStruct((t, V_DIM), f32),
                   jax.ShapeDtypeStruct((n_batch, N_HEADS, DK, DV), f32),
                   jax.ShapeDtypeStruct((n_batch, DK, LANES), f32),
                   jax.ShapeDtypeStruct((n_batch, 1, LANES), f32)],
        scratch_shapes=bufs + bufs + [pltpu.VMEM((N_HEADS, DK, DV), f32), pltpu.VMEM((N_HEADS, DK, 1), f32),
                                      pltpu.VMEM((1, LANES), f32)],
        compiler_params=_params(1),
        name="l0_inproj_mlstm",
    )(x2d, g, w_in, w_gate, gate_bias, sel)


def _mlstm_core_sample(q, k, v, gates, c0, n0, m0, n_batch, seq_len):
    t = q.shape[0]
    sel = _bcast_selector()
    nseq = CHUNK_ROWS // seq_len
    row = lambda n: pl.BlockSpec((CHUNK_ROWS, n), lambda i: (i, 0))
    cspec = pl.BlockSpec((nseq, N_HEADS, DK, DV), lambda i: (i, 0, 0, 0))
    nspec = pl.BlockSpec((nseq, QK_DIM), lambda i: (i, 0))
    mspec = pl.BlockSpec((nseq, LANES), lambda i: (i, 0))
    return pl.pallas_call(
        functools.partial(_mlstm_sample_body, nseq=nseq, cs=seq_len),
        grid=(t // CHUNK_ROWS,),
        in_specs=[row(QK_DIM), row(QK_DIM), row(V_DIM), row(2 * LANES), _const_spec(sel.shape), cspec, nspec, mspec],
        out_specs=[row(V_DIM), cspec, nspec, mspec],
        out_shape=[jax.ShapeDtypeStruct((t, V_DIM), f32),
                   jax.ShapeDtypeStruct((n_batch, N_HEADS, DK, DV), f32),
                   jax.ShapeDtypeStruct((n_batch, QK_DIM), f32),
                   jax.ShapeDtypeStruct((n_batch, LANES), f32)],
        compiler_params=_params(1),
        name="l0_mlstm_core",
    )(q, k, v, gates, sel, c0, n0, m0)


def _outproj_ffn_body(*refs, starts):
    n = len(starts) - 1
    hg_ref, wout_ref, gffn_ref, wup_ref, wdn_ref = refs[3 * n:3 * n + 5]
    step = pl.program_id(0)
    for i in range(n):
        h_ref, o_ref, x_ref = refs[3 * i:3 * i + 3]
        y_ref = refs[3 * n + 5 + i]

        @pl.when(jnp.logical_and(step >= starts[i], step < starts[i + 1]))
        def _():
            x1s = []
            for r in _row_splits(ROW_TILE):
                hn = h_ref[r, :] * hg_ref[...] * jax.nn.sigmoid(o_ref[r, :])
                x1s.append(x_ref[r, :] + _dot(hn.astype(bf16), wout_ref[...]))
            for r, y in zip(_row_splits(ROW_TILE), _ffn(x1s, gffn_ref, wup_ref, wdn_ref)):
                y_ref[r, :] = y


def _outproj_ffn(groups, head_g, w_out, g_ffn, w_up, w_down):
    starts = [0]
    for _, _, x2d in groups:
        starts.append(starts[-1] + x2d.shape[0] // ROW_TILE)

    def row(i, **kw):
        lo, n = starts[i], starts[i + 1] - starts[i]
        return pl.BlockSpec((ROW_TILE, D_MODEL), lambda s: (jnp.clip(s - lo, 0, n - 1), 0), **kw)

    in_specs = [row(i, **({} if i == 0 else dict(pipeline_mode=pl.Buffered(1))))
                for i in range(len(groups)) for _ in range(3)]
    in_specs += [_const_spec((1, V_DIM)), _const_spec(w_out.shape), _const_spec((1, D_MODEL)),
                 _layer_spec(w_up.shape, 0), _layer_spec(w_down.shape, 0)]
    return pl.pallas_call(
        functools.partial(_outproj_ffn_body, starts=tuple(starts)),
        grid=(starts[-1],),
        in_specs=in_specs,
        out_specs=[row(i) for i in range(len(groups))],
        out_shape=[jax.ShapeDtypeStruct(x2d.shape, f32) for _, _, x2d in groups],
        compiler_params=_params(1),
        name="l0_outproj_ffn",
    )(*[a for g in groups for a in g], head_g, w_out, g_ffn, w_up, w_down)


def _s5_prep_body(lre_ref, lim_ref, ldt_ref, bre_ref, bim_ref, cre_ref, cim_ref,
                  are_ref, aim_ref, wbr_ref, wbi_ref, vre_ref, vim_ref):
    lr = lre_ref[...]
    li = lim_ref[...]
    dt = jnp.exp(ldt_ref[...])
    mag = jnp.exp(lr * dt)
    a_re = mag * jnp.cos(li * dt)
    a_im = mag * jnp.sin(li * dt)
    den = lr * lr + li * li
    z_re = a_re - 1.0
    coef_re = ((z_re * lr + a_im * li) / den)[:, None, :]
    coef_im = ((a_im * lr - z_re * li) / den)[:, None, :]
    br = bre_ref[...]
    bi = bim_ref[...]
    are_ref[...] = a_re
    aim_ref[...] = a_im
    bb = (coef_re * br - coef_im * bi, coef_re * bi + coef_im * br)
    c_t = (cre_ref[...].reshape(S5_GROUPS * S5_GROUP, S5_STATE).T,
           cim_ref[...].reshape(S5_GROUPS * S5_GROUP, S5_STATE).T)

    groups = S5_BLOCK_CH // S5_GROUP
    log_ch, log_st = S5_GROUP.bit_length() - 1, S5_STATE.bit_length() - 1
    shape_b = (S5_BLOCK_CH, S5_BLOCK_ST)
    diag_b = ((lax.broadcasted_iota(jnp.int32, shape_b, 0) >> log_ch)
              == (lax.broadcasted_iota(jnp.int32, shape_b, 1) >> log_st))
    shape_c = (S5_BLOCK_ST, S5_BLOCK_CH)
    diag_c = ((lax.broadcasted_iota(jnp.int32, shape_c, 0) >> log_st)
              == (lax.broadcasted_iota(jnp.int32, shape_c, 1) >> log_ch))
    for j in range(N_S5_BLOCKS):
        for src, dst in zip(bb, (wbr_ref, wbi_ref)):
            blk = src[j * groups:(j + 1) * groups].reshape(S5_BLOCK_CH, S5_STATE)
            dst[j] = jnp.where(diag_b, jnp.concatenate([blk] * groups, axis=1), 0.0).astype(bf16)
        for src, dst in zip(c_t, (vre_ref, vim_ref)):
            blk = src[:, j * S5_BLOCK_CH:(j + 1) * S5_BLOCK_CH]
            rep = jnp.broadcast_to(blk[None], (groups, S5_STATE, S5_BLOCK_CH)).reshape(S5_BLOCK_ST, S5_BLOCK_CH)
            dst[j] = jnp.where(diag_c, rep, 0.0).astype(bf16)


def _s5_prep(lam_re, lam_im, log_dt, b_re_t, b_im_t, c_re, c_im):
    gp = jax.ShapeDtypeStruct((S5_GROUPS, S5_STATE), f32)
    wb = jax.ShapeDtypeStruct((N_S5_BLOCKS, S5_BLOCK_CH, S5_BLOCK_ST), bf16)
    vc = jax.ShapeDtypeStruct((N_S5_BLOCKS, S5_BLOCK_ST, S5_BLOCK_CH), bf16)
    return pl.pallas_call(_s5_prep_body, out_shape=[gp, gp, wb, wb, vc, vc], name="l1_s5_prep")(
        lam_re, lam_im, log_dt, b_re_t, b_im_t, c_re, c_im)


def _s5_body(x_ref, h0r_ref, h0i_ref, g_ref, are_ref, aim_ref, wbr_ref, wbi_ref, vre_ref, vim_ref,
             d_ref, wglu_ref, gffn_ref, wup_ref, wdn_ref, gfin_ref,
             y_ref, sre_ref, sim_ref, sr_scr, si_scr, hr_scr, hi_scr, act_scr, *io_scr, bt, lt, dma_io):
    rows = bt * lt
    step = pl.program_id(1)
    n_steps = pl.num_programs(1)

    @pl.when(step == 0)
    def _():
        sr_scr[...] = h0r_ref[...]
        si_scr[...] = h0i_ref[...]

    if dma_io:
        xbuf, ybuf, in_sem, out_sem = io_scr
        slot = lax.rem(step, 2)

        def in_copy(s, sl, b):
            return pltpu.make_async_copy(x_ref.at[b, pl.ds(s * lt, lt), :], xbuf.at[sl, :, b, :], in_sem.at[sl, b])

        def out_copy(s, sl, b):
            return pltpu.make_async_copy(ybuf.at[sl, :, b, :], y_ref.at[b, pl.ds(s * lt, lt), :], out_sem.at[sl, b])

        @pl.when(step == 0)
        def _():
            for b in range(bt):
                in_copy(0, 0, b).start()

        @pl.when(step + 1 < n_steps)
        def _():
            for b in range(bt):
                in_copy(step + 1, 1 - slot, b).start()

        for b in range(bt):
            in_copy(step, slot, b).wait()
        xt = xbuf[slot].reshape(rows, D_MODEL)
    else:
        xt = jnp.concatenate([x_ref[:, t, :] for t in range(lt)], axis=0)
    u = _rms(xt, g_ref[...])
    ub = u.astype(bf16)

    def project_in(j):
        ch = slice(j * S5_BLOCK_CH, (j + 1) * S5_BLOCK_CH)
        hr_scr[j % 2] = _dot(ub[:, ch], wbr_ref[j])
        hi_scr[j % 2] = _dot(ub[:, ch], wbi_ref[j])

    project_in(0)
    for j in range(N_S5_BLOCKS):
        ch = slice(j * S5_BLOCK_CH, (j + 1) * S5_BLOCK_CH)
        st = slice(j * S5_BLOCK_ST, (j + 1) * S5_BLOCK_ST)
        if j + 1 < N_S5_BLOCKS:
            project_in(j + 1)
        a_re = jnp.broadcast_to(are_ref[:, st], (bt, S5_BLOCK_ST))
        a_im = jnp.broadcast_to(aim_ref[:, st], (bt, S5_BLOCK_ST))
        s_re = sr_scr[:, st]
        s_im = si_scr[:, st]
        for t in range(lt):
            r = slice(t * bt, (t + 1) * bt)
            n_re = a_re * s_re - a_im * s_im + hr_scr[j % 2, r, :]
            n_im = a_re * s_im + a_im * s_re + hi_scr[j % 2, r, :]
            hr_scr[j % 2, r, :] = n_re
            hi_scr[j % 2, r, :] = n_im
            s_re, s_im = n_re, n_im
        sr_scr[:, st] = s_re
        si_scr[:, st] = s_im
        yj = (_dot(hr_scr[j % 2].astype(bf16), vre_ref[j]) - _dot(hi_scr[j % 2].astype(bf16), vim_ref[j])
              + d_ref[:, ch] * u[:, ch])
        act_scr[:, ch] = jax.nn.gelu(yj).astype(bf16)

    sre_ref[...] = sr_scr[...]
    sim_ref[...] = si_scr[...]
    x3s = []
    for r in _row_splits(rows):
        ag = _dot(act_scr[r, :], wglu_ref[...])
        x3s.append(xt[r, :] + ag[:, :D_MODEL] * jax.nn.sigmoid(ag[:, D_MODEL:]))
    y = jnp.concatenate([_rms(x4, gfin_ref[...]) for x4 in _ffn(x3s, gffn_ref, wup_ref, wdn_ref)], axis=0)
    if dma_io:
        @pl.when(step >= 2)
        def _():
            for b in range(bt):
                out_copy(step - 2, slot, b).wait()

        ybuf[slot] = y.reshape(lt, bt, D_MODEL)
        for b in range(bt):
            out_copy(step, slot, b).start()

        @pl.when(step == n_steps - 1)
        def _():
            @pl.when(step >= 1)
            def _():
                for b in range(bt):
                    out_copy(step - 1, 1 - slot, b).wait()

            for b in range(bt):
                out_copy(step, slot, b).wait()
    else:
        for t in range(lt):
            y_ref[:, t, :] = y[t * bt:(t + 1) * bt, :]


def _s5_layer(x3d, h0_re, h0_im, bt, lt, g_mix, a_re, a_im, wb_re, wb_im, v_re, v_im, d_skip,
              w_glu, g_ffn, w_up, w_down, g_final):
    n_batch, seq_len, _ = x3d.shape
    grid = (n_batch // bt, seq_len // lt)
    sspec = pl.BlockSpec((bt, S5_N), lambda i, t: (i, 0))
    dma_io = n_batch == bt
    if dma_io:
        xspec = pl.BlockSpec(memory_space=pl.ANY)
        io_scratch = [pltpu.VMEM((2, lt, bt, D_MODEL), f32), pltpu.VMEM((2, lt, bt, D_MODEL), f32),
                      pltpu.SemaphoreType.DMA((2, bt)), pltpu.SemaphoreType.DMA((2, bt))]
    else:
        xspec = pl.BlockSpec((bt, lt, D_MODEL), lambda i, t: (i, t, 0))
        io_scratch = []
    consts = (g_mix, a_re, a_im, wb_re, wb_im, v_re, v_im, d_skip, w_glu, g_ffn, w_up, w_down, g_final)
    const_specs = [_layer_spec(c.shape, 1) if c is w_up or c is w_down else _const_spec(c.shape) for c in consts]
    rows = bt * lt
    return pl.pallas_call(
        functools.partial(_s5_body, bt=bt, lt=lt, dma_io=dma_io),
        grid=grid,
        in_specs=[xspec, sspec, sspec] + const_specs,
        out_specs=[xspec, sspec, sspec],
        out_shape=[jax.ShapeDtypeStruct(x3d.shape, f32), jax.ShapeDtypeStruct((n_batch, S5_N), f32),
                   jax.ShapeDtypeStruct((n_batch, S5_N), f32)],
        scratch_shapes=[pltpu.VMEM((bt, S5_N), f32), pltpu.VMEM((bt, S5_N), f32),
                        pltpu.VMEM((2, rows, S5_BLOCK_ST), f32), pltpu.VMEM((2, rows, S5_BLOCK_ST), f32),
                        pltpu.VMEM((rows, D_MODEL), bf16)] + io_scratch,
        compiler_params=_params(2),
        name="l1_s5_ffn",
    )(x3d, h0_re, h0_im, *consts)


def _rep_lanes(x):
    return jnp.tile(x, (1,) * (x.ndim - 1) + (LANES // x.shape[-1],))


def _mlstm_mixer(x, state, p):
    n_batch, seq_len, _ = x.shape
    x2d = x.reshape(n_batch * seq_len, D_MODEL)
    if state is None:
        o, hmix, c_new, n_new, m_new = _l0_prompt(x2d, p["g_mix0"], p["w_in"], p["w_gate"], p["gate_bias"],
                                                  n_batch, seq_len)
        n_new = jnp.swapaxes(n_new[:, :, :N_HEADS], 1, 2)
    else:
        c0, n0, m0 = state
        q, k, v, o, gates = _inproj(x2d, p["g_mix0"], p["w_in"], p["w_gate"], p["gate_bias"])
        hmix, c_new, n_new, m_new = _mlstm_core_sample(
            q, k, v, gates, c0, n0.reshape(n_batch, QK_DIM), _rep_lanes(m0), n_batch, seq_len)
    states = (c_new.reshape(1, n_batch, N_HEADS, DK, DV),
              n_new.reshape(1, n_batch, N_HEADS, DK),
              m_new.reshape(n_batch, LANES)[:, :N_HEADS].reshape(1, n_batch, N_HEADS))
    return (hmix, o, x2d), states


def _s5_stage(x2, shape, state, p):
    n_batch, seq_len, _ = shape
    if state is None:
        h0_re = jnp.zeros((n_batch, S5_N), f32)
        h0_im = h0_re
        bt, lt = n_batch, ROW_TILE // n_batch
    else:
        h0_re, h0_im = (s.reshape(n_batch, S5_N) for s in state)
        bt, lt = ROW_TILE // seq_len, seq_len
    y, s_re, s_im = _s5_layer(x2.reshape(shape), h0_re, h0_im, bt, lt,
                              p["g_mix1"], p["a_re"], p["a_im"], p["wb_re"], p["wb_im"], p["v_re"], p["v_im"],
                              p["d_skip"], p["w_glu"], p["g_ffn1"], p["w_up"], p["w_down"], p["g_final"])
    return y, (s_re.reshape(1, n_batch, S5_GROUPS, S5_STATE), s_im.reshape(1, n_batch, S5_GROUPS, S5_STATE))


def kernel(x_prompt, x_sample, state_mlstm_C, state_mlstm_n, state_mlstm_m, state_s5_re, state_s5_im,
           norm_mix_g, norm_ffn_g, norm_final_g, mlstm_w_in, mlstm_b_i, mlstm_b_f, mlstm_head_norm_g,
           mlstm_w_out, s5_lambda_re, s5_lambda_im, s5_log_dt, s5_b_re, s5_b_im, s5_c_re, s5_c_im,
           s5_d, s5_w_glu, ffn_w_up, ffn_w_down):
    w_in = mlstm_w_in[0]
    n_qkvo = W_QKVO_COLS
    a_re, a_im, wb_re, wb_im, v_re, v_im = _s5_prep(
        s5_lambda_re[0], s5_lambda_im[0], s5_log_dt[0].reshape(S5_GROUPS, 1),
        jnp.swapaxes(s5_b_re[0], 1, 2), jnp.swapaxes(s5_b_im[0], 1, 2), s5_c_re[0], s5_c_im[0])
    p = dict(
        g_mix0=norm_mix_g[0].reshape(1, D_MODEL), g_mix1=norm_mix_g[1].reshape(1, D_MODEL),
        g_ffn0=norm_ffn_g[0].reshape(1, D_MODEL), g_ffn1=norm_ffn_g[1].reshape(1, D_MODEL),
        g_final=norm_final_g.reshape(1, D_MODEL),
        w_in=w_in.astype(bf16),
        w_gate=jnp.concatenate([_rep_lanes(w_in[:, n_qkvo:n_qkvo + N_HEADS]),
                                _rep_lanes(w_in[:, n_qkvo + N_HEADS:])], axis=1).astype(bf16),
        gate_bias=jnp.concatenate([_rep_lanes(mlstm_b_i[0][None]), _rep_lanes(mlstm_b_f[0][None])], axis=1),
        head_g=mlstm_head_norm_g[0].reshape(1, V_DIM),
        w_out=mlstm_w_out[0].astype(bf16),
        w_up=ffn_w_up.astype(bf16), w_down=ffn_w_down.astype(bf16),
        a_re=a_re.reshape(1, S5_N), a_im=a_im.reshape(1, S5_N),
        wb_re=wb_re, wb_im=wb_im, v_re=v_re, v_im=v_im,
        d_skip=s5_d[0].reshape(1, D_MODEL),
        w_glu=s5_w_glu[0].astype(bf16),
    )
    mix_p, mlstm_p = _mlstm_mixer(x_prompt, None, p)
    mix_s, mlstm_s = _mlstm_mixer(x_sample, (state_mlstm_C[0], state_mlstm_n[0], state_mlstm_m[0]), p)
    x2_p, x2_s = _outproj_ffn([mix_p, mix_s], p["head_g"], p["w_out"], p["g_ffn0"], p["w_up"], p["w_down"])
    y_p, s5_p = _s5_stage(x2_p, x_prompt.shape, None, p)
    y_s, s5_s = _s5_stage(x2_s, x_sample.shape, (state_s5_re[0], state_s5_im[0]), p)
    return (y_p, y_s) + mlstm_p + s5_p + mlstm_s + s5_s
```

```python
import functools
import types

import jax
import jax.numpy as jnp
import numpy as np
from jax import lax
from jax.experimental import pallas as pl
from jax.experimental.pallas import tpu as pltpu

f32 = jnp.float32
bf16 = jnp.bfloat16

D_MODEL = 1024
N_HEADS = 8
DK = 64
DV = 128
QK_DIM = N_HEADS * DK
V_DIM = N_HEADS * DV
W_QKVO_COLS = 2 * QK_DIM + 2 * V_DIM
D_FF = 4 * D_MODEL
S5_GROUPS = 64
S5_GROUP = 16
S5_STATE = 64
S5_N = S5_GROUPS * S5_STATE
EPS = 1e-6

LANES = 128
SUBLANES = 8
ROW_TILE = 512
CHUNK_ROWS = 256
PROMPT_CHUNKS = 2
FF_CHUNK = 1024
ROW_SPLITS = 2
S5_BLOCK_CH = LANES
S5_BLOCK_ST = S5_BLOCK_CH // S5_GROUP * S5_STATE
N_S5_BLOCKS = D_MODEL // S5_BLOCK_CH
VMEM_LIMIT_BYTES = 56 * 1024 * 1024
NEG_BIG = -1e30
LOG_HEADS = N_HEADS.bit_length() - 1
N_BCAST = 3


def _params(n_axes):
    return pltpu.CompilerParams(dimension_semantics=("arbitrary",) * n_axes,
                                vmem_limit_bytes=VMEM_LIMIT_BYTES)


def _const_spec(shape):
    nd = len(shape)
    return pl.BlockSpec(shape, lambda *_: (0,) * nd, pipeline_mode=pl.Buffered(1))


def _layer_spec(shape, layer):
    nd = len(shape)
    return pl.BlockSpec((None,) + tuple(shape[1:]), lambda *_: (layer,) + (0,) * (nd - 1),
                        pipeline_mode=pl.Buffered(1))


def _dot(a, b):
    return jnp.dot(a, b, preferred_element_type=f32)


def _dot_nt(a, b):
    return lax.dot_general(a, b, (((1,), (1,)), ((), ())), preferred_element_type=f32)


def _rms(x, g):
    return x * lax.rsqrt(jnp.mean(x * x, axis=-1, keepdims=True) + EPS) * g


def _row_splits(rows):
    size = rows // ROW_SPLITS
    return [slice(i * size, (i + 1) * size) for i in range(ROW_SPLITS)]


def _ffn(x1s, g_ref, wup_ref, wdn_ref):
    xn = [_rms(x1, g_ref[...]).astype(bf16) for x1 in x1s]
    acc = [None] * len(x1s)
    for c in range(D_FF // FF_CHUNK):
        cols = slice(c * FF_CHUNK, (c + 1) * FF_CHUNK)
        for i in range(len(x1s)):
            hid = _dot(xn[i], wup_ref[:, cols])
            hid = jnp.square(jnp.maximum(hid, 0.0)).astype(bf16)
            part = _dot(hid, wdn_ref[cols, :])
            acc[i] = part if acc[i] is None else acc[i] + part
    return [x1 + a for x1, a in zip(x1s, acc)]


def _inproj_body(x_ref, g_ref, w_ref, wg_ref, bias_ref, q_ref, k_ref, v_ref, o_ref, gate_ref):
    xn = _rms(x_ref[...], g_ref[...]).astype(bf16)
    q, k, v, o, logi, logf = _inproj_values(xn, w_ref, wg_ref, bias_ref)
    q_ref[...] = q
    k_ref[...] = k
    v_ref[...] = v
    o_ref[...] = o
    gate_ref[:, :LANES] = logi
    gate_ref[:, LANES:] = logf


def _inproj(x2d, g, w_in, w_gate, gate_bias):
    t = x2d.shape[0]
    row = lambda n: pl.BlockSpec((ROW_TILE, n), lambda i: (i, 0))
    return pl.pallas_call(
        _inproj_body,
        grid=(t // ROW_TILE,),
        in_specs=[row(D_MODEL), _const_spec((1, D_MODEL)), _const_spec(w_in.shape),
                  _const_spec(w_gate.shape), _const_spec((1, 2 * LANES))],
        out_specs=[row(QK_DIM), row(QK_DIM), row(V_DIM), row(V_DIM), row(2 * LANES)],
        out_shape=[jax.ShapeDtypeStruct((t, QK_DIM), bf16), jax.ShapeDtypeStruct((t, QK_DIM), f32),
                   jax.ShapeDtypeStruct((t, V_DIM), bf16), jax.ShapeDtypeStruct((t, V_DIM), f32),
                   jax.ShapeDtypeStruct((t, 2 * LANES), f32)],
        compiler_params=_params(1),
        name="l0_inproj",
    )(x2d, g, w_in, w_gate, gate_bias)


def _row_prefix(x, cs, rpos, is_max):
    sh = 1
    while sh < cs:
        prev = pltpu.roll(x, sh, 0)
        if is_max:
            x = jnp.maximum(x, jnp.where(rpos >= sh, prev, NEG_BIG))
        else:
            x = x + jnp.where(rpos >= sh, prev, 0.0)
        sh *= 2
    return x


def _seg_last(x, nseq, cs):
    if nseq == 1:
        return x[cs - 1:cs, :]
    last = x.reshape(nseq, cs, x.shape[1])[:, cs - 1:cs, :]
    return jnp.broadcast_to(last, (nseq, cs, x.shape[1])).reshape(nseq * cs, x.shape[1])


def _gate_algebra(logi, logf, m_in, nseq, cs, sel_ref):
    rows = nseq * cs
    rpos = lax.broadcasted_iota(jnp.int32, (rows, LANES), 0) & (cs - 1)
    b = _row_prefix(logf, cs, rpos, False)
    a = logi - b
    g = jnp.maximum(m_in, _row_prefix(a, cs, rpos, True))
    g_last = _seg_last(g, nseq, cs)
    w_state = jnp.exp(a - g_last)

    group = lax.broadcasted_iota(jnp.int32, (rows, LANES), 1) >> LOG_HEADS
    pieces = jnp.zeros((rows, LANES), f32)
    for i, val in enumerate((g, jnp.exp(m_in - g), jnp.exp(-(b + g)))):
        hi = val.astype(bf16).astype(f32)
        mid = (val - hi).astype(bf16).astype(f32)
        lo = (val - hi - mid).astype(bf16).astype(f32)
        for j, piece in enumerate((hi, mid, lo)):
            pieces = jnp.where(group == 3 * i + j, piece, pieces)
    bcast = _dot(pieces.astype(bf16), sel_ref[...])
    tile = lambda i, h: bcast[:, (i * N_HEADS + h) * DV:(i * N_HEADS + h + 1) * DV]
    return types.SimpleNamespace(
        a_t=a.T,
        w_state=w_state, w_state_t=w_state.T,
        decay=jnp.exp(m_in - g_last),
        m_new=_seg_last(b, nseq, cs) + g_last,
        g=lambda h: tile(0, h), w_inter=lambda h: tile(1, h), clamp=lambda h: tile(2, h))


def _head_outputs(qk, gates, inter, qn, v_h, h_ref, nseq, cs):
    rows = nseq * cs
    log_cs = cs.bit_length() - 1
    heads = range(N_HEADS)
    chunks = range(len(qk))
    ri = lax.broadcasted_iota(jnp.int32, (rows, rows), 0)
    ci = lax.broadcasted_iota(jnp.int32, (rows, rows), 1)
    causal = ci <= ri
    if nseq > 1:
        causal = jnp.logical_and(causal, (ri >> log_cs) == (ci >> log_cs))
    for c in chunks:
        for h in heads:
            g_rows = jnp.concatenate([gates[c].g(h)] * (rows // DV), axis=1)
            w = jnp.exp(jnp.where(causal, gates[c].a_t[h:h + 1, :] - g_rows, NEG_BIG))
            s = qk[c][h]() * w
            s_sum = jnp.sum(s, axis=1, keepdims=True)
            intra = _dot(s.astype(bf16), v_h[c][h])
            w_inter = gates[c].w_inter(h)
            num = w_inter * inter[c][h] + intra
            den = jnp.maximum(jnp.abs(w_inter * qn[c][h] + s_sum), gates[c].clamp(h))
            h_ref[c * rows:(c + 1) * rows, h * DV:(h + 1) * DV] = num * lax.rsqrt(
                jnp.mean(num * num, axis=1, keepdims=True) + EPS * (den * den))


def _sequence_chunks(q_ref, k_ref, v_ref, gate_ref, sel_ref, h_ref, c_state, n_state, m_in, n_chunks,
                     after_gates=lambda: None):
    rows = CHUNK_ROWS
    heads = range(N_HEADS)
    chunks = range(n_chunks)
    c_state, n_state = list(c_state), list(n_state)
    rsl = lambda c: slice(c * rows, (c + 1) * rows)
    q_h = [[q_ref[rsl(c), h * DK:(h + 1) * DK] for h in heads] for c in chunks]
    k_h = [[k_ref[rsl(c), h * DK:(h + 1) * DK] for h in heads] for c in chunks]
    v_h = [[v_ref[rsl(c), h * DV:(h + 1) * DV] for h in heads] for c in chunks]

    qk = [[functools.partial(_dot_nt, q_h[c][h], k_h[c][h].astype(bf16)) for h in heads] for c in chunks]

    gates = []
    for c in chunks:
        gates.append(_gate_algebra(gate_ref[rsl(c), :LANES], gate_ref[rsl(c), LANES:],
                                   m_in if c == 0 else gates[c - 1].m_new, 1, rows, sel_ref))
    after_gates()

    inter = [[None] * N_HEADS for _ in chunks]
    qn = [[None] * N_HEADS for _ in chunks]
    for c in chunks:
        ga = gates[c]
        k_t = k_ref[rsl(c), :].T
        for h in heads:
            rhs = jnp.concatenate([c_state[h].astype(bf16),
                                   jnp.broadcast_to(n_state[h], (DK, DV)).astype(bf16)], axis=1)
            inter2 = _dot(q_h[c][h], rhs)
            inter[c][h] = inter2[:, :DV]
            qn[c][h] = inter2[:, DV:]
            wk_t = k_t[h * DK:(h + 1) * DK, :] * ga.w_state_t[h:h + 1, :]
            dec = ga.decay[:, h:h + 1]
            c_state[h] = dec * c_state[h] + _dot(wk_t.astype(bf16), v_h[c][h])
            n_state[h] = dec * n_state[h] + jnp.sum(wk_t, axis=1, keepdims=True)

    _head_outputs(qk, gates, inter, qn, v_h, h_ref, 1, rows)
    return c_state, n_state, gates[-1].m_new


def _mlstm_sample_body(q_ref, k_ref, v_ref, gate_ref, sel_ref, c0_ref, n0_ref, m0_ref,
                       h_ref, cout_ref, nout_ref, mout_ref, *, nseq, cs):
    rows = nseq * cs
    log_cs = cs.bit_length() - 1
    log_dk = DK.bit_length() - 1
    heads = range(N_HEADS)
    q_h = [q_ref[:, h * DK:(h + 1) * DK] for h in heads]
    k_h = [k_ref[:, h * DK:(h + 1) * DK] for h in heads]
    v_h = [v_ref[:, h * DV:(h + 1) * DV] for h in heads]
    qk = [functools.partial(_dot_nt, q_h[h], k_h[h].astype(bf16)) for h in heads]
    m_in = jnp.broadcast_to(m0_ref[...][:, None, :], (nseq, cs, LANES)).reshape(rows, LANES)
    ga = _gate_algebra(gate_ref[:, :LANES], gate_ref[:, LANES:], m_in, nseq, cs, sel_ref)

    bd_q = ((lax.broadcasted_iota(jnp.int32, (rows, nseq * DK), 0) >> log_cs)
            == (lax.broadcasted_iota(jnp.int32, (rows, nseq * DK), 1) >> log_dk))
    bd_k = ((lax.broadcasted_iota(jnp.int32, (nseq * DK, rows), 0) >> log_dk)
            == (lax.broadcasted_iota(jnp.int32, (nseq * DK, rows), 1) >> log_cs))
    k_t = k_ref[...].T
    inter, qn = [], []
    for h in heads:
        c_prev = c0_ref[:, h].reshape(nseq * DK, DV)
        q32 = q_h[h].astype(f32)
        q_bd = jnp.where(bd_q, jnp.concatenate([q32] * nseq, axis=1), 0.0).astype(bf16)
        inter.append(_dot(q_bd, c_prev.astype(bf16)))
        n_prev = n0_ref[:, h * DK:(h + 1) * DK]
        n_rows = jnp.broadcast_to(n_prev[:, None, :], (nseq, cs, DK)).reshape(rows, DK)
        qn.append(jnp.sum(q32 * n_rows, axis=1, keepdims=True))

        wk_t = k_t[h * DK:(h + 1) * DK, :] * ga.w_state_t[h:h + 1, :]
        wk_bd = jnp.where(bd_k, jnp.broadcast_to(wk_t[None], (nseq, DK, rows)).reshape(nseq * DK, rows), 0.0)
        dc = _dot(wk_bd.astype(bf16), v_h[h])
        dec_col = ga.decay[:, h:h + 1]
        dec_rows = jnp.broadcast_to(dec_col.reshape(nseq, 1, cs, 1), (nseq, DK // cs, cs, 1)).reshape(nseq * DK, 1)
        cout_ref[:, h] = (dec_rows * c_prev + dc).reshape(nseq, DK, DV)
        dec_seq = jnp.max(dec_col.reshape(nseq, cs, 1), axis=1)
        nout_ref[:, h * DK:(h + 1) * DK] = dec_seq * n_prev + jnp.sum(
            (ga.w_state[:, h:h + 1] * k_h[h]).reshape(nseq, cs, DK), axis=1)
    mout_ref[...] = jnp.max(ga.m_new.reshape(nseq, cs, LANES), axis=1)
    _head_outputs([qk], [ga], [inter], [qn], [v_h], h_ref, nseq, cs)


def _inproj_values(xn, w_ref, wg_ref, bias_ref):
    q = _dot_nt(xn, w_ref[0:QK_DIM, :]).astype(bf16)
    k = _dot_nt(xn, w_ref[QK_DIM:2 * QK_DIM, :]) * (DK ** -0.5)
    v = _dot_nt(xn, w_ref[2 * QK_DIM:2 * QK_DIM + V_DIM, :]).astype(bf16)
    o = _dot_nt(xn, w_ref[2 * QK_DIM + V_DIM:W_QKVO_COLS, :])
    gz = _dot(xn, wg_ref[...]) + bias_ref[...]
    return q, k, v, o, gz[:, :LANES], jax.nn.log_sigmoid(gz[:, LANES:])


def _l0_prompt_body(x_ref, g_ref, w_ref, wg_ref, bias_ref, sel_ref,
                    o_ref, h_ref, cout_ref, nout_ref, mout_ref,
                    qa, ka, va, ga, qb, kb, vb, gb, c_scr, n_scr, m_scr, *, nt, n_chunks):
    step = pl.program_id(0)
    heads = range(N_HEADS)

    @pl.when(step == 0)
    def _():
        for ref in (qb, kb, vb, gb, c_scr, n_scr, m_scr):
            ref[...] = jnp.zeros_like(ref)

    def half_step(wr, rd):
        def project():
            q_w, k_w, v_w, g_w = wr
            xn = _rms(x_ref[...], g_ref[...]).astype(bf16)
            q, k, v, o, logi, logf = _inproj_values(xn, w_ref, wg_ref, bias_ref)
            q_w[...] = q
            k_w[...] = k
            v_w[...] = v
            o_ref[...] = o
            g_w[:, :LANES] = logi
            g_w[:, LANES:] = logf

        first = lax.rem(step + nt - 1, nt) == 0
        c_state = [jnp.where(first, 0.0, c_scr[h]) for h in heads]
        n_state = [jnp.where(first, 0.0, n_scr[h]) for h in heads]
        m_in = jnp.where(first, 0.0, m_scr[...])
        c_state, n_state, m_new = _sequence_chunks(*rd, sel_ref, h_ref, c_state, n_state, m_in, n_chunks,
                                                   after_gates=project)
        for h in heads:
            c_scr[h] = c_state[h]
            n_scr[h] = n_state[h]
        m_scr[...] = m_new

    @pl.when(lax.rem(step, 2) == 0)
    def _():
        half_step((qa, ka, va, ga), (qb, kb, vb, gb))

    @pl.when(lax.rem(step, 2) == 1)
    def _():
        half_step((qb, kb, vb, gb), (qa, ka, va, ga))

    @pl.when(jnp.logical_and(step >= 1, lax.rem(step, nt) == 0))
    def _():
        cout_ref[0] = c_scr[...]
        lane = lax.broadcasted_iota(jnp.int32, (DK, LANES), 1)
        n_mat = jnp.zeros((DK, LANES), f32)
        for h in heads:
            n_mat = jnp.where(lane == h, jnp.broadcast_to(n_scr[h], (DK, LANES)), n_mat)
        nout_ref[0] = n_mat
        mout_ref[0] = m_scr[...]


def _bcast_selector():
    src = np.arange(LANES)
    dst = np.arange(N_BCAST * N_HEADS * DV) // DV
    hit = (((src[:, None] >> LOG_HEADS) // 3 == dst[None, :] // N_HEADS)
           & ((src[:, None] & (N_HEADS - 1)) == dst[None, :] % N_HEADS))
    return jnp.asarray(hit, dtype=bf16)


def _l0_prompt(x2d, g, w_in, w_gate, gate_bias, n_batch, seq_len):
    t = x2d.shape[0]
    sel = _bcast_selector()
    rows = PROMPT_CHUNKS * CHUNK_ROWS
    nt = seq_len // rows
    n_blocks = t // rows
    cur = lambda s: (jnp.minimum(s, n_blocks - 1), 0)
    prev = lambda s: (jnp.maximum(s - 1, 0), 0)
    seq = lambda s: jnp.maximum(s - 1, 0) // nt
    bufs = [pltpu.VMEM((rows, QK_DIM), bf16), pltpu.VMEM((rows, QK_DIM), f32),
            pltpu.VMEM((rows, V_DIM), bf16), pltpu.VMEM((rows, 2 * LANES), f32)]
    return pl.pallas_call(
        functools.partial(_l0_prompt_body, nt=nt, n_chunks=PROMPT_CHUNKS),
        grid=(n_blocks + 1,),
        in_specs=[pl.BlockSpec((rows, D_MODEL), cur), _const_spec((1, D_MODEL)), _const_spec(w_in.shape),
                  _const_spec(w_gate.shape), _const_spec((1, 2 * LANES)), _const_spec(sel.shape)],
        out_specs=[pl.BlockSpec((rows, V_DIM), cur), pl.BlockSpec((rows, V_DIM), prev),
                   pl.BlockSpec((1, N_HEADS, DK, DV), lambda s: (seq(s), 0, 0, 0)),
                   pl.BlockSpec((1, DK, LANES), lambda s: (seq(s), 0, 0)),
                   pl.BlockSpec((1, 1, LANES), lambda s: (seq(s), 0, 0))],
        out_shape=[jax.ShapeDtypeStruct((t, V_DIM), f32), jax.ShapeDtypeStruct((t, V_DIM), f32),
                   jax.ShapeDtypeStruct((n_batch, N_HEADS, DK, DV), f32),
                   jax.ShapeDtypeStruct((n_batch, DK, LANES), f32),
                   jax.ShapeDtypeStruct((n_batch, 1, LANES), f32)],
        scratch_shapes=bufs + bufs + [pltpu.VMEM((N_HEADS, DK, DV), f32), pltpu.VMEM((N_HEADS, DK, 1), f32),
                                      pltpu.VMEM((1, LANES), f32)],
        compiler_params=_params(1),
        name="l0_inproj_mlstm",
    )(x2d, g, w_in, w_gate, gate_bias, sel)


def _mlstm_core_sample(q, k, v, gates, c0, n0, m0, n_batch, seq_len):
    t = q.shape[0]
    sel = _bcast_selector()
    nseq = CHUNK_ROWS // seq_len
    row = lambda n: pl.BlockSpec((CHUNK_ROWS, n), lambda i: (i, 0))
    cspec = pl.BlockSpec((nseq, N_HEADS, DK, DV), lambda i: (i, 0, 0, 0))
    nspec = pl.BlockSpec((nseq, QK_DIM), lambda i: (i, 0))
    mspec = pl.BlockSpec((nseq, LANES), lambda i: (i, 0))
    return pl.pallas_call(
        functools.partial(_mlstm_sample_body, nseq=nseq, cs=seq_len),
        grid=(t // CHUNK_ROWS,),
        in_specs=[row(QK_DIM), row(QK_DIM), row(V_DIM), row(2 * LANES), _const_spec(sel.shape), cspec, nspec, mspec],
        out_specs=[row(V_DIM), cspec, nspec, mspec],
        out_shape=[jax.ShapeDtypeStruct((t, V_DIM), f32),
                   jax.ShapeDtypeStruct((n_batch, N_HEADS, DK, DV), f32),
                   jax.ShapeDtypeStruct((n_batch, QK_DIM), f32),
                   jax.ShapeDtypeStruct((n_batch, LANES), f32)],
        compiler_params=_params(1),
        name="l0_mlstm_core",
    )(q, k, v, gates, sel, c0, n0, m0)


def _outproj_ffn_body(h_ref, o_ref, x_ref, hg_ref, wout_ref, gffn_ref, wup_ref, wdn_ref, y_ref):
    x1s = []
    for r in _row_splits(h_ref.shape[0]):
        hn = h_ref[r, :] * hg_ref[...] * jax.nn.sigmoid(o_ref[r, :])
        x1s.append(x_ref[r, :] + _dot(hn.astype(bf16), wout_ref[...]))
    for r, y in zip(_row_splits(h_ref.shape[0]), _ffn(x1s, gffn_ref, wup_ref, wdn_ref)):
        y_ref[r, :] = y


def _outproj_ffn(hmix, o, x2d, head_g, w_out, g_ffn, w_up, w_down):
    t = x2d.shape[0]
    row = pl.BlockSpec((ROW_TILE, D_MODEL), lambda i: (i, 0))
    return pl.pallas_call(
        _outproj_ffn_body,
        grid=(t // ROW_TILE,),
        in_specs=[row, row, row, _const_spec((1, V_DIM)), _const_spec(w_out.shape),
                  _const_spec((1, D_MODEL)), _layer_spec(w_up.shape, 0), _layer_spec(w_down.shape, 0)],
        out_specs=row,
        out_shape=jax.ShapeDtypeStruct((t, D_MODEL), f32),
        compiler_params=_params(1),
        name="l0_outproj_ffn",
    )(hmix, o, x2d, head_g, w_out, g_ffn, w_up, w_down)


def _s5_prep_body(lre_ref, lim_ref, ldt_ref, bre_ref, bim_ref, cre_ref, cim_ref,
                  are_ref, aim_ref, wbr_ref, wbi_ref, vre_ref, vim_ref):
    lr = lre_ref[...]
    li = lim_ref[...]
    dt = jnp.exp(ldt_ref[...])
    mag = jnp.exp(lr * dt)
    a_re = mag * jnp.cos(li * dt)
    a_im = mag * jnp.sin(li * dt)
    den = lr * lr + li * li
    z_re = a_re - 1.0
    coef_re = ((z_re * lr + a_im * li) / den)[:, None, :]
    coef_im = ((a_im * lr - z_re * li) / den)[:, None, :]
    br = bre_ref[...]
    bi = bim_ref[...]
    are_ref[...] = a_re
    aim_ref[...] = a_im
    bb = (coef_re * br - coef_im * bi, coef_re * bi + coef_im * br)
    c_t = (cre_ref[...].reshape(S5_GROUPS * S5_GROUP, S5_STATE).T,
           cim_ref[...].reshape(S5_GROUPS * S5_GROUP, S5_STATE).T)

    groups = S5_BLOCK_CH // S5_GROUP
    log_ch, log_st = S5_GROUP.bit_length() - 1, S5_STATE.bit_length() - 1
    shape_b = (S5_BLOCK_CH, S5_BLOCK_ST)
    diag_b = ((lax.broadcasted_iota(jnp.int32, shape_b, 0) >> log_ch)
              == (lax.broadcasted_iota(jnp.int32, shape_b, 1) >> log_st))
    shape_c = (S5_BLOCK_ST, S5_BLOCK_CH)
    diag_c = ((lax.broadcasted_iota(jnp.int32, shape_c, 0) >> log_st)
              == (lax.broadcasted_iota(jnp.int32, shape_c, 1) >> log_ch))
    for j in range(N_S5_BLOCKS):
        for src, dst in zip(bb, (wbr_ref, wbi_ref)):
            blk = src[j * groups:(j + 1) * groups].reshape(S5_BLOCK_CH, S5_STATE)
            dst[j] = jnp.where(diag_b, jnp.concatenate([blk] * groups, axis=1), 0.0).astype(bf16)
        for src, dst in zip(c_t, (vre_ref, vim_ref)):
            blk = src[:, j * S5_BLOCK_CH:(j + 1) * S5_BLOCK_CH]
            rep = jnp.broadcast_to(blk[None], (groups, S5_STATE, S5_BLOCK_CH)).reshape(S5_BLOCK_ST, S5_BLOCK_CH)
            dst[j] = jnp.where(diag_c, rep, 0.0).astype(bf16)


def _s5_prep(lam_re, lam_im, log_dt, b_re_t, b_im_t, c_re, c_im):
    gp = jax.ShapeDtypeStruct((S5_GROUPS, S5_STATE), f32)
    wb = jax.ShapeDtypeStruct((N_S5_BLOCKS, S5_BLOCK_CH, S5_BLOCK_ST), bf16)
    vc = jax.ShapeDtypeStruct((N_S5_BLOCKS, S5_BLOCK_ST, S5_BLOCK_CH), bf16)
    return pl.pallas_call(_s5_prep_body, out_shape=[gp, gp, wb, wb, vc, vc], name="l1_s5_prep")(
        lam_re, lam_im, log_dt, b_re_t, b_im_t, c_re, c_im)


def _s5_body(x_ref, h0r_ref, h0i_ref, g_ref, are_ref, aim_ref, wbr_ref, wbi_ref, vre_ref, vim_ref,
             d_ref, wglu_ref, gffn_ref, wup_ref, wdn_ref, gfin_ref,
             y_ref, sre_ref, sim_ref, sr_scr, si_scr, hr_scr, hi_scr, act_scr, *io_scr, bt, lt, dma_io):
    rows = bt * lt
    step = pl.program_id(1)
    n_steps = pl.num_programs(1)

    @pl.when(step == 0)
    def _():
        sr_scr[...] = h0r_ref[...]
        si_scr[...] = h0i_ref[...]

    if dma_io:
        xbuf, ybuf, in_sem, out_sem = io_scr
        slot = lax.rem(step, 2)

        def in_copy(s, sl, b):
            return pltpu.make_async_copy(x_ref.at[b, pl.ds(s * lt, lt), :], xbuf.at[sl, :, b, :], in_sem.at[sl, b])

        def out_copy(s, sl, b):
            return pltpu.make_async_copy(ybuf.at[sl, :, b, :], y_ref.at[b, pl.ds(s * lt, lt), :], out_sem.at[sl, b])

        @pl.when(step == 0)
        def _():
            for b in range(bt):
                in_copy(0, 0, b).start()

        @pl.when(step + 1 < n_steps)
        def _():
            for b in range(bt):
                in_copy(step + 1, 1 - slot, b).start()

        for b in range(bt):
            in_copy(step, slot, b).wait()
        xt = xbuf[slot].reshape(rows, D_MODEL)
    else:
        xt = jnp.concatenate([x_ref[:, t, :] for t in range(lt)], axis=0)
    u = _rms(xt, g_ref[...])
    ub = u.astype(bf16)

    nbuf = hr_scr.shape[0]

    def project_in(j):
        ch = slice(j * S5_BLOCK_CH, (j + 1) * S5_BLOCK_CH)
        hr_scr[j % nbuf] = _dot(ub[:, ch], wbr_ref[j])
        hi_scr[j % nbuf] = _dot(ub[:, ch], wbi_ref[j])

    def recur(j):
        st = slice(j * S5_BLOCK_ST, (j + 1) * S5_BLOCK_ST)
        a_re = jnp.broadcast_to(are_ref[:, st], (bt, S5_BLOCK_ST))
        a_im = jnp.broadcast_to(aim_ref[:, st], (bt, S5_BLOCK_ST))
        s_re = sr_scr[:, st]
        s_im = si_scr[:, st]
        for t in range(lt):
            r = slice(t * bt, (t + 1) * bt)
            n_re = a_re * s_re - a_im * s_im + hr_scr[j % nbuf, r, :]
            n_im = a_re * s_im + a_im * s_re + hi_scr[j % nbuf, r, :]
            hr_scr[j % nbuf, r, :] = n_re
            hi_scr[j % nbuf, r, :] = n_im
            s_re, s_im = n_re, n_im
        sr_scr[:, st] = s_re
        si_scr[:, st] = s_im

    def project_out(j):
        ch = slice(j * S5_BLOCK_CH, (j + 1) * S5_BLOCK_CH)
        yj = (_dot(hr_scr[j % nbuf].astype(bf16), vre_ref[j]) - _dot(hi_scr[j % nbuf].astype(bf16), vim_ref[j])
              + d_ref[:, ch] * u[:, ch])
        act_scr[:, ch] = jax.nn.gelu(yj).astype(bf16)

    project_in(0)
    for j in range(N_S5_BLOCKS + 1):
        if j + 1 < N_S5_BLOCKS:
            project_in(j + 1)
        if j >= 1:
            project_out(j - 1)
        if j < N_S5_BLOCKS:
            recur(j)

    sre_ref[...] = sr_scr[...]
    sim_ref[...] = si_scr[...]
    x3s = []
    for r in _row_splits(rows):
        ag = _dot(act_scr[r, :], wglu_ref[...])
        x3s.append(xt[r, :] + ag[:, :D_MODEL] * jax.nn.sigmoid(ag[:, D_MODEL:]))
    y = jnp.concatenate([_rms(x4, gfin_ref[...]) for x4 in _ffn(x3s, gffn_ref, wup_ref, wdn_ref)], axis=0)
    if dma_io:
        @pl.when(step >= 2)
        def _():
            for b in range(bt):
                out_copy(step - 2, slot, b).wait()

        ybuf[slot] = y.reshape(lt, bt, D_MODEL)
        for b in range(bt):
            out_copy(step, slot, b).start()

        @pl.when(step == n_steps - 1)
        def _():
            @pl.when(step >= 1)
            def _():
                for b in range(bt):
                    out_copy(step - 1, 1 - slot, b).wait()

            for b in range(bt):
                out_copy(step, slot, b).wait()
    else:
        for t in range(lt):
            y_ref[:, t, :] = y[t * bt:(t + 1) * bt, :]


def _s5_layer(x3d, h0_re, h0_im, bt, lt, g_mix, a_re, a_im, wb_re, wb_im, v_re, v_im, d_skip,
              w_glu, g_ffn, w_up, w_down, g_final):
    n_batch, seq_len, _ = x3d.shape
    grid = (n_batch // bt, seq_len // lt)
    sspec = pl.BlockSpec((bt, S5_N), lambda i, t: (i, 0))
    dma_io = n_batch == bt
    if dma_io:
        xspec = pl.BlockSpec(memory_space=pl.ANY)
        io_scratch = [pltpu.VMEM((2, lt, bt, D_MODEL), f32), pltpu.VMEM((2, lt, bt, D_MODEL), f32),
                      pltpu.SemaphoreType.DMA((2, bt)), pltpu.SemaphoreType.DMA((2, bt))]
    else:
        xspec = pl.BlockSpec((bt, lt, D_MODEL), lambda i, t: (i, t, 0))
        io_scratch = []
    consts = (g_mix, a_re, a_im, wb_re, wb_im, v_re, v_im, d_skip, w_glu, g_ffn, w_up, w_down, g_final)
    const_specs = [_layer_spec(c.shape, 1) if c is w_up or c is w_down else _const_spec(c.shape) for c in consts]
    rows = bt * lt
    return pl.pallas_call(
        functools.partial(_s5_body, bt=bt, lt=lt, dma_io=dma_io),
        grid=grid,
        in_specs=[xspec, sspec, sspec] + const_specs,
        out_specs=[xspec, sspec, sspec],
        out_shape=[jax.ShapeDtypeStruct(x3d.shape, f32), jax.ShapeDtypeStruct((n_batch, S5_N), f32),
                   jax.ShapeDtypeStruct((n_batch, S5_N), f32)],
        scratch_shapes=[pltpu.VMEM((bt, S5_N), f32), pltpu.VMEM((bt, S5_N), f32),
                        pltpu.VMEM((3, rows, S5_BLOCK_ST), f32), pltpu.VMEM((3, rows, S5_BLOCK_ST), f32),
                        pltpu.VMEM((rows, D_MODEL), bf16)] + io_scratch,
        compiler_params=_params(2),
        name="l1_s5_ffn",
    )(x3d, h0_re, h0_im, *consts)


def _rep_lanes(x):
    return jnp.tile(x, (1,) * (x.ndim - 1) + (LANES // x.shape[-1],))


def _mlstm_mixer(x, state, p):
    n_batch, seq_len, _ = x.shape
    x2d = x.reshape(n_batch * seq_len, D_MODEL)
    if state is None:
        o, hmix, c_new, n_new, m_new = _l0_prompt(x2d, p["g_mix0"], p["w_in"], p["w_gate"], p["gate_bias"],
                                                  n_batch, seq_len)
        n_new = jnp.swapaxes(n_new[:, :, :N_HEADS], 1, 2)
    else:
        c0, n0, m0 = state
        q, k, v, o, gates = _inproj(x2d, p["g_mix0"], p["w_in"], p["w_gate"], p["gate_bias"])
        hmix, c_new, n_new, m_new = _mlstm_core_sample(
            q, k, v, gates, c0, n0.reshape(n_batch, QK_DIM), _rep_lanes(m0), n_batch, seq_len)
    states = (c_new.reshape(1, n_batch, N_HEADS, DK, DV),
              n_new.reshape(1, n_batch, N_HEADS, DK),
              m_new.reshape(n_batch, LANES)[:, :N_HEADS].reshape(1, n_batch, N_HEADS))
    return (hmix, o, x2d), states


def _s5_stage(x2, shape, state, p):
    n_batch, seq_len, _ = shape
    if state is None:
        h0_re = jnp.zeros((n_batch, S5_N), f32)
        h0_im = h0_re
        bt, lt = n_batch, ROW_TILE // n_batch
    else:
        h0_re, h0_im = (s.reshape(n_batch, S5_N) for s in state)
        bt, lt = ROW_TILE // seq_len, seq_len
    y, s_re, s_im = _s5_layer(x2.reshape(shape), h0_re, h0_im, bt, lt,
                              p["g_mix1"], p["a_re"], p["a_im"], p["wb_re"], p["wb_im"], p["v_re"], p["v_im"],
                              p["d_skip"], p["w_glu"], p["g_ffn1"], p["w_up"], p["w_down"], p["g_final"])
    return y, (s_re.reshape(1, n_batch, S5_GROUPS, S5_STATE), s_im.reshape(1, n_batch, S5_GROUPS, S5_STATE))


def kernel(x_prompt, x_sample, state_mlstm_C, state_mlstm_n, state_mlstm_m, state_s5_re, state_s5_im,
           norm_mix_g, norm_ffn_g, norm_final_g, mlstm_w_in, mlstm_b_i, mlstm_b_f, mlstm_head_norm_g,
           mlstm_w_out, s5_lambda_re, s5_lambda_im, s5_log_dt, s5_b_re, s5_b_im, s5_c_re, s5_c_im,
           s5_d, s5_w_glu, ffn_w_up, ffn_w_down):
    w_in = mlstm_w_in[0]
    n_qkvo = W_QKVO_COLS
    a_re, a_im, wb_re, wb_im, v_re, v_im = _s5_prep(
        s5_lambda_re[0], s5_lambda_im[0], s5_log_dt[0].reshape(S5_GROUPS, 1),
        jnp.swapaxes(s5_b_re[0], 1, 2), jnp.swapaxes(s5_b_im[0], 1, 2), s5_c_re[0], s5_c_im[0])
    p = dict(
        g_mix0=norm_mix_g[0].reshape(1, D_MODEL), g_mix1=norm_mix_g[1].reshape(1, D_MODEL),
        g_ffn0=norm_ffn_g[0].reshape(1, D_MODEL), g_ffn1=norm_ffn_g[1].reshape(1, D_MODEL),
        g_final=norm_final_g.reshape(1, D_MODEL),
        w_in=jnp.swapaxes(w_in, 0, 1).astype(bf16),
        w_gate=jnp.concatenate([_rep_lanes(w_in[:, n_qkvo:n_qkvo + N_HEADS]),
                                _rep_lanes(w_in[:, n_qkvo + N_HEADS:])], axis=1).astype(bf16),
        gate_bias=jnp.concatenate([_rep_lanes(mlstm_b_i[0][None]), _rep_lanes(mlstm_b_f[0][None])], axis=1),
        head_g=mlstm_head_norm_g[0].reshape(1, V_DIM),
        w_out=mlstm_w_out[0].astype(bf16),
        w_up=ffn_w_up.astype(bf16), w_down=ffn_w_down.astype(bf16),
        a_re=a_re.reshape(1, S5_N), a_im=a_im.reshape(1, S5_N),
        wb_re=wb_re, wb_im=wb_im, v_re=v_re, v_im=v_im,
        d_skip=s5_d[0].reshape(1, D_MODEL),
        w_glu=s5_w_glu[0].astype(bf16),
    )
    mix_p, mlstm_p = _mlstm_mixer(x_prompt, None, p)
    mix_s, mlstm_s = _mlstm_mixer(x_sample, (state_mlstm_C[0], state_mlstm_n[0], state_mlstm_m[0]), p)
    x2_p, x2_s = (_outproj_ffn(*mix, p["head_g"], p["w_out"], p["g_ffn0"], p["w_up"], p["w_down"])
                  for mix in (mix_p, mix_s))
    y_p, s5_p = _s5_stage(x2_p, x_prompt.shape, None, p)
    y_s, s5_s = _s5_stage(x2_s, x_sample.shape, (state_s5_re[0], state_s5_im[0]), p)
    return (y_p, y_s) + mlstm_p + s5_p + mlstm_s + s5_s
```

```python
import functools
import types

import jax
import jax.numpy as jnp
import numpy as np
from jax import lax
from jax.experimental import pallas as pl
from jax.experimental.pallas import tpu as pltpu

f32 = jnp.float32
bf16 = jnp.bfloat16

D_MODEL = 1024
N_HEADS = 8
DK = 64
DV = 128
QK_DIM = N_HEADS * DK
V_DIM = N_HEADS * DV
W_QKVO_COLS = 2 * QK_DIM + 2 * V_DIM
D_FF = 4 * D_MODEL
S5_GROUPS = 64
S5_GROUP = 16
S5_STATE = 64
S5_N = S5_GROUPS * S5_STATE
EPS = 1e-6

LANES = 128
SUBLANES = 8
ROW_TILE = 512
CHUNK_ROWS = 256
PROMPT_CHUNKS = 2
SEQ_GROUP = 4
FF_CHUNK = 1024
ROW_SPLITS = 2
S5_BLOCK_CH = LANES
S5_BLOCK_ST = S5_BLOCK_CH // S5_GROUP * S5_STATE
N_S5_BLOCKS = D_MODEL // S5_BLOCK_CH
VMEM_LIMIT_BYTES = 56 * 1024 * 1024
NEG_BIG = -1e30
LOG_HEADS = N_HEADS.bit_length() - 1
N_BCAST = 3


def _params(n_axes):
    return pltpu.CompilerParams(dimension_semantics=("arbitrary",) * n_axes,
                                vmem_limit_bytes=VMEM_LIMIT_BYTES)


def _const_spec(shape):
    nd = len(shape)
    return pl.BlockSpec(shape, lambda *_: (0,) * nd, pipeline_mode=pl.Buffered(1))


def _layer_spec(shape, layer):
    nd = len(shape)
    return pl.BlockSpec((None,) + tuple(shape[1:]), lambda *_: (layer,) + (0,) * (nd - 1),
                        pipeline_mode=pl.Buffered(1))


def _dot(a, b):
    return jnp.dot(a, b, preferred_element_type=f32)


def _dot_nt(a, b):
    return lax.dot_general(a, b, (((1,), (1,)), ((), ())), preferred_element_type=f32)


def _rms(x, g):
    return x * lax.rsqrt(jnp.mean(x * x, axis=-1, keepdims=True) + EPS) * g


def _row_splits(rows):
    size = rows // ROW_SPLITS
    return [slice(i * size, (i + 1) * size) for i in range(ROW_SPLITS)]


def _ffn(x1s, g_ref, wup_ref, wdn_ref):
    xn = [_rms(x1, g_ref[...]).astype(bf16) for x1 in x1s]
    acc = [None] * len(x1s)
    for c in range(D_FF // FF_CHUNK):
        cols = slice(c * FF_CHUNK, (c + 1) * FF_CHUNK)
        for i in range(len(x1s)):
            hid = _dot(xn[i], wup_ref[:, cols])
            hid = jnp.square(jnp.maximum(hid, 0.0)).astype(bf16)
            part = _dot(hid, wdn_ref[cols, :])
            acc[i] = part if acc[i] is None else acc[i] + part
    return [x1 + a for x1, a in zip(x1s, acc)]


def _inproj_body(x_ref, g_ref, w_ref, wg_ref, bias_ref, q_ref, k_ref, v_ref, o_ref, gate_ref):
    xn = _rms(x_ref[...], g_ref[...]).astype(bf16)
    q, k, v, o, logi, logf = _inproj_values(xn, w_ref, wg_ref, bias_ref)
    q_ref[...] = q
    k_ref[...] = k
    v_ref[...] = v
    o_ref[...] = o
    gate_ref[:, :LANES] = logi
    gate_ref[:, LANES:] = logf


def _inproj(x2d, g, w_in, w_gate, gate_bias):
    t = x2d.shape[0]
    row = lambda n: pl.BlockSpec((ROW_TILE, n), lambda i: (i, 0))
    return pl.pallas_call(
        _inproj_body,
        grid=(t // ROW_TILE,),
        in_specs=[row(D_MODEL), _const_spec((1, D_MODEL)), _const_spec(w_in.shape),
                  _const_spec(w_gate.shape), _const_spec((1, 2 * LANES))],
        out_specs=[row(QK_DIM), row(QK_DIM), row(V_DIM), row(V_DIM), row(2 * LANES)],
        out_shape=[jax.ShapeDtypeStruct((t, QK_DIM), bf16), jax.ShapeDtypeStruct((t, QK_DIM), f32),
                   jax.ShapeDtypeStruct((t, V_DIM), bf16), jax.ShapeDtypeStruct((t, V_DIM), f32),
                   jax.ShapeDtypeStruct((t, 2 * LANES), f32)],
        compiler_params=_params(1),
        name="l0_inproj",
    )(x2d, g, w_in, w_gate, gate_bias)


def _row_prefix(x, cs, rpos, is_max):
    sh = 1
    while sh < cs:
        prev = pltpu.roll(x, sh, 0)
        if is_max:
            x = jnp.maximum(x, jnp.where(rpos >= sh, prev, NEG_BIG))
        else:
            x = x + jnp.where(rpos >= sh, prev, 0.0)
        sh *= 2
    return x


def _seg_last(x, nseq, cs):
    if nseq == 1:
        return x[cs - 1:cs, :]
    last = x.reshape(nseq, cs, x.shape[1])[:, cs - 1:cs, :]
    return jnp.broadcast_to(last, (nseq, cs, x.shape[1])).reshape(nseq * cs, x.shape[1])


def _gate_algebra(logi, logf, m_in, nseq, cs, sel_ref):
    rows = nseq * cs
    rpos = lax.broadcasted_iota(jnp.int32, (rows, LANES), 0) & (cs - 1)
    b = _row_prefix(logf, cs, rpos, False)
    a = logi - b
    g = jnp.maximum(m_in, _row_prefix(a, cs, rpos, True))
    g_last = _seg_last(g, nseq, cs)
    w_state = jnp.exp(a - g_last)

    group = lax.broadcasted_iota(jnp.int32, (rows, LANES), 1) >> LOG_HEADS
    pieces = jnp.zeros((rows, LANES), f32)
    for i, val in enumerate((g, jnp.exp(m_in - g), jnp.exp(-(b + g)))):
        hi = val.astype(bf16).astype(f32)
        mid = (val - hi).astype(bf16).astype(f32)
        lo = (val - hi - mid).astype(bf16).astype(f32)
        for j, piece in enumerate((hi, mid, lo)):
            pieces = jnp.where(group == 3 * i + j, piece, pieces)
    bcast = _dot(pieces.astype(bf16), sel_ref[...])
    tile = lambda i, h: bcast[:, (i * N_HEADS + h) * DV:(i * N_HEADS + h + 1) * DV]
    return types.SimpleNamespace(
        a_t=a.T,
        w_state=w_state, w_state_t=w_state.T,
        decay=jnp.exp(m_in - g_last),
        m_new=_seg_last(b, nseq, cs) + g_last,
        g=lambda h: tile(0, h), w_inter=lambda h: tile(1, h), clamp=lambda h: tile(2, h))


def _head_outputs(qk, gates, inter, qn, v_h, h_ref, nseq, cs):
    rows = nseq * cs
    log_cs = cs.bit_length() - 1
    heads = range(N_HEADS)
    chunks = range(len(qk))
    ri = lax.broadcasted_iota(jnp.int32, (rows, rows), 0)
    ci = lax.broadcasted_iota(jnp.int32, (rows, rows), 1)
    causal = ci <= ri
    if nseq > 1:
        causal = jnp.logical_and(causal, (ri >> log_cs) == (ci >> log_cs))
    for c in chunks:
        for h in heads:
            g_rows = jnp.concatenate([gates[c].g(h)] * (rows // DV), axis=1)
            w = jnp.exp(jnp.where(causal, gates[c].a_t[h:h + 1, :] - g_rows, NEG_BIG))
            s = qk[c][h]() * w
            s_sum = jnp.sum(s, axis=1, keepdims=True)
            intra = _dot(s.astype(bf16), v_h[c][h])
            w_inter = gates[c].w_inter(h)
            num = w_inter * inter[c][h] + intra
            den = jnp.maximum(jnp.abs(w_inter * qn[c][h] + s_sum), gates[c].clamp(h))
            h_ref[c * rows:(c + 1) * rows, h * DV:(h + 1) * DV] = num * lax.rsqrt(
                jnp.mean(num * num, axis=1, keepdims=True) + EPS * (den * den))


def _sequence_chunks(q_ref, k_ref, v_ref, gate_ref, sel_ref, h_ref, c_state, n_state, m_in, n_chunks,
                     after_gates=lambda: None):
    rows = CHUNK_ROWS
    heads = range(N_HEADS)
    chunks = range(n_chunks)
    c_state, n_state = list(c_state), list(n_state)
    rsl = lambda c: slice(c * rows, (c + 1) * rows)
    q_h = [[q_ref[rsl(c), h * DK:(h + 1) * DK] for h in heads] for c in chunks]
    k_h = [[k_ref[rsl(c), h * DK:(h + 1) * DK] for h in heads] for c in chunks]
    v_h = [[v_ref[rsl(c), h * DV:(h + 1) * DV] for h in heads] for c in chunks]

    qk = [[functools.partial(_dot_nt, q_h[c][h], k_h[c][h].astype(bf16)) for h in heads] for c in chunks]

    gates = []
    for c in chunks:
        gates.append(_gate_algebra(gate_ref[rsl(c), :LANES], gate_ref[rsl(c), LANES:],
                                   m_in if c == 0 else gates[c - 1].m_new, 1, rows, sel_ref))
    after_gates()

    inter = [[None] * N_HEADS for _ in chunks]
    qn = [[None] * N_HEADS for _ in chunks]
    for c in chunks:
        ga = gates[c]
        k_t = k_ref[rsl(c), :].T
        for h in heads:
            rhs = jnp.concatenate([c_state[h].astype(bf16),
                                   jnp.broadcast_to(n_state[h], (DK, DV)).astype(bf16)], axis=1)
            inter2 = _dot(q_h[c][h], rhs)
            inter[c][h] = inter2[:, :DV]
            qn[c][h] = inter2[:, DV:]
            wk_t = k_t[h * DK:(h + 1) * DK, :] * ga.w_state_t[h:h + 1, :]
            dec = ga.decay[:, h:h + 1]
            c_state[h] = dec * c_state[h] + _dot(wk_t.astype(bf16), v_h[c][h])
            n_state[h] = dec * n_state[h] + jnp.sum(wk_t, axis=1, keepdims=True)

    _head_outputs(qk, gates, inter, qn, v_h, h_ref, 1, rows)
    return c_state, n_state, gates[-1].m_new


def _mlstm_sample_body(q_ref, k_ref, v_ref, gate_ref, sel_ref, c0_ref, n0_ref, m0_ref,
                       h_ref, cout_ref, nout_ref, mout_ref, *, nseq, cs):
    rows = nseq * cs
    log_cs = cs.bit_length() - 1
    log_dk = DK.bit_length() - 1
    heads = range(N_HEADS)
    q_h = [q_ref[:, h * DK:(h + 1) * DK] for h in heads]
    k_h = [k_ref[:, h * DK:(h + 1) * DK] for h in heads]
    v_h = [v_ref[:, h * DV:(h + 1) * DV] for h in heads]
    qk = [functools.partial(_dot_nt, q_h[h], k_h[h].astype(bf16)) for h in heads]
    m_in = jnp.broadcast_to(m0_ref[...][:, None, :], (nseq, cs, LANES)).reshape(rows, LANES)
    ga = _gate_algebra(gate_ref[:, :LANES], gate_ref[:, LANES:], m_in, nseq, cs, sel_ref)

    grp_rows, grp_k = SEQ_GROUP * cs, SEQ_GROUP * DK
    bd_q = ((lax.broadcasted_iota(jnp.int32, (grp_rows, grp_k), 0) >> log_cs)
            == (lax.broadcasted_iota(jnp.int32, (grp_rows, grp_k), 1) >> log_dk))
    bd_k = ((lax.broadcasted_iota(jnp.int32, (nseq * DK, rows), 0) >> log_dk)
            == (lax.broadcasted_iota(jnp.int32, (nseq * DK, rows), 1) >> log_cs))
    k_t = k_ref[...].T
    inter, qn = [], []
    for h in heads:
        c_prev = c0_ref[:, h].reshape(nseq * DK, DV)
        c_bf = c_prev.astype(bf16)
        q32 = q_h[h].astype(f32)
        parts = []
        for j in range(nseq // SEQ_GROUP):
            q_j = q32[j * grp_rows:(j + 1) * grp_rows, :]
            q_bd = jnp.where(bd_q, jnp.concatenate([q_j] * SEQ_GROUP, axis=1), 0.0).astype(bf16)
            parts.append(_dot(q_bd, c_bf[j * grp_k:(j + 1) * grp_k, :]))
        inter.append(jnp.concatenate(parts, axis=0))
        n_prev = n0_ref[:, h * DK:(h + 1) * DK]
        n_rows = jnp.broadcast_to(n_prev[:, None, :], (nseq, cs, DK)).reshape(rows, DK)
        qn.append(jnp.sum(q32 * n_rows, axis=1, keepdims=True))

        wk_t = k_t[h * DK:(h + 1) * DK, :] * ga.w_state_t[h:h + 1, :]
        wk_bd = jnp.where(bd_k, jnp.broadcast_to(wk_t[None], (nseq, DK, rows)).reshape(nseq * DK, rows), 0.0)
        dc = _dot(wk_bd.astype(bf16), v_h[h])
        dec_col = ga.decay[:, h:h + 1]
        dec_rows = jnp.broadcast_to(dec_col.reshape(nseq, 1, cs, 1), (nseq, DK // cs, cs, 1)).reshape(nseq * DK, 1)
        cout_ref[:, h] = (dec_rows * c_prev + dc).reshape(nseq, DK, DV)
        dec_seq = jnp.max(dec_col.reshape(nseq, cs, 1), axis=1)
        nout_ref[:, h * DK:(h + 1) * DK] = dec_seq * n_prev + jnp.sum(
            (ga.w_state[:, h:h + 1] * k_h[h]).reshape(nseq, cs, DK), axis=1)
    mout_ref[...] = jnp.max(ga.m_new.reshape(nseq, cs, LANES), axis=1)
    _head_outputs([qk], [ga], [inter], [qn], [v_h], h_ref, nseq, cs)


def _inproj_values(xn, w_ref, wg_ref, bias_ref):
    q = _dot_nt(xn, w_ref[0:QK_DIM, :]).astype(bf16)
    k = _dot_nt(xn, w_ref[QK_DIM:2 * QK_DIM, :]) * (DK ** -0.5)
    v = _dot_nt(xn, w_ref[2 * QK_DIM:2 * QK_DIM + V_DIM, :]).astype(bf16)
    o = _dot_nt(xn, w_ref[2 * QK_DIM + V_DIM:W_QKVO_COLS, :])
    gz = _dot(xn, wg_ref[...]) + bias_ref[...]
    return q, k, v, o, gz[:, :LANES], jax.nn.log_sigmoid(gz[:, LANES:])


def _l0_prompt_body(x_ref, g_ref, w_ref, wg_ref, bias_ref, sel_ref,
                    o_ref, h_ref, cout_ref, nout_ref, mout_ref,
                    qa, ka, va, ga, qb, kb, vb, gb, c_scr, n_scr, m_scr, *, nt, n_chunks, last):
    step = pl.program_id(0)
    heads = range(N_HEADS)
    sets = ((qa, ka, va, ga), (qb, kb, vb, gb))

    def project(wr):
        q_w, k_w, v_w, g_w = wr
        xn = _rms(x_ref[...], g_ref[...]).astype(bf16)
        q, k, v, o, logi, logf = _inproj_values(xn, w_ref, wg_ref, bias_ref)
        q_w[...] = q
        k_w[...] = k
        v_w[...] = v
        o_ref[...] = o
        g_w[:, :LANES] = logi
        g_w[:, LANES:] = logf

    def core(rd, after_gates):
        first = lax.rem(step + nt - 1, nt) == 0
        c_state = [jnp.where(first, 0.0, c_scr[h]) for h in heads]
        n_state = [jnp.where(first, 0.0, n_scr[h]) for h in heads]
        m_in = jnp.where(first, 0.0, m_scr[...])
        c_state, n_state, m_new = _sequence_chunks(*rd, sel_ref, h_ref, c_state, n_state, m_in, n_chunks,
                                                   after_gates=after_gates)
        for h in heads:
            c_scr[h] = c_state[h]
            n_scr[h] = n_state[h]
        m_scr[...] = m_new

    @pl.when(step == 0)
    def _():
        for ref in (c_scr, n_scr, m_scr):
            ref[...] = jnp.zeros_like(ref)
        project(sets[0])

    for parity in range(2):
        wr, rd = sets[parity], sets[1 - parity]

        @pl.when(jnp.logical_and(jnp.logical_and(step > 0, step < last), lax.rem(step, 2) == parity))
        def _():
            core(rd, functools.partial(project, wr))

    @pl.when(step == last)
    def _():
        core(sets[1 - last % 2], lambda: None)

    @pl.when(jnp.logical_and(step >= 1, lax.rem(step, nt) == 0))
    def _():
        cout_ref[0] = c_scr[...]
        lane = lax.broadcasted_iota(jnp.int32, (DK, LANES), 1)
        n_mat = jnp.zeros((DK, LANES), f32)
        for h in heads:
            n_mat = jnp.where(lane == h, jnp.broadcast_to(n_scr[h], (DK, LANES)), n_mat)
        nout_ref[0] = n_mat
        mout_ref[0] = m_scr[...]


def _bcast_selector():
    src = np.arange(LANES)
    dst = np.arange(N_BCAST * N_HEADS * DV) // DV
    hit = (((src[:, None] >> LOG_HEADS) // 3 == dst[None, :] // N_HEADS)
           & ((src[:, None] & (N_HEADS - 1)) == dst[None, :] % N_HEADS))
    return jnp.asarray(hit, dtype=bf16)


def _l0_prompt(x2d, g, w_in, w_gate, gate_bias, n_batch, seq_len):
    t = x2d.shape[0]
    sel = _bcast_selector()
    rows = PROMPT_CHUNKS * CHUNK_ROWS
    nt = seq_len // rows
    n_blocks = t // rows
    cur = lambda s: (jnp.minimum(s, n_blocks - 1), 0)
    prev = lambda s: (jnp.maximum(s - 1, 0), 0)
    seq = lambda s: jnp.maximum(s - 1, 0) // nt
    bufs = [pltpu.VMEM((rows, QK_DIM), bf16), pltpu.VMEM((rows, QK_DIM), f32),
            pltpu.VMEM((rows, V_DIM), bf16), pltpu.VMEM((rows, 2 * LANES), f32)]
    return pl.pallas_call(
        functools.partial(_l0_prompt_body, nt=nt, n_chunks=PROMPT_CHUNKS, last=n_blocks),
        grid=(n_blocks + 1,),
        in_specs=[pl.BlockSpec((rows, D_MODEL), cur), _const_spec((1, D_MODEL)), _const_spec(w_in.shape),
                  _const_spec(w_gate.shape), _const_spec((1, 2 * LANES)), _const_spec(sel.shape)],
        out_specs=[pl.BlockSpec((rows, V_DIM), cur), pl.BlockSpec((rows, V_DIM), prev),
                   pl.BlockSpec((1, N_HEADS, DK, DV), lambda s: (seq(s), 0, 0, 0)),
                   pl.BlockSpec((1, DK, LANES), lambda s: (seq(s), 0, 0)),
                   pl.BlockSpec((1, 1, LANES), lambda s: (seq(s), 0, 0))],
        out_shape=[jax.ShapeDtypeStruct((t, V_DIM), f32), jax.ShapeDtypeStruct((t, V_DIM), f32),
                   jax.ShapeDtypeStruct((n_batch, N_HEADS, DK, DV), f32),
                   jax.ShapeDtypeStruct((n_batch, DK, LANES), f32),
                   jax.ShapeDtypeStruct((n_batch, 1, LANES), f32)],
        scratch_shapes=bufs + bufs + [pltpu.VMEM((N_HEADS, DK, DV), f32), pltpu.VMEM((N_HEADS, DK, 1), f32),
                                      pltpu.VMEM((1, LANES), f32)],
        compiler_params=_params(1),
        name="l0_inproj_mlstm",
    )(x2d, g, w_in, w_gate, gate_bias, sel)


def _mlstm_core_sample(q, k, v, gates, c0, n0, m0, n_batch, seq_len):
    t = q.shape[0]
    sel = _bcast_selector()
    nseq = CHUNK_ROWS // seq_len
    row = lambda n: pl.BlockSpec((CHUNK_ROWS, n), lambda i: (i, 0))
    cspec = pl.BlockSpec((nseq, N_HEADS, DK, DV), lambda i: (i, 0, 0, 0))
    nspec = pl.BlockSpec((nseq, QK_DIM), lambda i: (i, 0))
    mspec = pl.BlockSpec((nseq, LANES), lambda i: (i, 0))
    return pl.pallas_call(
        functools.partial(_mlstm_sample_body, nseq=nseq, cs=seq_len),
        grid=(t // CHUNK_ROWS,),
        in_specs=[row(QK_DIM), row(QK_DIM), row(V_DIM), row(2 * LANES), _const_spec(sel.shape), cspec, nspec, mspec],
        out_specs=[row(V_DIM), cspec, nspec, mspec],
        out_shape=[jax.ShapeDtypeStruct((t, V_DIM), f32),
                   jax.ShapeDtypeStruct((n_batch, N_HEADS, DK, DV), f32),
                   jax.ShapeDtypeStruct((n_batch, QK_DIM), f32),
                   jax.ShapeDtypeStruct((n_batch, LANES), f32)],
        compiler_params=_params(1),
        name="l0_mlstm_core",
    )(q, k, v, gates, sel, c0, n0, m0)


def _outproj_ffn_body(h_ref, o_ref, x_ref, hg_ref, wout_ref, gffn_ref, wup_ref, wdn_ref, y_ref):
    x1s = []
    for r in _row_splits(h_ref.shape[0]):
        hn = h_ref[r, :] * hg_ref[...] * jax.nn.sigmoid(o_ref[r, :])
        x1s.append(x_ref[r, :] + _dot(hn.astype(bf16), wout_ref[...]))
    for r, y in zip(_row_splits(h_ref.shape[0]), _ffn(x1s, gffn_ref, wup_ref, wdn_ref)):
        y_ref[r, :] = y


def _outproj_ffn(hmix, o, x2d, head_g, w_out, g_ffn, w_up, w_down):
    t = x2d.shape[0]
    row = pl.BlockSpec((ROW_TILE, D_MODEL), lambda i: (i, 0))
    return pl.pallas_call(
        _outproj_ffn_body,
        grid=(t // ROW_TILE,),
        in_specs=[row, row, row, _const_spec((1, V_DIM)), _const_spec(w_out.shape),
                  _const_spec((1, D_MODEL)), _layer_spec(w_up.shape, 0), _layer_spec(w_down.shape, 0)],
        out_specs=row,
        out_shape=jax.ShapeDtypeStruct((t, D_MODEL), f32),
        compiler_params=_params(1),
        name="l0_outproj_ffn",
    )(hmix, o, x2d, head_g, w_out, g_ffn, w_up, w_down)


def _s5_prep_body(lre_ref, lim_ref, ldt_ref, bre_ref, bim_ref, cre_ref, cim_ref,
                  are_ref, aim_ref, wbr_ref, wbi_ref, vre_ref, vim_ref):
    lr = lre_ref[...]
    li = lim_ref[...]
    dt = jnp.exp(ldt_ref[...])
    mag = jnp.exp(lr * dt)
    a_re = mag * jnp.cos(li * dt)
    a_im = mag * jnp.sin(li * dt)
    den = lr * lr + li * li
    z_re = a_re - 1.0
    coef_re = ((z_re * lr + a_im * li) / den)[:, None, :]
    coef_im = ((a_im * lr - z_re * li) / den)[:, None, :]
    br = bre_ref[...]
    bi = bim_ref[...]
    are_ref[...] = a_re
    aim_ref[...] = a_im
    bb = (coef_re * br - coef_im * bi, coef_re * bi + coef_im * br)
    c_t = (cre_ref[...].reshape(S5_GROUPS * S5_GROUP, S5_STATE).T,
           cim_ref[...].reshape(S5_GROUPS * S5_GROUP, S5_STATE).T)

    groups = S5_BLOCK_CH // S5_GROUP
    log_ch, log_st = S5_GROUP.bit_length() - 1, S5_STATE.bit_length() - 1
    shape_b = (S5_BLOCK_CH, S5_BLOCK_ST)
    diag_b = ((lax.broadcasted_iota(jnp.int32, shape_b, 0) >> log_ch)
              == (lax.broadcasted_iota(jnp.int32, shape_b, 1) >> log_st))
    shape_c = (S5_BLOCK_ST, S5_BLOCK_CH)
    diag_c = ((lax.broadcasted_iota(jnp.int32, shape_c, 0) >> log_st)
              == (lax.broadcasted_iota(jnp.int32, shape_c, 1) >> log_ch))
    for j in range(N_S5_BLOCKS):
        for src, dst in zip(bb, (wbr_ref, wbi_ref)):
            blk = src[j * groups:(j + 1) * groups].reshape(S5_BLOCK_CH, S5_STATE)
            dst[j] = jnp.where(diag_b, jnp.concatenate([blk] * groups, axis=1), 0.0).astype(bf16)
        for src, dst in zip(c_t, (vre_ref, vim_ref)):
            blk = src[:, j * S5_BLOCK_CH:(j + 1) * S5_BLOCK_CH]
            rep = jnp.broadcast_to(blk[None], (groups, S5_STATE, S5_BLOCK_CH)).reshape(S5_BLOCK_ST, S5_BLOCK_CH)
            dst[j] = jnp.where(diag_c, rep, 0.0).astype(bf16)


def _s5_prep(lam_re, lam_im, log_dt, b_re_t, b_im_t, c_re, c_im):
    gp = jax.ShapeDtypeStruct((S5_GROUPS, S5_STATE), f32)
    wb = jax.ShapeDtypeStruct((N_S5_BLOCKS, S5_BLOCK_CH, S5_BLOCK_ST), bf16)
    vc = jax.ShapeDtypeStruct((N_S5_BLOCKS, S5_BLOCK_ST, S5_BLOCK_CH), bf16)
    return pl.pallas_call(_s5_prep_body, out_shape=[gp, gp, wb, wb, vc, vc], name="l1_s5_prep")(
        lam_re, lam_im, log_dt, b_re_t, b_im_t, c_re, c_im)


def _s5_body(x_ref, h0r_ref, h0i_ref, g_ref, are_ref, aim_ref, wbr_ref, wbi_ref, vre_ref, vim_ref,
             d_ref, wglu_ref, gffn_ref, wup_ref, wdn_ref, gfin_ref,
             y_ref, sre_ref, sim_ref, sr_scr, si_scr, hr_scr, hi_scr, act_scr, *io_scr, bt, lt, dma_io):
    rows = bt * lt
    step = pl.program_id(1)
    n_steps = pl.num_programs(1)

    @pl.when(step == 0)
    def _():
        sr_scr[...] = h0r_ref[...]
        si_scr[...] = h0i_ref[...]

    if dma_io:
        xbuf, ybuf, in_sem, out_sem = io_scr
        slot = lax.rem(step, 2)

        def in_copy(s, sl, b):
            return pltpu.make_async_copy(x_ref.at[b, pl.ds(s * lt, lt), :], xbuf.at[sl, :, b, :], in_sem.at[sl, b])

        def out_copy(s, sl, b):
            return pltpu.make_async_copy(ybuf.at[sl, :, b, :], y_ref.at[b, pl.ds(s * lt, lt), :], out_sem.at[sl, b])

        @pl.when(step == 0)
        def _():
            for b in range(bt):
                in_copy(0, 0, b).start()

        @pl.when(step + 1 < n_steps)
        def _():
            for b in range(bt):
                in_copy(step + 1, 1 - slot, b).start()

        for b in range(bt):
            in_copy(step, slot, b).wait()
        xt = xbuf[slot].reshape(rows, D_MODEL)
    else:
        xt = jnp.concatenate([x_ref[:, t, :] for t in range(lt)], axis=0)
    u = _rms(xt, g_ref[...])
    ub = u.astype(bf16)

    nbuf = hr_scr.shape[0]

    def project_in(j):
        ch = slice(j * S5_BLOCK_CH, (j + 1) * S5_BLOCK_CH)
        hr_scr[j % nbuf] = _dot(ub[:, ch], wbr_ref[j])
        hi_scr[j % nbuf] = _dot(ub[:, ch], wbi_ref[j])

    def recur(j):
        st = slice(j * S5_BLOCK_ST, (j + 1) * S5_BLOCK_ST)
        a_re = jnp.broadcast_to(are_ref[:, st], (bt, S5_BLOCK_ST))
        a_im = jnp.broadcast_to(aim_ref[:, st], (bt, S5_BLOCK_ST))
        s_re = sr_scr[:, st]
        s_im = si_scr[:, st]
        for t in range(lt):
            r = slice(t * bt, (t + 1) * bt)
            n_re = a_re * s_re - a_im * s_im + hr_scr[j % nbuf, r, :]
            n_im = a_re * s_im + a_im * s_re + hi_scr[j % nbuf, r, :]
            hr_scr[j % nbuf, r, :] = n_re
            hi_scr[j % nbuf, r, :] = n_im
            s_re, s_im = n_re, n_im
        sr_scr[:, st] = s_re
        si_scr[:, st] = s_im

    def project_out(j):
        ch = slice(j * S5_BLOCK_CH, (j + 1) * S5_BLOCK_CH)
        yj = (_dot(hr_scr[j % nbuf].astype(bf16), vre_ref[j]) - _dot(hi_scr[j % nbuf].astype(bf16), vim_ref[j])
              + d_ref[:, ch] * u[:, ch])
        act_scr[:, ch] = jax.nn.gelu(yj).astype(bf16)

    project_in(0)
    for j in range(N_S5_BLOCKS + 1):
        if j + 1 < N_S5_BLOCKS:
            project_in(j + 1)
        if j >= 1:
            project_out(j - 1)
        if j < N_S5_BLOCKS:
            recur(j)

    sre_ref[...] = sr_scr[...]
    sim_ref[...] = si_scr[...]
    x3s = []
    for r in _row_splits(rows):
        ag = _dot(act_scr[r, :], wglu_ref[...])
        x3s.append(xt[r, :] + ag[:, :D_MODEL] * jax.nn.sigmoid(ag[:, D_MODEL:]))
    y = jnp.concatenate([_rms(x4, gfin_ref[...]) for x4 in _ffn(x3s, gffn_ref, wup_ref, wdn_ref)], axis=0)
    if dma_io:
        @pl.when(step >= 2)
        def _():
            for b in range(bt):
                out_copy(step - 2, slot, b).wait()

        ybuf[slot] = y.reshape(lt, bt, D_MODEL)
        for b in range(bt):
            out_copy(step, slot, b).start()

        @pl.when(step == n_steps - 1)
        def _():
            @pl.when(step >= 1)
            def _():
                for b in range(bt):
                    out_copy(step - 1, 1 - slot, b).wait()

            for b in range(bt):
                out_copy(step, slot, b).wait()
    else:
        for t in range(lt):
            y_ref[:, t, :] = y[t * bt:(t + 1) * bt, :]


def _s5_layer(x3d, h0_re, h0_im, bt, lt, g_mix, a_re, a_im, wb_re, wb_im, v_re, v_im, d_skip,
              w_glu, g_ffn, w_up, w_down, g_final):
    n_batch, seq_len, _ = x3d.shape
    grid = (n_batch // bt, seq_len // lt)
    sspec = pl.BlockSpec((bt, S5_N), lambda i, t: (i, 0))
    dma_io = n_batch == bt
    if dma_io:
        xspec = pl.BlockSpec(memory_space=pl.ANY)
        io_scratch = [pltpu.VMEM((2, lt, bt, D_MODEL), f32), pltpu.VMEM((2, lt, bt, D_MODEL), f32),
                      pltpu.SemaphoreType.DMA((2, bt)), pltpu.SemaphoreType.DMA((2, bt))]
    else:
        xspec = pl.BlockSpec((bt, lt, D_MODEL), lambda i, t: (i, t, 0))
        io_scratch = []
    consts = (g_mix, a_re, a_im, wb_re, wb_im, v_re, v_im, d_skip, w_glu, g_ffn, w_up, w_down, g_final)
    const_specs = [_layer_spec(c.shape, 1) if c is w_up or c is w_down else _const_spec(c.shape) for c in consts]
    rows = bt * lt
    return pl.pallas_call(
        functools.partial(_s5_body, bt=bt, lt=lt, dma_io=dma_io),
        grid=grid,
        in_specs=[xspec, sspec, sspec] + const_specs,
        out_specs=[xspec, sspec, sspec],
        out_shape=[jax.ShapeDtypeStruct(x3d.shape, f32), jax.ShapeDtypeStruct((n_batch, S5_N), f32),
                   jax.ShapeDtypeStruct((n_batch, S5_N), f32)],
        scratch_shapes=[pltpu.VMEM((bt, S5_N), f32), pltpu.VMEM((bt, S5_N), f32),
                        pltpu.VMEM((3, rows, S5_BLOCK_ST), f32), pltpu.VMEM((3, rows, S5_BLOCK_ST), f32),
                        pltpu.VMEM((rows, D_MODEL), bf16)] + io_scratch,
        compiler_params=_params(2),
        name="l1_s5_ffn",
    )(x3d, h0_re, h0_im, *consts)


def _rep_lanes(x):
    return jnp.tile(x, (1,) * (x.ndim - 1) + (LANES // x.shape[-1],))


def _mlstm_mixer(x, state, p):
    n_batch, seq_len, _ = x.shape
    x2d = x.reshape(n_batch * seq_len, D_MODEL)
    if state is None:
        o, hmix, c_new, n_new, m_new = _l0_prompt(x2d, p["g_mix0"], p["w_in"], p["w_gate"], p["gate_bias"],
                                                  n_batch, seq_len)
        n_new = jnp.swapaxes(n_new[:, :, :N_HEADS], 1, 2)
    else:
        c0, n0, m0 = state
        q, k, v, o, gates = _inproj(x2d, p["g_mix0"], p["w_in"], p["w_gate"], p["gate_bias"])
        hmix, c_new, n_new, m_new = _mlstm_core_sample(
            q, k, v, gates, c0, n0.reshape(n_batch, QK_DIM), _rep_lanes(m0), n_batch, seq_len)
    states = (c_new.reshape(1, n_batch, N_HEADS, DK, DV),
              n_new.reshape(1, n_batch, N_HEADS, DK),
              m_new.reshape(n_batch, LANES)[:, :N_HEADS].reshape(1, n_batch, N_HEADS))
    return (hmix, o, x2d), states


def _s5_stage(x2, shape, state, p):
    n_batch, seq_len, _ = shape
    if state is None:
        h0_re = jnp.zeros((n_batch, S5_N), f32)
        h0_im = h0_re
        bt, lt = n_batch, ROW_TILE // n_batch
    else:
        h0_re, h0_im = (s.reshape(n_batch, S5_N) for s in state)
        bt, lt = ROW_TILE // seq_len, seq_len
    y, s_re, s_im = _s5_layer(x2.reshape(shape), h0_re, h0_im, bt, lt,
                              p["g_mix1"], p["a_re"], p["a_im"], p["wb_re"], p["wb_im"], p["v_re"], p["v_im"],
                              p["d_skip"], p["w_glu"], p["g_ffn1"], p["w_up"], p["w_down"], p["g_final"])
    return y, (s_re.reshape(1, n_batch, S5_GROUPS, S5_STATE), s_im.reshape(1, n_batch, S5_GROUPS, S5_STATE))


def kernel(x_prompt, x_sample, state_mlstm_C, state_mlstm_n, state_mlstm_m, state_s5_re, state_s5_im,
           norm_mix_g, norm_ffn_g, norm_final_g, mlstm_w_in, mlstm_b_i, mlstm_b_f, mlstm_head_norm_g,
           mlstm_w_out, s5_lambda_re, s5_lambda_im, s5_log_dt, s5_b_re, s5_b_im, s5_c_re, s5_c_im,
           s5_d, s5_w_glu, ffn_w_up, ffn_w_down):
    w_in = mlstm_w_in[0]
    n_qkvo = W_QKVO_COLS
    a_re, a_im, wb_re, wb_im, v_re, v_im = _s5_prep(
        s5_lambda_re[0], s5_lambda_im[0], s5_log_dt[0].reshape(S5_GROUPS, 1),
        jnp.swapaxes(s5_b_re[0], 1, 2), jnp.swapaxes(s5_b_im[0], 1, 2), s5_c_re[0], s5_c_im[0])
    p = dict(
        g_mix0=norm_mix_g[0].reshape(1, D_MODEL), g_mix1=norm_mix_g[1].reshape(1, D_MODEL),
        g_ffn0=norm_ffn_g[0].reshape(1, D_MODEL), g_ffn1=norm_ffn_g[1].reshape(1, D_MODEL),
        g_final=norm_final_g.reshape(1, D_MODEL),
        w_in=jnp.swapaxes(w_in, 0, 1).astype(bf16),
        w_gate=jnp.concatenate([_rep_lanes(w_in[:, n_qkvo:n_qkvo + N_HEADS]),
                                _rep_lanes(w_in[:, n_qkvo + N_HEADS:])], axis=1).astype(bf16),
        gate_bias=jnp.concatenate([_rep_lanes(mlstm_b_i[0][None]), _rep_lanes(mlstm_b_f[0][None])], axis=1),
        head_g=mlstm_head_norm_g[0].reshape(1, V_DIM),
        w_out=mlstm_w_out[0].astype(bf16),
        w_up=ffn_w_up.astype(bf16), w_down=ffn_w_down.astype(bf16),
        a_re=a_re.reshape(1, S5_N), a_im=a_im.reshape(1, S5_N),
        wb_re=wb_re, wb_im=wb_im, v_re=v_re, v_im=v_im,
        d_skip=s5_d[0].reshape(1, D_MODEL),
        w_glu=s5_w_glu[0].astype(bf16),
    )
    mix_p, mlstm_p = _mlstm_mixer(x_prompt, None, p)
    mix_s, mlstm_s = _mlstm_mixer(x_sample, (state_mlstm_C[0], state_mlstm_n[0], state_mlstm_m[0]), p)
    x2_p, x2_s = (_outproj_ffn(*mix, p["head_g"], p["w_out"], p["g_ffn0"], p["w_up"], p["w_down"])
                  for mix in (mix_p, mix_s))
    y_p, s5_p = _s5_stage(x2_p, x_prompt.shape, None, p)
    y_s, s5_s = _s5_stage(x2_s, x_sample.shape, (state_s5_re[0], state_s5_im[0]), p)
    return (y_p, y_s) + mlstm_p + s5_p + mlstm_s + s5_s
```

```python
import functools
import types

import jax
import jax.numpy as jnp
import numpy as np
from jax import lax
from jax.experimental import pallas as pl
from jax.experimental.pallas import tpu as pltpu

f32 = jnp.float32
bf16 = jnp.bfloat16

D_MODEL = 1024
N_HEADS = 8
DK = 64
DV = 128
QK_DIM = N_HEADS * DK
V_DIM = N_HEADS * DV
W_QKVO_COLS = 2 * QK_DIM + 2 * V_DIM
D_FF = 4 * D_MODEL
S5_GROUPS = 64
S5_GROUP = 16
S5_STATE = 64
S5_N = S5_GROUPS * S5_STATE
EPS = 1e-6

LANES = 128
ROW_TILE = 512
CHUNK_ROWS = 256
PROMPT_CHUNKS = 2
SEQ_GROUP = 4
FF_CHUNK = 1024
ROW_SPLITS = 2
S5_BLOCK_CH = LANES
S5_BLOCK_ST = S5_BLOCK_CH // S5_GROUP * S5_STATE
N_S5_BLOCKS = D_MODEL // S5_BLOCK_CH
VMEM_LIMIT_BYTES = 56 * 1024 * 1024
NEG_BIG = -1e30
CLAMP_MAX = 2.0 ** 126
LOG_HEADS = N_HEADS.bit_length() - 1
N_BCAST = 3


def _params(n_axes):
    return pltpu.CompilerParams(dimension_semantics=("arbitrary",) * n_axes,
                                vmem_limit_bytes=VMEM_LIMIT_BYTES)


def _const_spec(shape):
    nd = len(shape)
    return pl.BlockSpec(shape, lambda *_: (0,) * nd, pipeline_mode=pl.Buffered(1))


def _layer_spec(shape, layer):
    nd = len(shape)
    return pl.BlockSpec((None,) + tuple(shape[1:]), lambda *_: (layer,) + (0,) * (nd - 1),
                        pipeline_mode=pl.Buffered(1))


def _dot(a, b):
    return jnp.dot(a, b, preferred_element_type=f32)


def _dot_nt(a, b):
    return lax.dot_general(a, b, (((1,), (1,)), ((), ())), preferred_element_type=f32)


def _rms(x, g):
    return x * lax.rsqrt(jnp.mean(x * x, axis=-1, keepdims=True) + EPS) * g


def _row_splits(rows):
    size = rows // ROW_SPLITS
    return [slice(i * size, (i + 1) * size) for i in range(ROW_SPLITS)]


def _ffn(x1s, g_ref, wup_ref, wdn_ref):
    xn = [_rms(x1, g_ref[...]).astype(bf16) for x1 in x1s]
    acc = [None] * len(x1s)
    for c in range(D_FF // FF_CHUNK):
        cols = slice(c * FF_CHUNK, (c + 1) * FF_CHUNK)
        for i in range(len(x1s)):
            hid = _dot(xn[i], wup_ref[:, cols])
            hid = jnp.square(jnp.maximum(hid, 0.0)).astype(bf16)
            part = _dot(hid, wdn_ref[cols, :])
            acc[i] = part if acc[i] is None else acc[i] + part
    return [x1 + a for x1, a in zip(x1s, acc)]


def _inproj_body(x_ref, g_ref, w_ref, wg_ref, bias_ref, q_ref, k_ref, v_ref, o_ref, gate_ref):
    xn = _rms(x_ref[...], g_ref[...]).astype(bf16)
    q, k, v, o, logi, logf = _inproj_values(xn, w_ref, wg_ref, bias_ref)
    q_ref[...] = q
    k_ref[...] = k
    v_ref[...] = v
    o_ref[...] = o
    gate_ref[:, :LANES] = logi
    gate_ref[:, LANES:] = logf


def _inproj(x2d, g, w_in, w_gate, gate_bias):
    t = x2d.shape[0]
    row = lambda n: pl.BlockSpec((ROW_TILE, n), lambda i: (i, 0))
    return pl.pallas_call(
        _inproj_body,
        grid=(t // ROW_TILE,),
        in_specs=[row(D_MODEL), _const_spec((1, D_MODEL)), _const_spec(w_in.shape),
                  _const_spec(w_gate.shape), _const_spec((1, 2 * LANES))],
        out_specs=[row(QK_DIM), row(QK_DIM), row(V_DIM), row(V_DIM), row(2 * LANES)],
        out_shape=[jax.ShapeDtypeStruct((t, QK_DIM), bf16), jax.ShapeDtypeStruct((t, QK_DIM), f32),
                   jax.ShapeDtypeStruct((t, V_DIM), bf16), jax.ShapeDtypeStruct((t, V_DIM), f32),
                   jax.ShapeDtypeStruct((t, 2 * LANES), f32)],
        compiler_params=_params(1),
        name="l0_inproj",
    )(x2d, g, w_in, w_gate, gate_bias)


def _row_prefix(x, cs, rpos, is_max):
    sh = 1
    while sh < cs:
        prev = pltpu.roll(x, sh, 0)
        if is_max:
            x = jnp.maximum(x, jnp.where(rpos >= sh, prev, NEG_BIG))
        else:
            x = x + jnp.where(rpos >= sh, prev, 0.0)
        sh *= 2
    return x


def _seg_last(x, nseq, cs):
    if nseq == 1:
        return x[cs - 1:cs, :]
    last = x.reshape(nseq, cs, x.shape[1])[:, cs - 1:cs, :]
    return jnp.broadcast_to(last, (nseq, cs, x.shape[1])).reshape(nseq * cs, x.shape[1])


def _gate_algebra(logi, logf, m_in, nseq, cs, sel_ref):
    rows = nseq * cs
    rpos = lax.broadcasted_iota(jnp.int32, (rows, LANES), 0) & (cs - 1)
    b = _row_prefix(logf, cs, rpos, False)
    a = logi - b
    g = jnp.maximum(m_in, _row_prefix(a, cs, rpos, True))
    g_last = _seg_last(g, nseq, cs)
    w_state = jnp.exp(a - g_last)

    group = lax.broadcasted_iota(jnp.int32, (rows, LANES), 1) >> LOG_HEADS
    pieces = jnp.zeros((rows, LANES), f32)
    clamp = jnp.minimum(jnp.exp(-(b + g)), CLAMP_MAX)
    for i, val in enumerate((g, jnp.exp(m_in - g), clamp)):
        hi = val.astype(bf16).astype(f32)
        mid = (val - hi).astype(bf16).astype(f32)
        lo = (val - hi - mid).astype(bf16).astype(f32)
        for j, piece in enumerate((hi, mid, lo)):
            pieces = jnp.where(group == 3 * i + j, piece, pieces)
    bcast = _dot(pieces.astype(bf16), sel_ref[...])
    tile = lambda i, h: bcast[:, (i * N_HEADS + h) * DV:(i * N_HEADS + h + 1) * DV]
    return types.SimpleNamespace(
        a_t=a.T,
        w_state=w_state, w_state_t=w_state.T,
        decay=jnp.exp(m_in - g_last),
        m_new=_seg_last(b, nseq, cs) + g_last,
        g=lambda h: tile(0, h), w_inter=lambda h: tile(1, h), clamp=lambda h: tile(2, h))


def _head_outputs(qk, gates, inter, qn, v_h, h_ref, nseq, cs):
    rows = nseq * cs
    log_cs = cs.bit_length() - 1
    heads = range(N_HEADS)
    chunks = range(len(qk))
    ri = lax.broadcasted_iota(jnp.int32, (rows, rows), 0)
    ci = lax.broadcasted_iota(jnp.int32, (rows, rows), 1)
    causal = ci <= ri
    if nseq > 1:
        causal = jnp.logical_and(causal, (ri >> log_cs) == (ci >> log_cs))
    for c in chunks:
        for h in heads:
            g_rows = jnp.concatenate([gates[c].g(h)] * (rows // DV), axis=1)
            w = jnp.exp(jnp.where(causal, gates[c].a_t[h:h + 1, :] - g_rows, NEG_BIG))
            s = qk[c][h]() * w
            s_sum = jnp.sum(s, axis=1, keepdims=True)
            intra = _dot(s.astype(bf16), v_h[c][h])
            w_inter = gates[c].w_inter(h)
            num = w_inter * inter[c][h] + intra
            den = jnp.maximum(jnp.abs(w_inter * qn[c][h] + s_sum), gates[c].clamp(h))
            h_ref[c * rows:(c + 1) * rows, h * DV:(h + 1) * DV] = num * lax.rsqrt(
                jnp.mean(num * num, axis=1, keepdims=True) + EPS * (den * den))


def _sequence_chunks(q_ref, k_ref, v_ref, gate_ref, sel_ref, h_ref, c_state, n_state, m_in, n_chunks,
                     after_gates=lambda: None):
    rows = CHUNK_ROWS
    heads = range(N_HEADS)
    chunks = range(n_chunks)
    c_state, n_state = list(c_state), list(n_state)
    rsl = lambda c: slice(c * rows, (c + 1) * rows)
    q_h = [[q_ref[rsl(c), h * DK:(h + 1) * DK] for h in heads] for c in chunks]
    k_h = [[k_ref[rsl(c), h * DK:(h + 1) * DK] for h in heads] for c in chunks]
    v_h = [[v_ref[rsl(c), h * DV:(h + 1) * DV] for h in heads] for c in chunks]

    qk = [[functools.partial(_dot_nt, q_h[c][h], k_h[c][h].astype(bf16)) for h in heads] for c in chunks]

    gates = []
    for c in chunks:
        gates.append(_gate_algebra(gate_ref[rsl(c), :LANES], gate_ref[rsl(c), LANES:],
                                   m_in if c == 0 else gates[c - 1].m_new, 1, rows, sel_ref))
    after_gates()

    inter = [[None] * N_HEADS for _ in chunks]
    qn = [[None] * N_HEADS for _ in chunks]
    for c in chunks:
        ga = gates[c]
        k_t = k_ref[rsl(c), :].T
        for h in heads:
            rhs = jnp.concatenate([c_state[h].astype(bf16),
                                   jnp.broadcast_to(n_state[h], (DK, DV)).astype(bf16)], axis=1)
            inter2 = _dot(q_h[c][h], rhs)
            inter[c][h] = inter2[:, :DV]
            qn[c][h] = inter2[:, DV:]
            wk_t = k_t[h * DK:(h + 1) * DK, :] * ga.w_state_t[h:h + 1, :]
            dec = ga.decay[:, h:h + 1]
            c_state[h] = dec * c_state[h] + _dot(wk_t.astype(bf16), v_h[c][h])
            n_state[h] = dec * n_state[h] + jnp.sum(wk_t, axis=1, keepdims=True)

    _head_outputs(qk, gates, inter, qn, v_h, h_ref, 1, rows)
    return c_state, n_state, gates[-1].m_new


def _mlstm_sample_body(q_ref, k_ref, v_ref, gate_ref, sel_ref, c0_ref, n0_ref, m0_ref,
                       h_ref, cout_ref, nout_ref, mout_ref, *, nseq, cs):
    rows = nseq * cs
    log_cs = cs.bit_length() - 1
    log_dk = DK.bit_length() - 1
    heads = range(N_HEADS)
    q_h = [q_ref[:, h * DK:(h + 1) * DK] for h in heads]
    k_h = [k_ref[:, h * DK:(h + 1) * DK] for h in heads]
    v_h = [v_ref[:, h * DV:(h + 1) * DV] for h in heads]
    qk = [functools.partial(_dot_nt, q_h[h], k_h[h].astype(bf16)) for h in heads]
    m_in = jnp.broadcast_to(m0_ref[...][:, None, :], (nseq, cs, LANES)).reshape(rows, LANES)
    ga = _gate_algebra(gate_ref[:, :LANES], gate_ref[:, LANES:], m_in, nseq, cs, sel_ref)

    grp_rows, grp_k = SEQ_GROUP * cs, SEQ_GROUP * DK
    bd_q = ((lax.broadcasted_iota(jnp.int32, (grp_rows, grp_k), 0) >> log_cs)
            == (lax.broadcasted_iota(jnp.int32, (grp_rows, grp_k), 1) >> log_dk))
    bd_k = ((lax.broadcasted_iota(jnp.int32, (nseq * DK, rows), 0) >> log_dk)
            == (lax.broadcasted_iota(jnp.int32, (nseq * DK, rows), 1) >> log_cs))
    k_t = k_ref[...].T
    inter, qn = [], []
    for h in heads:
        c_prev = c0_ref[:, h].reshape(nseq * DK, DV)
        c_bf = c_prev.astype(bf16)
        q32 = q_h[h].astype(f32)
        parts = []
        for j in range(nseq // SEQ_GROUP):
            q_j = q32[j * grp_rows:(j + 1) * grp_rows, :]
            q_bd = jnp.where(bd_q, jnp.concatenate([q_j] * SEQ_GROUP, axis=1), 0.0).astype(bf16)
            parts.append(_dot(q_bd, c_bf[j * grp_k:(j + 1) * grp_k, :]))
        inter.append(jnp.concatenate(parts, axis=0))
        n_prev = n0_ref[:, h * DK:(h + 1) * DK]
        n_rows = jnp.broadcast_to(n_prev[:, None, :], (nseq, cs, DK)).reshape(rows, DK)
        qn.append(jnp.sum(q32 * n_rows, axis=1, keepdims=True))

        wk_t = k_t[h * DK:(h + 1) * DK, :] * ga.w_state_t[h:h + 1, :]
        wk_bd = jnp.where(bd_k, jnp.broadcast_to(wk_t[None], (nseq, DK, rows)).reshape(nseq * DK, rows), 0.0)
        dc = _dot(wk_bd.astype(bf16), v_h[h])
        dec_col = ga.decay[:, h:h + 1]
        dec_rows = jnp.broadcast_to(dec_col.reshape(nseq, 1, cs, 1), (nseq, DK // cs, cs, 1)).reshape(nseq * DK, 1)
        cout_ref[:, h] = (dec_rows * c_prev + dc).reshape(nseq, DK, DV)
        dec_seq = jnp.max(dec_col.reshape(nseq, cs, 1), axis=1)
        nout_ref[:, h * DK:(h + 1) * DK] = dec_seq * n_prev + jnp.sum(
            (ga.w_state[:, h:h + 1] * k_h[h]).reshape(nseq, cs, DK), axis=1)
    mout_ref[...] = jnp.max(ga.m_new.reshape(nseq, cs, LANES), axis=1)
    _head_outputs([qk], [ga], [inter], [qn], [v_h], h_ref, nseq, cs)


def _inproj_values(xn, w_ref, wg_ref, bias_ref):
    q = _dot_nt(xn, w_ref[0:QK_DIM, :]).astype(bf16)
    k = _dot_nt(xn, w_ref[QK_DIM:2 * QK_DIM, :]) * (DK ** -0.5)
    v = _dot_nt(xn, w_ref[2 * QK_DIM:2 * QK_DIM + V_DIM, :]).astype(bf16)
    o = _dot_nt(xn, w_ref[2 * QK_DIM + V_DIM:W_QKVO_COLS, :])
    gz = _dot(xn, wg_ref[...]) + bias_ref[...]
    return q, k, v, o, gz[:, :LANES], jax.nn.log_sigmoid(gz[:, LANES:])


def _l0_prompt_body(x_ref, g_ref, w_ref, wg_ref, bias_ref, sel_ref,
                    o_ref, h_ref, cout_ref, nout_ref, mout_ref,
                    qa, ka, va, ga, qb, kb, vb, gb, c_scr, n_scr, m_scr, *, nt, n_chunks, last):
    step = pl.program_id(0)
    heads = range(N_HEADS)
    sets = ((qa, ka, va, ga), (qb, kb, vb, gb))

    def project(wr):
        q_w, k_w, v_w, g_w = wr
        xn = _rms(x_ref[...], g_ref[...]).astype(bf16)
        q, k, v, o, logi, logf = _inproj_values(xn, w_ref, wg_ref, bias_ref)
        q_w[...] = q
        k_w[...] = k
        v_w[...] = v
        o_ref[...] = o
        g_w[:, :LANES] = logi
        g_w[:, LANES:] = logf

    def core(rd, after_gates):
        first = lax.rem(step + nt - 1, nt) == 0
        c_state = [jnp.where(first, 0.0, c_scr[h]) for h in heads]
        n_state = [jnp.where(first, 0.0, n_scr[h]) for h in heads]
        m_in = jnp.where(first, 0.0, m_scr[...])
        c_state, n_state, m_new = _sequence_chunks(*rd, sel_ref, h_ref, c_state, n_state, m_in, n_chunks,
                                                   after_gates=after_gates)
        for h in heads:
            c_scr[h] = c_state[h]
            n_scr[h] = n_state[h]
        m_scr[...] = m_new

    @pl.when(step == 0)
    def _():
        for ref in (c_scr, n_scr, m_scr):
            ref[...] = jnp.zeros_like(ref)
        project(sets[0])

    for parity in range(2):
        wr, rd = sets[parity], sets[1 - parity]

        @pl.when(jnp.logical_and(jnp.logical_and(step > 0, step < last), lax.rem(step, 2) == parity))
        def _():
            core(rd, functools.partial(project, wr))

    @pl.when(step == last)
    def _():
        core(sets[1 - last % 2], lambda: None)

    @pl.when(jnp.logical_and(step >= 1, lax.rem(step, nt) == 0))
    def _():
        cout_ref[0] = c_scr[...]
        lane = lax.broadcasted_iota(jnp.int32, (DK, LANES), 1)
        n_mat = jnp.zeros((DK, LANES), f32)
        for h in heads:
            n_mat = jnp.where(lane == h, jnp.broadcast_to(n_scr[h], (DK, LANES)), n_mat)
        nout_ref[0] = n_mat
        mout_ref[0] = m_scr[...]


def _bcast_selector():
    src = np.arange(LANES)
    dst = np.arange(N_BCAST * N_HEADS * DV) // DV
    hit = (((src[:, None] >> LOG_HEADS) // 3 == dst[None, :] // N_HEADS)
           & ((src[:, None] & (N_HEADS - 1)) == dst[None, :] % N_HEADS))
    return jnp.asarray(hit, dtype=bf16)


def _l0_prompt(x2d, g, w_in, w_gate, gate_bias, n_batch, seq_len):
    t = x2d.shape[0]
    sel = _bcast_selector()
    rows = PROMPT_CHUNKS * CHUNK_ROWS
    nt = seq_len // rows
    n_blocks = t // rows
    cur = lambda s: (jnp.minimum(s, n_blocks - 1), 0)
    prev = lambda s: (jnp.maximum(s - 1, 0), 0)
    seq = lambda s: jnp.maximum(s - 1, 0) // nt
    bufs = [pltpu.VMEM((rows, QK_DIM), bf16), pltpu.VMEM((rows, QK_DIM), f32),
            pltpu.VMEM((rows, V_DIM), bf16), pltpu.VMEM((rows, 2 * LANES), f32)]
    return pl.pallas_call(
        functools.partial(_l0_prompt_body, nt=nt, n_chunks=PROMPT_CHUNKS, last=n_blocks),
        grid=(n_blocks + 1,),
        in_specs=[pl.BlockSpec((rows, D_MODEL), cur), _const_spec((1, D_MODEL)), _const_spec(w_in.shape),
                  _const_spec(w_gate.shape), _const_spec((1, 2 * LANES)), _const_spec(sel.shape)],
        out_specs=[pl.BlockSpec((rows, V_DIM), cur), pl.BlockSpec((rows, V_DIM), prev),
                   pl.BlockSpec((1, N_HEADS, DK, DV), lambda s: (seq(s), 0, 0, 0)),
                   pl.BlockSpec((1, DK, LANES), lambda s: (seq(s), 0, 0)),
                   pl.BlockSpec((1, 1, LANES), lambda s: (seq(s), 0, 0))],
        out_shape=[jax.ShapeDtypeStruct((t, V_DIM), f32), jax.ShapeDtypeStruct((t, V_DIM), f32),
                   jax.ShapeDtypeStruct((n_batch, N_HEADS, DK, DV), f32),
                   jax.ShapeDtypeStruct((n_batch, DK, LANES), f32),
                   jax.ShapeDtypeStruct((n_batch, 1, LANES), f32)],
        scratch_shapes=bufs + bufs + [pltpu.VMEM((N_HEADS, DK, DV), f32), pltpu.VMEM((N_HEADS, DK, 1), f32),
                                      pltpu.VMEM((1, LANES), f32)],
        compiler_params=_params(1),
        name="l0_inproj_mlstm",
    )(x2d, g, w_in, w_gate, gate_bias, sel)


def _mlstm_core_sample(q, k, v, gates, c0, n0, m0, n_batch, seq_len):
    t = q.shape[0]
    sel = _bcast_selector()
    nseq = CHUNK_ROWS // seq_len
    row = lambda n: pl.BlockSpec((CHUNK_ROWS, n), lambda i: (i, 0))
    cspec = pl.BlockSpec((nseq, N_HEADS, DK, DV), lambda i: (i, 0, 0, 0))
    nspec = pl.BlockSpec((nseq, QK_DIM), lambda i: (i, 0))
    mspec = pl.BlockSpec((nseq, LANES), lambda i: (i, 0))
    return pl.pallas_call(
        functools.partial(_mlstm_sample_body, nseq=nseq, cs=seq_len),
        grid=(t // CHUNK_ROWS,),
        in_specs=[row(QK_DIM), row(QK_DIM), row(V_DIM), row(2 * LANES), _const_spec(sel.shape), cspec, nspec, mspec],
        out_specs=[row(V_DIM), cspec, nspec, mspec],
        out_shape=[jax.ShapeDtypeStruct((t, V_DIM), f32),
                   jax.ShapeDtypeStruct((n_batch, N_HEADS, DK, DV), f32),
                   jax.ShapeDtypeStruct((n_batch, QK_DIM), f32),
                   jax.ShapeDtypeStruct((n_batch, LANES), f32)],
        compiler_params=_params(1),
        name="l0_mlstm_core",
    )(q, k, v, gates, sel, c0, n0, m0)


def _outproj_ffn_body(h_ref, o_ref, x_ref, hg_ref, wout_ref, gffn_ref, wup_ref, wdn_ref, y_ref):
    x1s = []
    for r in _row_splits(h_ref.shape[0]):
        hn = h_ref[r, :] * hg_ref[...] * jax.nn.sigmoid(o_ref[r, :])
        x1s.append(x_ref[r, :] + _dot(hn.astype(bf16), wout_ref[...]))
    for r, y in zip(_row_splits(h_ref.shape[0]), _ffn(x1s, gffn_ref, wup_ref, wdn_ref)):
        y_ref[r, :] = y


def _outproj_ffn(hmix, o, x2d, head_g, w_out, g_ffn, w_up, w_down):
    t = x2d.shape[0]
    row = pl.BlockSpec((ROW_TILE, D_MODEL), lambda i: (i, 0))
    return pl.pallas_call(
        _outproj_ffn_body,
        grid=(t // ROW_TILE,),
        in_specs=[row, row, row, _const_spec((1, V_DIM)), _const_spec(w_out.shape),
                  _const_spec((1, D_MODEL)), _layer_spec(w_up.shape, 0), _layer_spec(w_down.shape, 0)],
        out_specs=row,
        out_shape=jax.ShapeDtypeStruct((t, D_MODEL), f32),
        compiler_params=_params(1),
        name="l0_outproj_ffn",
    )(hmix, o, x2d, head_g, w_out, g_ffn, w_up, w_down)


def _s5_prep_body(lre_ref, lim_ref, ldt_ref, bre_ref, bim_ref, cre_ref, cim_ref,
                  are_ref, aim_ref, wbr_ref, wbi_ref, vre_ref, vim_ref):
    lr = lre_ref[...]
    li = lim_ref[...]
    dt = jnp.exp(ldt_ref[...])
    mag = jnp.exp(lr * dt)
    a_re = mag * jnp.cos(li * dt)
    a_im = mag * jnp.sin(li * dt)
    den = lr * lr + li * li
    z_re = a_re - 1.0
    coef_re = ((z_re * lr + a_im * li) / den)[:, None, :]
    coef_im = ((a_im * lr - z_re * li) / den)[:, None, :]
    br = bre_ref[...]
    bi = bim_ref[...]
    are_ref[...] = a_re
    aim_ref[...] = a_im
    bb = (coef_re * br - coef_im * bi, coef_re * bi + coef_im * br)
    c_t = (cre_ref[...].reshape(S5_GROUPS * S5_GROUP, S5_STATE).T,
           cim_ref[...].reshape(S5_GROUPS * S5_GROUP, S5_STATE).T)

    groups = S5_BLOCK_CH // S5_GROUP
    log_ch, log_st = S5_GROUP.bit_length() - 1, S5_STATE.bit_length() - 1
    shape_b = (S5_BLOCK_CH, S5_BLOCK_ST)
    diag_b = ((lax.broadcasted_iota(jnp.int32, shape_b, 0) >> log_ch)
              == (lax.broadcasted_iota(jnp.int32, shape_b, 1) >> log_st))
    shape_c = (S5_BLOCK_ST, S5_BLOCK_CH)
    diag_c = ((lax.broadcasted_iota(jnp.int32, shape_c, 0) >> log_st)
              == (lax.broadcasted_iota(jnp.int32, shape_c, 1) >> log_ch))
    for j in range(N_S5_BLOCKS):
        for src, dst in zip(bb, (wbr_ref, wbi_ref)):
            blk = src[j * groups:(j + 1) * groups].reshape(S5_BLOCK_CH, S5_STATE)
            dst[j] = jnp.where(diag_b, jnp.concatenate([blk] * groups, axis=1), 0.0).astype(bf16)
        for src, dst in zip(c_t, (vre_ref, vim_ref)):
            blk = src[:, j * S5_BLOCK_CH:(j + 1) * S5_BLOCK_CH]
            rep = jnp.broadcast_to(blk[None], (groups, S5_STATE, S5_BLOCK_CH)).reshape(S5_BLOCK_ST, S5_BLOCK_CH)
            dst[j] = jnp.where(diag_c, rep, 0.0).astype(bf16)


def _s5_prep(lam_re, lam_im, log_dt, b_re_t, b_im_t, c_re, c_im):
    gp = jax.ShapeDtypeStruct((S5_GROUPS, S5_STATE), f32)
    wb = jax.ShapeDtypeStruct((N_S5_BLOCKS, S5_BLOCK_CH, S5_BLOCK_ST), bf16)
    vc = jax.ShapeDtypeStruct((N_S5_BLOCKS, S5_BLOCK_ST, S5_BLOCK_CH), bf16)
    return pl.pallas_call(_s5_prep_body, out_shape=[gp, gp, wb, wb, vc, vc], name="l1_s5_prep")(
        lam_re, lam_im, log_dt, b_re_t, b_im_t, c_re, c_im)


def _s5_body(x_ref, h0r_ref, h0i_ref, g_ref, are_ref, aim_ref, wbr_ref, wbi_ref, vre_ref, vim_ref,
             d_ref, wglu_ref, gffn_ref, wup_ref, wdn_ref, gfin_ref,
             y_ref, sre_ref, sim_ref, sr_scr, si_scr, hr_scr, hi_scr, act_scr, *io_scr, bt, lt, dma_io):
    rows = bt * lt
    step = pl.program_id(1)
    n_steps = pl.num_programs(1)

    @pl.when(step == 0)
    def _():
        sr_scr[...] = h0r_ref[...]
        si_scr[...] = h0i_ref[...]

    if dma_io:
        xbuf, ybuf, in_sem, out_sem = io_scr
        slot = lax.rem(step, 2)

        def in_copy(s, sl, b):
            return pltpu.make_async_copy(x_ref.at[b, pl.ds(s * lt, lt), :], xbuf.at[sl, :, b, :], in_sem.at[sl, b])

        def out_copy(s, sl, b):
            return pltpu.make_async_copy(ybuf.at[sl, :, b, :], y_ref.at[b, pl.ds(s * lt, lt), :], out_sem.at[sl, b])

        @pl.when(step == 0)
        def _():
            for b in range(bt):
                in_copy(0, 0, b).start()

        @pl.when(step + 1 < n_steps)
        def _():
            for b in range(bt):
                in_copy(step + 1, 1 - slot, b).start()

        for b in range(bt):
            in_copy(step, slot, b).wait()
        xt = xbuf[slot].reshape(rows, D_MODEL)
    else:
        xt = jnp.concatenate([x_ref[:, t, :] for t in range(lt)], axis=0)
    u = _rms(xt, g_ref[...])
    ub = u.astype(bf16)

    nbuf = hr_scr.shape[0]

    def project_in(j):
        ch = slice(j * S5_BLOCK_CH, (j + 1) * S5_BLOCK_CH)
        hr_scr[j % nbuf] = _dot(ub[:, ch], wbr_ref[j])
        hi_scr[j % nbuf] = _dot(ub[:, ch], wbi_ref[j])

    def recur(j):
        st = slice(j * S5_BLOCK_ST, (j + 1) * S5_BLOCK_ST)
        a_re = jnp.broadcast_to(are_ref[:, st], (bt, S5_BLOCK_ST))
        a_im = jnp.broadcast_to(aim_ref[:, st], (bt, S5_BLOCK_ST))
        s_re = sr_scr[:, st]
        s_im = si_scr[:, st]
        for t in range(lt):
            r = slice(t * bt, (t + 1) * bt)
            n_re = a_re * s_re - a_im * s_im + hr_scr[j % nbuf, r, :]
            n_im = a_re * s_im + a_im * s_re + hi_scr[j % nbuf, r, :]
            hr_scr[j % nbuf, r, :] = n_re
            hi_scr[j % nbuf, r, :] = n_im
            s_re, s_im = n_re, n_im
        sr_scr[:, st] = s_re
        si_scr[:, st] = s_im

    def project_out(j):
        ch = slice(j * S5_BLOCK_CH, (j + 1) * S5_BLOCK_CH)
        yj = (_dot(hr_scr[j % nbuf].astype(bf16), vre_ref[j]) - _dot(hi_scr[j % nbuf].astype(bf16), vim_ref[j])
              + d_ref[:, ch] * u[:, ch])
        act_scr[:, ch] = jax.nn.gelu(yj).astype(bf16)

    project_in(0)
    for j in range(N_S5_BLOCKS + 1):
        if j + 1 < N_S5_BLOCKS:
            project_in(j + 1)
        if j >= 1:
            project_out(j - 1)
        if j < N_S5_BLOCKS:
            recur(j)

    sre_ref[...] = sr_scr[...]
    sim_ref[...] = si_scr[...]
    x3s = []
    for r in _row_splits(rows):
        ag = _dot(act_scr[r, :], wglu_ref[...])
        x3s.append(xt[r, :] + ag[:, :D_MODEL] * jax.nn.sigmoid(ag[:, D_MODEL:]))
    y = jnp.concatenate([_rms(x4, gfin_ref[...]) for x4 in _ffn(x3s, gffn_ref, wup_ref, wdn_ref)], axis=0)
    if dma_io:
        @pl.when(step >= 2)
        def _():
            for b in range(bt):
                out_copy(step - 2, slot, b).wait()

        ybuf[slot] = y.reshape(lt, bt, D_MODEL)
        for b in range(bt):
            out_copy(step, slot, b).start()

        @pl.when(step == n_steps - 1)
        def _():
            @pl.when(step >= 1)
            def _():
                for b in range(bt):
                    out_copy(step - 1, 1 - slot, b).wait()

            for b in range(bt):
                out_copy(step, slot, b).wait()
    else:
        for t in range(lt):
            y_ref[:, t, :] = y[t * bt:(t + 1) * bt, :]


def _s5_layer(x3d, h0_re, h0_im, bt, lt, g_mix, a_re, a_im, wb_re, wb_im, v_re, v_im, d_skip,
              w_glu, g_ffn, w_up, w_down, g_final):
    n_batch, seq_len, _ = x3d.shape
    grid = (n_batch // bt, seq_len // lt)
    sspec = pl.BlockSpec((bt, S5_N), lambda i, t: (i, 0))
    dma_io = n_batch == bt
    if dma_io:
        xspec = pl.BlockSpec(memory_space=pl.ANY)
        io_scratch = [pltpu.VMEM((2, lt, bt, D_MODEL), f32), pltpu.VMEM((2, lt, bt, D_MODEL), f32),
                      pltpu.SemaphoreType.DMA((2, bt)), pltpu.SemaphoreType.DMA((2, bt))]
    else:
        xspec = pl.BlockSpec((bt, lt, D_MODEL), lambda i, t: (i, t, 0))
        io_scratch = []
    consts = (g_mix, a_re, a_im, wb_re, wb_im, v_re, v_im, d_skip, w_glu, g_ffn, w_up, w_down, g_final)
    const_specs = [_layer_spec(c.shape, 1) if c is w_up or c is w_down else _const_spec(c.shape) for c in consts]
    rows = bt * lt
    return pl.pallas_call(
        functools.partial(_s5_body, bt=bt, lt=lt, dma_io=dma_io),
        grid=grid,
        in_specs=[xspec, sspec, sspec] + const_specs,
        out_specs=[xspec, sspec, sspec],
        out_shape=[jax.ShapeDtypeStruct(x3d.shape, f32), jax.ShapeDtypeStruct((n_batch, S5_N), f32),
                   jax.ShapeDtypeStruct((n_batch, S5_N), f32)],
        scratch_shapes=[pltpu.VMEM((bt, S5_N), f32), pltpu.VMEM((bt, S5_N), f32),
                        pltpu.VMEM((3, rows, S5_BLOCK_ST), f32), pltpu.VMEM((3, rows, S5_BLOCK_ST), f32),
                        pltpu.VMEM((rows, D_MODEL), bf16)] + io_scratch,
        compiler_params=_params(2),
        name="l1_s5_ffn",
    )(x3d, h0_re, h0_im, *consts)


def _rep_lanes(x):
    return jnp.tile(x, (1,) * (x.ndim - 1) + (LANES // x.shape[-1],))


def _mlstm_mixer(x, state, p):
    n_batch, seq_len, _ = x.shape
    x2d = x.reshape(n_batch * seq_len, D_MODEL)
    if state is None:
        o, hmix, c_new, n_new, m_new = _l0_prompt(x2d, p["g_mix0"], p["w_in"], p["w_gate"], p["gate_bias"],
                                                  n_batch, seq_len)
        n_new = jnp.swapaxes(n_new[:, :, :N_HEADS], 1, 2)
    else:
        c0, n0, m0 = state
        q, k, v, o, gates = _inproj(x2d, p["g_mix0"], p["w_in"], p["w_gate"], p["gate_bias"])
        hmix, c_new, n_new, m_new = _mlstm_core_sample(
            q, k, v, gates, c0, n0.reshape(n_batch, QK_DIM), _rep_lanes(m0), n_batch, seq_len)
    states = (c_new.reshape(1, n_batch, N_HEADS, DK, DV),
              n_new.reshape(1, n_batch, N_HEADS, DK),
              m_new.reshape(n_batch, LANES)[:, :N_HEADS].reshape(1, n_batch, N_HEADS))
    return (hmix, o, x2d), states


def _s5_stage(x2, shape, state, p):
    n_batch, seq_len, _ = shape
    if state is None:
        h0_re = jnp.zeros((n_batch, S5_N), f32)
        h0_im = h0_re
        bt, lt = n_batch, ROW_TILE // n_batch
    else:
        h0_re, h0_im = (s.reshape(n_batch, S5_N) for s in state)
        bt, lt = ROW_TILE // seq_len, seq_len
    y, s_re, s_im = _s5_layer(x2.reshape(shape), h0_re, h0_im, bt, lt,
                              p["g_mix1"], p["a_re"], p["a_im"], p["wb_re"], p["wb_im"], p["v_re"], p["v_im"],
                              p["d_skip"], p["w_glu"], p["g_ffn1"], p["w_up"], p["w_down"], p["g_final"])
    return y, (s_re.reshape(1, n_batch, S5_GROUPS, S5_STATE), s_im.reshape(1, n_batch, S5_GROUPS, S5_STATE))


def kernel(x_prompt, x_sample, state_mlstm_C, state_mlstm_n, state_mlstm_m, state_s5_re, state_s5_im,
           norm_mix_g, norm_ffn_g, norm_final_g, mlstm_w_in, mlstm_b_i, mlstm_b_f, mlstm_head_norm_g,
           mlstm_w_out, s5_lambda_re, s5_lambda_im, s5_log_dt, s5_b_re, s5_b_im, s5_c_re, s5_c_im,
           s5_d, s5_w_glu, ffn_w_up, ffn_w_down):
    w_in = mlstm_w_in[0]
    n_qkvo = W_QKVO_COLS
    a_re, a_im, wb_re, wb_im, v_re, v_im = _s5_prep(
        s5_lambda_re[0], s5_lambda_im[0], s5_log_dt[0].reshape(S5_GROUPS, 1),
        jnp.swapaxes(s5_b_re[0], 1, 2), jnp.swapaxes(s5_b_im[0], 1, 2), s5_c_re[0], s5_c_im[0])
    p = dict(
        g_mix0=norm_mix_g[0].reshape(1, D_MODEL), g_mix1=norm_mix_g[1].reshape(1, D_MODEL),
        g_ffn0=norm_ffn_g[0].reshape(1, D_MODEL), g_ffn1=norm_ffn_g[1].reshape(1, D_MODEL),
        g_final=norm_final_g.reshape(1, D_MODEL),
        w_in=jnp.swapaxes(w_in, 0, 1).astype(bf16),
        w_gate=jnp.concatenate([_rep_lanes(w_in[:, n_qkvo:n_qkvo + N_HEADS]),
                                _rep_lanes(w_in[:, n_qkvo + N_HEADS:])], axis=1).astype(bf16),
        gate_bias=jnp.concatenate([_rep_lanes(mlstm_b_i[0][None]), _rep_lanes(mlstm_b_f[0][None])], axis=1),
        head_g=mlstm_head_norm_g[0].reshape(1, V_DIM),
        w_out=mlstm_w_out[0].astype(bf16),
        w_up=ffn_w_up.astype(bf16), w_down=ffn_w_down.astype(bf16),
        a_re=a_re.reshape(1, S5_N), a_im=a_im.reshape(1, S5_N),
        wb_re=wb_re, wb_im=wb_im, v_re=v_re, v_im=v_im,
        d_skip=s5_d[0].reshape(1, D_MODEL),
        w_glu=s5_w_glu[0].astype(bf16),
    )
    mix_p, mlstm_p = _mlstm_mixer(x_prompt, None, p)
    mix_s, mlstm_s = _mlstm_mixer(x_sample, (state_mlstm_C[0], state_mlstm_n[0], state_mlstm_m[0]), p)
    x2_p, x2_s = (_outproj_ffn(*mix, p["head_g"], p["w_out"], p["g_ffn0"], p["w_up"], p["w_down"])
                  for mix in (mix_p, mix_s))
    y_p, s5_p = _s5_stage(x2_p, x_prompt.shape, None, p)
    y_s, s5_s = _s5_stage(x2_s, x_sample.shape, (state_s5_re[0], state_s5_im[0]), p)
    return (y_p, y_s) + mlstm_p + s5_p + mlstm_s + s5_s
```

```python
import functools
import types

import jax
import jax.numpy as jnp
import numpy as np
from jax import lax
from jax.experimental import pallas as pl
from jax.experimental.pallas import tpu as pltpu

f32 = jnp.float32
bf16 = jnp.bfloat16

D_MODEL = 1024
N_HEADS = 8
DK = 64
DV = 128
QK_DIM = N_HEADS * DK
V_DIM = N_HEADS * DV
W_QKVO_COLS = 2 * QK_DIM + 2 * V_DIM
D_FF = 4 * D_MODEL
S5_GROUPS = 64
S5_GROUP = 16
S5_STATE = 64
S5_N = S5_GROUPS * S5_STATE
EPS = 1e-6

LANES = 128
ROW_TILE = 512
CHUNK_ROWS = 256
PROMPT_CHUNKS = 2
SEQ_GROUP = 4
FF_CHUNK = 1024
ROW_SPLITS = 2
S5_BLOCK_CH = LANES
S5_BLOCK_ST = S5_BLOCK_CH // S5_GROUP * S5_STATE
N_S5_BLOCKS = D_MODEL // S5_BLOCK_CH
VMEM_LIMIT_BYTES = 56 * 1024 * 1024
NEG_BIG = -1e30
CLAMP_MAX = 2.0 ** 126
LOG_HEADS = N_HEADS.bit_length() - 1
N_BCAST = 3


def _params(n_axes):
    return pltpu.CompilerParams(dimension_semantics=("arbitrary",) * n_axes,
                                vmem_limit_bytes=VMEM_LIMIT_BYTES)


def _const_spec(shape):
    nd = len(shape)
    return pl.BlockSpec(shape, lambda *_: (0,) * nd, pipeline_mode=pl.Buffered(1))


def _layer_spec(shape, layer):
    nd = len(shape)
    return pl.BlockSpec((None,) + tuple(shape[1:]), lambda *_: (layer,) + (0,) * (nd - 1),
                        pipeline_mode=pl.Buffered(1))


def _dot(a, b):
    return jnp.dot(a, b, preferred_element_type=f32)


def _dot_nt(a, b):
    return lax.dot_general(a, b, (((1,), (1,)), ((), ())), preferred_element_type=f32)


def _rms(x, g):
    return x * lax.rsqrt(jnp.mean(x * x, axis=-1, keepdims=True) + EPS) * g


def _row_splits(rows):
    size = rows // ROW_SPLITS
    return [slice(i * size, (i + 1) * size) for i in range(ROW_SPLITS)]


def _ffn_weight_copies(wup_hbm, wdn_hbm, layer, wup_ref, wdn_ref, sem, first_sem):
    copies = []
    for c in range(D_FF // FF_CHUNK):
        cols = pl.ds(c * FF_CHUNK, FF_CHUNK)
        copies.append((pltpu.make_async_copy(wup_hbm.at[layer, :, cols], wup_ref.at[:, cols],
                                             sem.at[first_sem + 2 * c]),
                       pltpu.make_async_copy(wdn_hbm.at[layer, cols, :], wdn_ref.at[cols, :],
                                             sem.at[first_sem + 2 * c + 1])))
    return copies


def _first_step_and_rest(body, n_axes):
    later = pl.program_id(0) > 0
    for axis in range(1, n_axes):
        later = jnp.logical_or(later, pl.program_id(axis) > 0)
    pl.when(jnp.logical_not(later))(functools.partial(body, True))
    pl.when(later)(functools.partial(body, False))


def _ffn(x1s, g_ref, wup_ref, wdn_ref, chunk_copies=None):
    xn = [_rms(x1, g_ref[...]).astype(bf16) for x1 in x1s]
    acc = [None] * len(x1s)
    for c in range(D_FF // FF_CHUNK):
        cols = slice(c * FF_CHUNK, (c + 1) * FF_CHUNK)
        if chunk_copies is not None:
            for cp in chunk_copies[c]:
                cp.wait()
        for i in range(len(x1s)):
            hid = _dot(xn[i], wup_ref[:, cols])
            hid = jnp.square(jnp.maximum(hid, 0.0)).astype(bf16)
            part = _dot(hid, wdn_ref[cols, :])
            acc[i] = part if acc[i] is None else acc[i] + part
    return [x1 + a for x1, a in zip(x1s, acc)]


def _inproj_body(x_ref, g_ref, w_ref, wg_ref, bias_ref, q_ref, k_ref, v_ref, o_ref, gate_ref):
    xn = _rms(x_ref[...], g_ref[...]).astype(bf16)
    q, k, v, o, logi, logf = _inproj_values(xn, w_ref, wg_ref, bias_ref)
    q_ref[...] = q
    k_ref[...] = k
    v_ref[...] = v
    o_ref[...] = o
    gate_ref[:, :LANES] = logi
    gate_ref[:, LANES:] = logf


def _inproj(x2d, g, w_in, w_gate, gate_bias):
    t = x2d.shape[0]
    row = lambda n: pl.BlockSpec((ROW_TILE, n), lambda i: (i, 0))
    return pl.pallas_call(
        _inproj_body,
        grid=(t // ROW_TILE,),
        in_specs=[row(D_MODEL), _const_spec((1, D_MODEL)), _const_spec(w_in.shape),
                  _const_spec(w_gate.shape), _const_spec((1, 2 * LANES))],
        out_specs=[row(QK_DIM), row(QK_DIM), row(V_DIM), row(V_DIM), row(2 * LANES)],
        out_shape=[jax.ShapeDtypeStruct((t, QK_DIM), bf16), jax.ShapeDtypeStruct((t, QK_DIM), f32),
                   jax.ShapeDtypeStruct((t, V_DIM), bf16), jax.ShapeDtypeStruct((t, V_DIM), f32),
                   jax.ShapeDtypeStruct((t, 2 * LANES), f32)],
        compiler_params=_params(1),
        name="l0_inproj",
    )(x2d, g, w_in, w_gate, gate_bias)


def _row_prefix(x, cs, rpos, is_max):
    sh = 1
    while sh < cs:
        prev = pltpu.roll(x, sh, 0)
        if is_max:
            x = jnp.maximum(x, jnp.where(rpos >= sh, prev, NEG_BIG))
        else:
            x = x + jnp.where(rpos >= sh, prev, 0.0)
        sh *= 2
    return x


def _seg_last(x, nseq, cs):
    if nseq == 1:
        return x[cs - 1:cs, :]
    last = x.reshape(nseq, cs, x.shape[1])[:, cs - 1:cs, :]
    return jnp.broadcast_to(last, (nseq, cs, x.shape[1])).reshape(nseq * cs, x.shape[1])


def _gate_algebra(logi, logf, m_in, nseq, cs, sel_ref):
    rows = nseq * cs
    rpos = lax.broadcasted_iota(jnp.int32, (rows, LANES), 0) & (cs - 1)
    b = _row_prefix(logf, cs, rpos, False)
    a = logi - b
    g = jnp.maximum(m_in, _row_prefix(a, cs, rpos, True))
    g_last = _seg_last(g, nseq, cs)
    w_state = jnp.exp(a - g_last)

    group = lax.broadcasted_iota(jnp.int32, (rows, LANES), 1) >> LOG_HEADS
    pieces = jnp.zeros((rows, LANES), f32)
    clamp = jnp.minimum(jnp.exp(-(b + g)), CLAMP_MAX)
    for i, val in enumerate((g, jnp.exp(m_in - g), clamp)):
        hi = val.astype(bf16).astype(f32)
        mid = (val - hi).astype(bf16).astype(f32)
        lo = (val - hi - mid).astype(bf16).astype(f32)
        for j, piece in enumerate((hi, mid, lo)):
            pieces = jnp.where(group == 3 * i + j, piece, pieces)
    bcast = _dot(pieces.astype(bf16), sel_ref[...])
    tile = lambda i, h: bcast[:, (i * N_HEADS + h) * DV:(i * N_HEADS + h + 1) * DV]
    return types.SimpleNamespace(
        a_t=a.T,
        w_state=w_state, w_state_t=w_state.T,
        decay=jnp.exp(m_in - g_last),
        m_new=_seg_last(b, nseq, cs) + g_last,
        g=lambda h: tile(0, h), w_inter=lambda h: tile(1, h), clamp=lambda h: tile(2, h))


def _head_outputs(qk, gates, inter, qn, v_h, h_ref, nseq, cs):
    rows = nseq * cs
    log_cs = cs.bit_length() - 1
    heads = range(N_HEADS)
    chunks = range(len(qk))
    ri = lax.broadcasted_iota(jnp.int32, (rows, rows), 0)
    ci = lax.broadcasted_iota(jnp.int32, (rows, rows), 1)
    causal = ci <= ri
    if nseq > 1:
        causal = jnp.logical_and(causal, (ri >> log_cs) == (ci >> log_cs))
    for c in chunks:
        for h in heads:
            g_rows = jnp.concatenate([gates[c].g(h)] * (rows // DV), axis=1)
            w = jnp.exp(jnp.where(causal, gates[c].a_t[h:h + 1, :] - g_rows, NEG_BIG))
            s = qk[c][h]() * w
            s_sum = jnp.sum(s, axis=1, keepdims=True)
            intra = _dot(s.astype(bf16), v_h[c][h])
            w_inter = gates[c].w_inter(h)
            num = w_inter * inter[c][h] + intra
            den = jnp.maximum(jnp.abs(w_inter * qn[c][h] + s_sum), gates[c].clamp(h))
            h_ref[c * rows:(c + 1) * rows, h * DV:(h + 1) * DV] = num * lax.rsqrt(
                jnp.mean(num * num, axis=1, keepdims=True) + EPS * (den * den))


def _sequence_chunks(q_ref, k_ref, v_ref, gate_ref, sel_ref, h_ref, c_state, n_state, m_in, n_chunks,
                     after_gates=lambda: None):
    rows = CHUNK_ROWS
    heads = range(N_HEADS)
    chunks = range(n_chunks)
    c_state, n_state = list(c_state), list(n_state)
    rsl = lambda c: slice(c * rows, (c + 1) * rows)
    q_h = [[q_ref[rsl(c), h * DK:(h + 1) * DK] for h in heads] for c in chunks]
    k_h = [[k_ref[rsl(c), h * DK:(h + 1) * DK] for h in heads] for c in chunks]
    v_h = [[v_ref[rsl(c), h * DV:(h + 1) * DV] for h in heads] for c in chunks]

    qk = [[functools.partial(_dot_nt, q_h[c][h], k_h[c][h].astype(bf16)) for h in heads] for c in chunks]

    gates = []
    for c in chunks:
        gates.append(_gate_algebra(gate_ref[rsl(c), :LANES], gate_ref[rsl(c), LANES:],
                                   m_in if c == 0 else gates[c - 1].m_new, 1, rows, sel_ref))
    after_gates()

    inter = [[None] * N_HEADS for _ in chunks]
    qn = [[None] * N_HEADS for _ in chunks]
    for c in chunks:
        ga = gates[c]
        k_t = k_ref[rsl(c), :].T
        for h in heads:
            rhs = jnp.concatenate([c_state[h].astype(bf16),
                                   jnp.broadcast_to(n_state[h], (DK, DV)).astype(bf16)], axis=1)
            inter2 = _dot(q_h[c][h], rhs)
            inter[c][h] = inter2[:, :DV]
            qn[c][h] = inter2[:, DV:]
            wk_t = k_t[h * DK:(h + 1) * DK, :] * ga.w_state_t[h:h + 1, :]
            dec = ga.decay[:, h:h + 1]
            c_state[h] = dec * c_state[h] + _dot(wk_t.astype(bf16), v_h[c][h])
            n_state[h] = dec * n_state[h] + jnp.sum(wk_t, axis=1, keepdims=True)

    _head_outputs(qk, gates, inter, qn, v_h, h_ref, 1, rows)
    return c_state, n_state, gates[-1].m_new


def _mlstm_sample_body(q_ref, k_ref, v_ref, gate_ref, sel_ref, c0_ref, n0_ref, m0_ref,
                       h_ref, cout_ref, nout_ref, mout_ref, *, nseq, cs):
    rows = nseq * cs
    log_cs = cs.bit_length() - 1
    log_dk = DK.bit_length() - 1
    heads = range(N_HEADS)
    q_h = [q_ref[:, h * DK:(h + 1) * DK] for h in heads]
    k_h = [k_ref[:, h * DK:(h + 1) * DK] for h in heads]
    v_h = [v_ref[:, h * DV:(h + 1) * DV] for h in heads]
    qk = [functools.partial(_dot_nt, q_h[h], k_h[h].astype(bf16)) for h in heads]
    m_in = jnp.broadcast_to(m0_ref[...][:, None, :], (nseq, cs, LANES)).reshape(rows, LANES)
    ga = _gate_algebra(gate_ref[:, :LANES], gate_ref[:, LANES:], m_in, nseq, cs, sel_ref)

    grp_rows, grp_k = SEQ_GROUP * cs, SEQ_GROUP * DK
    bd_q = ((lax.broadcasted_iota(jnp.int32, (grp_rows, grp_k), 0) >> log_cs)
            == (lax.broadcasted_iota(jnp.int32, (grp_rows, grp_k), 1) >> log_dk))
    bd_k = ((lax.broadcasted_iota(jnp.int32, (nseq * DK, rows), 0) >> log_dk)
            == (lax.broadcasted_iota(jnp.int32, (nseq * DK, rows), 1) >> log_cs))
    k_t = k_ref[...].T
    inter, qn = [], []
    for h in heads:
        c_prev = c0_ref[:, h].reshape(nseq * DK, DV)
        c_bf = c_prev.astype(bf16)
        q32 = q_h[h].astype(f32)
        parts = []
        for j in range(nseq // SEQ_GROUP):
            q_j = q32[j * grp_rows:(j + 1) * grp_rows, :]
            q_bd = jnp.where(bd_q, jnp.concatenate([q_j] * SEQ_GROUP, axis=1), 0.0).astype(bf16)
            parts.append(_dot(q_bd, c_bf[j * grp_k:(j + 1) * grp_k, :]))
        inter.append(jnp.concatenate(parts, axis=0))
        n_prev = n0_ref[:, h * DK:(h + 1) * DK]
        n_rows = jnp.broadcast_to(n_prev[:, None, :], (nseq, cs, DK)).reshape(rows, DK)
        qn.append(jnp.sum(q32 * n_rows, axis=1, keepdims=True))

        wk_t = k_t[h * DK:(h + 1) * DK, :] * ga.w_state_t[h:h + 1, :]
        wk_bd = jnp.where(bd_k, jnp.broadcast_to(wk_t[None], (nseq, DK, rows)).reshape(nseq * DK, rows), 0.0)
        dc = _dot(wk_bd.astype(bf16), v_h[h])
        dec_col = ga.decay[:, h:h + 1]
        dec_rows = jnp.broadcast_to(dec_col.reshape(nseq, 1, cs, 1), (nseq, DK // cs, cs, 1)).reshape(nseq * DK, 1)
        cout_ref[:, h] = (dec_rows * c_prev + dc).reshape(nseq, DK, DV)
        dec_seq = jnp.max(dec_col.reshape(nseq, cs, 1), axis=1)
        nout_ref[:, h * DK:(h + 1) * DK] = dec_seq * n_prev + jnp.sum(
            (ga.w_state[:, h:h + 1] * k_h[h]).reshape(nseq, cs, DK), axis=1)
    mout_ref[...] = jnp.max(ga.m_new.reshape(nseq, cs, LANES), axis=1)
    _head_outputs([qk], [ga], [inter], [qn], [v_h], h_ref, nseq, cs)


def _inproj_values(xn, w_ref, wg_ref, bias_ref):
    q = _dot_nt(xn, w_ref[0:QK_DIM, :]).astype(bf16)
    k = _dot_nt(xn, w_ref[QK_DIM:2 * QK_DIM, :]) * (DK ** -0.5)
    v = _dot_nt(xn, w_ref[2 * QK_DIM:2 * QK_DIM + V_DIM, :]).astype(bf16)
    o = _dot_nt(xn, w_ref[2 * QK_DIM + V_DIM:W_QKVO_COLS, :])
    gz = _dot(xn, wg_ref[...]) + bias_ref[...]
    return q, k, v, o, gz[:, :LANES], jax.nn.log_sigmoid(gz[:, LANES:])


def _l0_prompt_body(x_ref, g_ref, w_ref, wg_ref, bias_ref, sel_ref,
                    o_ref, h_ref, cout_ref, nout_ref, mout_ref,
                    qa, ka, va, ga, qb, kb, vb, gb, c_scr, n_scr, m_scr, *, nt, n_chunks, last):
    step = pl.program_id(0)
    heads = range(N_HEADS)
    sets = ((qa, ka, va, ga), (qb, kb, vb, gb))

    def project(wr):
        q_w, k_w, v_w, g_w = wr
        xn = _rms(x_ref[...], g_ref[...]).astype(bf16)
        q, k, v, o, logi, logf = _inproj_values(xn, w_ref, wg_ref, bias_ref)
        q_w[...] = q
        k_w[...] = k
        v_w[...] = v
        o_ref[...] = o
        g_w[:, :LANES] = logi
        g_w[:, LANES:] = logf

    def core(rd, after_gates):
        first = lax.rem(step + nt - 1, nt) == 0
        c_state = [jnp.where(first, 0.0, c_scr[h]) for h in heads]
        n_state = [jnp.where(first, 0.0, n_scr[h]) for h in heads]
        m_in = jnp.where(first, 0.0, m_scr[...])
        c_state, n_state, m_new = _sequence_chunks(*rd, sel_ref, h_ref, c_state, n_state, m_in, n_chunks,
                                                   after_gates=after_gates)
        for h in heads:
            c_scr[h] = c_state[h]
            n_scr[h] = n_state[h]
        m_scr[...] = m_new

    @pl.when(step == 0)
    def _():
        for ref in (c_scr, n_scr, m_scr):
            ref[...] = jnp.zeros_like(ref)
        project(sets[0])

    for parity in range(2):
        wr, rd = sets[parity], sets[1 - parity]

        @pl.when(jnp.logical_and(jnp.logical_and(step > 0, step < last), lax.rem(step, 2) == parity))
        def _():
            core(rd, functools.partial(project, wr))

    @pl.when(step == last)
    def _():
        core(sets[1 - last % 2], lambda: None)

    @pl.when(jnp.logical_and(step >= 1, lax.rem(step, nt) == 0))
    def _():
        cout_ref[0] = c_scr[...]
        lane = lax.broadcasted_iota(jnp.int32, (DK, LANES), 1)
        n_mat = jnp.zeros((DK, LANES), f32)
        for h in heads:
            n_mat = jnp.where(lane == h, jnp.broadcast_to(n_scr[h], (DK, LANES)), n_mat)
        nout_ref[0] = n_mat
        mout_ref[0] = m_scr[...]


def _bcast_selector():
    src = np.arange(LANES)
    dst = np.arange(N_BCAST * N_HEADS * DV) // DV
    hit = (((src[:, None] >> LOG_HEADS) // 3 == dst[None, :] // N_HEADS)
           & ((src[:, None] & (N_HEADS - 1)) == dst[None, :] % N_HEADS))
    return jnp.asarray(hit, dtype=bf16)


def _l0_prompt(x2d, g, w_in, w_gate, gate_bias, n_batch, seq_len):
    t = x2d.shape[0]
    sel = _bcast_selector()
    rows = PROMPT_CHUNKS * CHUNK_ROWS
    nt = seq_len // rows
    n_blocks = t // rows
    cur = lambda s: (jnp.minimum(s, n_blocks - 1), 0)
    prev = lambda s: (jnp.maximum(s - 1, 0), 0)
    seq = lambda s: jnp.maximum(s - 1, 0) // nt
    bufs = [pltpu.VMEM((rows, QK_DIM), bf16), pltpu.VMEM((rows, QK_DIM), f32),
            pltpu.VMEM((rows, V_DIM), bf16), pltpu.VMEM((rows, 2 * LANES), f32)]
    return pl.pallas_call(
        functools.partial(_l0_prompt_body, nt=nt, n_chunks=PROMPT_CHUNKS, last=n_blocks),
        grid=(n_blocks + 1,),
        in_specs=[pl.BlockSpec((rows, D_MODEL), cur), _const_spec((1, D_MODEL)), _const_spec(w_in.shape),
                  _const_spec(w_gate.shape), _const_spec((1, 2 * LANES)), _const_spec(sel.shape)],
        out_specs=[pl.BlockSpec((rows, V_DIM), cur), pl.BlockSpec((rows, V_DIM), prev),
                   pl.BlockSpec((1, N_HEADS, DK, DV), lambda s: (seq(s), 0, 0, 0)),
                   pl.BlockSpec((1, DK, LANES), lambda s: (seq(s), 0, 0)),
                   pl.BlockSpec((1, 1, LANES), lambda s: (seq(s), 0, 0))],
        out_shape=[jax.ShapeDtypeStruct((t, V_DIM), f32), jax.ShapeDtypeStruct((t, V_DIM), f32),
                   jax.ShapeDtypeStruct((n_batch, N_HEADS, DK, DV), f32),
                   jax.ShapeDtypeStruct((n_batch, DK, LANES), f32),
                   jax.ShapeDtypeStruct((n_batch, 1, LANES), f32)],
        scratch_shapes=bufs + bufs + [pltpu.VMEM((N_HEADS, DK, DV), f32), pltpu.VMEM((N_HEADS, DK, 1), f32),
                                      pltpu.VMEM((1, LANES), f32)],
        compiler_params=_params(1),
        name="l0_inproj_mlstm",
    )(x2d, g, w_in, w_gate, gate_bias, sel)


def _mlstm_core_sample(q, k, v, gates, c0, n0, m0, n_batch, seq_len):
    t = q.shape[0]
    sel = _bcast_selector()
    nseq = CHUNK_ROWS // seq_len
    row = lambda n: pl.BlockSpec((CHUNK_ROWS, n), lambda i: (i, 0))
    cspec = pl.BlockSpec((nseq, N_HEADS, DK, DV), lambda i: (i, 0, 0, 0))
    nspec = pl.BlockSpec((nseq, QK_DIM), lambda i: (i, 0))
    mspec = pl.BlockSpec((nseq, LANES), lambda i: (i, 0))
    return pl.pallas_call(
        functools.partial(_mlstm_sample_body, nseq=nseq, cs=seq_len),
        grid=(t // CHUNK_ROWS,),
        in_specs=[row(QK_DIM), row(QK_DIM), row(V_DIM), row(2 * LANES), _const_spec(sel.shape), cspec, nspec, mspec],
        out_specs=[row(V_DIM), cspec, nspec, mspec],
        out_shape=[jax.ShapeDtypeStruct((t, V_DIM), f32),
                   jax.ShapeDtypeStruct((n_batch, N_HEADS, DK, DV), f32),
                   jax.ShapeDtypeStruct((n_batch, QK_DIM), f32),
                   jax.ShapeDtypeStruct((n_batch, LANES), f32)],
        compiler_params=_params(1),
        name="l0_mlstm_core",
    )(q, k, v, gates, sel, c0, n0, m0)


def _outproj_ffn_body(h_ref, o_ref, x_ref, hg_ref, gffn_ref, wout_hbm, wup_hbm, wdn_hbm, y_ref,
                      wout_ref, wup_ref, wdn_ref, sem):
    wout_copy = pltpu.make_async_copy(wout_hbm, wout_ref, sem.at[0])
    ffn_copies = _ffn_weight_copies(wup_hbm, wdn_hbm, 0, wup_ref, wdn_ref, sem, 1)

    def body(first):
        if first:
            wout_copy.start()
            for pair in ffn_copies:
                for cp in pair:
                    cp.start()
        hns = []
        for r in _row_splits(h_ref.shape[0]):
            hns.append((h_ref[r, :] * hg_ref[...] * jax.nn.sigmoid(o_ref[r, :])).astype(bf16))
        if first:
            wout_copy.wait()
        x1s = [x_ref[r, :] + _dot(hn, wout_ref[...]) for r, hn in zip(_row_splits(h_ref.shape[0]), hns)]
        ys = _ffn(x1s, gffn_ref, wup_ref, wdn_ref, ffn_copies if first else None)
        for r, y in zip(_row_splits(h_ref.shape[0]), ys):
            y_ref[r, :] = y

    _first_step_and_rest(body, 1)


def _outproj_ffn(hmix, o, x2d, head_g, w_out, g_ffn, w_up, w_down):
    t = x2d.shape[0]
    row = pl.BlockSpec((ROW_TILE, D_MODEL), lambda i: (i, 0))
    hbm = pl.BlockSpec(memory_space=pl.ANY)
    return pl.pallas_call(
        _outproj_ffn_body,
        grid=(t // ROW_TILE,),
        in_specs=[row, row, row, _const_spec((1, V_DIM)), _const_spec((1, D_MODEL)), hbm, hbm, hbm],
        out_specs=row,
        out_shape=jax.ShapeDtypeStruct((t, D_MODEL), f32),
        scratch_shapes=[pltpu.VMEM(w_out.shape, bf16), pltpu.VMEM(w_up.shape[1:], bf16),
                        pltpu.VMEM(w_down.shape[1:], bf16),
                        pltpu.SemaphoreType.DMA((1 + 2 * (D_FF // FF_CHUNK),))],
        compiler_params=_params(1),
        name="l0_outproj_ffn",
    )(hmix, o, x2d, head_g, g_ffn, w_out, w_up, w_down)


def _s5_prep_body(lre_ref, lim_ref, ldt_ref, bre_ref, bim_ref, cre_ref, cim_ref,
                  are_ref, aim_ref, wbr_ref, wbi_ref, vre_ref, vim_ref):
    lr = lre_ref[...]
    li = lim_ref[...]
    dt = jnp.exp(ldt_ref[...])
    mag = jnp.exp(lr * dt)
    a_re = mag * jnp.cos(li * dt)
    a_im = mag * jnp.sin(li * dt)
    den = lr * lr + li * li
    z_re = a_re - 1.0
    coef_re = ((z_re * lr + a_im * li) / den)[:, None, :]
    coef_im = ((a_im * lr - z_re * li) / den)[:, None, :]
    br = bre_ref[...]
    bi = bim_ref[...]
    are_ref[...] = a_re
    aim_ref[...] = a_im
    bb = (coef_re * br - coef_im * bi, coef_re * bi + coef_im * br)
    c_t = (cre_ref[...].reshape(S5_GROUPS * S5_GROUP, S5_STATE).T,
           cim_ref[...].reshape(S5_GROUPS * S5_GROUP, S5_STATE).T)

    groups = S5_BLOCK_CH // S5_GROUP
    log_ch, log_st = S5_GROUP.bit_length() - 1, S5_STATE.bit_length() - 1
    shape_b = (S5_BLOCK_CH, S5_BLOCK_ST)
    diag_b = ((lax.broadcasted_iota(jnp.int32, shape_b, 0) >> log_ch)
              == (lax.broadcasted_iota(jnp.int32, shape_b, 1) >> log_st))
    shape_c = (S5_BLOCK_ST, S5_BLOCK_CH)
    diag_c = ((lax.broadcasted_iota(jnp.int32, shape_c, 0) >> log_st)
              == (lax.broadcasted_iota(jnp.int32, shape_c, 1) >> log_ch))
    for j in range(N_S5_BLOCKS):
        for src, dst in zip(bb, (wbr_ref, wbi_ref)):
            blk = src[j * groups:(j + 1) * groups].reshape(S5_BLOCK_CH, S5_STATE)
            dst[j] = jnp.where(diag_b, jnp.concatenate([blk] * groups, axis=1), 0.0).astype(bf16)
        for src, dst in zip(c_t, (vre_ref, vim_ref)):
            blk = src[:, j * S5_BLOCK_CH:(j + 1) * S5_BLOCK_CH]
            rep = jnp.broadcast_to(blk[None], (groups, S5_STATE, S5_BLOCK_CH)).reshape(S5_BLOCK_ST, S5_BLOCK_CH)
            dst[j] = jnp.where(diag_c, rep, 0.0).astype(bf16)


def _s5_prep(lam_re, lam_im, log_dt, b_re_t, b_im_t, c_re, c_im):
    gp = jax.ShapeDtypeStruct((S5_GROUPS, S5_STATE), f32)
    wb = jax.ShapeDtypeStruct((N_S5_BLOCKS, S5_BLOCK_CH, S5_BLOCK_ST), bf16)
    vc = jax.ShapeDtypeStruct((N_S5_BLOCKS, S5_BLOCK_ST, S5_BLOCK_CH), bf16)
    return pl.pallas_call(_s5_prep_body, out_shape=[gp, gp, wb, wb, vc, vc], name="l1_s5_prep")(
        lam_re, lam_im, log_dt, b_re_t, b_im_t, c_re, c_im)


def _s5_body(x_ref, h0r_ref, h0i_ref, g_ref, are_ref, aim_ref, wbr_ref, wbi_ref, vre_ref, vim_ref,
             d_ref, gffn_ref, gfin_ref, wglu_in, wup_in, wdn_in,
             y_ref, sre_ref, sim_ref, sr_scr, si_scr, hr_scr, hi_scr, act_scr, *dma_scr, bt, lt, dma_io):
    rows = bt * lt
    step = pl.program_id(1)
    n_steps = pl.num_programs(1)
    if dma_io:
        wglu_ref, wup_ref, wdn_ref, w_sem, xbuf, ybuf, in_sem, out_sem = dma_scr
        wglu_copy = pltpu.make_async_copy(wglu_in, wglu_ref, w_sem.at[0])
        ffn_copies = _ffn_weight_copies(wup_in, wdn_in, 1, wup_ref, wdn_ref, w_sem, 1)
    else:
        wglu_ref, wup_ref, wdn_ref = wglu_in, wup_in, wdn_in

    @pl.when(step == 0)
    def _():
        sr_scr[...] = h0r_ref[...]
        si_scr[...] = h0i_ref[...]

    if dma_io:
        slot = lax.rem(step, 2)

        def in_copy(s, sl, b):
            return pltpu.make_async_copy(x_ref.at[b, pl.ds(s * lt, lt), :], xbuf.at[sl, :, b, :], in_sem.at[sl, b])

        def out_copy(s, sl, b):
            return pltpu.make_async_copy(ybuf.at[sl, :, b, :], y_ref.at[b, pl.ds(s * lt, lt), :], out_sem.at[sl, b])

        @pl.when(step == 0)
        def _():
            for b in range(bt):
                in_copy(0, 0, b).start()

        @pl.when(step + 1 < n_steps)
        def _():
            for b in range(bt):
                in_copy(step + 1, 1 - slot, b).start()

        for b in range(bt):
            in_copy(step, slot, b).wait()

        @pl.when(step >= 2)
        def _():
            for b in range(bt):
                out_copy(step - 2, slot, b).wait()

    nbuf = hr_scr.shape[0]

    def body(first):
        if first:
            wglu_copy.start()
            for pair in ffn_copies:
                for cp in pair:
                    cp.start()
        if dma_io:
            xt = xbuf[slot].reshape(rows, D_MODEL)
        else:
            xt = jnp.concatenate([x_ref[:, t, :] for t in range(lt)], axis=0)
        u = _rms(xt, g_ref[...])
        ub = u.astype(bf16)

        def project_in(j):
            ch = slice(j * S5_BLOCK_CH, (j + 1) * S5_BLOCK_CH)
            hr_scr[j % nbuf] = _dot(ub[:, ch], wbr_ref[j])
            hi_scr[j % nbuf] = _dot(ub[:, ch], wbi_ref[j])

        def recur(j):
            st = slice(j * S5_BLOCK_ST, (j + 1) * S5_BLOCK_ST)
            a_re = jnp.broadcast_to(are_ref[:, st], (bt, S5_BLOCK_ST))
            a_im = jnp.broadcast_to(aim_ref[:, st], (bt, S5_BLOCK_ST))
            s_re = sr_scr[:, st]
            s_im = si_scr[:, st]
            for t in range(lt):
                r = slice(t * bt, (t + 1) * bt)
                n_re = a_re * s_re - a_im * s_im + hr_scr[j % nbuf, r, :]
                n_im = a_re * s_im + a_im * s_re + hi_scr[j % nbuf, r, :]
                hr_scr[j % nbuf, r, :] = n_re
                hi_scr[j % nbuf, r, :] = n_im
                s_re, s_im = n_re, n_im
            sr_scr[:, st] = s_re
            si_scr[:, st] = s_im

        def project_out(j):
            ch = slice(j * S5_BLOCK_CH, (j + 1) * S5_BLOCK_CH)
            yj = (_dot(hr_scr[j % nbuf].astype(bf16), vre_ref[j]) - _dot(hi_scr[j % nbuf].astype(bf16), vim_ref[j])
                  + d_ref[:, ch] * u[:, ch])
            act_scr[:, ch] = jax.nn.gelu(yj).astype(bf16)

        project_in(0)
        for j in range(N_S5_BLOCKS + 1):
            if j + 1 < N_S5_BLOCKS:
                project_in(j + 1)
            if j >= 1:
                project_out(j - 1)
            if j < N_S5_BLOCKS:
                recur(j)

        sre_ref[...] = sr_scr[...]
        sim_ref[...] = si_scr[...]
        if first:
            wglu_copy.wait()
        x3s = []
        for r in _row_splits(rows):
            ag = _dot(act_scr[r, :], wglu_ref[...])
            x3s.append(xt[r, :] + ag[:, :D_MODEL] * jax.nn.sigmoid(ag[:, D_MODEL:]))
        x4s = _ffn(x3s, gffn_ref, wup_ref, wdn_ref, ffn_copies if first else None)
        y = jnp.concatenate([_rms(x4, gfin_ref[...]) for x4 in x4s], axis=0)
        if dma_io:
            ybuf[slot] = y.reshape(lt, bt, D_MODEL)
        else:
            for t in range(lt):
                y_ref[:, t, :] = y[t * bt:(t + 1) * bt, :]

    if not dma_io:
        body(False)
    else:
        _first_step_and_rest(body, 2)
        for b in range(bt):
            out_copy(step, slot, b).start()

        @pl.when(step == n_steps - 1)
        def _():
            @pl.when(step >= 1)
            def _():
                for b in range(bt):
                    out_copy(step - 1, 1 - slot, b).wait()

            for b in range(bt):
                out_copy(step, slot, b).wait()


def _s5_layer(x3d, h0_re, h0_im, bt, lt, g_mix, a_re, a_im, wb_re, wb_im, v_re, v_im, d_skip,
              w_glu, g_ffn, w_up, w_down, g_final):
    n_batch, seq_len, _ = x3d.shape
    grid = (n_batch // bt, seq_len // lt)
    sspec = pl.BlockSpec((bt, S5_N), lambda i, t: (i, 0))
    dma_io = n_batch == bt
    hbm = pl.BlockSpec(memory_space=pl.ANY)
    if dma_io:
        xspec = hbm
        weight_specs = [hbm, hbm, hbm]
        dma_scratch = [pltpu.VMEM(w_glu.shape, bf16), pltpu.VMEM(w_up.shape[1:], bf16),
                       pltpu.VMEM(w_down.shape[1:], bf16), pltpu.SemaphoreType.DMA((1 + 2 * (D_FF // FF_CHUNK),)),
                       pltpu.VMEM((2, lt, bt, D_MODEL), f32), pltpu.VMEM((2, lt, bt, D_MODEL), f32),
                       pltpu.SemaphoreType.DMA((2, bt)), pltpu.SemaphoreType.DMA((2, bt))]
    else:
        xspec = pl.BlockSpec((bt, lt, D_MODEL), lambda i, t: (i, t, 0))
        weight_specs = [_const_spec(w_glu.shape), _layer_spec(w_up.shape, 1), _layer_spec(w_down.shape, 1)]
        dma_scratch = []
    consts = (g_mix, a_re, a_im, wb_re, wb_im, v_re, v_im, d_skip, g_ffn, g_final)
    rows = bt * lt
    return pl.pallas_call(
        functools.partial(_s5_body, bt=bt, lt=lt, dma_io=dma_io),
        grid=grid,
        in_specs=[xspec, sspec, sspec] + [_const_spec(c.shape) for c in consts] + weight_specs,
        out_specs=[xspec, sspec, sspec],
        out_shape=[jax.ShapeDtypeStruct(x3d.shape, f32), jax.ShapeDtypeStruct((n_batch, S5_N), f32),
                   jax.ShapeDtypeStruct((n_batch, S5_N), f32)],
        scratch_shapes=[pltpu.VMEM((bt, S5_N), f32), pltpu.VMEM((bt, S5_N), f32),
                        pltpu.VMEM((3, rows, S5_BLOCK_ST), f32), pltpu.VMEM((3, rows, S5_BLOCK_ST), f32),
                        pltpu.VMEM((rows, D_MODEL), bf16)] + dma_scratch,
        compiler_params=_params(2),
        name="l1_s5_ffn",
    )(x3d, h0_re, h0_im, *consts, w_glu, w_up, w_down)


def _rep_lanes(x):
    return jnp.tile(x, (1,) * (x.ndim - 1) + (LANES // x.shape[-1],))


def _mlstm_mixer(x, state, p):
    n_batch, seq_len, _ = x.shape
    x2d = x.reshape(n_batch * seq_len, D_MODEL)
    if state is None:
        o, hmix, c_new, n_new, m_new = _l0_prompt(x2d, p["g_mix0"], p["w_in"], p["w_gate"], p["gate_bias"],
                                                  n_batch, seq_len)
        n_new = jnp.swapaxes(n_new[:, :, :N_HEADS], 1, 2)
    else:
        c0, n0, m0 = state
        q, k, v, o, gates = _inproj(x2d, p["g_mix0"], p["w_in"], p["w_gate"], p["gate_bias"])
        hmix, c_new, n_new, m_new = _mlstm_core_sample(
            q, k, v, gates, c0, n0.reshape(n_batch, QK_DIM), _rep_lanes(m0), n_batch, seq_len)
    states = (c_new.reshape(1, n_batch, N_HEADS, DK, DV),
              n_new.reshape(1, n_batch, N_HEADS, DK),
              m_new.reshape(n_batch, LANES)[:, :N_HEADS].reshape(1, n_batch, N_HEADS))
    return (hmix, o, x2d), states


def _s5_stage(x2, shape, state, p):
    n_batch, seq_len, _ = shape
    if state is None:
        h0_re = jnp.zeros((n_batch, S5_N), f32)
        h0_im = h0_re
        bt, lt = n_batch, ROW_TILE // n_batch
    else:
        h0_re, h0_im = (s.reshape(n_batch, S5_N) for s in state)
        bt, lt = ROW_TILE // seq_len, seq_len
    y, s_re, s_im = _s5_layer(x2.reshape(shape), h0_re, h0_im, bt, lt,
                              p["g_mix1"], p["a_re"], p["a_im"], p["wb_re"], p["wb_im"], p["v_re"], p["v_im"],
                              p["d_skip"], p["w_glu"], p["g_ffn1"], p["w_up"], p["w_down"], p["g_final"])
    return y, (s_re.reshape(1, n_batch, S5_GROUPS, S5_STATE), s_im.reshape(1, n_batch, S5_GROUPS, S5_STATE))


def kernel(x_prompt, x_sample, state_mlstm_C, state_mlstm_n, state_mlstm_m, state_s5_re, state_s5_im,
           norm_mix_g, norm_ffn_g, norm_final_g, mlstm_w_in, mlstm_b_i, mlstm_b_f, mlstm_head_norm_g,
           mlstm_w_out, s5_lambda_re, s5_lambda_im, s5_log_dt, s5_b_re, s5_b_im, s5_c_re, s5_c_im,
           s5_d, s5_w_glu, ffn_w_up, ffn_w_down):
    w_in = mlstm_w_in[0]
    n_qkvo = W_QKVO_COLS
    a_re, a_im, wb_re, wb_im, v_re, v_im = _s5_prep(
        s5_lambda_re[0], s5_lambda_im[0], s5_log_dt[0].reshape(S5_GROUPS, 1),
        jnp.swapaxes(s5_b_re[0], 1, 2), jnp.swapaxes(s5_b_im[0], 1, 2), s5_c_re[0], s5_c_im[0])
    p = dict(
        g_mix0=norm_mix_g[0].reshape(1, D_MODEL), g_mix1=norm_mix_g[1].reshape(1, D_MODEL),
        g_ffn0=norm_ffn_g[0].reshape(1, D_MODEL), g_ffn1=norm_ffn_g[1].reshape(1, D_MODEL),
        g_final=norm_final_g.reshape(1, D_MODEL),
        w_in=jnp.swapaxes(w_in, 0, 1).astype(bf16),
        w_gate=jnp.concatenate([_rep_lanes(w_in[:, n_qkvo:n_qkvo + N_HEADS]),
                                _rep_lanes(w_in[:, n_qkvo + N_HEADS:])], axis=1).astype(bf16),
        gate_bias=jnp.concatenate([_rep_lanes(mlstm_b_i[0][None]), _rep_lanes(mlstm_b_f[0][None])], axis=1),
        head_g=mlstm_head_norm_g[0].reshape(1, V_DIM),
        w_out=mlstm_w_out[0].astype(bf16),
        w_up=ffn_w_up.astype(bf16), w_down=ffn_w_down.astype(bf16),
        a_re=a_re.reshape(1, S5_N), a_im=a_im.reshape(1, S5_N),
        wb_re=wb_re, wb_im=wb_im, v_re=v_re, v_im=v_im,
        d_skip=s5_d[0].reshape(1, D_MODEL),
        w_glu=s5_w_glu[0].astype(bf16),
    )
    mix_p, mlstm_p = _mlstm_mixer(x_prompt, None, p)
    mix_s, mlstm_s = _mlstm_mixer(x_sample, (state_mlstm_C[0], state_mlstm_n[0], state_mlstm_m[0]), p)
    x2_p, x2_s = (_outproj_ffn(*mix, p["head_g"], p["w_out"], p["g_ffn0"], p["w_up"], p["w_down"])
                  for mix in (mix_p, mix_s))
    y_p, s5_p = _s5_stage(x2_p, x_prompt.shape, None, p)
    y_s, s5_s = _s5_stage(x2_s, x_sample.shape, (state_s5_re[0], state_s5_im[0]), p)
    return (y_p, y_s) + mlstm_p + s5_p + mlstm_s + s5_s
```

```python
import functools
import types

import jax
import jax.numpy as jnp
import numpy as np
from jax import lax
from jax.experimental import pallas as pl
from jax.experimental.pallas import tpu as pltpu

f32 = jnp.float32
bf16 = jnp.bfloat16

D_MODEL = 1024
N_HEADS = 8
DK = 64
DV = 128
QK_DIM = N_HEADS * DK
V_DIM = N_HEADS * DV
W_QKVO_COLS = 2 * QK_DIM + 2 * V_DIM
D_FF = 4 * D_MODEL
S5_GROUPS = 64
S5_GROUP = 16
S5_STATE = 64
S5_N = S5_GROUPS * S5_STATE
EPS = 1e-6

LANES = 128
ROW_TILE = 512
CHUNK_ROWS = 256
PROMPT_CHUNKS = 2
SEQ_GROUP = 4
FF_CHUNK = 1024
ROW_SPLITS = 2
S5_BLOCK_CH = LANES
S5_BLOCK_ST = S5_BLOCK_CH // S5_GROUP * S5_STATE
N_S5_BLOCKS = D_MODEL // S5_BLOCK_CH
VMEM_LIMIT_BYTES = 56 * 1024 * 1024
NEG_BIG = -1e30
CLAMP_MAX = 2.0 ** 126
LOG_HEADS = N_HEADS.bit_length() - 1
N_BCAST = 3


def _params(n_axes):
    return pltpu.CompilerParams(dimension_semantics=("arbitrary",) * n_axes,
                                vmem_limit_bytes=VMEM_LIMIT_BYTES)


def _const_spec(shape):
    nd = len(shape)
    return pl.BlockSpec(shape, lambda *_: (0,) * nd, pipeline_mode=pl.Buffered(1))


def _layer_spec(shape, layer):
    nd = len(shape)
    return pl.BlockSpec((None,) + tuple(shape[1:]), lambda *_: (layer,) + (0,) * (nd - 1),
                        pipeline_mode=pl.Buffered(1))


def _dot(a, b):
    return jnp.dot(a, b, preferred_element_type=f32)


def _dot_nt(a, b):
    return lax.dot_general(a, b, (((1,), (1,)), ((), ())), preferred_element_type=f32)


def _rms(x, g):
    return x * lax.rsqrt(jnp.mean(x * x, axis=-1, keepdims=True) + EPS) * g


def _row_splits(rows):
    size = rows // ROW_SPLITS
    return [slice(i * size, (i + 1) * size) for i in range(ROW_SPLITS)]


def _ffn(x1s, g_ref, wup_ref, wdn_ref):
    xn = [_rms(x1, g_ref[...]).astype(bf16) for x1 in x1s]
    acc = [None] * len(x1s)
    for c in range(D_FF // FF_CHUNK):
        cols = slice(c * FF_CHUNK, (c + 1) * FF_CHUNK)
        for i in range(len(x1s)):
            hid = _dot(xn[i], wup_ref[:, cols])
            hid = jnp.square(jnp.maximum(hid, 0.0)).astype(bf16)
            part = _dot(hid, wdn_ref[cols, :])
            acc[i] = part if acc[i] is None else acc[i] + part
    return [x1 + a for x1, a in zip(x1s, acc)]


def _inproj_body(x_ref, g_ref, w_ref, wg_ref, bias_ref, q_ref, k_ref, v_ref, o_ref, gate_ref):
    xn = _rms(x_ref[...], g_ref[...]).astype(bf16)
    q, k, v, o, logi, logf = _inproj_values(xn, w_ref, wg_ref, bias_ref)
    q_ref[...] = q
    k_ref[...] = k
    v_ref[...] = v
    o_ref[...] = o
    gate_ref[:, :LANES] = logi
    gate_ref[:, LANES:] = logf


def _inproj(x2d, g, w_in, w_gate, gate_bias):
    t = x2d.shape[0]
    row = lambda n: pl.BlockSpec((ROW_TILE, n), lambda i: (i, 0))
    return pl.pallas_call(
        _inproj_body,
        grid=(t // ROW_TILE,),
        in_specs=[row(D_MODEL), _const_spec((1, D_MODEL)), _const_spec(w_in.shape),
                  _const_spec(w_gate.shape), _const_spec((1, 2 * LANES))],
        out_specs=[row(QK_DIM), row(QK_DIM), row(V_DIM), row(V_DIM), row(2 * LANES)],
        out_shape=[jax.ShapeDtypeStruct((t, QK_DIM), bf16), jax.ShapeDtypeStruct((t, QK_DIM), f32),
                   jax.ShapeDtypeStruct((t, V_DIM), bf16), jax.ShapeDtypeStruct((t, V_DIM), f32),
                   jax.ShapeDtypeStruct((t, 2 * LANES), f32)],
        compiler_params=_params(1),
        name="l0_inproj",
    )(x2d, g, w_in, w_gate, gate_bias)


def _row_prefix(x, cs, rpos, is_max):
    sh = 1
    while sh < cs:
        prev = pltpu.roll(x, sh, 0)
        if is_max:
            x = jnp.maximum(x, jnp.where(rpos >= sh, prev, NEG_BIG))
        else:
            x = x + jnp.where(rpos >= sh, prev, 0.0)
        sh *= 2
    return x


def _seg_last(x, nseq, cs):
    if nseq == 1:
        return x[cs - 1:cs, :]
    last = x.reshape(nseq, cs, x.shape[1])[:, cs - 1:cs, :]
    return jnp.broadcast_to(last, (nseq, cs, x.shape[1])).reshape(nseq * cs, x.shape[1])


def _gate_algebra(logi, logf, m_in, nseq, cs, sel_ref):
    rows = nseq * cs
    rpos = lax.broadcasted_iota(jnp.int32, (rows, LANES), 0) & (cs - 1)
    b = _row_prefix(logf, cs, rpos, False)
    a = logi - b
    g = jnp.maximum(m_in, _row_prefix(a, cs, rpos, True))
    g_last = _seg_last(g, nseq, cs)
    w_state = jnp.exp(a - g_last)

    group = lax.broadcasted_iota(jnp.int32, (rows, LANES), 1) >> LOG_HEADS
    pieces = jnp.zeros((rows, LANES), f32)
    clamp = jnp.minimum(jnp.exp(-(b + g)), CLAMP_MAX)
    for i, val in enumerate((g, jnp.exp(m_in - g), clamp)):
        hi = val.astype(bf16).astype(f32)
        mid = (val - hi).astype(bf16).astype(f32)
        lo = (val - hi - mid).astype(bf16).astype(f32)
        for j, piece in enumerate((hi, mid, lo)):
            pieces = jnp.where(group == 3 * i + j, piece, pieces)
    bcast = _dot(pieces.astype(bf16), sel_ref[...])
    tile = lambda i, h: bcast[:, (i * N_HEADS + h) * DV:(i * N_HEADS + h + 1) * DV]
    return types.SimpleNamespace(
        a_t=a.T,
        w_state=w_state, w_state_t=w_state.T,
        decay=jnp.exp(m_in - g_last),
        m_new=_seg_last(b, nseq, cs) + g_last,
        g=lambda h: tile(0, h), w_inter=lambda h: tile(1, h), clamp=lambda h: tile(2, h))


def _head_outputs(qk, gates, inter, qn, v_h, h_ref, nseq, cs):
    rows = nseq * cs
    log_cs = cs.bit_length() - 1
    heads = range(N_HEADS)
    chunks = range(len(qk))
    ri = lax.broadcasted_iota(jnp.int32, (rows, rows), 0)
    ci = lax.broadcasted_iota(jnp.int32, (rows, rows), 1)
    causal = ci <= ri
    if nseq > 1:
        causal = jnp.logical_and(causal, (ri >> log_cs) == (ci >> log_cs))
    for c in chunks:
        for h in heads:
            g_rows = jnp.concatenate([gates[c].g(h)] * (rows // DV), axis=1)
            w = jnp.exp(jnp.where(causal, gates[c].a_t[h:h + 1, :] - g_rows, NEG_BIG))
            s = qk[c][h]() * w
            s_sum = jnp.sum(s, axis=1, keepdims=True)
            intra = _dot(s.astype(bf16), v_h[c][h])
            w_inter = gates[c].w_inter(h)
            num = w_inter * inter[c][h] + intra
            den = jnp.maximum(jnp.abs(w_inter * qn[c][h] + s_sum), gates[c].clamp(h))
            h_ref[c * rows:(c + 1) * rows, h * DV:(h + 1) * DV] = num * lax.rsqrt(
                jnp.mean(num * num, axis=1, keepdims=True) + EPS * (den * den))


def _sequence_chunks(q_ref, k_ref, v_ref, gate_ref, sel_ref, h_ref, c_state, n_state, m_in, n_chunks,
                     after_gates=lambda: None):
    rows = CHUNK_ROWS
    heads = range(N_HEADS)
    chunks = range(n_chunks)
    c_state, n_state = list(c_state), list(n_state)
    rsl = lambda c: slice(c * rows, (c + 1) * rows)
    q_h = [[q_ref[rsl(c), h * DK:(h + 1) * DK] for h in heads] for c in chunks]
    k_h = [[k_ref[rsl(c), h * DK:(h + 1) * DK] for h in heads] for c in chunks]
    v_h = [[v_ref[rsl(c), h * DV:(h + 1) * DV] for h in heads] for c in chunks]

    qk = [[functools.partial(_dot_nt, q_h[c][h], k_h[c][h].astype(bf16)) for h in heads] for c in chunks]

    gates = []
    for c in chunks:
        gates.append(_gate_algebra(gate_ref[rsl(c), :LANES], gate_ref[rsl(c), LANES:],
                                   m_in if c == 0 else gates[c - 1].m_new, 1, rows, sel_ref))
    after_gates()

    inter = [[None] * N_HEADS for _ in chunks]
    qn = [[None] * N_HEADS for _ in chunks]
    for c in chunks:
        ga = gates[c]
        k_t = k_ref[rsl(c), :].T
        for h in heads:
            rhs = jnp.concatenate([c_state[h].astype(bf16),
                                   jnp.broadcast_to(n_state[h], (DK, DV)).astype(bf16)], axis=1)
            inter2 = _dot(q_h[c][h], rhs)
            inter[c][h] = inter2[:, :DV]
            qn[c][h] = inter2[:, DV:]
            wk_t = k_t[h * DK:(h + 1) * DK, :] * ga.w_state_t[h:h + 1, :]
            dec = ga.decay[:, h:h + 1]
            c_state[h] = dec * c_state[h] + _dot(wk_t.astype(bf16), v_h[c][h])
            n_state[h] = dec * n_state[h] + jnp.sum(wk_t, axis=1, keepdims=True)

    _head_outputs(qk, gates, inter, qn, v_h, h_ref, 1, rows)
    return c_state, n_state, gates[-1].m_new


def _mlstm_sample_body(q_ref, k_ref, v_ref, gate_ref, sel_ref, c0_ref, n0_ref, m0_ref,
                       h_ref, cout_ref, nout_ref, mout_ref, *, nseq, cs):
    rows = nseq * cs
    log_cs = cs.bit_length() - 1
    log_dk = DK.bit_length() - 1
    heads = range(N_HEADS)
    q_h = [q_ref[:, h * DK:(h + 1) * DK] for h in heads]
    k_h = [k_ref[:, h * DK:(h + 1) * DK] for h in heads]
    v_h = [v_ref[:, h * DV:(h + 1) * DV] for h in heads]
    qk = [functools.partial(_dot_nt, q_h[h], k_h[h].astype(bf16)) for h in heads]
    m_in = jnp.broadcast_to(m0_ref[...][:, None, :], (nseq, cs, LANES)).reshape(rows, LANES)
    ga = _gate_algebra(gate_ref[:, :LANES], gate_ref[:, LANES:], m_in, nseq, cs, sel_ref)

    grp_rows, grp_k = SEQ_GROUP * cs, SEQ_GROUP * DK
    bd_q = ((lax.broadcasted_iota(jnp.int32, (grp_rows, grp_k), 0) >> log_cs)
            == (lax.broadcasted_iota(jnp.int32, (grp_rows, grp_k), 1) >> log_dk))
    bd_k = ((lax.broadcasted_iota(jnp.int32, (nseq * DK, rows), 0) >> log_dk)
            == (lax.broadcasted_iota(jnp.int32, (nseq * DK, rows), 1) >> log_cs))
    k_t = k_ref[...].T
    inter, qn = [], []
    for h in heads:
        c_prev = c0_ref[:, h].reshape(nseq * DK, DV)
        c_bf = c_prev.astype(bf16)
        q32 = q_h[h].astype(f32)
        parts = []
        for j in range(nseq // SEQ_GROUP):
            q_j = q32[j * grp_rows:(j + 1) * grp_rows, :]
            q_bd = jnp.where(bd_q, jnp.concatenate([q_j] * SEQ_GROUP, axis=1), 0.0).astype(bf16)
            parts.append(_dot(q_bd, c_bf[j * grp_k:(j + 1) * grp_k, :]))
        inter.append(jnp.concatenate(parts, axis=0))
        n_prev = n0_ref[:, h * DK:(h + 1) * DK]
        n_rows = jnp.broadcast_to(n_prev[:, None, :], (nseq, cs, DK)).reshape(rows, DK)
        qn.append(jnp.sum(q32 * n_rows, axis=1, keepdims=True))

        wk_t = k_t[h * DK:(h + 1) * DK, :] * ga.w_state_t[h:h + 1, :]
        wk_bd = jnp.where(bd_k, jnp.broadcast_to(wk_t[None], (nseq, DK, rows)).reshape(nseq * DK, rows), 0.0)
        dc = _dot(wk_bd.astype(bf16), v_h[h])
        dec_col = ga.decay[:, h:h + 1]
        dec_rows = jnp.broadcast_to(dec_col.reshape(nseq, 1, cs, 1), (nseq, DK // cs, cs, 1)).reshape(nseq * DK, 1)
        cout_ref[:, h] = (dec_rows * c_prev + dc).reshape(nseq, DK, DV)
        dec_seq = jnp.max(dec_col.reshape(nseq, cs, 1), axis=1)
        nout_ref[:, h * DK:(h + 1) * DK] = dec_seq * n_prev + jnp.sum(
            (ga.w_state[:, h:h + 1] * k_h[h]).reshape(nseq, cs, DK), axis=1)
    mout_ref[...] = jnp.max(ga.m_new.reshape(nseq, cs, LANES), axis=1)
    _head_outputs([qk], [ga], [inter], [qn], [v_h], h_ref, nseq, cs)


def _inproj_values(xn, w_ref, wg_ref, bias_ref):
    q = _dot_nt(xn, w_ref[0:QK_DIM, :]).astype(bf16)
    k = _dot_nt(xn, w_ref[QK_DIM:2 * QK_DIM, :]) * (DK ** -0.5)
    v = _dot_nt(xn, w_ref[2 * QK_DIM:2 * QK_DIM + V_DIM, :]).astype(bf16)
    o = _dot_nt(xn, w_ref[2 * QK_DIM + V_DIM:W_QKVO_COLS, :])
    gz = _dot(xn, wg_ref[...]) + bias_ref[...]
    return q, k, v, o, gz[:, :LANES], jax.nn.log_sigmoid(gz[:, LANES:])


def _l0_prompt_body(x_ref, g_ref, w_ref, wg_ref, bias_ref, sel_ref,
                    o_ref, h_ref, cout_ref, nout_ref, mout_ref,
                    qa, ka, va, ga, qb, kb, vb, gb, c_scr, n_scr, m_scr, *, nt, n_chunks, last):
    step = pl.program_id(0)
    heads = range(N_HEADS)
    sets = ((qa, ka, va, ga), (qb, kb, vb, gb))

    def project(wr):
        q_w, k_w, v_w, g_w = wr
        xn = _rms(x_ref[...], g_ref[...]).astype(bf16)
        q, k, v, o, logi, logf = _inproj_values(xn, w_ref, wg_ref, bias_ref)
        q_w[...] = q
        k_w[...] = k
        v_w[...] = v
        o_ref[...] = o
        g_w[:, :LANES] = logi
        g_w[:, LANES:] = logf

    def core(rd, after_gates):
        first = lax.rem(step + nt - 1, nt) == 0
        c_state = [jnp.where(first, 0.0, c_scr[h]) for h in heads]
        n_state = [jnp.where(first, 0.0, n_scr[h]) for h in heads]
        m_in = jnp.where(first, 0.0, m_scr[...])
        c_state, n_state, m_new = _sequence_chunks(*rd, sel_ref, h_ref, c_state, n_state, m_in, n_chunks,
                                                   after_gates=after_gates)
        for h in heads:
            c_scr[h] = c_state[h]
            n_scr[h] = n_state[h]
        m_scr[...] = m_new

    @pl.when(step == 0)
    def _():
        for ref in (c_scr, n_scr, m_scr):
            ref[...] = jnp.zeros_like(ref)
        project(sets[0])

    for parity in range(2):
        wr, rd = sets[parity], sets[1 - parity]

        @pl.when(jnp.logical_and(jnp.logical_and(step > 0, step < last), lax.rem(step, 2) == parity))
        def _():
            core(rd, functools.partial(project, wr))

    @pl.when(step == last)
    def _():
        core(sets[1 - last % 2], lambda: None)

    @pl.when(jnp.logical_and(step >= 1, lax.rem(step, nt) == 0))
    def _():
        cout_ref[0] = c_scr[...]
        lane = lax.broadcasted_iota(jnp.int32, (DK, LANES), 1)
        n_mat = jnp.zeros((DK, LANES), f32)
        for h in heads:
            n_mat = jnp.where(lane == h, jnp.broadcast_to(n_scr[h], (DK, LANES)), n_mat)
        nout_ref[0] = n_mat
        mout_ref[0] = m_scr[...]


def _bcast_selector():
    src = np.arange(LANES)
    dst = np.arange(N_BCAST * N_HEADS * DV) // DV
    hit = (((src[:, None] >> LOG_HEADS) // 3 == dst[None, :] // N_HEADS)
           & ((src[:, None] & (N_HEADS - 1)) == dst[None, :] % N_HEADS))
    return jnp.asarray(hit, dtype=bf16)


def _l0_prompt(x2d, g, w_in, w_gate, gate_bias, n_batch, seq_len):
    t = x2d.shape[0]
    sel = _bcast_selector()
    rows = PROMPT_CHUNKS * CHUNK_ROWS
    nt = seq_len // rows
    n_blocks = t // rows
    cur = lambda s: (jnp.minimum(s, n_blocks - 1), 0)
    prev = lambda s: (jnp.maximum(s - 1, 0), 0)
    seq = lambda s: jnp.maximum(s - 1, 0) // nt
    bufs = [pltpu.VMEM((rows, QK_DIM), bf16), pltpu.VMEM((rows, QK_DIM), f32),
            pltpu.VMEM((rows, V_DIM), bf16), pltpu.VMEM((rows, 2 * LANES), f32)]
    return pl.pallas_call(
        functools.partial(_l0_prompt_body, nt=nt, n_chunks=PROMPT_CHUNKS, last=n_blocks),
        grid=(n_blocks + 1,),
        in_specs=[pl.BlockSpec((rows, D_MODEL), cur), _const_spec((1, D_MODEL)), _const_spec(w_in.shape),
                  _const_spec(w_gate.shape), _const_spec((1, 2 * LANES)), _const_spec(sel.shape)],
        out_specs=[pl.BlockSpec((rows, V_DIM), cur), pl.BlockSpec((rows, V_DIM), prev),
                   pl.BlockSpec((1, N_HEADS, DK, DV), lambda s: (seq(s), 0, 0, 0)),
                   pl.BlockSpec((1, DK, LANES), lambda s: (seq(s), 0, 0)),
                   pl.BlockSpec((1, 1, LANES), lambda s: (seq(s), 0, 0))],
        out_shape=[jax.ShapeDtypeStruct((t, V_DIM), f32), jax.ShapeDtypeStruct((t, V_DIM), f32),
                   jax.ShapeDtypeStruct((n_batch, N_HEADS, DK, DV), f32),
                   jax.ShapeDtypeStruct((n_batch, DK, LANES), f32),
                   jax.ShapeDtypeStruct((n_batch, 1, LANES), f32)],
        scratch_shapes=bufs + bufs + [pltpu.VMEM((N_HEADS, DK, DV), f32), pltpu.VMEM((N_HEADS, DK, 1), f32),
                                      pltpu.VMEM((1, LANES), f32)],
        compiler_params=_params(1),
        name="l0_inproj_mlstm",
    )(x2d, g, w_in, w_gate, gate_bias, sel)


def _mlstm_core_sample(q, k, v, gates, c0, n0, m0, n_batch, seq_len):
    t = q.shape[0]
    sel = _bcast_selector()
    nseq = CHUNK_ROWS // seq_len
    row = lambda n: pl.BlockSpec((CHUNK_ROWS, n), lambda i: (i, 0))
    cspec = pl.BlockSpec((nseq, N_HEADS, DK, DV), lambda i: (i, 0, 0, 0))
    nspec = pl.BlockSpec((nseq, QK_DIM), lambda i: (i, 0))
    mspec = pl.BlockSpec((nseq, LANES), lambda i: (i, 0))
    return pl.pallas_call(
        functools.partial(_mlstm_sample_body, nseq=nseq, cs=seq_len),
        grid=(t // CHUNK_ROWS,),
        in_specs=[row(QK_DIM), row(QK_DIM), row(V_DIM), row(2 * LANES), _const_spec(sel.shape), cspec, nspec, mspec],
        out_specs=[row(V_DIM), cspec, nspec, mspec],
        out_shape=[jax.ShapeDtypeStruct((t, V_DIM), f32),
                   jax.ShapeDtypeStruct((n_batch, N_HEADS, DK, DV), f32),
                   jax.ShapeDtypeStruct((n_batch, QK_DIM), f32),
                   jax.ShapeDtypeStruct((n_batch, LANES), f32)],
        compiler_params=_params(1),
        name="l0_mlstm_core",
    )(q, k, v, gates, sel, c0, n0, m0)


def _outproj_ffn_body(h_ref, o_ref, x_ref, hg_ref, wout_ref, gffn_ref, wup_ref, wdn_ref, y_ref):
    x1s = []
    for r in _row_splits(h_ref.shape[0]):
        hn = h_ref[r, :] * hg_ref[...] * jax.nn.sigmoid(o_ref[r, :])
        x1s.append(x_ref[r, :] + _dot(hn.astype(bf16), wout_ref[...]))
    for r, y in zip(_row_splits(h_ref.shape[0]), _ffn(x1s, gffn_ref, wup_ref, wdn_ref)):
        y_ref[r, :] = y


def _outproj_ffn(hmix, o, x2d, head_g, w_out, g_ffn, w_up, w_down):
    t = x2d.shape[0]
    row = pl.BlockSpec((ROW_TILE, D_MODEL), lambda i: (i, 0))
    return pl.pallas_call(
        _outproj_ffn_body,
        grid=(t // ROW_TILE,),
        in_specs=[row, row, row, _const_spec((1, V_DIM)), _const_spec(w_out.shape),
                  _const_spec((1, D_MODEL)), _layer_spec(w_up.shape, 0), _layer_spec(w_down.shape, 0)],
        out_specs=row,
        out_shape=jax.ShapeDtypeStruct((t, D_MODEL), f32),
        compiler_params=_params(1),
        name="l0_outproj_ffn",
    )(hmix, o, x2d, head_g, w_out, g_ffn, w_up, w_down)


def _s5_prep_body(lre_ref, lim_ref, ldt_ref, bre_ref, bim_ref, cre_ref, cim_ref,
                  are_ref, aim_ref, wbr_ref, wbi_ref, vre_ref, vim_ref):
    lr = lre_ref[...]
    li = lim_ref[...]
    dt = jnp.exp(ldt_ref[...])
    mag = jnp.exp(lr * dt)
    a_re = mag * jnp.cos(li * dt)
    a_im = mag * jnp.sin(li * dt)
    den = lr * lr + li * li
    z_re = a_re - 1.0
    coef_re = ((z_re * lr + a_im * li) / den)[:, None, :]
    coef_im = ((a_im * lr - z_re * li) / den)[:, None, :]
    br = bre_ref[...]
    bi = bim_ref[...]
    are_ref[...] = a_re
    aim_ref[...] = a_im
    bb = (coef_re * br - coef_im * bi, coef_re * bi + coef_im * br)
    c_t = (cre_ref[...].reshape(S5_GROUPS * S5_GROUP, S5_STATE).T,
           cim_ref[...].reshape(S5_GROUPS * S5_GROUP, S5_STATE).T)

    groups = S5_BLOCK_CH // S5_GROUP
    log_ch, log_st = S5_GROUP.bit_length() - 1, S5_STATE.bit_length() - 1
    shape_b = (S5_BLOCK_CH, S5_BLOCK_ST)
    diag_b = ((lax.broadcasted_iota(jnp.int32, shape_b, 0) >> log_ch)
              == (lax.broadcasted_iota(jnp.int32, shape_b, 1) >> log_st))
    shape_c = (S5_BLOCK_ST, S5_BLOCK_CH)
    diag_c = ((lax.broadcasted_iota(jnp.int32, shape_c, 0) >> log_st)
              == (lax.broadcasted_iota(jnp.int32, shape_c, 1) >> log_ch))
    for j in range(N_S5_BLOCKS):
        for src, dst in zip(bb, (wbr_ref, wbi_ref)):
            blk = src[j * groups:(j + 1) * groups].reshape(S5_BLOCK_CH, S5_STATE)
            dst[j] = jnp.where(diag_b, jnp.concatenate([blk] * groups, axis=1), 0.0).astype(bf16)
        for src, dst in zip(c_t, (vre_ref, vim_ref)):
            blk = src[:, j * S5_BLOCK_CH:(j + 1) * S5_BLOCK_CH]
            rep = jnp.broadcast_to(blk[None], (groups, S5_STATE, S5_BLOCK_CH)).reshape(S5_BLOCK_ST, S5_BLOCK_CH)
            dst[j] = jnp.where(diag_c, rep, 0.0).astype(bf16)


def _s5_prep(lam_re, lam_im, log_dt, b_re_t, b_im_t, c_re, c_im):
    gp = jax.ShapeDtypeStruct((S5_GROUPS, S5_STATE), f32)
    wb = jax.ShapeDtypeStruct((N_S5_BLOCKS, S5_BLOCK_CH, S5_BLOCK_ST), bf16)
    vc = jax.ShapeDtypeStruct((N_S5_BLOCKS, S5_BLOCK_ST, S5_BLOCK_CH), bf16)
    return pl.pallas_call(_s5_prep_body, out_shape=[gp, gp, wb, wb, vc, vc], name="l1_s5_prep")(
        lam_re, lam_im, log_dt, b_re_t, b_im_t, c_re, c_im)


def _s5_body(x_ref, h0r_ref, h0i_ref, g_ref, are_ref, aim_ref, wbr_ref, wbi_ref, vre_ref, vim_ref,
             d_ref, wglu_ref, gffn_ref, wup_ref, wdn_ref, gfin_ref,
             y_ref, sre_ref, sim_ref, sr_scr, si_scr, hr_scr, hi_scr, act_scr, *io_scr, bt, lt, dma_io):
    rows = bt * lt
    step = pl.program_id(1)
    n_steps = pl.num_programs(1)

    @pl.when(step == 0)
    def _():
        sr_scr[...] = h0r_ref[...]
        si_scr[...] = h0i_ref[...]

    if dma_io:
        xbuf, ybuf, in_sem, out_sem = io_scr
        slot = lax.rem(step, 2)

        def in_copy(s, sl, b):
            return pltpu.make_async_copy(x_ref.at[b, pl.ds(s * lt, lt), :], xbuf.at[sl, :, b, :], in_sem.at[sl, b])

        def out_copy(s, sl, b):
            return pltpu.make_async_copy(ybuf.at[sl, :, b, :], y_ref.at[b, pl.ds(s * lt, lt), :], out_sem.at[sl, b])

        @pl.when(step == 0)
        def _():
            for b in range(bt):
                in_copy(0, 0, b).start()

        @pl.when(step + 1 < n_steps)
        def _():
            for b in range(bt):
                in_copy(step + 1, 1 - slot, b).start()

        for b in range(bt):
            in_copy(step, slot, b).wait()

        @pl.when(step >= 2)
        def _():
            for b in range(bt):
                out_copy(step - 2, slot, b).wait()

        xt = xbuf[slot].reshape(rows, D_MODEL)
    else:
        xt = jnp.concatenate([x_ref[:, t, :] for t in range(lt)], axis=0)
    u = _rms(xt, g_ref[...])
    ub = u.astype(bf16)

    nbuf = hr_scr.shape[0]

    def project_in(j):
        ch = slice(j * S5_BLOCK_CH, (j + 1) * S5_BLOCK_CH)
        hr_scr[j % nbuf] = _dot(ub[:, ch], wbr_ref[j])
        hi_scr[j % nbuf] = _dot(ub[:, ch], wbi_ref[j])

    def recur(j):
        st = slice(j * S5_BLOCK_ST, (j + 1) * S5_BLOCK_ST)
        a_re = jnp.broadcast_to(are_ref[:, st], (bt, S5_BLOCK_ST))
        a_im = jnp.broadcast_to(aim_ref[:, st], (bt, S5_BLOCK_ST))
        s_re = sr_scr[:, st]
        s_im = si_scr[:, st]
        for t in range(lt):
            r = slice(t * bt, (t + 1) * bt)
            n_re = a_re * s_re - a_im * s_im + hr_scr[j % nbuf, r, :]
            n_im = a_re * s_im + a_im * s_re + hi_scr[j % nbuf, r, :]
            hr_scr[j % nbuf, r, :] = n_re
            hi_scr[j % nbuf, r, :] = n_im
            s_re, s_im = n_re, n_im
        sr_scr[:, st] = s_re
        si_scr[:, st] = s_im

    def project_out(j):
        ch = slice(j * S5_BLOCK_CH, (j + 1) * S5_BLOCK_CH)
        yj = (_dot(hr_scr[j % nbuf].astype(bf16), vre_ref[j]) - _dot(hi_scr[j % nbuf].astype(bf16), vim_ref[j])
              + d_ref[:, ch] * u[:, ch])
        act_scr[:, ch] = jax.nn.gelu(yj).astype(bf16)

    project_in(0)
    for j in range(N_S5_BLOCKS + 1):
        if j + 1 < N_S5_BLOCKS:
            project_in(j + 1)
        if j >= 1:
            project_out(j - 1)
        if j < N_S5_BLOCKS:
            recur(j)

    sre_ref[...] = sr_scr[...]
    sim_ref[...] = si_scr[...]
    x3s = []
    for r in _row_splits(rows):
        ag = _dot(act_scr[r, :], wglu_ref[...])
        x3s.append(xt[r, :] + ag[:, :D_MODEL] * jax.nn.sigmoid(ag[:, D_MODEL:]))
    y = jnp.concatenate([_rms(x4, gfin_ref[...]) for x4 in _ffn(x3s, gffn_ref, wup_ref, wdn_ref)], axis=0)
    if dma_io:
        ybuf[slot] = y.reshape(lt, bt, D_MODEL)
        for b in range(bt):
            out_copy(step, slot, b).start()

        @pl.when(step == n_steps - 1)
        def _():
            @pl.when(step >= 1)
            def _():
                for b in range(bt):
                    out_copy(step - 1, 1 - slot, b).wait()

            for b in range(bt):
                out_copy(step, slot, b).wait()
    else:
        for t in range(lt):
            y_ref[:, t, :] = y[t * bt:(t + 1) * bt, :]


def _s5_layer(x3d, h0_re, h0_im, bt, lt, g_mix, a_re, a_im, wb_re, wb_im, v_re, v_im, d_skip,
              w_glu, g_ffn, w_up, w_down, g_final):
    n_batch, seq_len, _ = x3d.shape
    grid = (n_batch // bt, seq_len // lt)
    sspec = pl.BlockSpec((bt, S5_N), lambda i, t: (i, 0))
    dma_io = n_batch == bt
    if dma_io:
        xspec = pl.BlockSpec(memory_space=pl.ANY)
        io_scratch = [pltpu.VMEM((2, lt, bt, D_MODEL), f32), pltpu.VMEM((2, lt, bt, D_MODEL), f32),
                      pltpu.SemaphoreType.DMA((2, bt)), pltpu.SemaphoreType.DMA((2, bt))]
    else:
        xspec = pl.BlockSpec((bt, lt, D_MODEL), lambda i, t: (i, t, 0))
        io_scratch = []
    consts = (g_mix, a_re, a_im, wb_re, wb_im, v_re, v_im, d_skip, w_glu, g_ffn, w_up, w_down, g_final)
    const_specs = [_layer_spec(c.shape, 1) if c is w_up or c is w_down else _const_spec(c.shape) for c in consts]
    rows = bt * lt
    return pl.pallas_call(
        functools.partial(_s5_body, bt=bt, lt=lt, dma_io=dma_io),
        grid=grid,
        in_specs=[xspec, sspec, sspec] + const_specs,
        out_specs=[xspec, sspec, sspec],
        out_shape=[jax.ShapeDtypeStruct(x3d.shape, f32), jax.ShapeDtypeStruct((n_batch, S5_N), f32),
                   jax.ShapeDtypeStruct((n_batch, S5_N), f32)],
        scratch_shapes=[pltpu.VMEM((bt, S5_N), f32), pltpu.VMEM((bt, S5_N), f32),
                        pltpu.VMEM((3, rows, S5_BLOCK_ST), f32), pltpu.VMEM((3, rows, S5_BLOCK_ST), f32),
                        pltpu.VMEM((rows, D_MODEL), bf16)] + io_scratch,
        compiler_params=_params(2),
        name="l1_s5_ffn",
    )(x3d, h0_re, h0_im, *consts)


def _rep_lanes(x):
    return jnp.tile(x, (1,) * (x.ndim - 1) + (LANES // x.shape[-1],))


def _mlstm_mixer(x, state, p):
    n_batch, seq_len, _ = x.shape
    x2d = x.reshape(n_batch * seq_len, D_MODEL)
    if state is None:
        o, hmix, c_new, n_new, m_new = _l0_prompt(x2d, p["g_mix0"], p["w_in"], p["w_gate"], p["gate_bias"],
                                                  n_batch, seq_len)
        n_new = jnp.swapaxes(n_new[:, :, :N_HEADS], 1, 2)
    else:
        c0, n0, m0 = state
        q, k, v, o, gates = _inproj(x2d, p["g_mix0"], p["w_in"], p["w_gate"], p["gate_bias"])
        hmix, c_new, n_new, m_new = _mlstm_core_sample(
            q, k, v, gates, c0, n0.reshape(n_batch, QK_DIM), _rep_lanes(m0), n_batch, seq_len)
    states = (c_new.reshape(1, n_batch, N_HEADS, DK, DV),
              n_new.reshape(1, n_batch, N_HEADS, DK),
              m_new.reshape(n_batch, LANES)[:, :N_HEADS].reshape(1, n_batch, N_HEADS))
    return (hmix, o, x2d), states


def _s5_stage(x2, shape, state, p):
    n_batch, seq_len, _ = shape
    if state is None:
        h0_re = jnp.zeros((n_batch, S5_N), f32)
        h0_im = h0_re
        bt, lt = n_batch, ROW_TILE // n_batch
    else:
        h0_re, h0_im = (s.reshape(n_batch, S5_N) for s in state)
        bt, lt = ROW_TILE // seq_len, seq_len
    y, s_re, s_im = _s5_layer(x2.reshape(shape), h0_re, h0_im, bt, lt,
                              p["g_mix1"], p["a_re"], p["a_im"], p["wb_re"], p["wb_im"], p["v_re"], p["v_im"],
                              p["d_skip"], p["w_glu"], p["g_ffn1"], p["w_up"], p["w_down"], p["g_final"])
    return y, (s_re.reshape(1, n_batch, S5_GROUPS, S5_STATE), s_im.reshape(1, n_batch, S5_GROUPS, S5_STATE))


def kernel(x_prompt, x_sample, state_mlstm_C, state_mlstm_n, state_mlstm_m, state_s5_re, state_s5_im,
           norm_mix_g, norm_ffn_g, norm_final_g, mlstm_w_in, mlstm_b_i, mlstm_b_f, mlstm_head_norm_g,
           mlstm_w_out, s5_lambda_re, s5_lambda_im, s5_log_dt, s5_b_re, s5_b_im, s5_c_re, s5_c_im,
           s5_d, s5_w_glu, ffn_w_up, ffn_w_down):
    w_in = mlstm_w_in[0]
    n_qkvo = W_QKVO_COLS
    a_re, a_im, wb_re, wb_im, v_re, v_im = _s5_prep(
        s5_lambda_re[0], s5_lambda_im[0], s5_log_dt[0].reshape(S5_GROUPS, 1),
        jnp.swapaxes(s5_b_re[0], 1, 2), jnp.swapaxes(s5_b_im[0], 1, 2), s5_c_re[0], s5_c_im[0])
    p = dict(
        g_mix0=norm_mix_g[0].reshape(1, D_MODEL), g_mix1=norm_mix_g[1].reshape(1, D_MODEL),
        g_ffn0=norm_ffn_g[0].reshape(1, D_MODEL), g_ffn1=norm_ffn_g[1].reshape(1, D_MODEL),
        g_final=norm_final_g.reshape(1, D_MODEL),
        w_in=jnp.swapaxes(w_in, 0, 1).astype(bf16),
        w_gate=jnp.concatenate([_rep_lanes(w_in[:, n_qkvo:n_qkvo + N_HEADS]),
                                _rep_lanes(w_in[:, n_qkvo + N_HEADS:])], axis=1).astype(bf16),
        gate_bias=jnp.concatenate([_rep_lanes(mlstm_b_i[0][None]), _rep_lanes(mlstm_b_f[0][None])], axis=1),
        head_g=mlstm_head_norm_g[0].reshape(1, V_DIM),
        w_out=mlstm_w_out[0].astype(bf16),
        w_up=ffn_w_up.astype(bf16), w_down=ffn_w_down.astype(bf16),
        a_re=a_re.reshape(1, S5_N), a_im=a_im.reshape(1, S5_N),
        wb_re=wb_re, wb_im=wb_im, v_re=v_re, v_im=v_im,
        d_skip=s5_d[0].reshape(1, D_MODEL),
        w_glu=s5_w_glu[0].astype(bf16),
    )
    mix_p, mlstm_p = _mlstm_mixer(x_prompt, None, p)
    mix_s, mlstm_s = _mlstm_mixer(x_sample, (state_mlstm_C[0], state_mlstm_n[0], state_mlstm_m[0]), p)
    x2_p, x2_s = (_outproj_ffn(*mix, p["head_g"], p["w_out"], p["g_ffn0"], p["w_up"], p["w_down"])
                  for mix in (mix_p, mix_s))
    y_p, s5_p = _s5_stage(x2_p, x_prompt.shape, None, p)
    y_s, s5_s = _s5_stage(x2_s, x_sample.shape, (state_s5_re[0], state_s5_im[0]), p)
    return (y_p, y_s) + mlstm_p + s5_p + mlstm_s + s5_s
```

```python
import functools
import types

import jax
import jax.numpy as jnp
import numpy as np
from jax import lax
from jax.experimental import pallas as pl
from jax.experimental.pallas import tpu as pltpu

f32 = jnp.float32
bf16 = jnp.bfloat16

D_MODEL = 1024
N_HEADS = 8
DK = 64
DV = 128
QK_DIM = N_HEADS * DK
V_DIM = N_HEADS * DV
W_QKVO_COLS = 2 * QK_DIM + 2 * V_DIM
D_FF = 4 * D_MODEL
S5_GROUPS = 64
S5_GROUP = 16
S5_STATE = 64
S5_N = S5_GROUPS * S5_STATE
EPS = 1e-6

LANES = 128
ROW_TILE = 512
CHUNK_ROWS = 256
PROMPT_CHUNKS = 2
SEQ_GROUP = 4
FF_CHUNK = 1024
ROW_SPLITS = 2
S5_BLOCK_CH = LANES
S5_BLOCK_ST = S5_BLOCK_CH // S5_GROUP * S5_STATE
N_S5_BLOCKS = D_MODEL // S5_BLOCK_CH
VMEM_LIMIT_BYTES = 56 * 1024 * 1024
NEG_BIG = -1e30
CLAMP_MAX = 2.0 ** 126
LOG_HEADS = N_HEADS.bit_length() - 1
N_BCAST = 3


def _params(n_axes):
    return pltpu.CompilerParams(dimension_semantics=("arbitrary",) * n_axes,
                                vmem_limit_bytes=VMEM_LIMIT_BYTES)


def _const_spec(shape):
    nd = len(shape)
    return pl.BlockSpec(shape, lambda *_: (0,) * nd, pipeline_mode=pl.Buffered(1))


def _layer_spec(shape, layer):
    nd = len(shape)
    return pl.BlockSpec((None,) + tuple(shape[1:]), lambda *_: (layer,) + (0,) * (nd - 1),
                        pipeline_mode=pl.Buffered(1))


def _dot(a, b):
    return jnp.dot(a, b, preferred_element_type=f32)


def _dot_nt(a, b):
    return lax.dot_general(a, b, (((1,), (1,)), ((), ())), preferred_element_type=f32)


def _rms(x, g):
    return x * lax.rsqrt(jnp.mean(x * x, axis=-1, keepdims=True) + EPS) * g


def _row_splits(rows):
    size = rows // ROW_SPLITS
    return [slice(i * size, (i + 1) * size) for i in range(ROW_SPLITS)]


def _after(x, earlier):
    zero = lax.shift_right_logical(lax.shift_right_logical(earlier.astype(jnp.int32), 16), 16)
    return x + zero.astype(x.dtype)


def _ffn(x1s, g_ref, wup_ref, wdn_ref, finish=None):
    n_chunks = D_FF // FF_CHUNK
    xn = [_rms(x1, g_ref[...]).astype(bf16) for x1 in x1s]
    acc = [None] * len(x1s)
    done = [None] * len(x1s)
    for c in range(n_chunks):
        cols = slice(c * FF_CHUNK, (c + 1) * FF_CHUNK)
        for i in range(len(x1s)):
            hid = _dot(xn[i], wup_ref[:, cols])
            hid = jnp.square(jnp.maximum(hid, 0.0))
            if finish is not None and c == n_chunks - 1 and i == len(x1s) - 1:
                for d in done[:i]:
                    hid = _after(hid, d)
            part = _dot(hid.astype(bf16), wdn_ref[cols, :])
            acc[i] = part if acc[i] is None else acc[i] + part
            if c == n_chunks - 1:
                done[i] = x1s[i] + acc[i]
                if finish is not None:
                    done[i] = finish(done[i])
    return done


def _inproj_body(x_ref, g_ref, w_ref, wg_ref, bias_ref, q_ref, k_ref, v_ref, o_ref, gate_ref):
    xn = _rms(x_ref[...], g_ref[...]).astype(bf16)
    q, k, v, o, logi, logf = _inproj_values(xn, w_ref, wg_ref, bias_ref)
    q_ref[...] = q
    k_ref[...] = k
    v_ref[...] = v
    o_ref[...] = o
    gate_ref[:, :LANES] = logi
    gate_ref[:, LANES:] = logf


def _inproj(x2d, g, w_in, w_gate, gate_bias):
    t = x2d.shape[0]
    row = lambda n: pl.BlockSpec((ROW_TILE, n), lambda i: (i, 0))
    return pl.pallas_call(
        _inproj_body,
        grid=(t // ROW_TILE,),
        in_specs=[row(D_MODEL), _const_spec((1, D_MODEL)), _const_spec(w_in.shape),
                  _const_spec(w_gate.shape), _const_spec((1, 2 * LANES))],
        out_specs=[row(QK_DIM), row(QK_DIM), row(V_DIM), row(V_DIM), row(2 * LANES)],
        out_shape=[jax.ShapeDtypeStruct((t, QK_DIM), bf16), jax.ShapeDtypeStruct((t, QK_DIM), f32),
                   jax.ShapeDtypeStruct((t, V_DIM), bf16), jax.ShapeDtypeStruct((t, V_DIM), f32),
                   jax.ShapeDtypeStruct((t, 2 * LANES), f32)],
        compiler_params=_params(1),
        name="l0_inproj",
    )(x2d, g, w_in, w_gate, gate_bias)


def _row_prefix(x, cs, rpos, is_max):
    sh = 1
    while sh < cs:
        prev = pltpu.roll(x, sh, 0)
        if is_max:
            x = jnp.maximum(x, jnp.where(rpos >= sh, prev, NEG_BIG))
        else:
            x = x + jnp.where(rpos >= sh, prev, 0.0)
        sh *= 2
    return x


def _seg_last(x, nseq, cs):
    if nseq == 1:
        return x[cs - 1:cs, :]
    last = x.reshape(nseq, cs, x.shape[1])[:, cs - 1:cs, :]
    return jnp.broadcast_to(last, (nseq, cs, x.shape[1])).reshape(nseq * cs, x.shape[1])


def _gate_algebra(logi, logf, m_in, nseq, cs, sel_ref):
    rows = nseq * cs
    rpos = lax.broadcasted_iota(jnp.int32, (rows, LANES), 0) & (cs - 1)
    b = _row_prefix(logf, cs, rpos, False)
    a = logi - b
    g = jnp.maximum(m_in, _row_prefix(a, cs, rpos, True))
    g_last = _seg_last(g, nseq, cs)
    w_state = jnp.exp(a - g_last)

    group = lax.broadcasted_iota(jnp.int32, (rows, LANES), 1) >> LOG_HEADS
    pieces = jnp.zeros((rows, LANES), f32)
    clamp = jnp.minimum(jnp.exp(-(b + g)), CLAMP_MAX)
    for i, val in enumerate((g, jnp.exp(m_in - g), clamp)):
        hi = val.astype(bf16).astype(f32)
        mid = (val - hi).astype(bf16).astype(f32)
        lo = (val - hi - mid).astype(bf16).astype(f32)
        for j, piece in enumerate((hi, mid, lo)):
            pieces = jnp.where(group == 3 * i + j, piece, pieces)
    bcast = _dot(pieces.astype(bf16), sel_ref[...])
    tile = lambda i, h: bcast[:, (i * N_HEADS + h) * DV:(i * N_HEADS + h + 1) * DV]
    return types.SimpleNamespace(
        a_t=a.T,
        w_state=w_state, w_state_t=w_state.T,
        decay=jnp.exp(m_in - g_last),
        m_new=_seg_last(b, nseq, cs) + g_last,
        g=lambda h: tile(0, h), w_inter=lambda h: tile(1, h), clamp=lambda h: tile(2, h))


def _head_outputs(qk, gates, inter, qn, v_h, h_ref, nseq, cs):
    rows = nseq * cs
    log_cs = cs.bit_length() - 1
    heads = range(N_HEADS)
    chunks = range(len(qk))
    ri = lax.broadcasted_iota(jnp.int32, (rows, rows), 0)
    ci = lax.broadcasted_iota(jnp.int32, (rows, rows), 1)
    causal = ci <= ri
    if nseq > 1:
        causal = jnp.logical_and(causal, (ri >> log_cs) == (ci >> log_cs))
    for c in chunks:
        for h in heads:
            g_rows = jnp.concatenate([gates[c].g(h)] * (rows // DV), axis=1)
            w = jnp.exp(jnp.where(causal, gates[c].a_t[h:h + 1, :] - g_rows, NEG_BIG))
            s = qk[c][h]() * w
            s_sum = jnp.sum(s, axis=1, keepdims=True)
            intra = _dot(s.astype(bf16), v_h[c][h])
            w_inter = gates[c].w_inter(h)
            num = w_inter * inter[c][h] + intra
            den = jnp.maximum(jnp.abs(w_inter * qn[c][h] + s_sum), gates[c].clamp(h))
            h_ref[c * rows:(c + 1) * rows, h * DV:(h + 1) * DV] = num * lax.rsqrt(
                jnp.mean(num * num, axis=1, keepdims=True) + EPS * (den * den))


def _sequence_chunks(q_ref, k_ref, v_ref, gate_ref, sel_ref, h_ref, c_state, n_state, m_in, n_chunks,
                     after_gates=lambda: None):
    rows = CHUNK_ROWS
    heads = range(N_HEADS)
    chunks = range(n_chunks)
    c_state, n_state = list(c_state), list(n_state)
    rsl = lambda c: slice(c * rows, (c + 1) * rows)
    q_h = [[q_ref[rsl(c), h * DK:(h + 1) * DK] for h in heads] for c in chunks]
    k_h = [[k_ref[rsl(c), h * DK:(h + 1) * DK] for h in heads] for c in chunks]
    v_h = [[v_ref[rsl(c), h * DV:(h + 1) * DV] for h in heads] for c in chunks]

    qk = [[functools.partial(_dot_nt, q_h[c][h], k_h[c][h].astype(bf16)) for h in heads] for c in chunks]

    gates = []
    for c in chunks:
        gates.append(_gate_algebra(gate_ref[rsl(c), :LANES], gate_ref[rsl(c), LANES:],
                                   m_in if c == 0 else gates[c - 1].m_new, 1, rows, sel_ref))
    after_gates()

    inter = [[None] * N_HEADS for _ in chunks]
    qn = [[None] * N_HEADS for _ in chunks]
    for c in chunks:
        ga = gates[c]
        k_t = k_ref[rsl(c), :].T
        for h in heads:
            rhs = jnp.concatenate([c_state[h].astype(bf16),
                                   jnp.broadcast_to(n_state[h], (DK, DV)).astype(bf16)], axis=1)
            inter2 = _dot(q_h[c][h], rhs)
            inter[c][h] = inter2[:, :DV]
            qn[c][h] = inter2[:, DV:]
            wk_t = k_t[h * DK:(h + 1) * DK, :] * ga.w_state_t[h:h + 1, :]
            dec = ga.decay[:, h:h + 1]
            c_state[h] = dec * c_state[h] + _dot(wk_t.astype(bf16), v_h[c][h])
            n_state[h] = dec * n_state[h] + jnp.sum(wk_t, axis=1, keepdims=True)

    _head_outputs(qk, gates, inter, qn, v_h, h_ref, 1, rows)
    return c_state, n_state, gates[-1].m_new


def _mlstm_sample_body(q_ref, k_ref, v_ref, gate_ref, sel_ref, c0_ref, n0_ref, m0_ref,
                       h_ref, cout_ref, nout_ref, mout_ref, *, nseq, cs):
    rows = nseq * cs
    log_cs = cs.bit_length() - 1
    log_dk = DK.bit_length() - 1
    heads = range(N_HEADS)
    q_h = [q_ref[:, h * DK:(h + 1) * DK] for h in heads]
    k_h = [k_ref[:, h * DK:(h + 1) * DK] for h in heads]
    v_h = [v_ref[:, h * DV:(h + 1) * DV] for h in heads]
    qk = [functools.partial(_dot_nt, q_h[h], k_h[h].astype(bf16)) for h in heads]
    m_in = jnp.broadcast_to(m0_ref[...][:, None, :], (nseq, cs, LANES)).reshape(rows, LANES)
    ga = _gate_algebra(gate_ref[:, :LANES], gate_ref[:, LANES:], m_in, nseq, cs, sel_ref)

    grp_rows, grp_k = SEQ_GROUP * cs, SEQ_GROUP * DK
    bd_q = ((lax.broadcasted_iota(jnp.int32, (grp_rows, grp_k), 0) >> log_cs)
            == (lax.broadcasted_iota(jnp.int32, (grp_rows, grp_k), 1) >> log_dk))
    bd_k = ((lax.broadcasted_iota(jnp.int32, (nseq * DK, rows), 0) >> log_dk)
            == (lax.broadcasted_iota(jnp.int32, (nseq * DK, rows), 1) >> log_cs))
    k_t = k_ref[...].T
    inter, qn = [], []
    for h in heads:
        c_prev = c0_ref[:, h].reshape(nseq * DK, DV)
        c_bf = c_prev.astype(bf16)
        q32 = q_h[h].astype(f32)
        parts = []
        for j in range(nseq // SEQ_GROUP):
            q_j = q32[j * grp_rows:(j + 1) * grp_rows, :]
            q_bd = jnp.where(bd_q, jnp.concatenate([q_j] * SEQ_GROUP, axis=1), 0.0).astype(bf16)
            parts.append(_dot(q_bd, c_bf[j * grp_k:(j + 1) * grp_k, :]))
        inter.append(jnp.concatenate(parts, axis=0))
        n_prev = n0_ref[:, h * DK:(h + 1) * DK]
        n_rows = jnp.broadcast_to(n_prev[:, None, :], (nseq, cs, DK)).reshape(rows, DK)
        qn.append(jnp.sum(q32 * n_rows, axis=1, keepdims=True))

        wk_t = k_t[h * DK:(h + 1) * DK, :] * ga.w_state_t[h:h + 1, :]
        wk_bd = jnp.where(bd_k, jnp.broadcast_to(wk_t[None], (nseq, DK, rows)).reshape(nseq * DK, rows), 0.0)
        dc = _dot(wk_bd.astype(bf16), v_h[h])
        dec_col = ga.decay[:, h:h + 1]
        dec_rows = jnp.broadcast_to(dec_col.reshape(nseq, 1, cs, 1), (nseq, DK // cs, cs, 1)).reshape(nseq * DK, 1)
        cout_ref[:, h] = (dec_rows * c_prev + dc).reshape(nseq, DK, DV)
        dec_seq = jnp.max(dec_col.reshape(nseq, cs, 1), axis=1)
        nout_ref[:, h * DK:(h + 1) * DK] = dec_seq * n_prev + jnp.sum(
            (ga.w_state[:, h:h + 1] * k_h[h]).reshape(nseq, cs, DK), axis=1)
    mout_ref[...] = jnp.max(ga.m_new.reshape(nseq, cs, LANES), axis=1)
    _head_outputs([qk], [ga], [inter], [qn], [v_h], h_ref, nseq, cs)


def _inproj_values(xn, w_ref, wg_ref, bias_ref):
    q = _dot_nt(xn, w_ref[0:QK_DIM, :]).astype(bf16)
    k = _dot_nt(xn, w_ref[QK_DIM:2 * QK_DIM, :]) * (DK ** -0.5)
    v = _dot_nt(xn, w_ref[2 * QK_DIM:2 * QK_DIM + V_DIM, :]).astype(bf16)
    o = _dot_nt(xn, w_ref[2 * QK_DIM + V_DIM:W_QKVO_COLS, :])
    gz = _dot(xn, wg_ref[...]) + bias_ref[...]
    return q, k, v, o, gz[:, :LANES], jax.nn.log_sigmoid(gz[:, LANES:])


def _l0_prompt_body(x_ref, g_ref, w_ref, wg_ref, bias_ref, sel_ref,
                    o_ref, h_ref, cout_ref, nout_ref, mout_ref,
                    qa, ka, va, ga, qb, kb, vb, gb, c_scr, n_scr, m_scr, *, nt, n_chunks, last):
    step = pl.program_id(0)
    heads = range(N_HEADS)
    sets = ((qa, ka, va, ga), (qb, kb, vb, gb))

    def project(wr):
        q_w, k_w, v_w, g_w = wr
        xn = _rms(x_ref[...], g_ref[...]).astype(bf16)
        q, k, v, o, logi, logf = _inproj_values(xn, w_ref, wg_ref, bias_ref)
        q_w[...] = q
        k_w[...] = k
        v_w[...] = v
        o_ref[...] = o
        g_w[:, :LANES] = logi
        g_w[:, LANES:] = logf

    def core(rd, after_gates):
        first = lax.rem(step + nt - 1, nt) == 0
        c_state = [jnp.where(first, 0.0, c_scr[h]) for h in heads]
        n_state = [jnp.where(first, 0.0, n_scr[h]) for h in heads]
        m_in = jnp.where(first, 0.0, m_scr[...])
        c_state, n_state, m_new = _sequence_chunks(*rd, sel_ref, h_ref, c_state, n_state, m_in, n_chunks,
                                                   after_gates=after_gates)
        for h in heads:
            c_scr[h] = c_state[h]
            n_scr[h] = n_state[h]
        m_scr[...] = m_new

    @pl.when(step == 0)
    def _():
        for ref in (c_scr, n_scr, m_scr):
            ref[...] = jnp.zeros_like(ref)
        project(sets[0])

    for parity in range(2):
        wr, rd = sets[parity], sets[1 - parity]

        @pl.when(jnp.logical_and(jnp.logical_and(step > 0, step < last), lax.rem(step, 2) == parity))
        def _():
            core(rd, functools.partial(project, wr))

    @pl.when(step == last)
    def _():
        core(sets[1 - last % 2], lambda: None)

    @pl.when(jnp.logical_and(step >= 1, lax.rem(step, nt) == 0))
    def _():
        cout_ref[0] = c_scr[...]
        lane = lax.broadcasted_iota(jnp.int32, (DK, LANES), 1)
        n_mat = jnp.zeros((DK, LANES), f32)
        for h in heads:
            n_mat = jnp.where(lane == h, jnp.broadcast_to(n_scr[h], (DK, LANES)), n_mat)
        nout_ref[0] = n_mat
        mout_ref[0] = m_scr[...]


def _bcast_selector():
    src = np.arange(LANES)
    dst = np.arange(N_BCAST * N_HEADS * DV) // DV
    hit = (((src[:, None] >> LOG_HEADS) // 3 == dst[None, :] // N_HEADS)
           & ((src[:, None] & (N_HEADS - 1)) == dst[None, :] % N_HEADS))
    return jnp.asarray(hit, dtype=bf16)


def _l0_prompt(x2d, g, w_in, w_gate, gate_bias, n_batch, seq_len):
    t = x2d.shape[0]
    sel = _bcast_selector()
    rows = PROMPT_CHUNKS * CHUNK_ROWS
    nt = seq_len // rows
    n_blocks = t // rows
    cur = lambda s: (jnp.minimum(s, n_blocks - 1), 0)
    prev = lambda s: (jnp.maximum(s - 1, 0), 0)
    seq = lambda s: jnp.maximum(s - 1, 0) // nt
    bufs = [pltpu.VMEM((rows, QK_DIM), bf16), pltpu.VMEM((rows, QK_DIM), f32),
            pltpu.VMEM((rows, V_DIM), bf16), pltpu.VMEM((rows, 2 * LANES), f32)]
    return pl.pallas_call(
        functools.partial(_l0_prompt_body, nt=nt, n_chunks=PROMPT_CHUNKS, last=n_blocks),
        grid=(n_blocks + 1,),
        in_specs=[pl.BlockSpec((rows, D_MODEL), cur), _const_spec((1, D_MODEL)), _const_spec(w_in.shape),
                  _const_spec(w_gate.shape), _const_spec((1, 2 * LANES)), _const_spec(sel.shape)],
        out_specs=[pl.BlockSpec((rows, V_DIM), cur), pl.BlockSpec((rows, V_DIM), prev),
                   pl.BlockSpec((1, N_HEADS, DK, DV), lambda s: (seq(s), 0, 0, 0)),
                   pl.BlockSpec((1, DK, LANES), lambda s: (seq(s), 0, 0)),
                   pl.BlockSpec((1, 1, LANES), lambda s: (seq(s), 0, 0))],
        out_shape=[jax.ShapeDtypeStruct((t, V_DIM), f32), jax.ShapeDtypeStruct((t, V_DIM), f32),
                   jax.ShapeDtypeStruct((n_batch, N_HEADS, DK, DV), f32),
                   jax.ShapeDtypeStruct((n_batch, DK, LANES), f32),
                   jax.ShapeDtypeStruct((n_batch, 1, LANES), f32)],
        scratch_shapes=bufs + bufs + [pltpu.VMEM((N_HEADS, DK, DV), f32), pltpu.VMEM((N_HEADS, DK, 1), f32),
                                      pltpu.VMEM((1, LANES), f32)],
        compiler_params=_params(1),
        name="l0_inproj_mlstm",
    )(x2d, g, w_in, w_gate, gate_bias, sel)


def _mlstm_core_sample(q, k, v, gates, c0, n0, m0, n_batch, seq_len):
    t = q.shape[0]
    sel = _bcast_selector()
    nseq = CHUNK_ROWS // seq_len
    row = lambda n: pl.BlockSpec((CHUNK_ROWS, n), lambda i: (i, 0))
    cspec = pl.BlockSpec((nseq, N_HEADS, DK, DV), lambda i: (i, 0, 0, 0))
    nspec = pl.BlockSpec((nseq, QK_DIM), lambda i: (i, 0))
    mspec = pl.BlockSpec((nseq, LANES), lambda i: (i, 0))
    return pl.pallas_call(
        functools.partial(_mlstm_sample_body, nseq=nseq, cs=seq_len),
        grid=(t // CHUNK_ROWS,),
        in_specs=[row(QK_DIM), row(QK_DIM), row(V_DIM), row(2 * LANES), _const_spec(sel.shape), cspec, nspec, mspec],
        out_specs=[row(V_DIM), cspec, nspec, mspec],
        out_shape=[jax.ShapeDtypeStruct((t, V_DIM), f32),
                   jax.ShapeDtypeStruct((n_batch, N_HEADS, DK, DV), f32),
                   jax.ShapeDtypeStruct((n_batch, QK_DIM), f32),
                   jax.ShapeDtypeStruct((n_batch, LANES), f32)],
        compiler_params=_params(1),
        name="l0_mlstm_core",
    )(q, k, v, gates, sel, c0, n0, m0)


def _outproj_ffn_body(h_ref, o_ref, x_ref, hg_ref, wout_ref, gffn_ref, wup_ref, wdn_ref, y_ref):
    x1s = []
    for r in _row_splits(h_ref.shape[0]):
        hn = h_ref[r, :] * hg_ref[...] * jax.nn.sigmoid(o_ref[r, :])
        x1s.append(x_ref[r, :] + _dot(hn.astype(bf16), wout_ref[...]))
    for r, y in zip(_row_splits(h_ref.shape[0]), _ffn(x1s, gffn_ref, wup_ref, wdn_ref)):
        y_ref[r, :] = y


def _outproj_ffn(hmix, o, x2d, head_g, w_out, g_ffn, w_up, w_down):
    t = x2d.shape[0]
    row = pl.BlockSpec((ROW_TILE, D_MODEL), lambda i: (i, 0))
    return pl.pallas_call(
        _outproj_ffn_body,
        grid=(t // ROW_TILE,),
        in_specs=[row, row, row, _const_spec((1, V_DIM)), _const_spec(w_out.shape),
                  _const_spec((1, D_MODEL)), _layer_spec(w_up.shape, 0), _layer_spec(w_down.shape, 0)],
        out_specs=row,
        out_shape=jax.ShapeDtypeStruct((t, D_MODEL), f32),
        compiler_params=_params(1),
        name="l0_outproj_ffn",
    )(hmix, o, x2d, head_g, w_out, g_ffn, w_up, w_down)


def _s5_prep_body(lre_ref, lim_ref, ldt_ref, bre_ref, bim_ref, cre_ref, cim_ref,
                  are_ref, aim_ref, wbr_ref, wbi_ref, vre_ref, vim_ref):
    lr = lre_ref[...]
    li = lim_ref[...]
    dt = jnp.exp(ldt_ref[...])
    mag = jnp.exp(lr * dt)
    a_re = mag * jnp.cos(li * dt)
    a_im = mag * jnp.sin(li * dt)
    den = lr * lr + li * li
    z_re = a_re - 1.0
    coef_re = ((z_re * lr + a_im * li) / den)[:, None, :]
    coef_im = ((a_im * lr - z_re * li) / den)[:, None, :]
    br = bre_ref[...]
    bi = bim_ref[...]
    are_ref[...] = a_re
    aim_ref[...] = a_im
    bb = (coef_re * br - coef_im * bi, coef_re * bi + coef_im * br)
    c_t = (cre_ref[...].reshape(S5_GROUPS * S5_GROUP, S5_STATE).T,
           cim_ref[...].reshape(S5_GROUPS * S5_GROUP, S5_STATE).T)

    groups = S5_BLOCK_CH // S5_GROUP
    log_ch, log_st = S5_GROUP.bit_length() - 1, S5_STATE.bit_length() - 1
    shape_b = (S5_BLOCK_CH, S5_BLOCK_ST)
    diag_b = ((lax.broadcasted_iota(jnp.int32, shape_b, 0) >> log_ch)
              == (lax.broadcasted_iota(jnp.int32, shape_b, 1) >> log_st))
    shape_c = (S5_BLOCK_ST, S5_BLOCK_CH)
    diag_c = ((lax.broadcasted_iota(jnp.int32, shape_c, 0) >> log_st)
              == (lax.broadcasted_iota(jnp.int32, shape_c, 1) >> log_ch))
    for j in range(N_S5_BLOCKS):
        for src, dst in zip(bb, (wbr_ref, wbi_ref)):
            blk = src[j * groups:(j + 1) * groups].reshape(S5_BLOCK_CH, S5_STATE)
            dst[j] = jnp.where(diag_b, jnp.concatenate([blk] * groups, axis=1), 0.0).astype(bf16)
        for src, dst in zip(c_t, (vre_ref, vim_ref)):
            blk = src[:, j * S5_BLOCK_CH:(j + 1) * S5_BLOCK_CH]
            rep = jnp.broadcast_to(blk[None], (groups, S5_STATE, S5_BLOCK_CH)).reshape(S5_BLOCK_ST, S5_BLOCK_CH)
            dst[j] = jnp.where(diag_c, rep, 0.0).astype(bf16)


def _s5_prep(lam_re, lam_im, log_dt, b_re_t, b_im_t, c_re, c_im):
    gp = jax.ShapeDtypeStruct((S5_GROUPS, S5_STATE), f32)
    wb = jax.ShapeDtypeStruct((N_S5_BLOCKS, S5_BLOCK_CH, S5_BLOCK_ST), bf16)
    vc = jax.ShapeDtypeStruct((N_S5_BLOCKS, S5_BLOCK_ST, S5_BLOCK_CH), bf16)
    return pl.pallas_call(_s5_prep_body, out_shape=[gp, gp, wb, wb, vc, vc], name="l1_s5_prep")(
        lam_re, lam_im, log_dt, b_re_t, b_im_t, c_re, c_im)


def _s5_body(x_ref, h0r_ref, h0i_ref, g_ref, are_ref, aim_ref, wbr_ref, wbi_ref, vre_ref, vim_ref,
             d_ref, wglu_ref, gffn_ref, wup_ref, wdn_ref, gfin_ref,
             y_ref, sre_ref, sim_ref, sr_scr, si_scr, hr_scr, hi_scr, act_scr, *io_scr, bt, lt, dma_io):
    rows = bt * lt
    step = pl.program_id(1)
    n_steps = pl.num_programs(1)

    @pl.when(step == 0)
    def _():
        sr_scr[...] = h0r_ref[...]
        si_scr[...] = h0i_ref[...]

    if dma_io:
        xbuf, ybuf, in_sem, out_sem = io_scr
        slot = lax.rem(step, 2)

        def in_copy(s, sl, b):
            return pltpu.make_async_copy(x_ref.at[b, pl.ds(s * lt, lt), :], xbuf.at[sl, :, b, :], in_sem.at[sl, b])

        def out_copy(s, sl, b):
            return pltpu.make_async_copy(ybuf.at[sl, :, b, :], y_ref.at[b, pl.ds(s * lt, lt), :], out_sem.at[sl, b])

        @pl.when(step == 0)
        def _():
            for b in range(bt):
                in_copy(0, 0, b).start()

        @pl.when(step + 1 < n_steps)
        def _():
            for b in range(bt):
                in_copy(step + 1, 1 - slot, b).start()

        for b in range(bt):
            in_copy(step, slot, b).wait()

        @pl.when(step >= 2)
        def _():
            for b in range(bt):
                out_copy(step - 2, slot, b).wait()

        xt = xbuf[slot].reshape(rows, D_MODEL)
    else:
        xt = jnp.concatenate([x_ref[:, t, :] for t in range(lt)], axis=0)
    u = _rms(xt, g_ref[...])
    ub = u.astype(bf16)

    nbuf = hr_scr.shape[0]

    def project_in(j):
        ch = slice(j * S5_BLOCK_CH, (j + 1) * S5_BLOCK_CH)
        hr_scr[j % nbuf] = _dot(ub[:, ch], wbr_ref[j])
        hi_scr[j % nbuf] = _dot(ub[:, ch], wbi_ref[j])

    def recur(j):
        st = slice(j * S5_BLOCK_ST, (j + 1) * S5_BLOCK_ST)
        a_re = jnp.broadcast_to(are_ref[:, st], (bt, S5_BLOCK_ST))
        a_im = jnp.broadcast_to(aim_ref[:, st], (bt, S5_BLOCK_ST))
        s_re = sr_scr[:, st]
        s_im = si_scr[:, st]
        for t in range(lt):
            r = slice(t * bt, (t + 1) * bt)
            n_re = a_re * s_re - a_im * s_im + hr_scr[j % nbuf, r, :]
            n_im = a_re * s_im + a_im * s_re + hi_scr[j % nbuf, r, :]
            hr_scr[j % nbuf, r, :] = n_re
            hi_scr[j % nbuf, r, :] = n_im
            s_re, s_im = n_re, n_im
        sr_scr[:, st] = s_re
        si_scr[:, st] = s_im

    def project_out(j):
        ch = slice(j * S5_BLOCK_CH, (j + 1) * S5_BLOCK_CH)
        yj = (_dot(hr_scr[j % nbuf].astype(bf16), vre_ref[j]) - _dot(hi_scr[j % nbuf].astype(bf16), vim_ref[j])
              + d_ref[:, ch] * u[:, ch])
        act_scr[:, ch] = jax.nn.gelu(yj).astype(bf16)

    project_in(0)
    for j in range(N_S5_BLOCKS + 1):
        if j + 1 < N_S5_BLOCKS:
            project_in(j + 1)
        if j >= 1:
            project_out(j - 1)
        if j < N_S5_BLOCKS:
            recur(j)

    sre_ref[...] = sr_scr[...]
    sim_ref[...] = si_scr[...]
    x3s = []
    for r in _row_splits(rows):
        ag = _dot(act_scr[r, :], wglu_ref[...])
        x3s.append(xt[r, :] + ag[:, :D_MODEL] * jax.nn.sigmoid(ag[:, D_MODEL:]))
    y = jnp.concatenate(_ffn(x3s, gffn_ref, wup_ref, wdn_ref, finish=lambda x4: _rms(x4, gfin_ref[...])), axis=0)
    if dma_io:
        ybuf[slot] = y.reshape(lt, bt, D_MODEL)
        for b in range(bt):
            out_copy(step, slot, b).start()

        @pl.when(step == n_steps - 1)
        def _():
            @pl.when(step >= 1)
            def _():
                for b in range(bt):
                    out_copy(step - 1, 1 - slot, b).wait()

            for b in range(bt):
                out_copy(step, slot, b).wait()
    else:
        for t in range(lt):
            y_ref[:, t, :] = y[t * bt:(t + 1) * bt, :]


def _s5_layer(x3d, h0_re, h0_im, bt, lt, g_mix, a_re, a_im, wb_re, wb_im, v_re, v_im, d_skip,
              w_glu, g_ffn, w_up, w_down, g_final):
    n_batch, seq_len, _ = x3d.shape
    grid = (n_batch // bt, seq_len // lt)
    sspec = pl.BlockSpec((bt, S5_N), lambda i, t: (i, 0))
    dma_io = n_batch == bt
    if dma_io:
        xspec = pl.BlockSpec(memory_space=pl.ANY)
        io_scratch = [pltpu.VMEM((2, lt, bt, D_MODEL), f32), pltpu.VMEM((2, lt, bt, D_MODEL), f32),
                      pltpu.SemaphoreType.DMA((2, bt)), pltpu.SemaphoreType.DMA((2, bt))]
    else:
        xspec = pl.BlockSpec((bt, lt, D_MODEL), lambda i, t: (i, t, 0))
        io_scratch = []
    consts = (g_mix, a_re, a_im, wb_re, wb_im, v_re, v_im, d_skip, w_glu, g_ffn, w_up, w_down, g_final)
    const_specs = [_layer_spec(c.shape, 1) if c is w_up or c is w_down else _const_spec(c.shape) for c in consts]
    rows = bt * lt
    return pl.pallas_call(
        functools.partial(_s5_body, bt=bt, lt=lt, dma_io=dma_io),
        grid=grid,
        in_specs=[xspec, sspec, sspec] + const_specs,
        out_specs=[xspec, sspec, sspec],
        out_shape=[jax.ShapeDtypeStruct(x3d.shape, f32), jax.ShapeDtypeStruct((n_batch, S5_N), f32),
                   jax.ShapeDtypeStruct((n_batch, S5_N), f32)],
        scratch_shapes=[pltpu.VMEM((bt, S5_N), f32), pltpu.VMEM((bt, S5_N), f32),
                        pltpu.VMEM((3, rows, S5_BLOCK_ST), f32), pltpu.VMEM((3, rows, S5_BLOCK_ST), f32),
                        pltpu.VMEM((rows, D_MODEL), bf16)] + io_scratch,
        compiler_params=_params(2),
        name="l1_s5_ffn",
    )(x3d, h0_re, h0_im, *consts)


def _rep_lanes(x):
    return jnp.tile(x, (1,) * (x.ndim - 1) + (LANES // x.shape[-1],))


def _mlstm_mixer(x, state, p):
    n_batch, seq_len, _ = x.shape
    x2d = x.reshape(n_batch * seq_len, D_MODEL)
    if state is None:
        o, hmix, c_new, n_new, m_new = _l0_prompt(x2d, p["g_mix0"], p["w_in"], p["w_gate"], p["gate_bias"],
                                                  n_batch, seq_len)
        n_new = jnp.swapaxes(n_new[:, :, :N_HEADS], 1, 2)
    else:
        c0, n0, m0 = state
        q, k, v, o, gates = _inproj(x2d, p["g_mix0"], p["w_in"], p["w_gate"], p["gate_bias"])
        hmix, c_new, n_new, m_new = _mlstm_core_sample(
            q, k, v, gates, c0, n0.reshape(n_batch, QK_DIM), _rep_lanes(m0), n_batch, seq_len)
    states = (c_new.reshape(1, n_batch, N_HEADS, DK, DV),
              n_new.reshape(1, n_batch, N_HEADS, DK),
              m_new.reshape(n_batch, LANES)[:, :N_HEADS].reshape(1, n_batch, N_HEADS))
    return (hmix, o, x2d), states


def _s5_stage(x2, shape, state, p):
    n_batch, seq_len, _ = shape
    if state is None:
        h0_re = jnp.zeros((n_batch, S5_N), f32)
        h0_im = h0_re
        bt, lt = n_batch, ROW_TILE // n_batch
    else:
        h0_re, h0_im = (s.reshape(n_batch, S5_N) for s in state)
        bt, lt = ROW_TILE // seq_len, seq_len
    y, s_re, s_im = _s5_layer(x2.reshape(shape), h0_re, h0_im, bt, lt,
                              p["g_mix1"], p["a_re"], p["a_im"], p["wb_re"], p["wb_im"], p["v_re"], p["v_im"],
                              p["d_skip"], p["w_glu"], p["g_ffn1"], p["w_up"], p["w_down"], p["g_final"])
    return y, (s_re.reshape(1, n_batch, S5_GROUPS, S5_STATE), s_im.reshape(1, n_batch, S5_GROUPS, S5_STATE))


def kernel(x_prompt, x_sample, state_mlstm_C, state_mlstm_n, state_mlstm_m, state_s5_re, state_s5_im,
           norm_mix_g, norm_ffn_g, norm_final_g, mlstm_w_in, mlstm_b_i, mlstm_b_f, mlstm_head_norm_g,
           mlstm_w_out, s5_lambda_re, s5_lambda_im, s5_log_dt, s5_b_re, s5_b_im, s5_c_re, s5_c_im,
           s5_d, s5_w_glu, ffn_w_up, ffn_w_down):
    w_in = mlstm_w_in[0]
    n_qkvo = W_QKVO_COLS
    a_re, a_im, wb_re, wb_im, v_re, v_im = _s5_prep(
        s5_lambda_re[0], s5_lambda_im[0], s5_log_dt[0].reshape(S5_GROUPS, 1),
        jnp.swapaxes(s5_b_re[0], 1, 2), jnp.swapaxes(s5_b_im[0], 1, 2), s5_c_re[0], s5_c_im[0])
    p = dict(
        g_mix0=norm_mix_g[0].reshape(1, D_MODEL), g_mix1=norm_mix_g[1].reshape(1, D_MODEL),
        g_ffn0=norm_ffn_g[0].reshape(1, D_MODEL), g_ffn1=norm_ffn_g[1].reshape(1, D_MODEL),
        g_final=norm_final_g.reshape(1, D_MODEL),
        w_in=jnp.swapaxes(w_in, 0, 1).astype(bf16),
        w_gate=jnp.concatenate([_rep_lanes(w_in[:, n_qkvo:n_qkvo + N_HEADS]),
                                _rep_lanes(w_in[:, n_qkvo + N_HEADS:])], axis=1).astype(bf16),
        gate_bias=jnp.concatenate([_rep_lanes(mlstm_b_i[0][None]), _rep_lanes(mlstm_b_f[0][None])], axis=1),
        head_g=mlstm_head_norm_g[0].reshape(1, V_DIM),
        w_out=mlstm_w_out[0].astype(bf16),
        w_up=ffn_w_up.astype(bf16), w_down=ffn_w_down.astype(bf16),
        a_re=a_re.reshape(1, S5_N), a_im=a_im.reshape(1, S5_N),
        wb_re=wb_re, wb_im=wb_im, v_re=v_re, v_im=v_im,
        d_skip=s5_d[0].reshape(1, D_MODEL),
        w_glu=s5_w_glu[0].astype(bf16),
    )
    mix_p, mlstm_p = _mlstm_mixer(x_prompt, None, p)
    mix_s, mlstm_s = _mlstm_mixer(x_sample, (state_mlstm_C[0], state_mlstm_n[0], state_mlstm_m[0]), p)
    x2_p, x2_s = (_outproj_ffn(*mix, p["head_g"], p["w_out"], p["g_ffn0"], p["w_up"], p["w_down"])
                  for mix in (mix_p, mix_s))
    y_p, s5_p = _s5_stage(x2_p, x_prompt.shape, None, p)
    y_s, s5_s = _s5_stage(x2_s, x_sample.shape, (state_s5_re[0], state_s5_im[0]), p)
    return (y_p, y_s) + mlstm_p + s5_p + mlstm_s + s5_s
```

```python
import functools
import types

import jax
import jax.numpy as jnp
import numpy as np
from jax import lax
from jax.experimental import pallas as pl
from jax.experimental.pallas import tpu as pltpu

f32 = jnp.float32
bf16 = jnp.bfloat16

D_MODEL = 1024
N_HEADS = 8
DK = 64
DV = 128
QK_DIM = N_HEADS * DK
V_DIM = N_HEADS * DV
W_QKVO_COLS = 2 * QK_DIM + 2 * V_DIM
D_FF = 4 * D_MODEL
S5_GROUPS = 64
S5_GROUP = 16
S5_STATE = 64
S5_N = S5_GROUPS * S5_STATE
EPS = 1e-6

LANES = 128
ROW_TILE = 512
CHUNK_ROWS = 256
PROMPT_CHUNKS = 2
SEQ_GROUP = 4
FF_CHUNK = 1024
ROW_SPLITS = 2
S5_BLOCK_CH = LANES
S5_BLOCK_ST = S5_BLOCK_CH // S5_GROUP * S5_STATE
N_S5_BLOCKS = D_MODEL // S5_BLOCK_CH
VMEM_LIMIT_BYTES = 56 * 1024 * 1024
NEG_BIG = -1e30
CLAMP_MAX = 2.0 ** 126
LOG_HEADS = N_HEADS.bit_length() - 1
N_BCAST = 3


def _params(n_axes):
    return pltpu.CompilerParams(dimension_semantics=("arbitrary",) * n_axes,
                                vmem_limit_bytes=VMEM_LIMIT_BYTES)


def _const_spec(shape):
    nd = len(shape)
    return pl.BlockSpec(shape, lambda *_: (0,) * nd, pipeline_mode=pl.Buffered(1))


def _layer_spec(shape, layer):
    nd = len(shape)
    return pl.BlockSpec((None,) + tuple(shape[1:]), lambda *_: (layer,) + (0,) * (nd - 1),
                        pipeline_mode=pl.Buffered(1))


def _dot(a, b):
    return jnp.dot(a, b, preferred_element_type=f32)


def _dot_nt(a, b):
    return lax.dot_general(a, b, (((1,), (1,)), ((), ())), preferred_element_type=f32)


def _rms(x, g):
    return x * lax.rsqrt(jnp.mean(x * x, axis=-1, keepdims=True) + EPS) * g


def _row_splits(rows):
    size = rows // ROW_SPLITS
    return [slice(i * size, (i + 1) * size) for i in range(ROW_SPLITS)]


def _after(x, earlier):
    zero = lax.shift_right_logical(lax.shift_right_logical(earlier.astype(jnp.int32), 16), 16)
    return x + zero.astype(x.dtype)


def _ffn(x1s, g_ref, wup_ref, wdn_ref, finish=None):
    n_chunks = D_FF // FF_CHUNK
    xn = [_rms(x1, g_ref[...]).astype(bf16) for x1 in x1s]
    acc = [None] * len(x1s)
    done = [None] * len(x1s)
    for c in range(n_chunks):
        cols = slice(c * FF_CHUNK, (c + 1) * FF_CHUNK)
        for i in range(len(x1s)):
            hid = _dot(xn[i], wup_ref[:, cols])
            hid = jnp.square(jnp.maximum(hid, 0.0))
            if finish is not None and c == n_chunks - 1 and i == len(x1s) - 1:
                for d in done[:i]:
                    hid = _after(hid, d)
            part = _dot(hid.astype(bf16), wdn_ref[cols, :])
            acc[i] = part if acc[i] is None else acc[i] + part
            if c == n_chunks - 1:
                done[i] = x1s[i] + acc[i]
                if finish is not None:
                    done[i] = finish(done[i])
    return done


def _inproj_body(x_ref, g_ref, w_ref, wg_ref, bias_ref, q_ref, k_ref, v_ref, o_ref, gate_ref):
    xn = _rms(x_ref[...], g_ref[...]).astype(bf16)
    q, k, v, o, logi, logf = _inproj_values(xn, w_ref, wg_ref, bias_ref)
    q_ref[...] = q
    k_ref[...] = k
    v_ref[...] = v
    o_ref[...] = o
    gate_ref[:, :LANES] = logi
    gate_ref[:, LANES:] = logf


def _inproj(x2d, g, w_in, w_gate, gate_bias):
    t = x2d.shape[0]
    row = lambda n: pl.BlockSpec((ROW_TILE, n), lambda i: (i, 0))
    return pl.pallas_call(
        _inproj_body,
        grid=(t // ROW_TILE,),
        in_specs=[row(D_MODEL), _const_spec((1, D_MODEL)), _const_spec(w_in.shape),
                  _const_spec(w_gate.shape), _const_spec((1, 2 * LANES))],
        out_specs=[row(QK_DIM), row(QK_DIM), row(V_DIM), row(V_DIM), row(2 * LANES)],
        out_shape=[jax.ShapeDtypeStruct((t, QK_DIM), bf16), jax.ShapeDtypeStruct((t, QK_DIM), f32),
                   jax.ShapeDtypeStruct((t, V_DIM), bf16), jax.ShapeDtypeStruct((t, V_DIM), f32),
                   jax.ShapeDtypeStruct((t, 2 * LANES), f32)],
        compiler_params=_params(1),
        name="l0_inproj",
    )(x2d, g, w_in, w_gate, gate_bias)


def _row_prefix(x, cs, rpos, is_max):
    sh = 1
    while sh < cs:
        prev = pltpu.roll(x, sh, 0)
        if is_max:
            x = jnp.maximum(x, jnp.where(rpos >= sh, prev, NEG_BIG))
        else:
            x = x + jnp.where(rpos >= sh, prev, 0.0)
        sh *= 2
    return x


def _seg_last(x, nseq, cs):
    if nseq == 1:
        return x[cs - 1:cs, :]
    last = x.reshape(nseq, cs, x.shape[1])[:, cs - 1:cs, :]
    return jnp.broadcast_to(last, (nseq, cs, x.shape[1])).reshape(nseq * cs, x.shape[1])


def _gate_algebra(logi, logf, m_in, nseq, cs, sel_ref):
    rows = nseq * cs
    rpos = lax.broadcasted_iota(jnp.int32, (rows, LANES), 0) & (cs - 1)
    b = _row_prefix(logf, cs, rpos, False)
    a = logi - b
    g = jnp.maximum(m_in, _row_prefix(a, cs, rpos, True))
    g_last = _seg_last(g, nseq, cs)
    w_state = jnp.exp(a - g_last)

    group = lax.broadcasted_iota(jnp.int32, (rows, LANES), 1) >> LOG_HEADS
    pieces = jnp.zeros((rows, LANES), f32)
    clamp = jnp.minimum(jnp.exp(-(b + g)), CLAMP_MAX)
    for i, val in enumerate((g, jnp.exp(m_in - g), clamp)):
        hi = val.astype(bf16).astype(f32)
        mid = (val - hi).astype(bf16).astype(f32)
        lo = (val - hi - mid).astype(bf16).astype(f32)
        for j, piece in enumerate((hi, mid, lo)):
            pieces = jnp.where(group == 3 * i + j, piece, pieces)
    bcast = _dot(pieces.astype(bf16), sel_ref[...])
    tile = lambda i, h: bcast[:, (i * N_HEADS + h) * DV:(i * N_HEADS + h + 1) * DV]
    return types.SimpleNamespace(
        a_t=a.T,
        w_state=w_state, w_state_t=w_state.T,
        decay=jnp.exp(m_in - g_last),
        m_new=_seg_last(b, nseq, cs) + g_last,
        g=lambda h: tile(0, h), w_inter=lambda h: tile(1, h), clamp=lambda h: tile(2, h))


def _head_outputs(qk, gates, inter, qn, v_h, h_ref, nseq, cs):
    rows = nseq * cs
    log_cs = cs.bit_length() - 1
    heads = range(N_HEADS)
    chunks = range(len(qk))
    ri = lax.broadcasted_iota(jnp.int32, (rows, rows), 0)
    ci = lax.broadcasted_iota(jnp.int32, (rows, rows), 1)
    causal = ci <= ri
    if nseq > 1:
        causal = jnp.logical_and(causal, (ri >> log_cs) == (ci >> log_cs))
    for c in chunks:
        for h in heads:
            g_rows = jnp.concatenate([gates[c].g(h)] * (rows // DV), axis=1)
            w = jnp.exp(jnp.where(causal, gates[c].a_t[h:h + 1, :] - g_rows, NEG_BIG))
            s = qk[c][h]() * w
            s_sum = jnp.sum(s, axis=1, keepdims=True)
            intra = _dot(s.astype(bf16), v_h[c][h])
            w_inter = gates[c].w_inter(h)
            num = w_inter * inter[c][h] + intra
            den = jnp.maximum(jnp.abs(w_inter * qn[c][h] + s_sum), gates[c].clamp(h))
            h_ref[c * rows:(c + 1) * rows, h * DV:(h + 1) * DV] = num * lax.rsqrt(
                jnp.mean(num * num, axis=1, keepdims=True) + EPS * (den * den))


def _sequence_chunks(q_ref, k_ref, v_ref, gate_ref, sel_ref, h_ref, c_state, n_state, m_in, n_chunks,
                     after_gates=lambda: None):
    rows = CHUNK_ROWS
    heads = range(N_HEADS)
    chunks = range(n_chunks)
    c_state, n_state = list(c_state), list(n_state)
    rsl = lambda c: slice(c * rows, (c + 1) * rows)
    q_h = [[q_ref[rsl(c), h * DK:(h + 1) * DK] for h in heads] for c in chunks]
    k_h = [[k_ref[rsl(c), h * DK:(h + 1) * DK] for h in heads] for c in chunks]
    v_h = [[v_ref[rsl(c), h * DV:(h + 1) * DV] for h in heads] for c in chunks]

    qk = [[functools.partial(_dot_nt, q_h[c][h], k_h[c][h].astype(bf16)) for h in heads] for c in chunks]

    gates = []
    for c in chunks:
        gates.append(_gate_algebra(gate_ref[rsl(c), :LANES], gate_ref[rsl(c), LANES:],
                                   m_in if c == 0 else gates[c - 1].m_new, 1, rows, sel_ref))
    after_gates()

    inter = [[None] * N_HEADS for _ in chunks]
    qn = [[None] * N_HEADS for _ in chunks]
    for c in chunks:
        ga = gates[c]
        k_t = k_ref[rsl(c), :].T
        for h in heads:
            rhs = jnp.concatenate([c_state[h].astype(bf16),
                                   jnp.broadcast_to(n_state[h], (DK, DV)).astype(bf16)], axis=1)
            inter2 = _dot(q_h[c][h], rhs)
            inter[c][h] = inter2[:, :DV]
            qn[c][h] = inter2[:, DV:]
            wk_t = k_t[h * DK:(h + 1) * DK, :] * ga.w_state_t[h:h + 1, :]
            dec = ga.decay[:, h:h + 1]
            c_state[h] = dec * c_state[h] + _dot(wk_t.astype(bf16), v_h[c][h])
            n_state[h] = dec * n_state[h] + jnp.sum(wk_t, axis=1, keepdims=True)

    _head_outputs(qk, gates, inter, qn, v_h, h_ref, 1, rows)
    return c_state, n_state, gates[-1].m_new


def _mlstm_sample_body(q_ref, k_ref, v_ref, gate_ref, sel_ref, c0_ref, n0_ref, m0_ref,
                       h_ref, cout_ref, nout_ref, mout_ref, *, nseq, cs):
    rows = nseq * cs
    log_cs = cs.bit_length() - 1
    log_dk = DK.bit_length() - 1
    heads = range(N_HEADS)
    q_h = [q_ref[:, h * DK:(h + 1) * DK] for h in heads]
    k_h = [k_ref[:, h * DK:(h + 1) * DK] for h in heads]
    v_h = [v_ref[:, h * DV:(h + 1) * DV] for h in heads]
    qk = [functools.partial(_dot_nt, q_h[h], k_h[h].astype(bf16)) for h in heads]
    m_in = jnp.broadcast_to(m0_ref[...][:, None, :], (nseq, cs, LANES)).reshape(rows, LANES)
    ga = _gate_algebra(gate_ref[:, :LANES], gate_ref[:, LANES:], m_in, nseq, cs, sel_ref)

    grp_rows, grp_k = SEQ_GROUP * cs, SEQ_GROUP * DK
    bd_q = ((lax.broadcasted_iota(jnp.int32, (grp_rows, grp_k), 0) >> log_cs)
            == (lax.broadcasted_iota(jnp.int32, (grp_rows, grp_k), 1) >> log_dk))
    bd_k = ((lax.broadcasted_iota(jnp.int32, (nseq * DK, rows), 0) >> log_dk)
            == (lax.broadcasted_iota(jnp.int32, (nseq * DK, rows), 1) >> log_cs))
    k_t = k_ref[...].T
    inter, qn = [], []
    for h in heads:
        c_prev = c0_ref[:, h].reshape(nseq * DK, DV)
        c_bf = c_prev.astype(bf16)
        q32 = q_h[h].astype(f32)
        parts = []
        for j in range(nseq // SEQ_GROUP):
            q_j = q32[j * grp_rows:(j + 1) * grp_rows, :]
            q_bd = jnp.where(bd_q, jnp.concatenate([q_j] * SEQ_GROUP, axis=1), 0.0).astype(bf16)
            parts.append(_dot(q_bd, c_bf[j * grp_k:(j + 1) * grp_k, :]))
        inter.append(jnp.concatenate(parts, axis=0))
        n_prev = n0_ref[:, h * DK:(h + 1) * DK]
        n_rows = jnp.broadcast_to(n_prev[:, None, :], (nseq, cs, DK)).reshape(rows, DK)
        qn.append(jnp.sum(q32 * n_rows, axis=1, keepdims=True))

        wk_t = k_t[h * DK:(h + 1) * DK, :] * ga.w_state_t[h:h + 1, :]
        wk_bd = jnp.where(bd_k, jnp.broadcast_to(wk_t[None], (nseq, DK, rows)).reshape(nseq * DK, rows), 0.0)
        dc = _dot(wk_bd.astype(bf16), v_h[h])
        dec_col = ga.decay[:, h:h + 1]
        dec_rows = jnp.broadcast_to(dec_col.reshape(nseq, 1, cs, 1), (nseq, DK // cs, cs, 1)).reshape(nseq * DK, 1)
        cout_ref[:, h] = (dec_rows * c_prev + dc).reshape(nseq, DK, DV)
        dec_seq = jnp.max(dec_col.reshape(nseq, cs, 1), axis=1)
        nout_ref[:, h * DK:(h + 1) * DK] = dec_seq * n_prev + jnp.sum(
            (ga.w_state[:, h:h + 1] * k_h[h]).reshape(nseq, cs, DK), axis=1)
    mout_ref[...] = jnp.max(ga.m_new.reshape(nseq, cs, LANES), axis=1)
    _head_outputs([qk], [ga], [inter], [qn], [v_h], h_ref, nseq, cs)


def _inproj_values(xn, w_ref, wg_ref, bias_ref):
    q = _dot_nt(xn, w_ref[0:QK_DIM, :]).astype(bf16)
    k = _dot_nt(xn, w_ref[QK_DIM:2 * QK_DIM, :]) * (DK ** -0.5)
    v = _dot_nt(xn, w_ref[2 * QK_DIM:2 * QK_DIM + V_DIM, :]).astype(bf16)
    o = _dot_nt(xn, w_ref[2 * QK_DIM + V_DIM:W_QKVO_COLS, :])
    gz = _dot(xn, wg_ref[...]) + bias_ref[...]
    return q, k, v, o, gz[:, :LANES], jax.nn.log_sigmoid(gz[:, LANES:])


def _l0_prompt_body(x_ref, g_ref, w_ref, wg_ref, bias_ref, sel_ref,
                    o_ref, h_ref, cout_ref, nout_ref, mout_ref,
                    qa, ka, va, ga, qb, kb, vb, gb, c_scr, n_scr, m_scr, *, nt, n_chunks, last):
    step = pl.program_id(0)
    heads = range(N_HEADS)
    sets = ((qa, ka, va, ga), (qb, kb, vb, gb))

    def project(wr):
        q_w, k_w, v_w, g_w = wr
        xn = _rms(x_ref[...], g_ref[...]).astype(bf16)
        q, k, v, o, logi, logf = _inproj_values(xn, w_ref, wg_ref, bias_ref)
        q_w[...] = q
        k_w[...] = k
        v_w[...] = v
        o_ref[...] = o
        g_w[:, :LANES] = logi
        g_w[:, LANES:] = logf

    def core(rd, after_gates):
        first = lax.rem(step + nt - 1, nt) == 0
        c_state = [jnp.where(first, 0.0, c_scr[h]) for h in heads]
        n_state = [jnp.where(first, 0.0, n_scr[h]) for h in heads]
        m_in = jnp.where(first, 0.0, m_scr[...])
        c_state, n_state, m_new = _sequence_chunks(*rd, sel_ref, h_ref, c_state, n_state, m_in, n_chunks,
                                                   after_gates=after_gates)
        for h in heads:
            c_scr[h] = c_state[h]
            n_scr[h] = n_state[h]
        m_scr[...] = m_new

    @pl.when(step == 0)
    def _():
        for ref in (c_scr, n_scr, m_scr):
            ref[...] = jnp.zeros_like(ref)
        project(sets[0])

    for parity in range(2):
        wr, rd = sets[parity], sets[1 - parity]

        @pl.when(jnp.logical_and(jnp.logical_and(step > 0, step < last), lax.rem(step, 2) == parity))
        def _():
            core(rd, functools.partial(project, wr))

    @pl.when(step == last)
    def _():
        core(sets[1 - last % 2], lambda: None)

    @pl.when(jnp.logical_and(step >= 1, lax.rem(step, nt) == 0))
    def _():
        cout_ref[0] = c_scr[...]
        lane = lax.broadcasted_iota(jnp.int32, (DK, LANES), 1)
        n_mat = jnp.zeros((DK, LANES), f32)
        for h in heads:
            n_mat = jnp.where(lane == h, jnp.broadcast_to(n_scr[h], (DK, LANES)), n_mat)
        nout_ref[0] = n_mat
        mout_ref[0] = m_scr[...]


def _bcast_selector():
    src = np.arange(LANES)
    dst = np.arange(N_BCAST * N_HEADS * DV) // DV
    hit = (((src[:, None] >> LOG_HEADS) // 3 == dst[None, :] // N_HEADS)
           & ((src[:, None] & (N_HEADS - 1)) == dst[None, :] % N_HEADS))
    return jnp.asarray(hit, dtype=bf16)


def _l0_prompt(x2d, g, w_in, w_gate, gate_bias, n_batch, seq_len):
    t = x2d.shape[0]
    sel = _bcast_selector()
    rows = PROMPT_CHUNKS * CHUNK_ROWS
    nt = seq_len // rows
    n_blocks = t // rows
    cur = lambda s: (jnp.minimum(s, n_blocks - 1), 0)
    prev = lambda s: (jnp.maximum(s - 1, 0), 0)
    seq = lambda s: jnp.maximum(s - 1, 0) // nt
    bufs = [pltpu.VMEM((rows, QK_DIM), bf16), pltpu.VMEM((rows, QK_DIM), f32),
            pltpu.VMEM((rows, V_DIM), bf16), pltpu.VMEM((rows, 2 * LANES), f32)]
    return pl.pallas_call(
        functools.partial(_l0_prompt_body, nt=nt, n_chunks=PROMPT_CHUNKS, last=n_blocks),
        grid=(n_blocks + 1,),
        in_specs=[pl.BlockSpec((rows, D_MODEL), cur), _const_spec((1, D_MODEL)), _const_spec(w_in.shape),
                  _const_spec(w_gate.shape), _const_spec((1, 2 * LANES)), _const_spec(sel.shape)],
        out_specs=[pl.BlockSpec((rows, V_DIM), cur), pl.BlockSpec((rows, V_DIM), prev),
                   pl.BlockSpec((1, N_HEADS, DK, DV), lambda s: (seq(s), 0, 0, 0)),
                   pl.BlockSpec((1, DK, LANES), lambda s: (seq(s), 0, 0)),
                   pl.BlockSpec((1, 1, LANES), lambda s: (seq(s), 0, 0))],
        out_shape=[jax.ShapeDtypeStruct((t, V_DIM), f32), jax.ShapeDtypeStruct((t, V_DIM), f32),
                   jax.ShapeDtypeStruct((n_batch, N_HEADS, DK, DV), f32),
                   jax.ShapeDtypeStruct((n_batch, DK, LANES), f32),
                   jax.ShapeDtypeStruct((n_batch, 1, LANES), f32)],
        scratch_shapes=bufs + bufs + [pltpu.VMEM((N_HEADS, DK, DV), f32), pltpu.VMEM((N_HEADS, DK, 1), f32),
                                      pltpu.VMEM((1, LANES), f32)],
        compiler_params=_params(1),
        name="l0_inproj_mlstm",
    )(x2d, g, w_in, w_gate, gate_bias, sel)


def _mlstm_core_sample(q, k, v, gates, c0, n0, m0, n_batch, seq_len):
    t = q.shape[0]
    sel = _bcast_selector()
    nseq = CHUNK_ROWS // seq_len
    row = lambda n: pl.BlockSpec((CHUNK_ROWS, n), lambda i: (i, 0))
    cspec = pl.BlockSpec((nseq, N_HEADS, DK, DV), lambda i: (i, 0, 0, 0))
    nspec = pl.BlockSpec((nseq, QK_DIM), lambda i: (i, 0))
    mspec = pl.BlockSpec((nseq, LANES), lambda i: (i, 0))
    return pl.pallas_call(
        functools.partial(_mlstm_sample_body, nseq=nseq, cs=seq_len),
        grid=(t // CHUNK_ROWS,),
        in_specs=[row(QK_DIM), row(QK_DIM), row(V_DIM), row(2 * LANES), _const_spec(sel.shape), cspec, nspec, mspec],
        out_specs=[row(V_DIM), cspec, nspec, mspec],
        out_shape=[jax.ShapeDtypeStruct((t, V_DIM), f32),
                   jax.ShapeDtypeStruct((n_batch, N_HEADS, DK, DV), f32),
                   jax.ShapeDtypeStruct((n_batch, QK_DIM), f32),
                   jax.ShapeDtypeStruct((n_batch, LANES), f32)],
        compiler_params=_params(1),
        name="l0_mlstm_core",
    )(q, k, v, gates, sel, c0, n0, m0)


def _outproj_ffn_body(h_ref, o_ref, x_ref, hg_ref, wout_ref, gffn_ref, wup_ref, wdn_ref, y_ref):
    x1s = []
    for r in _row_splits(h_ref.shape[0]):
        hn = h_ref[r, :] * hg_ref[...] * jax.nn.sigmoid(o_ref[r, :])
        x1s.append(x_ref[r, :] + _dot(hn.astype(bf16), wout_ref[...]))
    for r, y in zip(_row_splits(h_ref.shape[0]), _ffn(x1s, gffn_ref, wup_ref, wdn_ref, finish=lambda x2: x2)):
        y_ref[r, :] = y


def _outproj_ffn(hmix, o, x2d, head_g, w_out, g_ffn, w_up, w_down):
    t = x2d.shape[0]
    row = pl.BlockSpec((ROW_TILE, D_MODEL), lambda i: (i, 0))
    return pl.pallas_call(
        _outproj_ffn_body,
        grid=(t // ROW_TILE,),
        in_specs=[row, row, row, _const_spec((1, V_DIM)), _const_spec(w_out.shape),
                  _const_spec((1, D_MODEL)), _layer_spec(w_up.shape, 0), _layer_spec(w_down.shape, 0)],
        out_specs=row,
        out_shape=jax.ShapeDtypeStruct((t, D_MODEL), f32),
        compiler_params=_params(1),
        name="l0_outproj_ffn",
    )(hmix, o, x2d, head_g, w_out, g_ffn, w_up, w_down)


def _s5_prep_body(lre_ref, lim_ref, ldt_ref, bre_ref, bim_ref, cre_ref, cim_ref,
                  are_ref, aim_ref, wbr_ref, wbi_ref, vre_ref, vim_ref):
    lr = lre_ref[...]
    li = lim_ref[...]
    dt = jnp.exp(ldt_ref[...])
    mag = jnp.exp(lr * dt)
    a_re = mag * jnp.cos(li * dt)
    a_im = mag * jnp.sin(li * dt)
    den = lr * lr + li * li
    z_re = a_re - 1.0
    coef_re = ((z_re * lr + a_im * li) / den)[:, None, :]
    coef_im = ((a_im * lr - z_re * li) / den)[:, None, :]
    br = bre_ref[...]
    bi = bim_ref[...]
    are_ref[...] = a_re
    aim_ref[...] = a_im
    bb = (coef_re * br - coef_im * bi, coef_re * bi + coef_im * br)
    c_t = (cre_ref[...].reshape(S5_GROUPS * S5_GROUP, S5_STATE).T,
           cim_ref[...].reshape(S5_GROUPS * S5_GROUP, S5_STATE).T)

    groups = S5_BLOCK_CH // S5_GROUP
    log_ch, log_st = S5_GROUP.bit_length() - 1, S5_STATE.bit_length() - 1
    shape_b = (S5_BLOCK_CH, S5_BLOCK_ST)
    diag_b = ((lax.broadcasted_iota(jnp.int32, shape_b, 0) >> log_ch)
              == (lax.broadcasted_iota(jnp.int32, shape_b, 1) >> log_st))
    shape_c = (S5_BLOCK_ST, S5_BLOCK_CH)
    diag_c = ((lax.broadcasted_iota(jnp.int32, shape_c, 0) >> log_st)
              == (lax.broadcasted_iota(jnp.int32, shape_c, 1) >> log_ch))
    for j in range(N_S5_BLOCKS):
        for src, dst in zip(bb, (wbr_ref, wbi_ref)):
            blk = src[j * groups:(j + 1) * groups].reshape(S5_BLOCK_CH, S5_STATE)
            dst[j] = jnp.where(diag_b, jnp.concatenate([blk] * groups, axis=1), 0.0).astype(bf16)
        for src, dst in zip(c_t, (vre_ref, vim_ref)):
            blk = src[:, j * S5_BLOCK_CH:(j + 1) * S5_BLOCK_CH]
            rep = jnp.broadcast_to(blk[None], (groups, S5_STATE, S5_BLOCK_CH)).reshape(S5_BLOCK_ST, S5_BLOCK_CH)
            dst[j] = jnp.where(diag_c, rep, 0.0).astype(bf16)


def _s5_prep(lam_re, lam_im, log_dt, b_re_t, b_im_t, c_re, c_im):
    gp = jax.ShapeDtypeStruct((S5_GROUPS, S5_STATE), f32)
    wb = jax.ShapeDtypeStruct((N_S5_BLOCKS, S5_BLOCK_CH, S5_BLOCK_ST), bf16)
    vc = jax.ShapeDtypeStruct((N_S5_BLOCKS, S5_BLOCK_ST, S5_BLOCK_CH), bf16)
    return pl.pallas_call(_s5_prep_body, out_shape=[gp, gp, wb, wb, vc, vc], name="l1_s5_prep")(
        lam_re, lam_im, log_dt, b_re_t, b_im_t, c_re, c_im)


def _s5_body(x_ref, h0r_ref, h0i_ref, g_ref, are_ref, aim_ref, wbr_ref, wbi_ref, vre_ref, vim_ref,
             d_ref, wglu_ref, gffn_ref, wup_ref, wdn_ref, gfin_ref,
             y_ref, sre_ref, sim_ref, sr_scr, si_scr, hr_scr, hi_scr, act_scr, *io_scr, bt, lt, dma_io):
    rows = bt * lt
    step = pl.program_id(1)
    n_steps = pl.num_programs(1)

    @pl.when(step == 0)
    def _():
        sr_scr[...] = h0r_ref[...]
        si_scr[...] = h0i_ref[...]

    if dma_io:
        xbuf, ybuf, in_sem, out_sem = io_scr
        slot = lax.rem(step, 2)

        def in_copy(s, sl, b):
            return pltpu.make_async_copy(x_ref.at[b, pl.ds(s * lt, lt), :], xbuf.at[sl, :, b, :], in_sem.at[sl, b])

        def out_copy(s, sl, b):
            return pltpu.make_async_copy(ybuf.at[sl, :, b, :], y_ref.at[b, pl.ds(s * lt, lt), :], out_sem.at[sl, b])

        @pl.when(step == 0)
        def _():
            for b in range(bt):
                in_copy(0, 0, b).start()

        @pl.when(step + 1 < n_steps)
        def _():
            for b in range(bt):
                in_copy(step + 1, 1 - slot, b).start()

        for b in range(bt):
            in_copy(step, slot, b).wait()

        @pl.when(step >= 2)
        def _():
            for b in range(bt):
                out_copy(step - 2, slot, b).wait()

        xt = xbuf[slot].reshape(rows, D_MODEL)
    else:
        xt = jnp.concatenate([x_ref[:, t, :] for t in range(lt)], axis=0)
    u = _rms(xt, g_ref[...])
    ub = u.astype(bf16)

    nbuf = hr_scr.shape[0]

    def project_in(j):
        ch = slice(j * S5_BLOCK_CH, (j + 1) * S5_BLOCK_CH)
        hr_scr[j % nbuf] = _dot(ub[:, ch], wbr_ref[j])
        hi_scr[j % nbuf] = _dot(ub[:, ch], wbi_ref[j])

    def recur(j):
        st = slice(j * S5_BLOCK_ST, (j + 1) * S5_BLOCK_ST)
        a_re = jnp.broadcast_to(are_ref[:, st], (bt, S5_BLOCK_ST))
        a_im = jnp.broadcast_to(aim_ref[:, st], (bt, S5_BLOCK_ST))
        s_re = sr_scr[:, st]
        s_im = si_scr[:, st]
        for t in range(lt):
            r = slice(t * bt, (t + 1) * bt)
            n_re = a_re * s_re - a_im * s_im + hr_scr[j % nbuf, r, :]
            n_im = a_re * s_im + a_im * s_re + hi_scr[j % nbuf, r, :]
            hr_scr[j % nbuf, r, :] = n_re
            hi_scr[j % nbuf, r, :] = n_im
            s_re, s_im = n_re, n_im
        sr_scr[:, st] = s_re
        si_scr[:, st] = s_im

    def project_out(j):
        ch = slice(j * S5_BLOCK_CH, (j + 1) * S5_BLOCK_CH)
        yj = (_dot(hr_scr[j % nbuf].astype(bf16), vre_ref[j]) - _dot(hi_scr[j % nbuf].astype(bf16), vim_ref[j])
              + d_ref[:, ch] * u[:, ch])
        act_scr[:, ch] = jax.nn.gelu(yj).astype(bf16)

    project_in(0)
    for j in range(N_S5_BLOCKS + 1):
        if j + 1 < N_S5_BLOCKS:
            project_in(j + 1)
        if j >= 1:
            project_out(j - 1)
        if j < N_S5_BLOCKS:
            recur(j)

    sre_ref[...] = sr_scr[...]
    sim_ref[...] = si_scr[...]
    x3s = []
    for r in _row_splits(rows):
        ag = _dot(act_scr[r, :], wglu_ref[...])
        x3s.append(xt[r, :] + ag[:, :D_MODEL] * jax.nn.sigmoid(ag[:, D_MODEL:]))
    y = jnp.concatenate(_ffn(x3s, gffn_ref, wup_ref, wdn_ref, finish=lambda x4: _rms(x4, gfin_ref[...])), axis=0)
    if dma_io:
        ybuf[slot] = y.reshape(lt, bt, D_MODEL)
        for b in range(bt):
            out_copy(step, slot, b).start()

        @pl.when(step == n_steps - 1)
        def _():
            @pl.when(step >= 1)
            def _():
                for b in range(bt):
                    out_copy(step - 1, 1 - slot, b).wait()

            for b in range(bt):
                out_copy(step, slot, b).wait()
    else:
        for t in range(lt):
            y_ref[:, t, :] = y[t * bt:(t + 1) * bt, :]


def _s5_layer(x3d, h0_re, h0_im, bt, lt, g_mix, a_re, a_im, wb_re, wb_im, v_re, v_im, d_skip,
              w_glu, g_ffn, w_up, w_down, g_final):
    n_batch, seq_len, _ = x3d.shape
    grid = (n_batch // bt, seq_len // lt)
    sspec = pl.BlockSpec((bt, S5_N), lambda i, t: (i, 0))
    dma_io = n_batch == bt
    if dma_io:
        xspec = pl.BlockSpec(memory_space=pl.ANY)
        io_scratch = [pltpu.VMEM((2, lt, bt, D_MODEL), f32), pltpu.VMEM((2, lt, bt, D_MODEL), f32),
                      pltpu.SemaphoreType.DMA((2, bt)), pltpu.SemaphoreType.DMA((2, bt))]
    else:
        xspec = pl.BlockSpec((bt, lt, D_MODEL), lambda i, t: (i, t, 0))
        io_scratch = []
    consts = (g_mix, a_re, a_im, wb_re, wb_im, v_re, v_im, d_skip, w_glu, g_ffn, w_up, w_down, g_final)
    const_specs = [_layer_spec(c.shape, 1) if c is w_up or c is w_down else _const_spec(c.shape) for c in consts]
    rows = bt * lt
    return pl.pallas_call(
        functools.partial(_s5_body, bt=bt, lt=lt, dma_io=dma_io),
        grid=grid,
        in_specs=[xspec, sspec, sspec] + const_specs,
        out_specs=[xspec, sspec, sspec],
        out_shape=[jax.ShapeDtypeStruct(x3d.shape, f32), jax.ShapeDtypeStruct((n_batch, S5_N), f32),
                   jax.ShapeDtypeStruct((n_batch, S5_N), f32)],
        scratch_shapes=[pltpu.VMEM((bt, S5_N), f32), pltpu.VMEM((bt, S5_N), f32),
                        pltpu.VMEM((3, rows, S5_BLOCK_ST), f32), pltpu.VMEM((3, rows, S5_BLOCK_ST), f32),
                        pltpu.VMEM((rows, D_MODEL), bf16)] + io_scratch,
        compiler_params=_params(2),
        name="l1_s5_ffn",
    )(x3d, h0_re, h0_im, *consts)


def _rep_lanes(x):
    return jnp.tile(x, (1,) * (x.ndim - 1) + (LANES // x.shape[-1],))


def _mlstm_mixer(x, state, p):
    n_batch, seq_len, _ = x.shape
    x2d = x.reshape(n_batch * seq_len, D_MODEL)
    if state is None:
        o, hmix, c_new, n_new, m_new = _l0_prompt(x2d, p["g_mix0"], p["w_in"], p["w_gate"], p["gate_bias"],
                                                  n_batch, seq_len)
        n_new = jnp.swapaxes(n_new[:, :, :N_HEADS], 1, 2)
    else:
        c0, n0, m0 = state
        q, k, v, o, gates = _inproj(x2d, p["g_mix0"], p["w_in"], p["w_gate"], p["gate_bias"])
        hmix, c_new, n_new, m_new = _mlstm_core_sample(
            q, k, v, gates, c0, n0.reshape(n_batch, QK_DIM), _rep_lanes(m0), n_batch, seq_len)
    states = (c_new.reshape(1, n_batch, N_HEADS, DK, DV),
              n_new.reshape(1, n_batch, N_HEADS, DK),
              m_new.reshape(n_batch, LANES)[:, :N_HEADS].reshape(1, n_batch, N_HEADS))
    return (hmix, o, x2d), states


def _s5_stage(x2, shape, state, p):
    n_batch, seq_len, _ = shape
    if state is None:
        h0_re = jnp.zeros((n_batch, S5_N), f32)
        h0_im = h0_re
        bt, lt = n_batch, ROW_TILE // n_batch
    else:
        h0_re, h0_im = (s.reshape(n_batch, S5_N) for s in state)
        bt, lt = ROW_TILE // seq_len, seq_len
    y, s_re, s_im = _s5_layer(x2.reshape(shape), h0_re, h0_im, bt, lt,
                              p["g_mix1"], p["a_re"], p["a_im"], p["wb_re"], p["wb_im"], p["v_re"], p["v_im"],
                              p["d_skip"], p["w_glu"], p["g_ffn1"], p["w_up"], p["w_down"], p["g_final"])
    return y, (s_re.reshape(1, n_batch, S5_GROUPS, S5_STATE), s_im.reshape(1, n_batch, S5_GROUPS, S5_STATE))


def kernel(x_prompt, x_sample, state_mlstm_C, state_mlstm_n, state_mlstm_m, state_s5_re, state_s5_im,
           norm_mix_g, norm_ffn_g, norm_final_g, mlstm_w_in, mlstm_b_i, mlstm_b_f, mlstm_head_norm_g,
           mlstm_w_out, s5_lambda_re, s5_lambda_im, s5_log_dt, s5_b_re, s5_b_im, s5_c_re, s5_c_im,
           s5_d, s5_w_glu, ffn_w_up, ffn_w_down):
    w_in = mlstm_w_in[0]
    n_qkvo = W_QKVO_COLS
    a_re, a_im, wb_re, wb_im, v_re, v_im = _s5_prep(
        s5_lambda_re[0], s5_lambda_im[0], s5_log_dt[0].reshape(S5_GROUPS, 1),
        jnp.swapaxes(s5_b_re[0], 1, 2), jnp.swapaxes(s5_b_im[0], 1, 2), s5_c_re[0], s5_c_im[0])
    p = dict(
        g_mix0=norm_mix_g[0].reshape(1, D_MODEL), g_mix1=norm_mix_g[1].reshape(1, D_MODEL),
        g_ffn0=norm_ffn_g[0].reshape(1, D_MODEL), g_ffn1=norm_ffn_g[1].reshape(1, D_MODEL),
        g_final=norm_final_g.reshape(1, D_MODEL),
        w_in=jnp.swapaxes(w_in, 0, 1).astype(bf16),
        w_gate=jnp.concatenate([_rep_lanes(w_in[:, n_qkvo:n_qkvo + N_HEADS]),
                                _rep_lanes(w_in[:, n_qkvo + N_HEADS:])], axis=1).astype(bf16),
        gate_bias=jnp.concatenate([_rep_lanes(mlstm_b_i[0][None]), _rep_lanes(mlstm_b_f[0][None])], axis=1),
        head_g=mlstm_head_norm_g[0].reshape(1, V_DIM),
        w_out=mlstm_w_out[0].astype(bf16),
        w_up=ffn_w_up.astype(bf16), w_down=ffn_w_down.astype(bf16),
        a_re=a_re.reshape(1, S5_N), a_im=a_im.reshape(1, S5_N),
        wb_re=wb_re, wb_im=wb_im, v_re=v_re, v_im=v_im,
        d_skip=s5_d[0].reshape(1, D_MODEL),
        w_glu=s5_w_glu[0].astype(bf16),
    )
    mix_p, mlstm_p = _mlstm_mixer(x_prompt, None, p)
    mix_s, mlstm_s = _mlstm_mixer(x_sample, (state_mlstm_C[0], state_mlstm_n[0], state_mlstm_m[0]), p)
    x2_p, x2_s = (_outproj_ffn(*mix, p["head_g"], p["w_out"], p["g_ffn0"], p["w_up"], p["w_down"])
                  for mix in (mix_p, mix_s))
    y_p, s5_p = _s5_stage(x2_p, x_prompt.shape, None, p)
    y_s, s5_s = _s5_stage(x2_s, x_sample.shape, (state_s5_re[0], state_s5_im[0]), p)
    return (y_p, y_s) + mlstm_p + s5_p + mlstm_s + s5_s
```
